```python
import jax, jax.numpy as jnp
from jax import lax
import numpy as np

D_MODEL = 4096
BATCH = 2
SEQ = 4096
DEPTH = 4

N_MEM = 256
GROUP_WIDTH = D_MODEL // 4
D_MIX = 4 * GROUP_WIDTH
ROPE_THETA = 500000.0
QBLK = 128

MLA_HEADS = 8
MLA_NOPE = 128
MLA_ROPE = 64
MLA_V = GROUP_WIDTH // MLA_HEADS
MLA_Q_LORA = GROUP_WIDTH
MLA_KV_LORA = GROUP_WIDTH // 2

DSA_HEADS = 8
DSA_KV_HEADS = 2
DSA_GROUP = DSA_HEADS // DSA_KV_HEADS
DSA_HEAD_DIM = GROUP_WIDTH // DSA_HEADS
DSA_ROT = DSA_HEAD_DIM // 4
IDX_HEADS = 16
IDX_DIM = 64
IDX_ROT = IDX_DIM // 4
TOPK_MAX = 256

LRU_WIDTH = GROUP_WIDTH
LRU_BLOCKS = 16
LRU_BLOCK_DIM = LRU_WIDTH // LRU_BLOCKS
CONV_WIDTH = 4
LRU_C = 8.0

MEM_HEADS = 4
MEM_HEAD_DIM = GROUP_WIDTH // MEM_HEADS

DEEPNORM_ALPHA = (2 * DEPTH) ** 0.25
DEEPNORM_BETA = (8 * DEPTH) ** -0.25
LN_EPS = 1e-5
RMS_EPS = 1e-6

IN_SIZES = (
    MLA_Q_LORA, MLA_KV_LORA, MLA_ROPE, GROUP_WIDTH,
    GROUP_WIDTH, DSA_KV_HEADS * DSA_HEAD_DIM, DSA_KV_HEADS * DSA_HEAD_DIM,
    IDX_HEADS * IDX_DIM, IDX_DIM, IDX_HEADS, GROUP_WIDTH,
    LRU_WIDTH, GROUP_WIDTH,
    GROUP_WIDTH, GROUP_WIDTH,
)
D_IN = sum(IN_SIZES)

kernel_name = "hybrid_mla_dsa_rglru_mem_deepnorm"


def layer_norm(x, g, b):
    xf = x.astype(jnp.float32)
    mu = jnp.mean(xf, axis=-1, keepdims=True)
    var = jnp.mean(jnp.square(xf - mu), axis=-1, keepdims=True)
    return ((xf - mu) * lax.rsqrt(var + LN_EPS) * g + b).astype(x.dtype)


def rms_norm(x, g):
    xf = x.astype(jnp.float32)
    return (xf * lax.rsqrt(jnp.mean(jnp.square(xf), axis=-1, keepdims=True) + RMS_EPS) * g).astype(x.dtype)


def rope_tables(positions, rot_dim):
    inv = ROPE_THETA ** (-jnp.arange(0, rot_dim, 2, dtype=jnp.float32) / rot_dim)
    ang = positions.astype(jnp.float32)[:, :, None] * inv
    return jnp.cos(ang)[:, :, None, :], jnp.sin(ang)[:, :, None, :]


def apply_rope(x, cos, sin):
    half = cos.shape[-1]
    x1, x2, rest = x[..., :half], x[..., half:2 * half], x[..., 2 * half:]
    c, s = cos.astype(x.dtype), sin.astype(x.dtype)
    return jnp.concatenate([x1 * c - x2 * s, x2 * c + x1 * s, rest], axis=-1)


def sweep_query_blocks(fn, q_arrays):
    b, s = q_arrays[0].shape[:2]
    nb = s // QBLK
    blocks = tuple(a.reshape((b, nb, QBLK) + a.shape[2:]).swapaxes(0, 1) for a in q_arrays)
    out = lax.map(lambda args: fn(args[0], *args[1:]), (jnp.arange(nb),) + blocks)
    return out.swapaxes(0, 1).reshape(b, s, -1)


def mla_mixer(c_q, c_kv, k_rope_in, g_cq, g_ckv, w_uq, w_ukv, rope):
    b, s, _ = c_q.shape
    cos, sin = rope
    q = (rms_norm(c_q, g_cq) @ w_uq).reshape(b, s, MLA_HEADS, MLA_NOPE + MLA_ROPE)
    q_nope, q_rope = q[..., :MLA_NOPE], apply_rope(q[..., MLA_NOPE:], cos, sin)
    kv = (rms_norm(c_kv, g_ckv) @ w_ukv).reshape(b, s, MLA_HEADS, MLA_NOPE + MLA_V)
    k_nope, v = kv[..., :MLA_NOPE], kv[..., MLA_NOPE:]
    k_rope = apply_rope(k_rope_in[:, :, None, :], cos, sin)[:, :, 0]
    scale = (MLA_NOPE + MLA_ROPE) ** -0.5
    kpos = jnp.arange(s)

    def block(bi, qn, qr):
        qpos = bi * QBLK + jnp.arange(QBLK)
        sc = (jnp.einsum('bqhd,bkhd->bhqk', qn, k_nope)
              + jnp.einsum('bqhr,bkr->bhqk', qr, k_rope)).astype(jnp.float32) * scale
        sc = jnp.where(kpos[None, :] <= qpos[:, None], sc, -jnp.inf)
        p = jax.nn.softmax(sc, axis=-1).astype(v.dtype)
        o = jnp.einsum('bhqk,bkhd->bqhd', p, v)
        return o.reshape(o.shape[0], QBLK, -1)

    return sweep_query_blocks(block, (q_nope, q_rope))


def dsa_mixer(q, k, v, q_idx, k_idx, w_idx, rope_h, rope_i):
    b, s, _ = q.shape
    q = apply_rope(q.reshape(b, s, DSA_HEADS, DSA_HEAD_DIM), *rope_h)
    q = q.reshape(b, s, DSA_KV_HEADS, DSA_GROUP, DSA_HEAD_DIM)
    k = apply_rope(k.reshape(b, s, DSA_KV_HEADS, DSA_HEAD_DIM), *rope_h)
    v = v.reshape(b, s, DSA_KV_HEADS, DSA_HEAD_DIM)
    q_idx = apply_rope(q_idx.reshape(b, s, IDX_HEADS, IDX_DIM), *rope_i)
    k_idx = apply_rope(k_idx[:, :, None, :], *rope_i)[:, :, 0]
    w_idx = w_idx * (IDX_HEADS ** -0.5)
    topk = min(TOPK_MAX, s // 4)
    kpos = jnp.arange(s)
    gather = jax.vmap(lambda arr, ids: arr[ids])

    def block(bi, qb, qib, wib):
        qpos = bi * QBLK + jnp.arange(QBLK)
        causal = kpos[None, :] <= qpos[:, None]
        rel = jax.nn.relu(jnp.einsum('bqhd,bsd->bqhs', qib, k_idx).astype(jnp.float32) * IDX_DIM ** -0.5)
        score = jnp.einsum('bqh,bqhs->bqs', wib.astype(jnp.float32), rel)
        score = jnp.where(causal[None], score, -jnp.inf)
        _, idx = lax.top_k(score, topk)
        valid = idx <= qpos[None, :, None]
        kg = gather(k, idx)
        vg = gather(v, idx)
        sc = jnp.einsum('bqngd,bqsnd->bqngs', qb, kg).astype(jnp.float32) * DSA_HEAD_DIM ** -0.5
        sc = jnp.where(valid[:, :, None, None, :], sc, -jnp.inf)
        p = jax.nn.softmax(sc, axis=-1).astype(vg.dtype)
        o = jnp.einsum('bqngs,bqsnd->bqngd', p, vg)
        return o.reshape(o.shape[0], QBLK, -1)

    return sweep_query_blocks(block, (q, q_idx, w_idx))


def rglru_mixer(u, conv_w, conv_b, w_a, b_a, w_x, b_x, lam):
    b, s, _ = u.shape
    u = lax.conv_general_dilated(u, conv_w[:, None, :], window_strides=(1,),
                                 padding=[(CONV_WIDTH - 1, 0)],
                                 dimension_numbers=('NWC', 'WIO', 'NWC'),
                                 feature_group_count=LRU_WIDTH) + conv_b
    ub = u.reshape(b, s, LRU_BLOCKS, LRU_BLOCK_DIM)
    r = jax.nn.sigmoid(jnp.einsum('bsnd,nde->bsne', ub, w_a).reshape(b, s, -1) + b_a)
    i = jax.nn.sigmoid(jnp.einsum('bsnd,nde->bsne', ub, w_x).reshape(b, s, -1) + b_x)
    log_a = -LRU_C * r.astype(jnp.float32) * jax.nn.softplus(-lam.astype(jnp.float32))
    a = jnp.exp(log_a)
    inp = jnp.sqrt(-jnp.expm1(2.0 * log_a)) * (i * u).astype(jnp.float32)

    def combine(left, right):
        a_l, h_l = left
        a_r, h_r = right
        return a_l * a_r, a_r * h_l + h_r

    _, h = lax.associative_scan(combine, (a, inp), axis=1)
    return h.astype(u.dtype)


def memory_mixer(q, mem, w_mk, w_mv):
    b, s, _ = q.shape
    q = q.reshape(b, s, MEM_HEADS, MEM_HEAD_DIM)
    km = (mem @ w_mk).reshape(b, N_MEM, MEM_HEADS, MEM_HEAD_DIM)
    vm = (mem @ w_mv).reshape(b, N_MEM, MEM_HEADS, MEM_HEAD_DIM)
    sc = jnp.einsum('bqhd,bmhd->bhqm', q, km).astype(jnp.float32) * MEM_HEAD_DIM ** -0.5
    p = jax.nn.softmax(sc, axis=-1).astype(vm.dtype)
    return jnp.einsum('bhqm,bmhd->bqhd', p, vm).reshape(b, s, -1)


def hybrid_layer(x, mem, rope_mla, rope_dsa, rope_idx, w_in, g_cq, g_ckv, w_uq, w_ukv,
                 conv_w, conv_b, w_rg_a, b_rg_a, w_rg_x, b_rg_x, lru_lambda,
                 w_mem_k, w_mem_v, w_o, ln_g, ln_b):
    h = x @ w_in
    splits = [int(c) for c in np.cumsum(IN_SIZES)[:-1]]
    (a_cq, a_ckv, a_kr, a_gate,
     b_q, b_k, b_v, b_qi, b_ki, b_wi, b_gate,
     c_u, c_gate, d_q, d_gate) = jnp.split(h, splits, axis=-1)
    ya = mla_mixer(a_cq, a_ckv, a_kr, g_cq, g_ckv, w_uq, w_ukv, rope_mla)
    yb = dsa_mixer(b_q, b_k, b_v, b_qi, b_ki, b_wi, rope_dsa, rope_idx)
    yc = rglru_mixer(c_u, conv_w, conv_b, w_rg_a, b_rg_a, w_rg_x, b_rg_x, lru_lambda)
    yd = memory_mixer(d_q, mem, w_mem_k, w_mem_v)
    mixed = jnp.concatenate([ya * jax.nn.silu(a_gate), yb * jax.nn.silu(b_gate),
                             yc * jax.nn.silu(c_gate), yd * jax.nn.silu(d_gate)], axis=-1)
    y = mixed @ w_o
    return layer_norm(DEEPNORM_ALPHA * x + y, ln_g, ln_b)


def setup_inputs(seed: int = 0) -> dict:
    key = jax.random.key(seed)
    ks = jax.random.split(key, 24)
    f32 = jnp.float32

    def nrm(k, shape, scale):
        return jax.random.normal(k, shape, f32) * scale

    x = nrm(ks[0], (BATCH, SEQ, D_MODEL), 1.0)
    mem = nrm(ks[1], (BATCH, N_MEM, D_MODEL), 1.0)
    offs = jax.random.randint(ks[2], (BATCH, 1), 0, 1024, dtype=jnp.int32)
    positions = offs + jnp.arange(SEQ, dtype=jnp.int32)[None, :]
    w_in = nrm(ks[3], (DEPTH, D_MODEL, D_IN), D_MODEL ** -0.5)
    g_cq = 1.0 + nrm(ks[4], (DEPTH, MLA_Q_LORA), 0.02)
    g_ckv = 1.0 + nrm(ks[5], (DEPTH, MLA_KV_LORA), 0.02)
    w_uq = nrm(ks[6], (DEPTH, MLA_Q_LORA, MLA_HEADS * (MLA_NOPE + MLA_ROPE)), MLA_Q_LORA ** -0.5)
    w_ukv = nrm(ks[7], (DEPTH, MLA_KV_LORA, MLA_HEADS * (MLA_NOPE + MLA_V)), MLA_KV_LORA ** -0.5)
    conv_w = nrm(ks[8], (DEPTH, CONV_WIDTH, LRU_WIDTH), CONV_WIDTH ** -0.5)
    conv_b = nrm(ks[9], (DEPTH, LRU_WIDTH), 0.01)
    w_rg_a = nrm(ks[10], (DEPTH, LRU_BLOCKS, LRU_BLOCK_DIM, LRU_BLOCK_DIM), LRU_BLOCK_DIM ** -0.5)
    b_rg_a = nrm(ks[11], (DEPTH, LRU_WIDTH), 0.01)
    w_rg_x = nrm(ks[12], (DEPTH, LRU_BLOCKS, LRU_BLOCK_DIM, LRU_BLOCK_DIM), LRU_BLOCK_DIM ** -0.5)
    b_rg_x = nrm(ks[13], (DEPTH, LRU_WIDTH), 0.01)
    a_base = jax.random.uniform(ks[14], (DEPTH, LRU_WIDTH), f32, 0.9, 0.999)
    p = a_base ** (1.0 / LRU_C)
    lru_lambda = jnp.log(p) - jnp.log1p(-p)
    w_mem_k = nrm(ks[15], (DEPTH, D_MODEL, GROUP_WIDTH), D_MODEL ** -0.5)
    w_mem_v = nrm(ks[16], (DEPTH, D_MODEL, GROUP_WIDTH), D_MODEL ** -0.5)
    w_o = nrm(ks[17], (DEPTH, D_MIX, D_MODEL), D_MIX ** -0.5 * DEEPNORM_BETA)
    ln_g = 1.0 + nrm(ks[18], (DEPTH, D_MODEL), 0.02)
    ln_b = nrm(ks[19], (DEPTH, D_MODEL), 0.02)
    return {"x": x, "mem": mem, "positions": positions, "w_in": w_in, "g_cq": g_cq,
            "g_ckv": g_ckv, "w_uq": w_uq, "w_ukv": w_ukv, "conv_w": conv_w, "conv_b": conv_b,
            "w_rg_a": w_rg_a, "b_rg_a": b_rg_a, "w_rg_x": w_rg_x, "b_rg_x": b_rg_x,
            "lru_lambda": lru_lambda, "w_mem_k": w_mem_k, "w_mem_v": w_mem_v, "w_o": w_o,
            "ln_g": ln_g, "ln_b": ln_b}


def reference(x, mem, positions, w_in, g_cq, g_ckv, w_uq, w_ukv, conv_w, conv_b,
              w_rg_a, b_rg_a, w_rg_x, b_rg_x, lru_lambda, w_mem_k, w_mem_v, w_o, ln_g, ln_b):
    rope_mla = rope_tables(positions, MLA_ROPE)
    rope_dsa = rope_tables(positions, DSA_ROT)
    rope_idx = rope_tables(positions, IDX_ROT)
    for l in range(DEPTH):
        x = hybrid_layer(x, mem, rope_mla, rope_dsa, rope_idx, w_in[l], g_cq[l], g_ckv[l],
                         w_uq[l], w_ukv[l], conv_w[l], conv_b[l], w_rg_a[l], b_rg_a[l],
                         w_rg_x[l], b_rg_x[l], lru_lambda[l], w_mem_k[l], w_mem_v[l], w_o[l],
                         ln_g[l], ln_b[l])
    return x
```

```python
import functools

import numpy as np
import jax
import jax.numpy as jnp
from jax import lax
from jax.experimental import pallas as pl
from jax.experimental.pallas import tpu as pltpu

F32 = jnp.float32
BF16 = jnp.bfloat16
I32 = jnp.int32

DEPTH = 4
D_MODEL = 4096
N_MEM = 256
GROUP_WIDTH = D_MODEL // 4
ROPE_THETA = 500000.0
MLA_HEADS = 8
MLA_NOPE = 128
MLA_ROPE = 64
MLA_V = 128
MLA_Q_LORA = GROUP_WIDTH
MLA_KV_LORA = GROUP_WIDTH // 2
DSA_HEADS = 8
DSA_KV_HEADS = 2
DSA_GROUP = DSA_HEADS // DSA_KV_HEADS
DSA_HEAD_DIM = 128
DSA_ROT = DSA_HEAD_DIM // 4
IDX_HEADS = 16
IDX_DIM = 64
IDX_ROT = IDX_DIM // 4
TOPK_MAX = 256
LRU_WIDTH = GROUP_WIDTH
LRU_BLOCKS = 16
LRU_BLOCK_DIM = LRU_WIDTH // LRU_BLOCKS
CONV_WIDTH = 4
LRU_C = 8.0
MEM_HEADS = 4
MEM_HEAD_DIM = GROUP_WIDTH // MEM_HEADS
DEEPNORM_ALPHA = (2 * DEPTH) ** 0.25
LN_EPS = 1e-5
RMS_EPS = 1e-6

IN_SIZES = (
    MLA_Q_LORA, MLA_KV_LORA, MLA_ROPE, GROUP_WIDTH,
    GROUP_WIDTH, DSA_KV_HEADS * DSA_HEAD_DIM, DSA_KV_HEADS * DSA_HEAD_DIM,
    IDX_HEADS * IDX_DIM, IDX_DIM, IDX_HEADS, GROUP_WIDTH,
    LRU_WIDTH, GROUP_WIDTH,
    GROUP_WIDTH, GROUP_WIDTH,
)

LANES = 128
V7X_VMEM_BUDGET = 56 * 1024 * 1024

COL_CQ = 0
COL_AG = 1024
COL_BQ = 2048
COL_QI = 3072
COL_BG = 4096
COL_U = 5120
COL_CG = 6144
COL_DQ = 7168
COL_DG = 8192
COL_CKV = 9216
COL_BK = 9728
COL_BV = 9984
COL_KR = 10240
COL_KI = 10368
N_PACKED = 10496

TAB_MLA_C, TAB_MLA_S, TAB_DSA_C, TAB_DSA_S, TAB_IDX_C, TAB_IDX_S = range(6)
N_TABS = 6

NEG_BIG = -1e30
INT_MIN = -2 ** 31
KEY_CHUNK = 256
DSA_QBLK = 128


def _vmem_limit(nbytes):
    return int(min(V7X_VMEM_BUDGET, max(nbytes, 16 * 1024 * 1024)))


def _params(nbytes, ndims):
    return pltpu.CompilerParams(dimension_semantics=("arbitrary",) * ndims,
                                vmem_limit_bytes=_vmem_limit(nbytes))


def _mm_kernel(*refs, n_lhs, kg):
    w_ref = refs[n_lhs]
    o_ref = refs[n_lhs + 1]
    acc = None
    for g in range(n_lhs):
        part = jnp.dot(refs[g][...], w_ref[g * kg:(g + 1) * kg, :], preferred_element_type=F32)
        acc = part if acc is None else acc + part
    o_ref[...] = acc.astype(o_ref.dtype)


def _matmul(lhs_list, w, out_dtype, tm, tn, name):
    m, kg = lhs_list[0].shape
    k, n = w.shape
    n_lhs = len(lhs_list)
    assert k == kg * n_lhs and m % tm == 0 and n % tn == 0
    est = 2 * (tm * k * 2 + k * tn * 2 + tm * tn * jnp.dtype(out_dtype).itemsize) + tm * tn * 8
    return pl.pallas_call(
        functools.partial(_mm_kernel, n_lhs=n_lhs, kg=kg),
        grid=(m // tm, n // tn),
        in_specs=[pl.BlockSpec((tm, kg), lambda i, j: (i, 0)) for _ in range(n_lhs)]
        + [pl.BlockSpec((k, tn), lambda i, j: (0, j))],
        out_specs=pl.BlockSpec((tm, tn), lambda i, j: (i, j)),
        out_shape=jax.ShapeDtypeStruct((m, n), out_dtype),
        compiler_params=_params(est + (4 << 20), 2),
        name=name,
    )(*lhs_list, w)


def _rope(v, c_tab, s_tab, half, period):
    width = v.shape[-1]
    lane = lax.broadcasted_iota(I32, v.shape, 1) & (period - 1)
    swapped = jnp.where(lane < half, pltpu.roll(v, width - half, 1), pltpu.roll(v, half, 1))
    return v * c_tab + swapped * s_tab


def _tile_lanes(t, reps):
    return t if reps == 1 else jnp.concatenate([t] * reps, axis=1)


def _rms(x, g):
    return x * lax.rsqrt(jnp.mean(x * x, axis=-1, keepdims=True) + RMS_EPS) * g


def _silu(g):
    return g * jax.nn.sigmoid(g)


def _prep_mla_kernel(cq_ref, ckv_ref, kr_ref, tab_ref, gcq_ref, gckv_ref, wuq_ref, wk_ref, wvt_ref,
                     q_out, k_out, vt_out):
    scale = (MLA_NOPE + MLA_ROPE) ** -0.5
    nq = _rms(cq_ref[...], gcq_ref[...]).astype(BF16)
    nkv = _rms(ckv_ref[...], gckv_ref[...]).astype(BF16)
    c_tab = tab_ref[:, TAB_MLA_C * LANES:(TAB_MLA_C + 1) * LANES]
    s_tab = tab_ref[:, TAB_MLA_S * LANES:(TAB_MLA_S + 1) * LANES]
    qf = jnp.dot(nq, wuq_ref[...], preferred_element_type=F32) * scale
    kn = jnp.dot(nkv, wk_ref[...], preferred_element_type=F32)
    kr = _rope(kr_ref[...], c_tab, s_tab, MLA_ROPE // 2, LANES).astype(BF16)
    for h in range(MLA_HEADS):
        base = h * 2 * LANES
        q_out[:, base:base + LANES] = qf[:, base:base + LANES].astype(BF16)
        q_out[:, base + LANES:base + 2 * LANES] = _rope(
            qf[:, base + LANES:base + 2 * LANES], c_tab, s_tab, MLA_ROPE // 2, LANES).astype(BF16)
        k_out[:, base:base + LANES] = kn[:, h * LANES:(h + 1) * LANES].astype(BF16)
        k_out[:, base + LANES:base + 2 * LANES] = kr
    vt = lax.dot_general(wvt_ref[...], nkv, (((1,), (1,)), ((), ())), preferred_element_type=F32)
    vt_out[...] = vt.astype(BF16)


def _prep_mla(h2d, tabs, g_cq, g_ckv, wuq_p, wk, wvt, batch, seq, tm):
    m = batch * seq
    nt = seq // tm
    hd = MLA_HEADS * 2 * LANES
    return pl.pallas_call(
        _prep_mla_kernel,
        grid=(batch, nt),
        in_specs=[
            pl.BlockSpec((tm, MLA_Q_LORA), lambda b, i: (b * nt + i, COL_CQ // MLA_Q_LORA)),
            pl.BlockSpec((tm, MLA_KV_LORA), lambda b, i: (b * nt + i, COL_CKV // MLA_KV_LORA)),
            pl.BlockSpec((tm, LANES), lambda b, i: (b * nt + i, COL_KR // LANES)),
            pl.BlockSpec((tm, N_TABS * LANES), lambda b, i: (b * nt + i, 0)),
            pl.BlockSpec((1, MLA_Q_LORA), lambda b, i: (0, 0)),
            pl.BlockSpec((1, MLA_KV_LORA), lambda b, i: (0, 0)),
            pl.BlockSpec((MLA_Q_LORA, hd), lambda b, i: (0, 0)),
            pl.BlockSpec((MLA_KV_LORA, MLA_HEADS * MLA_NOPE), lambda b, i: (0, 0)),
            pl.BlockSpec((MLA_HEADS * MLA_V, MLA_KV_LORA), lambda b, i: (0, 0)),
        ],
        out_specs=[
            pl.BlockSpec((tm, hd), lambda b, i: (b * nt + i, 0)),
            pl.BlockSpec((tm, hd), lambda b, i: (b * nt + i, 0)),
            pl.BlockSpec((None, None, MLA_HEADS * MLA_V, tm), lambda b, i: (b, i, 0, 0)),
        ],
        out_shape=[
            jax.ShapeDtypeStruct((m, hd), BF16),
            jax.ShapeDtypeStruct((m, hd), BF16),
            jax.ShapeDtypeStruct((batch, nt, MLA_HEADS * MLA_V, tm), BF16),
        ],
        compiler_params=_params(40 << 20, 2),
        name="prep_mla",
    )(h2d, h2d, h2d, tabs, g_cq, g_ckv, wuq_p, wk, wvt)


def _softmax_step(s, vt_chunk, m, l, acc):
    m_new = jnp.maximum(m, jnp.max(s, axis=0, keepdims=True))
    alpha = jnp.exp(m - m_new)
    p = jnp.exp(s - m_new)
    l_new = alpha * l + jnp.sum(p, axis=0, keepdims=True)
    acc_new = alpha * acc + jnp.dot(vt_chunk, p.astype(BF16), preferred_element_type=F32)
    return m_new, l_new, acc_new


_NT = (((1,), (1,)), ((), ()))


def _mla_attn_kernel(q_ref, k_ref, vt_ref, gate_ref, o_ref, *, tile):
    i = pl.program_id(2)
    q = q_ref[...]

    def step(c, carry, neg):
        kc = k_ref[pl.ds(pl.multiple_of(c * tile, tile), tile), :]
        s = lax.dot_general(kc, q, _NT, preferred_element_type=F32)
        if neg is not None:
            s = s + neg
        return _softmax_step(s, vt_ref[c], *carry)

    init = (jnp.full((1, tile), NEG_BIG, F32), jnp.zeros((1, tile), F32),
            jnp.zeros((MLA_V, tile), F32))
    carry = lax.fori_loop(0, i, lambda c, cr: step(c, cr, None), init)
    kidx = lax.broadcasted_iota(I32, (tile, tile), 0)
    qidx = lax.broadcasted_iota(I32, (tile, tile), 1)
    _, l, acc = step(i, carry, jnp.where(kidx <= qidx, 0.0, NEG_BIG))
    o = (acc / l).T
    o_ref[...] = (o * _silu(gate_ref[...])).astype(o_ref.dtype)


def _mla_attn(q, k, vt, h2d, batch, seq, tile):
    m = batch * seq
    nt = seq // tile
    return pl.pallas_call(
        functools.partial(_mla_attn_kernel, tile=tile),
        grid=(batch, MLA_HEADS, nt),
        in_specs=[
            pl.BlockSpec((tile, 2 * LANES), lambda b, h, i: (b * nt + i, h)),
            pl.BlockSpec((seq, 2 * LANES), lambda b, h, i: (b, h)),
            pl.BlockSpec((None, nt, MLA_V, tile), lambda b, h, i: (b, 0, h, 0)),
            pl.BlockSpec((tile, LANES), lambda b, h, i: (b * nt + i, COL_AG // LANES + h)),
        ],
        out_specs=pl.BlockSpec((tile, LANES), lambda b, h, i: (b * nt + i, h)),
        out_shape=jax.ShapeDtypeStruct((m, GROUP_WIDTH), BF16),
        compiler_params=_params(32 << 20, 3),
        name="mla_attn",
    )(q, k, vt, h2d)


def _prep_dsa_kernel(q_ref, k_ref, v_ref, qi_ref, ki_ref, tab_ref,
                     q_out, k_out, vt_out, qi_out, ki_out, wt_out):
    def tab(t):
        return tab_ref[:, t * LANES:(t + 1) * LANES]

    scale = DSA_HEAD_DIM ** -0.5
    q = q_ref[...]
    q_out[...] = (_rope(q, _tile_lanes(tab(TAB_DSA_C), DSA_HEADS), _tile_lanes(tab(TAB_DSA_S), DSA_HEADS),
                        DSA_ROT // 2, LANES) * scale).astype(BF16)
    k = k_ref[...]
    k_out[...] = _rope(k, _tile_lanes(tab(TAB_DSA_C), DSA_KV_HEADS), _tile_lanes(tab(TAB_DSA_S), DSA_KV_HEADS),
                       DSA_ROT // 2, LANES).astype(BF16)
    vt_out[...] = v_ref[...].T.astype(BF16)
    reps = IDX_HEADS * IDX_DIM // LANES
    qi_out[...] = _rope(qi_ref[...], _tile_lanes(tab(TAB_IDX_C), reps), _tile_lanes(tab(TAB_IDX_S), reps),
                        IDX_ROT // 2, IDX_DIM)
    small = ki_ref[...]
    lane = lax.broadcasted_iota(I32, small.shape, 1)
    c_ki = jnp.where(lane < IDX_DIM, tab(TAB_IDX_C), 1.0)
    s_ki = jnp.where(lane < IDX_DIM, tab(TAB_IDX_S), 0.0)
    ki_out[...] = _rope(small, c_ki, s_ki, IDX_ROT // 2, IDX_DIM)
    wt = small.T[IDX_DIM:IDX_DIM + IDX_HEADS, :]
    wt_out[...] = wt * (IDX_HEADS ** -0.5 * IDX_DIM ** -0.5)


def _prep_dsa(h2d, tabs, batch, seq):
    tm = KEY_CHUNK
    m = batch * seq
    nt = seq // tm
    kvw = DSA_KV_HEADS * DSA_HEAD_DIM

    def row(b, i):
        return b * nt + i

    return pl.pallas_call(
        _prep_dsa_kernel,
        grid=(batch, nt),
        in_specs=[
            pl.BlockSpec((tm, GROUP_WIDTH), lambda b, i: (row(b, i), COL_BQ // GROUP_WIDTH)),
            pl.BlockSpec((tm, kvw), lambda b, i: (row(b, i), COL_BK // kvw)),
            pl.BlockSpec((tm, kvw), lambda b, i: (row(b, i), COL_BV // kvw)),
            pl.BlockSpec((tm, GROUP_WIDTH), lambda b, i: (row(b, i), COL_QI // GROUP_WIDTH)),
            pl.BlockSpec((tm, LANES), lambda b, i: (row(b, i), COL_KI // LANES)),
            pl.BlockSpec((tm, N_TABS * LANES), lambda b, i: (row(b, i), 0)),
        ],
        out_specs=[
            pl.BlockSpec((tm, GROUP_WIDTH), lambda b, i: (row(b, i), 0)),
            pl.BlockSpec((tm, kvw), lambda b, i: (row(b, i), 0)),
            pl.BlockSpec((None, None, kvw, tm), lambda b, i: (b, i, 0, 0)),
            pl.BlockSpec((tm, GROUP_WIDTH), lambda b, i: (row(b, i), 0)),
            pl.BlockSpec((tm, LANES), lambda b, i: (row(b, i), 0)),
            pl.BlockSpec((None, IDX_HEADS, tm), lambda b, i: (b, 0, i)),
        ],
        out_shape=[
            jax.ShapeDtypeStruct((m, GROUP_WIDTH), BF16),
            jax.ShapeDtypeStruct((m, kvw), BF16),
            jax.ShapeDtypeStruct((batch, nt, kvw, tm), BF16),
            jax.ShapeDtypeStruct((m, GROUP_WIDTH), F32),
            jax.ShapeDtypeStruct((m, LANES), F32),
            jax.ShapeDtypeStruct((batch, IDX_HEADS, seq), F32),
        ],
        compiler_params=_params(24 << 20, 2),
        name="prep_dsa",
    )(h2d, h2d, h2d, h2d, h2d, tabs)


def _sortable_key(score):
    bits = lax.bitcast_convert_type(score, I32)
    key = jnp.where(bits < 0, bits ^ jnp.int32(0x7FFFFFFF), bits)
    return jnp.where(score == 0.0, 0, key)


def _dsa_kernel(qi4_ref, ki4_ref, wt_ref, q_ref, k_ref, vt_ref, gate_ref, o_ref, key_ref,
                *, topk, seq_bits):
    i = pl.program_id(1)
    n_vis = i // 2
    n_chunks = n_vis + 1
    wt = wt_ref[...]
    ch = KEY_CHUNK
    qb = DSA_QBLK

    def rows(c):
        return pl.ds(pl.multiple_of(c * ch, ch), ch)

    def score_chunk(c):
        kc = ki4_ref[rows(c), :]
        tot = None
        for p in range(IDX_HEADS // 2):
            s = lax.dot_general(kc, qi4_ref[p], _NT, preferred_element_type=F32)
            r = jnp.maximum(s, 0.0)
            t = r[:, :qb] * wt[2 * p:2 * p + 1, :] + r[:, qb:] * wt[2 * p + 1:2 * p + 2, :]
            tot = t if tot is None else tot + t
        return tot

    def phase1(c, carry):
        key_ref[rows(c), :] = _sortable_key(score_chunk(c))
        return carry

    lax.fori_loop(0, n_vis, phase1, 0)
    kidx = n_vis * ch + lax.broadcasted_iota(I32, (ch, qb), 0)
    qidx = i * qb + lax.broadcasted_iota(I32, (ch, qb), 1)
    key_ref[rows(n_vis), :] = jnp.where(kidx <= qidx, _sortable_key(score_chunk(n_vis)), INT_MIN)

    def count(pred):
        def body(c, acc):
            idx = c * ch + lax.broadcasted_iota(I32, (ch, qb), 0)
            hit = pred(key_ref[rows(c), :], idx).astype(I32)
            return acc + jnp.sum(hit.reshape(ch // 8, 8, qb), axis=0)
        acc = lax.fori_loop(0, n_chunks, body, jnp.zeros((8, qb), I32))
        return jnp.sum(acc, axis=0, keepdims=True)

    def count_ge(t):
        return count(lambda keys, idx: keys >= t)

    thr = jnp.where(count_ge(jnp.zeros((1, qb), I32)) >= topk, 0, INT_MIN).astype(I32)

    def bit_step(b, t):
        cand = t + jnp.left_shift(jnp.int32(1), 30 - b)
        return jnp.where(count_ge(cand) >= topk, cand, t)

    thr = lax.fori_loop(0, 31, bit_step, thr)
    c_ge = count_ge(thr)
    need = topk - count_ge(thr + 1)
    tied = jnp.logical_and(c_ge > topk, thr > INT_MIN)

    def tie_search():
        def step(b, lo):
            cand = lo + jnp.left_shift(jnp.int32(1), seq_bits - 1 - b)
            below = count(lambda keys, idx: jnp.logical_and(keys == thr, idx < cand))
            return jnp.where(below < need, cand, lo)
        return lax.fori_loop(0, seq_bits, step, jnp.zeros((1, qb), I32))

    last_eq = lax.cond(jnp.max(tied.astype(I32)) > 0, tie_search,
                       lambda: jnp.full((1, qb), 2 ** seq_bits, I32))
    thr_sel = jnp.maximum(thr, INT_MIN + 1)

    qs = [jnp.concatenate([q_ref[:, (n * DSA_GROUP + g) * LANES:(n * DSA_GROUP + g + 1) * LANES]
                           for g in range(DSA_GROUP)], axis=0) for n in range(DSA_KV_HEADS)]
    nq = DSA_GROUP * qb

    def phase3(c, carry):
        keys = key_ref[rows(c), :]
        idx = c * ch + lax.broadcasted_iota(I32, (ch, qb), 0)
        drop = jnp.logical_and(keys == thr, idx > last_eq)
        sel = jnp.logical_and(keys >= thr_sel, jnp.logical_not(drop))
        neg = jnp.where(sel, 0.0, NEG_BIG)
        neg = jnp.concatenate([neg] * DSA_GROUP, axis=1)
        out = []
        for n in range(DSA_KV_HEADS):
            kc = k_ref[rows(c), n * LANES:(n + 1) * LANES]
            s = lax.dot_general(kc, qs[n], _NT, preferred_element_type=F32) + neg
            out.append(_softmax_step(s, vt_ref[c, n * LANES:(n + 1) * LANES, :], *carry[n]))
        return tuple(out)

    init = tuple((jnp.full((1, nq), NEG_BIG, F32), jnp.zeros((1, nq), F32),
                  jnp.zeros((DSA_HEAD_DIM, nq), F32)) for _ in range(DSA_KV_HEADS))
    res = lax.fori_loop(0, n_chunks, phase3, init)
    for n in range(DSA_KV_HEADS):
        _, l, acc = res[n]
        o = (acc / l).T
        for g in range(DSA_GROUP):
            col = (n * DSA_GROUP + g) * LANES
            o_ref[:, col:col + LANES] = (o[g * qb:(g + 1) * qb, :]
                                         * _silu(gate_ref[:, col:col + LANES])).astype(o_ref.dtype)


def _dsa(qi4, ki4, wt, q, k, vt, h2d, batch, seq):
    m = batch * seq
    nq = seq // DSA_QBLK
    nck = seq // KEY_CHUNK
    kvw = DSA_KV_HEADS * DSA_HEAD_DIM
    topk = min(TOPK_MAX, seq // 4)
    seq_bits = int(np.log2(seq))
    assert 2 ** seq_bits == seq
    return pl.pallas_call(
        functools.partial(_dsa_kernel, topk=topk, seq_bits=seq_bits),
        grid=(batch, nq),
        in_specs=[
            pl.BlockSpec((None, None, IDX_HEADS // 2, 2 * DSA_QBLK, 4 * IDX_DIM),
                         lambda b, i: (b, i, 0, 0, 0)),
            pl.BlockSpec((seq, 4 * IDX_DIM), lambda b, i: (b, 0)),
            pl.BlockSpec((None, IDX_HEADS, DSA_QBLK), lambda b, i: (b, 0, i)),
            pl.BlockSpec((DSA_QBLK, GROUP_WIDTH), lambda b, i: (b * nq + i, 0)),
            pl.BlockSpec((seq, kvw), lambda b, i: (b, 0)),
            pl.BlockSpec((None, nck, kvw, KEY_CHUNK), lambda b, i: (b, 0, 0, 0)),
            pl.BlockSpec((DSA_QBLK, GROUP_WIDTH), lambda b, i: (b * nq + i, COL_BG // GROUP_WIDTH)),
        ],
        out_specs=pl.BlockSpec((DSA_QBLK, GROUP_WIDTH), lambda b, i: (b * nq + i, 0)),
        out_shape=jax.ShapeDtypeStruct((m, GROUP_WIDTH), BF16),
        scratch_shapes=[pltpu.VMEM((seq, DSA_QBLK), I32)],
        compiler_params=_params(32 << 20, 2),
        name="dsa",
    )(qi4, ki4, wt, q, k, vt, h2d)


def _rglru_kernel(u_ref, gate_ref, cw_ref, cb_ref, wa_ref, ba_ref, wx_ref, bx_ref, lam_ref,
                  o_ref, halo_ref, h_ref, a_scr, x_scr, y_scr, *, batch, tt):
    t = pl.program_id(0)

    @pl.when(t == 0)
    def _():
        halo_ref[...] = jnp.zeros_like(halo_ref)
        h_ref[...] = jnp.zeros_like(h_ref)

    lam = lam_ref[...]
    z = -lam
    softplus = jnp.maximum(z, 0.0) + jnp.log1p(jnp.exp(-jnp.abs(z)))
    row = lax.broadcasted_iota(I32, (tt, LRU_WIDTH), 0)
    for b in range(batch):
        u = u_ref[b]
        prev = halo_ref[b]
        p1, p2, p3 = prev[7:8, :], prev[6:7, :], prev[5:6, :]
        s1 = jnp.where(row == 0, p1, pltpu.roll(u, 1, 0))
        s2 = jnp.where(row == 0, p2, jnp.where(row == 1, p1, pltpu.roll(u, 2, 0)))
        s3 = jnp.where(row == 0, p3, jnp.where(row == 1, p2, jnp.where(row == 2, p1, pltpu.roll(u, 3, 0))))
        conv = (cw_ref[3:4, :] * u + cw_ref[2:3, :] * s1 + cw_ref[1:2, :] * s2 + cw_ref[0:1, :] * s3
                + cb_ref[...])
        halo_ref[b] = u[tt - 8:, :]
        cb16 = conv.astype(BF16)
        r = jax.nn.sigmoid(jnp.dot(cb16, wa_ref[...], preferred_element_type=F32) + ba_ref[...])
        ig = jax.nn.sigmoid(jnp.dot(cb16, wx_ref[...], preferred_element_type=F32) + bx_ref[...])
        log_a = -LRU_C * r * softplus
        a_scr[b] = jnp.exp(log_a)
        th = jnp.tanh(log_a)
        x_scr[b] = jnp.sqrt(-2.0 * th / (1.0 - th)) * (ig * conv)

    def step(s, h):
        h = a_scr[:, pl.ds(s, 1), :] * h + x_scr[:, pl.ds(s, 1), :]
        y_scr[:, pl.ds(s, 1), :] = h
        return h

    h_ref[...] = lax.fori_loop(0, tt, step, h_ref[...], unroll=8)
    o_ref[...] = (y_scr[...] * _silu(gate_ref[...])).astype(o_ref.dtype)


def _rglru(h3d, conv_w, conv_b, wa_bd, b_a, wx_bd, b_x, lam, batch, seq, tt):
    nt = seq // tt
    w = LRU_WIDTH

    def vec(r):
        return pl.BlockSpec((r, w), lambda t: (0, 0))

    return pl.pallas_call(
        functools.partial(_rglru_kernel, batch=batch, tt=tt),
        grid=(nt,),
        in_specs=[
            pl.BlockSpec((batch, tt, w), lambda t: (0, t, COL_U // w)),
            pl.BlockSpec((batch, tt, w), lambda t: (0, t, COL_CG // w)),
            vec(CONV_WIDTH), vec(1),
            pl.BlockSpec((w, w), lambda t: (0, 0)), vec(1),
            pl.BlockSpec((w, w), lambda t: (0, 0)), vec(1),
            vec(1),
        ],
        out_specs=pl.BlockSpec((batch, tt, w), lambda t: (0, t, 0)),
        out_shape=jax.ShapeDtypeStruct((batch, seq, w), BF16),
        scratch_shapes=[pltpu.VMEM((batch, 8, w), F32), pltpu.VMEM((batch, 1, w), F32),
                        pltpu.VMEM((batch, tt, w), F32), pltpu.VMEM((batch, tt, w), F32),
                        pltpu.VMEM((batch, tt, w), F32)],
        compiler_params=_params(32 << 20, 1),
        name="rglru",
    )(h3d, h3d, conv_w, conv_b, wa_bd, b_a, wx_bd, b_x, lam)


def _mem_attn_kernel(q_ref, gate_ref, km_ref, vm_ref, o_ref):
    scale = MEM_HEAD_DIM ** -0.5
    for h in range(MEM_HEADS):
        cols = slice(h * MEM_HEAD_DIM, (h + 1) * MEM_HEAD_DIM)
        qh = (q_ref[:, cols] * scale).astype(BF16)
        s = lax.dot_general(qh, km_ref[:, cols], _NT, preferred_element_type=F32)
        p = jnp.exp(s - jnp.max(s, axis=-1, keepdims=True))
        l = jnp.sum(p, axis=-1, keepdims=True)
        o = jnp.dot(p.astype(BF16), vm_ref[:, cols], preferred_element_type=F32) / l
        o_ref[:, cols] = (o * _silu(gate_ref[:, cols])).astype(o_ref.dtype)


def _mem_attn(h2d, kvm, batch, seq, tm):
    m = batch * seq
    nt = seq // tm
    w = GROUP_WIDTH
    return pl.pallas_call(
        _mem_attn_kernel,
        grid=(batch, nt),
        in_specs=[
            pl.BlockSpec((tm, w), lambda b, i: (b * nt + i, COL_DQ // w)),
            pl.BlockSpec((tm, w), lambda b, i: (b * nt + i, COL_DG // w)),
            pl.BlockSpec((N_MEM, w), lambda b, i: (b, 0)),
            pl.BlockSpec((N_MEM, w), lambda b, i: (b, 1)),
        ],
        out_specs=pl.BlockSpec((tm, w), lambda b, i: (b * nt + i, 0)),
        out_shape=jax.ShapeDtypeStruct((m, w), BF16),
        compiler_params=_params(24 << 20, 2),
        name="mem_attn",
    )(h2d, h2d, kvm, kvm)


def _deepnorm_kernel(x_ref, y_ref, g_ref, b_ref, o_ref, o16_ref):
    z = DEEPNORM_ALPHA * x_ref[...] + y_ref[...]
    mu = jnp.mean(z, axis=-1, keepdims=True)
    d = z - mu
    var = jnp.mean(d * d, axis=-1, keepdims=True)
    out = d * lax.rsqrt(var + LN_EPS) * g_ref[...] + b_ref[...]
    o_ref[...] = out
    o16_ref[...] = out.astype(BF16)


def _deepnorm(x, y, g, b, tm):
    m, d = x.shape
    return pl.pallas_call(
        _deepnorm_kernel,
        grid=(m // tm,),
        in_specs=[pl.BlockSpec((tm, d), lambda i: (i, 0)), pl.BlockSpec((tm, d), lambda i: (i, 0)),
                  pl.BlockSpec((1, d), lambda i: (0, 0)), pl.BlockSpec((1, d), lambda i: (0, 0))],
        out_specs=[pl.BlockSpec((tm, d), lambda i: (i, 0)), pl.BlockSpec((tm, d), lambda i: (i, 0))],
        out_shape=[jax.ShapeDtypeStruct((m, d), F32), jax.ShapeDtypeStruct((m, d), BF16)],
        compiler_params=_params(40 << 20, 1),
        name="deepnorm",
    )(x, y, g, b)


def _rope_tabs(positions):
    pos = positions.astype(F32)
    b, s = pos.shape

    def one(rot, period):
        inv = ROPE_THETA ** (-jnp.arange(0, rot, 2, dtype=F32) / rot)
        ang = pos[:, :, None] * inv
        c, sn = jnp.cos(ang), jnp.sin(ang)
        rest = period - rot
        cp = jnp.concatenate([c, c, jnp.ones((b, s, rest), F32)], axis=-1)
        sp = jnp.concatenate([-sn, sn, jnp.zeros((b, s, rest), F32)], axis=-1)
        reps = LANES // period
        return jnp.tile(cp, (1, 1, reps)), jnp.tile(sp, (1, 1, reps))

    parts = one(MLA_ROPE, LANES) + one(DSA_ROT, LANES) + one(IDX_ROT, IDX_DIM)
    return jnp.concatenate(parts, axis=-1).reshape(b * s, N_TABS * LANES)


def _pack_w_in(w_in):
    splits = [int(c) for c in np.cumsum(IN_SIZES)[:-1]]
    (cq, ckv, kr, ag, bq, bk, bv, qi, ki, wi, bg, u, cg, dq, dg) = jnp.split(w_in, splits, axis=-1)

    def pad(a, width):
        return jnp.pad(a, [(0, 0)] * (a.ndim - 1) + [(0, width - a.shape[-1])])

    packed = jnp.concatenate([cq, ag, bq, qi, bg, u, cg, dq, dg, ckv, bk, bv,
                              pad(kr, LANES), pad(jnp.concatenate([ki, wi], axis=-1), LANES)], axis=-1)
    assert packed.shape[-1] == N_PACKED
    return packed.astype(BF16)


def _block_diag(w):
    l = w.shape[0]
    eye = jnp.eye(LRU_BLOCKS, dtype=w.dtype)
    bd = jnp.einsum('lnde,nm->lndme', w, eye)
    return bd.reshape(l, LRU_WIDTH, LRU_WIDTH).astype(BF16)


def _split_hi_lo(a):
    hi = a.astype(BF16)
    lo = (a - hi.astype(F32)).astype(BF16)
    return hi, lo


def _indexer_operands(qi_r, ki_r, batch, seq):
    q = qi_r.reshape(batch, seq, IDX_HEADS, IDX_DIM)
    qh, ql = _split_hi_lo(q)
    q4 = jnp.concatenate([qh, ql, qh, ql], axis=-1)
    nq = seq // DSA_QBLK
    q4 = q4.reshape(batch, nq, DSA_QBLK, IDX_HEADS // 2, 2, 4 * IDX_DIM)
    q4 = q4.transpose(0, 1, 3, 4, 2, 5).reshape(batch, nq, IDX_HEADS // 2, 2 * DSA_QBLK, 4 * IDX_DIM)
    kh, kl = _split_hi_lo(ki_r[:, :IDX_DIM])
    k4 = jnp.concatenate([kh, kh, kl, kl], axis=-1)
    return q4, k4


def _layer(x, x16, mem16, tabs, lw, batch, seq):
    m = batch * seq
    h2d = _matmul([x16], lw["w_in"], F32, tm=min(1024, m), tn=256, name="in_proj")
    tile_a = min(512, seq)
    qa, ka, vta = _prep_mla(h2d, tabs, lw["g_cq"], lw["g_ckv"], lw["w_uq"], lw["w_uk"], lw["w_uvt"],
                            batch, seq, tile_a)
    ya = _mla_attn(qa, ka, vta, h2d, batch, seq, tile_a)
    qb, kb, vtb, qi_r, ki_r, wt = _prep_dsa(h2d, tabs, batch, seq)
    qi4, ki4 = _indexer_operands(qi_r, ki_r, batch, seq)
    yb = _dsa(qi4, ki4, wt, qb, kb, vtb, h2d, batch, seq)
    yc = _rglru(h2d.reshape(batch, seq, N_PACKED), lw["conv_w"], lw["conv_b"], lw["w_rg_a"], lw["b_rg_a"],
                lw["w_rg_x"], lw["b_rg_x"], lw["lam"], batch, seq, min(256, seq))
    yc = yc.reshape(m, GROUP_WIDTH)
    kvm = _matmul([mem16], lw["w_mem"], BF16, tm=mem16.shape[0], tn=256, name="mem_proj")
    yd = _mem_attn(h2d, kvm, batch, seq, min(512, seq))
    y = _matmul([ya, yb, yc, yd], lw["w_o"], F32, tm=min(1024, m), tn=256, name="out_proj")
    return _deepnorm(x, y, lw["ln_g"], lw["ln_b"], min(256, m))


def kernel(x, mem, positions, w_in, g_cq, g_ckv, w_uq, w_ukv, conv_w, conv_b, w_rg_a, b_rg_a, w_rg_x,
           b_rg_x, lru_lambda, w_mem_k, w_mem_v, w_o, ln_g, ln_b):
    batch, seq, d = x.shape
    depth = w_in.shape[0]
    tabs = _rope_tabs(positions)
    w_in_p = _pack_w_in(w_in)
    w_uq_p = jnp.pad(w_uq.reshape(depth, MLA_Q_LORA, MLA_HEADS, MLA_NOPE + MLA_ROPE),
                     [(0, 0), (0, 0), (0, 0), (0, 2 * LANES - MLA_NOPE - MLA_ROPE)])
    w_uq_p = w_uq_p.reshape(depth, MLA_Q_LORA, MLA_HEADS * 2 * LANES).astype(BF16)
    w_ukv4 = w_ukv.reshape(depth, MLA_KV_LORA, MLA_HEADS, MLA_NOPE + MLA_V)
    w_uk = w_ukv4[..., :MLA_NOPE].reshape(depth, MLA_KV_LORA, MLA_HEADS * MLA_NOPE).astype(BF16)
    w_uvt = w_ukv4[..., MLA_NOPE:].reshape(depth, MLA_KV_LORA, MLA_HEADS * MLA_V)
    w_uvt = w_uvt.transpose(0, 2, 1).astype(BF16)
    wa_bd = _block_diag(w_rg_a)
    wx_bd = _block_diag(w_rg_x)
    w_mem = jnp.concatenate([w_mem_k, w_mem_v], axis=-1).astype(BF16)
    w_o16 = w_o.astype(BF16)
    mem16 = mem.reshape(batch * mem.shape[1], d).astype(BF16)

    xf = x.reshape(batch * seq, d)
    x16 = xf.astype(BF16)
    for l in range(depth):
        lw = dict(w_in=w_in_p[l], g_cq=g_cq[l][None], g_ckv=g_ckv[l][None], w_uq=w_uq_p[l], w_uk=w_uk[l],
                  w_uvt=w_uvt[l], conv_w=conv_w[l], conv_b=conv_b[l][None], w_rg_a=wa_bd[l],
                  b_rg_a=b_rg_a[l][None], w_rg_x=wx_bd[l], b_rg_x=b_rg_x[l][None], lam=lru_lambda[l][None],
                  w_mem=w_mem[l], w_o=w_o16[l], ln_g=ln_g[l][None], ln_b=ln_b[l][None])
        xf, x16 = _layer(xf, x16, mem16, tabs, lw, batch, seq)
    return xf.reshape(batch, seq, d)
```

```python
import functools

import numpy as np
import jax
import jax.numpy as jnp
from jax import lax
from jax.experimental import pallas as pl
from jax.experimental.pallas import tpu as pltpu

F32 = jnp.float32
BF16 = jnp.bfloat16
I32 = jnp.int32

DEPTH = 4
D_MODEL = 4096
N_MEM = 256
GROUP_WIDTH = D_MODEL // 4
ROPE_THETA = 500000.0
MLA_HEADS = 8
MLA_NOPE = 128
MLA_ROPE = 64
MLA_V = 128
MLA_Q_LORA = GROUP_WIDTH
MLA_KV_LORA = GROUP_WIDTH // 2
DSA_HEADS = 8
DSA_KV_HEADS = 2
DSA_GROUP = DSA_HEADS // DSA_KV_HEADS
DSA_HEAD_DIM = 128
DSA_ROT = DSA_HEAD_DIM // 4
IDX_HEADS = 16
IDX_DIM = 64
IDX_ROT = IDX_DIM // 4
TOPK_MAX = 256
LRU_WIDTH = GROUP_WIDTH
LRU_BLOCKS = 16
LRU_BLOCK_DIM = LRU_WIDTH // LRU_BLOCKS
CONV_WIDTH = 4
LRU_C = 8.0
MEM_HEADS = 4
MEM_HEAD_DIM = GROUP_WIDTH // MEM_HEADS
DEEPNORM_ALPHA = (2 * DEPTH) ** 0.25
LN_EPS = 1e-5
RMS_EPS = 1e-6

IN_SIZES = (
    MLA_Q_LORA, MLA_KV_LORA, MLA_ROPE, GROUP_WIDTH,
    GROUP_WIDTH, DSA_KV_HEADS * DSA_HEAD_DIM, DSA_KV_HEADS * DSA_HEAD_DIM,
    IDX_HEADS * IDX_DIM, IDX_DIM, IDX_HEADS, GROUP_WIDTH,
    LRU_WIDTH, GROUP_WIDTH,
    GROUP_WIDTH, GROUP_WIDTH,
)

LANES = 128
V7X_VMEM_BUDGET = 56 * 1024 * 1024

COL_CQ = 0
COL_AG = 1024
COL_BQ = 2048
COL_QI = 3072
COL_BG = 4096
COL_U = 5120
COL_CG = 6144
COL_DQ = 7168
COL_DG = 8192
COL_CKV = 9216
COL_BK = 9728
COL_BV = 9984
COL_KR = 10240
COL_KI = 10368
N_PACKED = 10496

TAB_MLA_C, TAB_MLA_S, TAB_DSA_C, TAB_DSA_S, TAB_IDX_C, TAB_IDX_S = range(6)
N_TABS = 6

NEG_BIG = -1e30
INT_MIN = -2 ** 31
KEY_CHUNK = 256
DSA_QBLK = 128


def _vmem_limit(nbytes):
    return int(min(V7X_VMEM_BUDGET, max(nbytes, 16 * 1024 * 1024)))


def _params(nbytes, ndims):
    return pltpu.CompilerParams(dimension_semantics=("arbitrary",) * ndims,
                                vmem_limit_bytes=_vmem_limit(nbytes))


def _mm_kernel(*refs, n_lhs, kg):
    w_ref = refs[n_lhs]
    o_ref = refs[n_lhs + 1]
    acc = None
    for g in range(n_lhs):
        part = jnp.dot(refs[g][...], w_ref[g * kg:(g + 1) * kg, :].astype(BF16), preferred_element_type=F32)
        acc = part if acc is None else acc + part
    o_ref[...] = acc.astype(o_ref.dtype)


def _matmul(lhs_list, w, out_dtype, tm, tn, name):
    m, kg = lhs_list[0].shape
    k, n = w.shape
    n_lhs = len(lhs_list)
    assert k == kg * n_lhs and m % tm == 0 and n % tn == 0
    est = 2 * (tm * k * 2 + k * tn * w.dtype.itemsize + tm * tn * jnp.dtype(out_dtype).itemsize) + tm * tn * 8
    return pl.pallas_call(
        functools.partial(_mm_kernel, n_lhs=n_lhs, kg=kg),
        grid=(m // tm, n // tn),
        in_specs=[pl.BlockSpec((tm, kg), lambda i, j: (i, 0)) for _ in range(n_lhs)]
        + [pl.BlockSpec((k, tn), lambda i, j: (0, j))],
        out_specs=pl.BlockSpec((tm, tn), lambda i, j: (i, j)),
        out_shape=jax.ShapeDtypeStruct((m, n), out_dtype),
        compiler_params=_params(est + (4 << 20), 2),
        name=name,
    )(*lhs_list, w)


def _rope(v, c_tab, s_tab, half, period):
    width = v.shape[-1]
    lane = lax.broadcasted_iota(I32, v.shape, 1) & (period - 1)
    swapped = jnp.where(lane < half, pltpu.roll(v, width - half, 1), pltpu.roll(v, half, 1))
    return v * c_tab + swapped * s_tab


def _tile_lanes(t, reps):
    return t if reps == 1 else jnp.concatenate([t] * reps, axis=1)


def _rms(x, g):
    return x * lax.rsqrt(jnp.mean(x * x, axis=-1, keepdims=True) + RMS_EPS) * g


def _silu(g):
    return g * jax.nn.sigmoid(g)


def _prep_mla_kernel(cq_ref, ckv_ref, kr_ref, tab_ref, gcq_ref, gckv_ref, wuq_ref, wk_ref, wvt_ref,
                     q_out, k_out, vt_out):
    scale = (MLA_NOPE + MLA_ROPE) ** -0.5 * LOG2E
    nq = _rms(cq_ref[...], gcq_ref[...]).astype(BF16)
    nkv = _rms(ckv_ref[...], gckv_ref[...]).astype(BF16)
    c_tab = tab_ref[:, TAB_MLA_C * LANES:(TAB_MLA_C + 1) * LANES]
    s_tab = tab_ref[:, TAB_MLA_S * LANES:(TAB_MLA_S + 1) * LANES]
    qf = jnp.dot(nq, wuq_ref[...], preferred_element_type=F32) * scale
    kn = jnp.dot(nkv, wk_ref[...], preferred_element_type=F32)
    kr = _rope(kr_ref[...], c_tab, s_tab, MLA_ROPE // 2, LANES).astype(BF16)
    for h in range(MLA_HEADS):
        base = h * 2 * LANES
        q_out[:, base:base + LANES] = qf[:, base:base + LANES].astype(BF16)
        q_out[:, base + LANES:base + 2 * LANES] = _rope(
            qf[:, base + LANES:base + 2 * LANES], c_tab, s_tab, MLA_ROPE // 2, LANES).astype(BF16)
        k_out[:, base:base + LANES] = kn[:, h * LANES:(h + 1) * LANES].astype(BF16)
        k_out[:, base + LANES:base + 2 * LANES] = kr
    vt = lax.dot_general(wvt_ref[...], nkv, (((1,), (1,)), ((), ())), preferred_element_type=F32)
    half = vt.shape[1] // 2
    vt_out[0] = vt[:, :half].astype(BF16)
    vt_out[1] = vt[:, half:].astype(BF16)


def _prep_mla(h2d, tabs, g_cq, g_ckv, wuq_p, wk, wvt, batch, seq, tm):
    m = batch * seq
    nt = seq // tm
    hd = MLA_HEADS * 2 * LANES
    return pl.pallas_call(
        _prep_mla_kernel,
        grid=(batch, nt),
        in_specs=[
            pl.BlockSpec((tm, MLA_Q_LORA), lambda b, i: (b * nt + i, COL_CQ // MLA_Q_LORA)),
            pl.BlockSpec((tm, MLA_KV_LORA), lambda b, i: (b * nt + i, COL_CKV // MLA_KV_LORA)),
            pl.BlockSpec((tm, LANES), lambda b, i: (b * nt + i, COL_KR // LANES)),
            pl.BlockSpec((tm, N_TABS * LANES), lambda b, i: (b * nt + i, 0)),
            pl.BlockSpec((1, MLA_Q_LORA), lambda b, i: (0, 0)),
            pl.BlockSpec((1, MLA_KV_LORA), lambda b, i: (0, 0)),
            pl.BlockSpec((MLA_Q_LORA, hd), lambda b, i: (0, 0)),
            pl.BlockSpec((MLA_KV_LORA, MLA_HEADS * MLA_NOPE), lambda b, i: (0, 0)),
            pl.BlockSpec((MLA_HEADS * MLA_V, MLA_KV_LORA), lambda b, i: (0, 0)),
        ],
        out_specs=[
            pl.BlockSpec((tm, hd), lambda b, i: (b * nt + i, 0)),
            pl.BlockSpec((tm, hd), lambda b, i: (b * nt + i, 0)),
            pl.BlockSpec((None, 2, MLA_HEADS * MLA_V, tm // 2), lambda b, i: (b, i, 0, 0)),
        ],
        out_shape=[
            jax.ShapeDtypeStruct((m, hd), BF16),
            jax.ShapeDtypeStruct((m, hd), BF16),
            jax.ShapeDtypeStruct((batch, 2 * nt, MLA_HEADS * MLA_V, tm // 2), BF16),
        ],
        compiler_params=_params(40 << 20, 2),
        name="prep_mla",
    )(h2d, h2d, h2d, tabs, g_cq, g_ckv, wuq_p, wk, wvt)


def _softmax_step(s, col_max, vt_chunk, m, l, acc):
    m_new = jnp.maximum(m, col_max)
    alpha = jnp.exp2(m - m_new)
    p = jnp.exp2(s - m_new)
    l_new = alpha * l + jnp.sum(p, axis=0, keepdims=True)
    acc_new = alpha * acc + jnp.dot(vt_chunk, p.astype(BF16), preferred_element_type=F32)
    return m_new, l_new, acc_new


LOG2E = 1.4426950408889634


_NT = (((1,), (1,)), ((), ()))


def _mla_attn_kernel(q_ref, k_ref, vt_ref, gate_ref, o_ref, s0_ref, s1_ref, *, tile, ch):
    i = pl.program_id(2)
    q = q_ref[...]
    slots = (s0_ref, s1_ref)

    def produce(c, slot, diag_offset=None):
        kc = k_ref[pl.ds(pl.multiple_of(c * ch, ch), ch), :]
        s = lax.dot_general(kc, q, _NT, preferred_element_type=F32)
        if diag_offset is not None:
            kidx = lax.broadcasted_iota(I32, (ch, tile), 0) + diag_offset
            qidx = lax.broadcasted_iota(I32, (ch, tile), 1)
            s = jnp.where(kidx <= qidx, s, NEG_BIG)
        slots[slot][...] = s
        return jnp.max(s, axis=0, keepdims=True)

    def consume(c, slot, col_max, carry):
        return _softmax_step(slots[slot][...], col_max, vt_ref[c], *carry)

    init = (jnp.full((1, tile), NEG_BIG, F32), jnp.zeros((1, tile), F32),
            jnp.zeros((MLA_V, tile), F32))
    diag = 2 * i
    col_a = produce(diag, 0, 0)
    col_b = produce(diag + 1, 1, ch)
    carry = consume(diag, 0, col_a, init)

    def pair(j, state):
        cr, col1, c1 = state
        col0 = produce(2 * j, 0)
        cr = consume(c1, 1, col1, cr)
        col1 = produce(2 * j + 1, 1)
        cr = consume(2 * j, 0, col0, cr)
        return cr, col1, 2 * j + 1

    state = lax.fori_loop(0, i // 2, lambda jj, st: pair(2 * jj + 1, pair(2 * jj, st)),
                          (carry, col_b, diag + 1))
    carry, col1, c1 = lax.fori_loop(0, i & 1, lambda _, st: pair(i - 1, st), state)
    _, l, acc = consume(c1, 1, col1, carry)
    o = (acc / l).T
    o_ref[...] = (o * _silu(gate_ref[...])).astype(o_ref.dtype)


def _mla_attn(q, k, vt, h2d, batch, seq, tile):
    m = batch * seq
    nt = seq // tile
    ch = tile // 2
    return pl.pallas_call(
        functools.partial(_mla_attn_kernel, tile=tile, ch=ch),
        grid=(batch, MLA_HEADS, nt),
        in_specs=[
            pl.BlockSpec((tile, 2 * LANES), lambda b, h, i: (b * nt + i, h)),
            pl.BlockSpec((seq, 2 * LANES), lambda b, h, i: (b, h)),
            pl.BlockSpec((None, 2 * nt, MLA_V, ch), lambda b, h, i: (b, 0, h, 0)),
            pl.BlockSpec((tile, LANES), lambda b, h, i: (b * nt + i, COL_AG // LANES + h)),
        ],
        out_specs=pl.BlockSpec((tile, LANES), lambda b, h, i: (b * nt + i, h)),
        out_shape=jax.ShapeDtypeStruct((m, GROUP_WIDTH), BF16),
        scratch_shapes=[pltpu.VMEM((ch, tile), F32), pltpu.VMEM((ch, tile), F32)],
        compiler_params=_params(32 << 20, 3),
        name="mla_attn",
    )(q, k, vt, h2d)


def _hi_lo(v):
    hi = v.astype(BF16).astype(F32)
    return hi, v - hi


def _prep_dsa_kernel(q_ref, k_ref, v_ref, qi_ref, ki_ref, tab_ref,
                     q_out, k_out, vt_out, qi4_out, ki4_out, wt_out):
    def tab(t):
        return tab_ref[:, t * LANES:(t + 1) * LANES]

    scale = DSA_HEAD_DIM ** -0.5 * LOG2E
    q = q_ref[...]
    q_out[...] = (_rope(q, _tile_lanes(tab(TAB_DSA_C), DSA_HEADS), _tile_lanes(tab(TAB_DSA_S), DSA_HEADS),
                        DSA_ROT // 2, LANES) * scale).astype(BF16)
    k = k_ref[...]
    k_out[...] = _rope(k, _tile_lanes(tab(TAB_DSA_C), DSA_KV_HEADS), _tile_lanes(tab(TAB_DSA_S), DSA_KV_HEADS),
                       DSA_ROT // 2, LANES).astype(BF16)
    vt_out[...] = v_ref[...].T.astype(BF16)
    reps = IDX_HEADS * IDX_DIM // LANES
    qi = _rope(qi_ref[...], _tile_lanes(tab(TAB_IDX_C), reps), _tile_lanes(tab(TAB_IDX_S), reps),
               IDX_ROT // 2, IDX_DIM)
    qb = DSA_QBLK
    first_half = lax.broadcasted_iota(I32, (qb, LANES), 1) < IDX_DIM
    for blk in range(q.shape[0] // qb):
        for pair in range(IDX_HEADS // 2):
            hi, lo = _hi_lo(qi[blk * qb:(blk + 1) * qb, pair * LANES:(pair + 1) * LANES])
            even = jnp.where(first_half, hi, pltpu.roll(lo, IDX_DIM, 1)).astype(BF16)
            odd = jnp.where(first_half, pltpu.roll(hi, IDX_DIM, 1), lo).astype(BF16)
            for half in range(2):
                qi4_out[blk, pair, 0:qb, half * LANES:(half + 1) * LANES] = even
                qi4_out[blk, pair, qb:2 * qb, half * LANES:(half + 1) * LANES] = odd
    small = ki_ref[...]
    lane = lax.broadcasted_iota(I32, small.shape, 1)
    c_ki = jnp.where(lane < IDX_DIM, tab(TAB_IDX_C), 1.0)
    s_ki = jnp.where(lane < IDX_DIM, tab(TAB_IDX_S), 0.0)
    hi, lo = _hi_lo(_rope(small, c_ki, s_ki, IDX_ROT // 2, IDX_DIM))
    ki4_out[:, 0:LANES] = jnp.where(lane < IDX_DIM, hi, pltpu.roll(hi, IDX_DIM, 1)).astype(BF16)
    ki4_out[:, LANES:2 * LANES] = jnp.where(lane < IDX_DIM, lo, pltpu.roll(lo, IDX_DIM, 1)).astype(BF16)
    wt = small.T[IDX_DIM:IDX_DIM + IDX_HEADS, :]
    wt_out[...] = wt * (IDX_HEADS ** -0.5 * IDX_DIM ** -0.5)


def _prep_dsa(h2d, tabs, batch, seq):
    tm = KEY_CHUNK
    m = batch * seq
    nt = seq // tm
    qpt = tm // DSA_QBLK
    kvw = DSA_KV_HEADS * DSA_HEAD_DIM

    def row(b, i):
        return b * nt + i

    return pl.pallas_call(
        _prep_dsa_kernel,
        grid=(batch, nt),
        in_specs=[
            pl.BlockSpec((tm, GROUP_WIDTH), lambda b, i: (row(b, i), COL_BQ // GROUP_WIDTH)),
            pl.BlockSpec((tm, kvw), lambda b, i: (row(b, i), COL_BK // kvw)),
            pl.BlockSpec((tm, kvw), lambda b, i: (row(b, i), COL_BV // kvw)),
            pl.BlockSpec((tm, GROUP_WIDTH), lambda b, i: (row(b, i), COL_QI // GROUP_WIDTH)),
            pl.BlockSpec((tm, LANES), lambda b, i: (row(b, i), COL_KI // LANES)),
            pl.BlockSpec((tm, N_TABS * LANES), lambda b, i: (row(b, i), 0)),
        ],
        out_specs=[
            pl.BlockSpec((tm, GROUP_WIDTH), lambda b, i: (row(b, i), 0)),
            pl.BlockSpec((tm, kvw), lambda b, i: (row(b, i), 0)),
            pl.BlockSpec((None, None, kvw, tm), lambda b, i: (b, i, 0, 0)),
            pl.BlockSpec((None, qpt, IDX_HEADS // 2, 2 * DSA_QBLK, 4 * IDX_DIM), lambda b, i: (b, i, 0, 0, 0)),
            pl.BlockSpec((tm, 4 * IDX_DIM), lambda b, i: (row(b, i), 0)),
            pl.BlockSpec((None, IDX_HEADS, tm), lambda b, i: (b, 0, i)),
        ],
        out_shape=[
            jax.ShapeDtypeStruct((m, GROUP_WIDTH), BF16),
            jax.ShapeDtypeStruct((m, kvw), BF16),
            jax.ShapeDtypeStruct((batch, nt, kvw, tm), BF16),
            jax.ShapeDtypeStruct((batch, seq // DSA_QBLK, IDX_HEADS // 2, 2 * DSA_QBLK, 4 * IDX_DIM), BF16),
            jax.ShapeDtypeStruct((m, 4 * IDX_DIM), BF16),
            jax.ShapeDtypeStruct((batch, IDX_HEADS, seq), F32),
        ],
        compiler_params=_params(24 << 20, 2),
        name="prep_dsa",
    )(h2d, h2d, h2d, h2d, h2d, tabs)


def _sortable_key(score):
    bits = lax.bitcast_convert_type(score, I32)
    key = jnp.where(bits < 0, bits ^ jnp.int32(0x7FFFFFFF), bits)
    return jnp.where(score == 0.0, 0, key)


def _dsa_kernel(qi4_ref, ki4_ref, wt_ref, q_ref, k_ref, vt_ref, gate_ref, o_ref, key_ref, s0_ref, s1_ref,
                *, topk, seq_bits):
    i = pl.program_id(1)
    n_vis = i // 2
    n_chunks = n_vis + 1
    wt = wt_ref[...]
    ch = KEY_CHUNK
    qb = DSA_QBLK

    def rows(c):
        return pl.ds(pl.multiple_of(c * ch, ch), ch)

    def score_chunk(c):
        kc = ki4_ref[rows(c), :]
        tot = None
        for p in range(IDX_HEADS // 2):
            s = lax.dot_general(kc, qi4_ref[p], _NT, preferred_element_type=F32)
            r = jnp.maximum(s, 0.0)
            t = r[:, :qb] * wt[2 * p:2 * p + 1, :] + r[:, qb:] * wt[2 * p + 1:2 * p + 2, :]
            tot = t if tot is None else tot + t
        return tot

    def phase1(c, carry):
        key_ref[rows(c), :] = _sortable_key(score_chunk(c))
        return carry

    lax.fori_loop(0, n_vis, phase1, 0)
    kidx = n_vis * ch + lax.broadcasted_iota(I32, (ch, qb), 0)
    qidx = i * qb + lax.broadcasted_iota(I32, (ch, qb), 1)
    key_ref[rows(n_vis), :] = jnp.where(kidx <= qidx, _sortable_key(score_chunk(n_vis)), INT_MIN)

    def count(pred):
        def body(c, acc):
            idx = c * ch + lax.broadcasted_iota(I32, (ch, qb), 0)
            hit = pred(key_ref[rows(c), :], idx).astype(I32)
            return acc + jnp.sum(hit.reshape(ch // 8, 8, qb), axis=0)
        acc = lax.fori_loop(0, n_chunks, body, jnp.zeros((8, qb), I32))
        return jnp.sum(acc, axis=0, keepdims=True)

    def count_ge(t):
        return count(lambda keys, idx: keys >= t)

    thr = jnp.where(count_ge(jnp.zeros((1, qb), I32)) >= topk, 0, INT_MIN).astype(I32)

    def bit_step(b, t):
        cand = t + jnp.left_shift(jnp.int32(1), 30 - b)
        return jnp.where(count_ge(cand) >= topk, cand, t)

    thr = lax.fori_loop(0, 31, bit_step, thr)
    c_ge = count_ge(thr)
    need = topk - count_ge(thr + 1)
    tied = jnp.logical_and(c_ge > topk, thr > INT_MIN)

    def tie_search():
        def step(b, lo):
            cand = lo + jnp.left_shift(jnp.int32(1), seq_bits - 1 - b)
            below = count(lambda keys, idx: jnp.logical_and(keys == thr, idx < cand))
            return jnp.where(below < need, cand, lo)
        return lax.fori_loop(0, seq_bits, step, jnp.zeros((1, qb), I32))

    last_eq = lax.cond(jnp.max(tied.astype(I32)) > 0, tie_search,
                       lambda: jnp.full((1, qb), 2 ** seq_bits, I32))
    thr_sel = jnp.maximum(thr, INT_MIN + 1)

    qs = [jnp.concatenate([q_ref[:, (n * DSA_GROUP + g) * LANES:(n * DSA_GROUP + g + 1) * LANES]
                           for g in range(DSA_GROUP)], axis=0) for n in range(DSA_KV_HEADS)]
    nq = DSA_GROUP * qb

    slots = (s0_ref, s1_ref)

    def produce(c, slot):
        keys = key_ref[rows(c), :]
        idx = c * ch + lax.broadcasted_iota(I32, (ch, qb), 0)
        drop = jnp.logical_and(keys == thr, idx > last_eq)
        sel = jnp.logical_and(keys >= thr_sel, jnp.logical_not(drop))
        neg = jnp.where(sel, 0.0, NEG_BIG)
        neg = jnp.concatenate([neg] * DSA_GROUP, axis=1)
        col_max = []
        for n in range(DSA_KV_HEADS):
            kc = k_ref[rows(c), n * LANES:(n + 1) * LANES]
            s = lax.dot_general(kc, qs[n], _NT, preferred_element_type=F32) + neg
            slots[slot][n] = s
            col_max.append(jnp.max(s, axis=0, keepdims=True))
        return tuple(col_max)

    def consume(c, slot, col_max, carry):
        return tuple(_softmax_step(slots[slot][n], col_max[n],
                                   vt_ref[c, n * LANES:(n + 1) * LANES, :], *carry[n])
                     for n in range(DSA_KV_HEADS))

    def phase3(j, state):
        carry, col0 = state
        col1 = produce(2 * j + 1, 1)
        carry = consume(2 * j, 0, col0, carry)
        col0 = produce(2 * j + 2, 0)
        carry = consume(2 * j + 1, 1, col1, carry)
        return carry, col0

    init = tuple((jnp.full((1, nq), NEG_BIG, F32), jnp.zeros((1, nq), F32),
                  jnp.zeros((DSA_HEAD_DIM, nq), F32)) for _ in range(DSA_KV_HEADS))
    pairs = (n_chunks - 1) // 2
    carry, col0 = lax.fori_loop(0, pairs, phase3, (init, produce(0, 0)))
    last = 2 * pairs

    def tail_two(cr):
        col1 = produce(last + 1, 1)
        cr = consume(last, 0, col0, cr)
        return consume(last + 1, 1, col1, cr)

    res = lax.cond(n_chunks - 1 > last, tail_two, lambda cr: consume(last, 0, col0, cr), carry)
    for n in range(DSA_KV_HEADS):
        _, l, acc = res[n]
        o = (acc / l).T
        for g in range(DSA_GROUP):
            col = (n * DSA_GROUP + g) * LANES
            o_ref[:, col:col + LANES] = (o[g * qb:(g + 1) * qb, :]
                                         * _silu(gate_ref[:, col:col + LANES])).astype(o_ref.dtype)


def _dsa(qi4, ki4, wt, q, k, vt, h2d, batch, seq):
    m = batch * seq
    nq = seq // DSA_QBLK
    nck = seq // KEY_CHUNK
    kvw = DSA_KV_HEADS * DSA_HEAD_DIM
    topk = min(TOPK_MAX, seq // 4)
    seq_bits = int(np.log2(seq))
    assert 2 ** seq_bits == seq
    return pl.pallas_call(
        functools.partial(_dsa_kernel, topk=topk, seq_bits=seq_bits),
        grid=(batch, nq),
        in_specs=[
            pl.BlockSpec((None, None, IDX_HEADS // 2, 2 * DSA_QBLK, 4 * IDX_DIM),
                         lambda b, i: (b, i, 0, 0, 0)),
            pl.BlockSpec((seq, 4 * IDX_DIM), lambda b, i: (b, 0)),
            pl.BlockSpec((None, IDX_HEADS, DSA_QBLK), lambda b, i: (b, 0, i)),
            pl.BlockSpec((DSA_QBLK, GROUP_WIDTH), lambda b, i: (b * nq + i, 0)),
            pl.BlockSpec((seq, kvw), lambda b, i: (b, 0)),
            pl.BlockSpec((None, nck, kvw, KEY_CHUNK), lambda b, i: (b, 0, 0, 0)),
            pl.BlockSpec((DSA_QBLK, GROUP_WIDTH), lambda b, i: (b * nq + i, COL_BG // GROUP_WIDTH)),
        ],
        out_specs=pl.BlockSpec((DSA_QBLK, GROUP_WIDTH), lambda b, i: (b * nq + i, 0)),
        out_shape=jax.ShapeDtypeStruct((m, GROUP_WIDTH), BF16),
        scratch_shapes=[pltpu.VMEM((seq, DSA_QBLK), I32),
                        pltpu.VMEM((DSA_KV_HEADS, KEY_CHUNK, DSA_GROUP * DSA_QBLK), F32),
                        pltpu.VMEM((DSA_KV_HEADS, KEY_CHUNK, DSA_GROUP * DSA_QBLK), F32)],
        compiler_params=_params(32 << 20, 2),
        name="dsa",
    )(qi4, ki4, wt, q, k, vt, h2d)


def _rglru_kernel(u_ref, gate_ref, cw_ref, cb_ref, wa_ref, ba_ref, wx_ref, bx_ref, lam_ref,
                  o_ref, halo_ref, h_ref, a_scr, x_scr, y_scr, *, batch, tt):
    t = pl.program_id(0)

    @pl.when(t == 0)
    def _():
        halo_ref[...] = jnp.zeros_like(halo_ref)
        h_ref[...] = jnp.zeros_like(h_ref)

    lam = lam_ref[...]
    z = -lam
    softplus = jnp.maximum(z, 0.0) + jnp.log1p(jnp.exp(-jnp.abs(z)))
    row = lax.broadcasted_iota(I32, (tt, LRU_WIDTH), 0)
    for b in range(batch):
        u = u_ref[b]
        prev = halo_ref[b]
        p1, p2, p3 = prev[7:8, :], prev[6:7, :], prev[5:6, :]
        s1 = jnp.where(row == 0, p1, pltpu.roll(u, 1, 0))
        s2 = jnp.where(row == 0, p2, jnp.where(row == 1, p1, pltpu.roll(u, 2, 0)))
        s3 = jnp.where(row == 0, p3, jnp.where(row == 1, p2, jnp.where(row == 2, p1, pltpu.roll(u, 3, 0))))
        conv = (cw_ref[3:4, :] * u + cw_ref[2:3, :] * s1 + cw_ref[1:2, :] * s2 + cw_ref[0:1, :] * s3
                + cb_ref[...])
        halo_ref[b] = u[tt - 8:, :]
        cb16 = conv.astype(BF16)
        r = jax.nn.sigmoid(jnp.dot(cb16, wa_ref[...], preferred_element_type=F32) + ba_ref[...])
        ig = jax.nn.sigmoid(jnp.dot(cb16, wx_ref[...], preferred_element_type=F32) + bx_ref[...])
        log_a = -LRU_C * r * softplus
        a_scr[b] = jnp.exp(log_a)
        th = jnp.tanh(log_a)
        x_scr[b] = jnp.sqrt(-2.0 * th / (1.0 - th)) * (ig * conv)

    def step(s, h):
        h = a_scr[:, pl.ds(s, 1), :] * h + x_scr[:, pl.ds(s, 1), :]
        y_scr[:, pl.ds(s, 1), :] = h
        return h

    h_ref[...] = lax.fori_loop(0, tt, step, h_ref[...], unroll=8)
    o_ref[...] = (y_scr[...] * _silu(gate_ref[...])).astype(o_ref.dtype)


def _rglru(h3d, conv_w, conv_b, wa_bd, b_a, wx_bd, b_x, lam, batch, seq, tt):
    nt = seq // tt
    w = LRU_WIDTH

    def vec(r):
        return pl.BlockSpec((r, w), lambda t: (0, 0))

    return pl.pallas_call(
        functools.partial(_rglru_kernel, batch=batch, tt=tt),
        grid=(nt,),
        in_specs=[
            pl.BlockSpec((batch, tt, w), lambda t: (0, t, COL_U // w)),
            pl.BlockSpec((batch, tt, w), lambda t: (0, t, COL_CG // w)),
            vec(CONV_WIDTH), vec(1),
            pl.BlockSpec((w, w), lambda t: (0, 0)), vec(1),
            pl.BlockSpec((w, w), lambda t: (0, 0)), vec(1),
            vec(1),
        ],
        out_specs=pl.BlockSpec((batch, tt, w), lambda t: (0, t, 0)),
        out_shape=jax.ShapeDtypeStruct((batch, seq, w), BF16),
        scratch_shapes=[pltpu.VMEM((batch, 8, w), F32), pltpu.VMEM((batch, 1, w), F32),
                        pltpu.VMEM((batch, tt, w), F32), pltpu.VMEM((batch, tt, w), F32),
                        pltpu.VMEM((batch, tt, w), F32)],
        compiler_params=_params(32 << 20, 1),
        name="rglru",
    )(h3d, h3d, conv_w, conv_b, wa_bd, b_a, wx_bd, b_x, lam)


def _mem_attn_kernel(q_ref, gate_ref, km_ref, vm_ref, o_ref):
    scale = MEM_HEAD_DIM ** -0.5
    for h in range(MEM_HEADS):
        cols = slice(h * MEM_HEAD_DIM, (h + 1) * MEM_HEAD_DIM)
        qh = (q_ref[:, cols] * scale).astype(BF16)
        s = lax.dot_general(qh, km_ref[:, cols], _NT, preferred_element_type=F32)
        p = jnp.exp(s - jnp.max(s, axis=-1, keepdims=True))
        l = jnp.sum(p, axis=-1, keepdims=True)
        o = jnp.dot(p.astype(BF16), vm_ref[:, cols], preferred_element_type=F32) / l
        o_ref[:, cols] = (o * _silu(gate_ref[:, cols])).astype(o_ref.dtype)


def _mem_attn(h2d, km, vm, batch, seq, tm):
    m = batch * seq
    nt = seq // tm
    w = GROUP_WIDTH
    return pl.pallas_call(
        _mem_attn_kernel,
        grid=(batch, nt),
        in_specs=[
            pl.BlockSpec((tm, w), lambda b, i: (b * nt + i, COL_DQ // w)),
            pl.BlockSpec((tm, w), lambda b, i: (b * nt + i, COL_DG // w)),
            pl.BlockSpec((N_MEM, w), lambda b, i: (b, 0)),
            pl.BlockSpec((N_MEM, w), lambda b, i: (b, 0)),
        ],
        out_specs=pl.BlockSpec((tm, w), lambda b, i: (b * nt + i, 0)),
        out_shape=jax.ShapeDtypeStruct((m, w), BF16),
        compiler_params=_params(24 << 20, 2),
        name="mem_attn",
    )(h2d, h2d, km, vm)


def _deepnorm_kernel(x_ref, y_ref, g_ref, b_ref, o_ref, o16_ref):
    z = DEEPNORM_ALPHA * x_ref[...] + y_ref[...]
    mu = jnp.mean(z, axis=-1, keepdims=True)
    d = z - mu
    var = jnp.mean(d * d, axis=-1, keepdims=True)
    out = d * lax.rsqrt(var + LN_EPS) * g_ref[...] + b_ref[...]
    o_ref[...] = out
    o16_ref[...] = out.astype(BF16)


def _deepnorm(x, y, g, b, tm):
    m, d = x.shape
    return pl.pallas_call(
        _deepnorm_kernel,
        grid=(m // tm,),
        in_specs=[pl.BlockSpec((tm, d), lambda i: (i, 0)), pl.BlockSpec((tm, d), lambda i: (i, 0)),
                  pl.BlockSpec((1, d), lambda i: (0, 0)), pl.BlockSpec((1, d), lambda i: (0, 0))],
        out_specs=[pl.BlockSpec((tm, d), lambda i: (i, 0)), pl.BlockSpec((tm, d), lambda i: (i, 0))],
        out_shape=[jax.ShapeDtypeStruct((m, d), F32), jax.ShapeDtypeStruct((m, d), BF16)],
        compiler_params=_params(40 << 20, 1),
        name="deepnorm",
    )(x, y, g, b)


def _rope_tabs(positions):
    pos = positions.astype(F32)
    b, s = pos.shape

    def one(rot, period):
        inv = ROPE_THETA ** (-jnp.arange(0, rot, 2, dtype=F32) / rot)
        ang = pos[:, :, None] * inv
        c, sn = jnp.cos(ang), jnp.sin(ang)
        rest = period - rot
        cp = jnp.concatenate([c, c, jnp.ones((b, s, rest), F32)], axis=-1)
        sp = jnp.concatenate([-sn, sn, jnp.zeros((b, s, rest), F32)], axis=-1)
        reps = LANES // period
        return jnp.tile(cp, (1, 1, reps)), jnp.tile(sp, (1, 1, reps))

    parts = one(MLA_ROPE, LANES) + one(DSA_ROT, LANES) + one(IDX_ROT, IDX_DIM)
    return jnp.concatenate(parts, axis=-1).reshape(b * s, N_TABS * LANES)


def _pack_w_in(w_in):
    splits = [int(c) for c in np.cumsum(IN_SIZES)[:-1]]
    (cq, ckv, kr, ag, bq, bk, bv, qi, ki, wi, bg, u, cg, dq, dg) = jnp.split(w_in, splits, axis=-1)

    def pad(a, width):
        return jnp.pad(a, [(0, 0)] * (a.ndim - 1) + [(0, width - a.shape[-1])])

    packed = jnp.concatenate([cq, ag, bq, qi, bg, u, cg, dq, dg, ckv, bk, bv,
                              pad(kr, LANES), pad(jnp.concatenate([ki, wi], axis=-1), LANES)], axis=-1)
    assert packed.shape[-1] == N_PACKED
    return packed.astype(BF16)


def _block_diag(w):
    l = w.shape[0]
    eye = jnp.eye(LRU_BLOCKS, dtype=w.dtype)
    bd = jnp.einsum('lnde,nm->lndme', w, eye)
    return bd.reshape(l, LRU_WIDTH, LRU_WIDTH).astype(BF16)


def _layer(x, x16, mem16, tabs, lw, batch, seq):
    m = batch * seq
    h2d = _matmul([x16], lw["w_in"], F32, tm=min(1024, m), tn=256, name="in_proj")
    tile_a = min(512, seq)
    qa, ka, vta = _prep_mla(h2d, tabs, lw["g_cq"], lw["g_ckv"], lw["w_uq"], lw["w_uk"], lw["w_uvt"],
                            batch, seq, tile_a)
    ya = _mla_attn(qa, ka, vta, h2d, batch, seq, tile_a)
    qb, kb, vtb, qi4, ki4, wt = _prep_dsa(h2d, tabs, batch, seq)
    yb = _dsa(qi4, ki4, wt, qb, kb, vtb, h2d, batch, seq)
    yc = _rglru(h2d.reshape(batch, seq, N_PACKED), lw["conv_w"], lw["conv_b"], lw["w_rg_a"], lw["b_rg_a"],
                lw["w_rg_x"], lw["b_rg_x"], lw["lam"], batch, seq, min(256, seq))
    yc = yc.reshape(m, GROUP_WIDTH)
    km = _matmul([mem16], lw["w_mem_k"], BF16, tm=mem16.shape[0], tn=256, name="mem_proj_k")
    vm = _matmul([mem16], lw["w_mem_v"], BF16, tm=mem16.shape[0], tn=256, name="mem_proj_v")
    yd = _mem_attn(h2d, km, vm, batch, seq, min(512, seq))
    y = _matmul([ya, yb, yc, yd], lw["w_o"], F32, tm=min(1024, m), tn=256, name="out_proj")
    return _deepnorm(x, y, lw["ln_g"], lw["ln_b"], min(256, m))


def kernel(x, mem, positions, w_in, g_cq, g_ckv, w_uq, w_ukv, conv_w, conv_b, w_rg_a, b_rg_a, w_rg_x,
           b_rg_x, lru_lambda, w_mem_k, w_mem_v, w_o, ln_g, ln_b):
    batch, seq, d = x.shape
    depth = w_in.shape[0]
    tabs = _rope_tabs(positions)
    w_in_p = _pack_w_in(w_in)
    w_uq_p = jnp.pad(w_uq.reshape(depth, MLA_Q_LORA, MLA_HEADS, MLA_NOPE + MLA_ROPE),
                     [(0, 0), (0, 0), (0, 0), (0, 2 * LANES - MLA_NOPE - MLA_ROPE)])
    w_uq_p = w_uq_p.reshape(depth, MLA_Q_LORA, MLA_HEADS * 2 * LANES).astype(BF16)
    w_ukv4 = w_ukv.reshape(depth, MLA_KV_LORA, MLA_HEADS, MLA_NOPE + MLA_V)
    w_uk = w_ukv4[..., :MLA_NOPE].reshape(depth, MLA_KV_LORA, MLA_HEADS * MLA_NOPE).astype(BF16)
    w_uvt = w_ukv4[..., MLA_NOPE:].reshape(depth, MLA_KV_LORA, MLA_HEADS * MLA_V)
    w_uvt = w_uvt.transpose(0, 2, 1).astype(BF16)
    wa_bd = _block_diag(w_rg_a)
    wx_bd = _block_diag(w_rg_x)
    mem16 = mem.reshape(batch * mem.shape[1], d).astype(BF16)

    xf = x.reshape(batch * seq, d)
    x16 = xf.astype(BF16)
    for l in range(depth):
        lw = dict(w_in=w_in_p[l], g_cq=g_cq[l][None], g_ckv=g_ckv[l][None], w_uq=w_uq_p[l], w_uk=w_uk[l],
                  w_uvt=w_uvt[l], conv_w=conv_w[l], conv_b=conv_b[l][None], w_rg_a=wa_bd[l],
                  b_rg_a=b_rg_a[l][None], w_rg_x=wx_bd[l], b_rg_x=b_rg_x[l][None], lam=lru_lambda[l][None],
                  w_mem_k=w_mem_k[l], w_mem_v=w_mem_v[l], w_o=w_o[l], ln_g=ln_g[l][None], ln_b=ln_b[l][None])
        xf, x16 = _layer(xf, x16, mem16, tabs, lw, batch, seq)
    return xf.reshape(batch, seq, d)
```

```python
import functools

import numpy as np
import jax
import jax.numpy as jnp
from jax import lax
from jax.experimental import pallas as pl
from jax.experimental.pallas import tpu as pltpu

F32 = jnp.float32
BF16 = jnp.bfloat16
I32 = jnp.int32
I16 = jnp.int16

DEPTH = 4
D_MODEL = 4096
N_MEM = 256
GROUP_WIDTH = D_MODEL // 4
ROPE_THETA = 500000.0
MLA_HEADS = 8
MLA_NOPE = 128
MLA_ROPE = 64
MLA_V = 128
MLA_Q_LORA = GROUP_WIDTH
MLA_KV_LORA = GROUP_WIDTH // 2
DSA_HEADS = 8
DSA_KV_HEADS = 2
DSA_GROUP = DSA_HEADS // DSA_KV_HEADS
DSA_HEAD_DIM = 128
DSA_ROT = DSA_HEAD_DIM // 4
IDX_HEADS = 16
IDX_DIM = 64
IDX_ROT = IDX_DIM // 4
TOPK_MAX = 256
LRU_WIDTH = GROUP_WIDTH
LRU_BLOCKS = 16
LRU_BLOCK_DIM = LRU_WIDTH // LRU_BLOCKS
CONV_WIDTH = 4
LRU_C = 8.0
MEM_HEADS = 4
MEM_HEAD_DIM = GROUP_WIDTH // MEM_HEADS
DEEPNORM_ALPHA = (2 * DEPTH) ** 0.25
LN_EPS = 1e-5
RMS_EPS = 1e-6

IN_SIZES = (
    MLA_Q_LORA, MLA_KV_LORA, MLA_ROPE, GROUP_WIDTH,
    GROUP_WIDTH, DSA_KV_HEADS * DSA_HEAD_DIM, DSA_KV_HEADS * DSA_HEAD_DIM,
    IDX_HEADS * IDX_DIM, IDX_DIM, IDX_HEADS, GROUP_WIDTH,
    LRU_WIDTH, GROUP_WIDTH,
    GROUP_WIDTH, GROUP_WIDTH,
)

LANES = 128
V7X_VMEM_BUDGET = 56 * 1024 * 1024

COL_CQ = 0
COL_AG = 1024
COL_BQ = 2048
COL_QI = 3072
COL_BG = 4096
COL_U = 5120
COL_CG = 6144
COL_DQ = 7168
COL_DG = 8192
COL_CKV = 9216
COL_BK = 9728
COL_BV = 9984
COL_KR = 10240
COL_KI = 10368
N_PACKED = 10496

TAB_MLA_C, TAB_MLA_S, TAB_DSA_C, TAB_DSA_S, TAB_IDX_C, TAB_IDX_S = range(6)
N_TABS = 6

NEG_BIG = -1e30
INT_MIN = -2 ** 31
I16_MIN = -2 ** 15
KEY_CHUNK = 256
DSA_QBLK = 128


def _vmem_limit(nbytes):
    return int(min(V7X_VMEM_BUDGET, max(nbytes, 16 * 1024 * 1024)))


def _params(nbytes, ndims):
    return pltpu.CompilerParams(dimension_semantics=("arbitrary",) * ndims,
                                vmem_limit_bytes=_vmem_limit(nbytes))


def _mm_kernel(*refs, n_lhs, kg):
    w_ref = refs[n_lhs]
    o_ref = refs[n_lhs + 1]
    acc = None
    for g in range(n_lhs):
        part = jnp.dot(refs[g][...], w_ref[g * kg:(g + 1) * kg, :].astype(BF16), preferred_element_type=F32)
        acc = part if acc is None else acc + part
    o_ref[...] = acc.astype(o_ref.dtype)


def _matmul(lhs_list, w, out_dtype, tm, tn, name):
    m, kg = lhs_list[0].shape
    k, n = w.shape
    n_lhs = len(lhs_list)
    assert k == kg * n_lhs and m % tm == 0 and n % tn == 0
    est = 2 * (tm * k * 2 + k * tn * w.dtype.itemsize + tm * tn * jnp.dtype(out_dtype).itemsize) + tm * tn * 8
    return pl.pallas_call(
        functools.partial(_mm_kernel, n_lhs=n_lhs, kg=kg),
        grid=(m // tm, n // tn),
        in_specs=[pl.BlockSpec((tm, kg), lambda i, j: (i, 0)) for _ in range(n_lhs)]
        + [pl.BlockSpec((k, tn), lambda i, j: (0, j))],
        out_specs=pl.BlockSpec((tm, tn), lambda i, j: (i, j)),
        out_shape=jax.ShapeDtypeStruct((m, n), out_dtype),
        compiler_params=_params(est + (8 << 20), 2),
        name=name,
    )(*lhs_list, w)


def _rope(v, c_tab, s_tab, half, period):
    width = v.shape[-1]
    lane = lax.broadcasted_iota(I32, v.shape, 1) & (period - 1)
    swapped = jnp.where(lane < half, pltpu.roll(v, width - half, 1), pltpu.roll(v, half, 1))
    return v * c_tab + swapped * s_tab


def _tile_lanes(t, reps):
    return t if reps == 1 else jnp.concatenate([t] * reps, axis=1)


def _rms(x, g):
    return x * lax.rsqrt(jnp.mean(x * x, axis=-1, keepdims=True) + RMS_EPS) * g


def _silu(g):
    return g * jax.nn.sigmoid(g)


def _prep_mla_kernel(cq_ref, ckv_ref, kr_ref, tab_ref, gcq_ref, gckv_ref, wuq_ref, wk_ref, wvt_ref,
                     q_out, k_out, vt_out):
    scale = (MLA_NOPE + MLA_ROPE) ** -0.5 * LOG2E
    nq = _rms(cq_ref[...], gcq_ref[...]).astype(BF16)
    nkv = _rms(ckv_ref[...], gckv_ref[...]).astype(BF16)
    c_tab = tab_ref[:, TAB_MLA_C * LANES:(TAB_MLA_C + 1) * LANES]
    s_tab = tab_ref[:, TAB_MLA_S * LANES:(TAB_MLA_S + 1) * LANES]
    qf = jnp.dot(nq, wuq_ref[...], preferred_element_type=F32) * scale
    kn = jnp.dot(nkv, wk_ref[...], preferred_element_type=F32)
    kr = _rope(kr_ref[...], c_tab, s_tab, MLA_ROPE // 2, LANES).astype(BF16)
    for h in range(MLA_HEADS):
        base = h * 2 * LANES
        q_out[:, base:base + LANES] = qf[:, base:base + LANES].astype(BF16)
        q_out[:, base + LANES:base + 2 * LANES] = _rope(
            qf[:, base + LANES:base + 2 * LANES], c_tab, s_tab, MLA_ROPE // 2, LANES).astype(BF16)
        k_out[:, base:base + LANES] = kn[:, h * LANES:(h + 1) * LANES].astype(BF16)
        k_out[:, base + LANES:base + 2 * LANES] = kr
    vt = lax.dot_general(wvt_ref[...], nkv, (((1,), (1,)), ((), ())), preferred_element_type=F32)
    half = vt.shape[1] // 2
    vt_out[0] = vt[:, :half].astype(BF16)
    vt_out[1] = vt[:, half:].astype(BF16)


def _prep_mla(h2d, tabs, g_cq, g_ckv, wuq_p, wk, wvt, batch, seq, tm):
    m = batch * seq
    nt = seq // tm
    hd = MLA_HEADS * 2 * LANES
    return pl.pallas_call(
        _prep_mla_kernel,
        grid=(batch, nt),
        in_specs=[
            pl.BlockSpec((tm, MLA_Q_LORA), lambda b, i: (b * nt + i, COL_CQ // MLA_Q_LORA)),
            pl.BlockSpec((tm, MLA_KV_LORA), lambda b, i: (b * nt + i, COL_CKV // MLA_KV_LORA)),
            pl.BlockSpec((tm, LANES), lambda b, i: (b * nt + i, COL_KR // LANES)),
            pl.BlockSpec((tm, N_TABS * LANES), lambda b, i: (b * nt + i, 0)),
            pl.BlockSpec((1, MLA_Q_LORA), lambda b, i: (0, 0)),
            pl.BlockSpec((1, MLA_KV_LORA), lambda b, i: (0, 0)),
            pl.BlockSpec((MLA_Q_LORA, hd), lambda b, i: (0, 0)),
            pl.BlockSpec((MLA_KV_LORA, MLA_HEADS * MLA_NOPE), lambda b, i: (0, 0)),
            pl.BlockSpec((MLA_HEADS * MLA_V, MLA_KV_LORA), lambda b, i: (0, 0)),
        ],
        out_specs=[
            pl.BlockSpec((tm, hd), lambda b, i: (b * nt + i, 0)),
            pl.BlockSpec((tm, hd), lambda b, i: (b * nt + i, 0)),
            pl.BlockSpec((None, 2, MLA_HEADS * MLA_V, tm // 2), lambda b, i: (b, i, 0, 0)),
        ],
        out_shape=[
            jax.ShapeDtypeStruct((m, hd), BF16),
            jax.ShapeDtypeStruct((m, hd), BF16),
            jax.ShapeDtypeStruct((batch, 2 * nt, MLA_HEADS * MLA_V, tm // 2), BF16),
        ],
        compiler_params=_params(40 << 20, 2),
        name="prep_mla",
    )(h2d, h2d, h2d, tabs, g_cq, g_ckv, wuq_p, wk, wvt)


def _softmax_step(s, col_max, vt_chunk, m, l, acc):
    m_new = jnp.maximum(m, col_max)
    alpha = jnp.exp2(m - m_new)
    p = jnp.exp2(s - m_new)
    l_new = alpha * l + jnp.sum(p, axis=0, keepdims=True)
    acc_new = alpha * acc + jnp.dot(vt_chunk, p.astype(BF16), preferred_element_type=F32)
    return m_new, l_new, acc_new


LOG2E = 1.4426950408889634


_NT = (((1,), (1,)), ((), ()))


def _mla_attn_kernel(q_ref, k_ref, vt_ref, gate_ref, o_ref, s0_ref, s1_ref, *, tile, ch):
    i = pl.program_id(2)
    q = q_ref[...]
    slots = (s0_ref, s1_ref)

    def produce(c, slot, diag_offset=None):
        kc = k_ref[pl.ds(pl.multiple_of(c * ch, ch), ch), :]
        s = lax.dot_general(kc, q, _NT, preferred_element_type=F32)
        if diag_offset is not None:
            kidx = lax.broadcasted_iota(I32, (ch, tile), 0) + diag_offset
            qidx = lax.broadcasted_iota(I32, (ch, tile), 1)
            s = jnp.where(kidx <= qidx, s, NEG_BIG)
        slots[slot][...] = s
        return jnp.max(s, axis=0, keepdims=True)

    def consume(c, slot, col_max, carry):
        return _softmax_step(slots[slot][...], col_max, vt_ref[c], *carry)

    init = (jnp.full((1, tile), NEG_BIG, F32), jnp.zeros((1, tile), F32),
            jnp.zeros((MLA_V, tile), F32))
    diag = 2 * i
    col_a = produce(diag, 0, 0)
    col_b = produce(diag + 1, 1, ch)
    carry = consume(diag, 0, col_a, init)

    def pair(j, state):
        cr, col1, c1 = state
        col0 = produce(2 * j, 0)
        cr = consume(c1, 1, col1, cr)
        col1 = produce(2 * j + 1, 1)
        cr = consume(2 * j, 0, col0, cr)
        return cr, col1, 2 * j + 1

    state = lax.fori_loop(0, i // 2, lambda jj, st: pair(2 * jj + 1, pair(2 * jj, st)),
                          (carry, col_b, diag + 1))
    carry, col1, c1 = lax.fori_loop(0, i & 1, lambda _, st: pair(i - 1, st), state)
    _, l, acc = consume(c1, 1, col1, carry)
    o = (acc / l).T
    o_ref[...] = (o * _silu(gate_ref[...])).astype(o_ref.dtype)


def _mla_attn(q, k, vt, h2d, batch, seq, tile):
    m = batch * seq
    nt = seq // tile
    ch = tile // 2
    return pl.pallas_call(
        functools.partial(_mla_attn_kernel, tile=tile, ch=ch),
        grid=(batch, MLA_HEADS, nt),
        in_specs=[
            pl.BlockSpec((tile, 2 * LANES), lambda b, h, i: (b * nt + i, h)),
            pl.BlockSpec((seq, 2 * LANES), lambda b, h, i: (b, h)),
            pl.BlockSpec((None, 2 * nt, MLA_V, ch), lambda b, h, i: (b, 0, h, 0)),
            pl.BlockSpec((tile, LANES), lambda b, h, i: (b * nt + i, COL_AG // LANES + h)),
        ],
        out_specs=pl.BlockSpec((tile, LANES), lambda b, h, i: (b * nt + i, h)),
        out_shape=jax.ShapeDtypeStruct((m, GROUP_WIDTH), BF16),
        scratch_shapes=[pltpu.VMEM((ch, tile), F32), pltpu.VMEM((ch, tile), F32)],
        compiler_params=_params(32 << 20, 3),
        name="mla_attn",
    )(q, k, vt, h2d)


def _hi_lo(v):
    hi = v.astype(BF16).astype(F32)
    return hi, v - hi


def _prep_dsa_kernel(q_ref, k_ref, v_ref, qi_ref, ki_ref, tab_ref,
                     q_out, k_out, vt_out, qi4_out, ki4_out, wt_out):
    def tab(t):
        return tab_ref[:, t * LANES:(t + 1) * LANES]

    scale = DSA_HEAD_DIM ** -0.5 * LOG2E
    q = q_ref[...]
    q_out[...] = (_rope(q, _tile_lanes(tab(TAB_DSA_C), DSA_HEADS), _tile_lanes(tab(TAB_DSA_S), DSA_HEADS),
                        DSA_ROT // 2, LANES) * scale).astype(BF16)
    k = k_ref[...]
    k_out[...] = _rope(k, _tile_lanes(tab(TAB_DSA_C), DSA_KV_HEADS), _tile_lanes(tab(TAB_DSA_S), DSA_KV_HEADS),
                       DSA_ROT // 2, LANES).astype(BF16)
    vt_out[...] = v_ref[...].T.astype(BF16)
    reps = IDX_HEADS * IDX_DIM // LANES
    qi = _rope(qi_ref[...], _tile_lanes(tab(TAB_IDX_C), reps), _tile_lanes(tab(TAB_IDX_S), reps),
               IDX_ROT // 2, IDX_DIM)
    qb = DSA_QBLK
    first_half = lax.broadcasted_iota(I32, (qb, LANES), 1) < IDX_DIM
    for blk in range(q.shape[0] // qb):
        for pair in range(IDX_HEADS // 2):
            hi, lo = _hi_lo(qi[blk * qb:(blk + 1) * qb, pair * LANES:(pair + 1) * LANES])
            even = jnp.where(first_half, hi, pltpu.roll(lo, IDX_DIM, 1)).astype(BF16)
            odd = jnp.where(first_half, pltpu.roll(hi, IDX_DIM, 1), lo).astype(BF16)
            for half in range(2):
                qi4_out[blk, pair, 0:qb, half * LANES:(half + 1) * LANES] = even
                qi4_out[blk, pair, qb:2 * qb, half * LANES:(half + 1) * LANES] = odd
    small = ki_ref[...]
    lane = lax.broadcasted_iota(I32, small.shape, 1)
    c_ki = jnp.where(lane < IDX_DIM, tab(TAB_IDX_C), 1.0)
    s_ki = jnp.where(lane < IDX_DIM, tab(TAB_IDX_S), 0.0)
    hi, lo = _hi_lo(_rope(small, c_ki, s_ki, IDX_ROT // 2, IDX_DIM))
    ki4_out[:, 0:LANES] = jnp.where(lane < IDX_DIM, hi, pltpu.roll(hi, IDX_DIM, 1)).astype(BF16)
    ki4_out[:, LANES:2 * LANES] = jnp.where(lane < IDX_DIM, lo, pltpu.roll(lo, IDX_DIM, 1)).astype(BF16)
    wt = small.T[IDX_DIM:IDX_DIM + IDX_HEADS, :]
    wt_out[...] = wt * (IDX_HEADS ** -0.5 * IDX_DIM ** -0.5)


def _prep_dsa(h2d, tabs, batch, seq):
    tm = KEY_CHUNK
    m = batch * seq
    nt = seq // tm
    qpt = tm // DSA_QBLK
    kvw = DSA_KV_HEADS * DSA_HEAD_DIM

    def row(b, i):
        return b * nt + i

    return pl.pallas_call(
        _prep_dsa_kernel,
        grid=(batch, nt),
        in_specs=[
            pl.BlockSpec((tm, GROUP_WIDTH), lambda b, i: (row(b, i), COL_BQ // GROUP_WIDTH)),
            pl.BlockSpec((tm, kvw), lambda b, i: (row(b, i), COL_BK // kvw)),
            pl.BlockSpec((tm, kvw), lambda b, i: (row(b, i), COL_BV // kvw)),
            pl.BlockSpec((tm, GROUP_WIDTH), lambda b, i: (row(b, i), COL_QI // GROUP_WIDTH)),
            pl.BlockSpec((tm, LANES), lambda b, i: (row(b, i), COL_KI // LANES)),
            pl.BlockSpec((tm, N_TABS * LANES), lambda b, i: (row(b, i), 0)),
        ],
        out_specs=[
            pl.BlockSpec((tm, GROUP_WIDTH), lambda b, i: (row(b, i), 0)),
            pl.BlockSpec((tm, kvw), lambda b, i: (row(b, i), 0)),
            pl.BlockSpec((None, None, kvw, tm), lambda b, i: (b, i, 0, 0)),
            pl.BlockSpec((None, qpt, IDX_HEADS // 2, 2 * DSA_QBLK, 4 * IDX_DIM), lambda b, i: (b, i, 0, 0, 0)),
            pl.BlockSpec((tm, 4 * IDX_DIM), lambda b, i: (row(b, i), 0)),
            pl.BlockSpec((None, IDX_HEADS, tm), lambda b, i: (b, 0, i)),
        ],
        out_shape=[
            jax.ShapeDtypeStruct((m, GROUP_WIDTH), BF16),
            jax.ShapeDtypeStruct((m, kvw), BF16),
            jax.ShapeDtypeStruct((batch, nt, kvw, tm), BF16),
            jax.ShapeDtypeStruct((batch, seq // DSA_QBLK, IDX_HEADS // 2, 2 * DSA_QBLK, 4 * IDX_DIM), BF16),
            jax.ShapeDtypeStruct((m, 4 * IDX_DIM), BF16),
            jax.ShapeDtypeStruct((batch, IDX_HEADS, seq), F32),
        ],
        compiler_params=_params(24 << 20, 2),
        name="prep_dsa",
    )(h2d, h2d, h2d, h2d, h2d, tabs)


def _sortable_key(score):
    bits = lax.bitcast_convert_type(score, I32)
    key = jnp.where(bits < 0, bits ^ jnp.int32(0x7FFFFFFF), bits)
    return jnp.where(score == 0.0, 0, key)


def _dsa_kernel(qi4_ref, ki4_ref, wt_ref, q_ref, k_ref, vt_ref, gate_ref, o_ref,
                key_ref, hi_ref, lo_ref, mlo_ref, s0_ref, s1_ref,
                *, topk, seq_bits):
    i = pl.program_id(1)
    n_vis = i // 2
    n_chunks = n_vis + 1
    wt = wt_ref[...]
    ch = KEY_CHUNK
    qb = DSA_QBLK

    def rows(c):
        return pl.ds(pl.multiple_of(c * ch, ch), ch)

    def score_chunk(c):
        kc = ki4_ref[rows(c), :]
        tot = None
        for p in range(IDX_HEADS // 2):
            s = lax.dot_general(kc, qi4_ref[p], _NT, preferred_element_type=F32)
            r = jnp.maximum(s, 0.0)
            t = r[:, :qb] * wt[2 * p:2 * p + 1, :] + r[:, qb:] * wt[2 * p + 1:2 * p + 2, :]
            tot = t if tot is None else tot + t
        return tot

    def store_keys(c, key):
        key_ref[rows(c), :] = key
        hi_ref[rows(c), :] = (key >> 16).astype(I16)
        lo_ref[rows(c), :] = ((key & 0xFFFF) - 32768).astype(I16)

    def phase1(c, carry):
        store_keys(c, _sortable_key(score_chunk(c)))
        return carry

    lax.fori_loop(0, n_vis, phase1, 0)
    kidx = n_vis * ch + lax.broadcasted_iota(I32, (ch, qb), 0)
    qidx = i * qb + lax.broadcasted_iota(I32, (ch, qb), 1)
    store_keys(n_vis, jnp.where(kidx <= qidx, _sortable_key(score_chunk(n_vis)), INT_MIN))

    @pl.when((n_chunks & 1) == 1)
    def _():
        hi_ref[rows(n_chunks), :] = jnp.full((ch, qb), I16_MIN, I16)
        lo_ref[rows(n_chunks), :] = jnp.full((ch, qb), I16_MIN, I16)

    n_pairs = (n_chunks + 1) // 2

    def rows2(p):
        return pl.ds(pl.multiple_of(p * 2 * ch, 2 * ch), 2 * ch)

    def count16(ref, pred):
        def body(p, acc):
            hit = pred(ref[rows2(p), :]).astype(I16)
            parts = [hit[g * 16:(g + 1) * 16, :] for g in range(2 * ch // 16)]
            while len(parts) > 1:
                parts = [parts[a] + parts[a + 1] for a in range(0, len(parts), 2)]
            return acc + parts[0]
        acc = lax.fori_loop(0, n_pairs, body, jnp.zeros((16, qb), I16))
        return jnp.sum(acc.astype(I32), axis=0, keepdims=True)

    def search16(ref, target):
        t0 = jnp.where(count16(ref, lambda blk: blk >= jnp.int16(0)) >= target, 0, I16_MIN).astype(I32)

        def step(b, t):
            cand = t + jnp.left_shift(jnp.int32(1), 14 - b)
            cand16 = cand.astype(I16)
            return jnp.where(count16(ref, lambda blk: blk >= cand16) >= target, cand, t)

        return lax.fori_loop(0, 15, step, t0)

    t_hi = search16(hi_ref, topk)
    t_hi16 = t_hi.astype(I16)
    c_hi_gt = count16(hi_ref, lambda blk: blk > t_hi16)
    c_hi_ge = count16(hi_ref, lambda blk: blk >= t_hi16)

    def mask_low(p, carry):
        mlo_ref[rows2(p), :] = jnp.where(hi_ref[rows2(p), :] == t_hi16, lo_ref[rows2(p), :], jnp.int16(I16_MIN))
        return carry

    lax.fori_loop(0, n_pairs, mask_low, 0)
    t_lo = search16(mlo_ref, topk - c_hi_gt)
    t_lo16 = t_lo.astype(I16)
    found = t_hi > I16_MIN
    thr = jnp.where(found, (t_hi << 16) | ((t_lo + 32768) & 0xFFFF), INT_MIN)
    c_gt = c_hi_gt + count16(mlo_ref, lambda blk: blk > t_lo16)
    c_ge = c_hi_gt + jnp.where(t_lo > I16_MIN, count16(mlo_ref, lambda blk: blk >= t_lo16), c_hi_ge - c_hi_gt)
    need = topk - c_gt
    tied = jnp.logical_and(c_ge > topk, found)

    def count(pred):
        def body(c, acc):
            idx = c * ch + lax.broadcasted_iota(I32, (ch, qb), 0)
            hit = pred(key_ref[rows(c), :], idx).astype(I32)
            return acc + jnp.sum(hit.reshape(ch // 8, 8, qb), axis=0)
        acc = lax.fori_loop(0, n_chunks, body, jnp.zeros((8, qb), I32))
        return jnp.sum(acc, axis=0, keepdims=True)

    def tie_search():
        def step(b, lo):
            cand = lo + jnp.left_shift(jnp.int32(1), seq_bits - 1 - b)
            below = count(lambda keys, idx: jnp.logical_and(keys == thr, idx < cand))
            return jnp.where(below < need, cand, lo)
        return lax.fori_loop(0, seq_bits, step, jnp.zeros((1, qb), I32))

    last_eq = lax.cond(jnp.max(tied.astype(I32)) > 0, tie_search,
                       lambda: jnp.full((1, qb), 2 ** seq_bits, I32))
    thr_sel = jnp.maximum(thr, INT_MIN + 1)

    qs = [jnp.concatenate([q_ref[:, (n * DSA_GROUP + g) * LANES:(n * DSA_GROUP + g + 1) * LANES]
                           for g in range(DSA_GROUP)], axis=0) for n in range(DSA_KV_HEADS)]
    nq = DSA_GROUP * qb

    slots = (s0_ref, s1_ref)

    def produce(c, slot):
        keys = key_ref[rows(c), :]
        idx = c * ch + lax.broadcasted_iota(I32, (ch, qb), 0)
        drop = jnp.logical_and(keys == thr, idx > last_eq)
        sel = jnp.logical_and(keys >= thr_sel, jnp.logical_not(drop))
        neg = jnp.where(sel, 0.0, NEG_BIG)
        neg = jnp.concatenate([neg] * DSA_GROUP, axis=1)
        col_max = []
        for n in range(DSA_KV_HEADS):
            kc = k_ref[rows(c), n * LANES:(n + 1) * LANES]
            s = lax.dot_general(kc, qs[n], _NT, preferred_element_type=F32) + neg
            slots[slot][n] = s
            col_max.append(jnp.max(s, axis=0, keepdims=True))
        return tuple(col_max)

    def consume(c, slot, col_max, carry):
        return tuple(_softmax_step(slots[slot][n], col_max[n],
                                   vt_ref[c, n * LANES:(n + 1) * LANES, :], *carry[n])
                     for n in range(DSA_KV_HEADS))

    def phase3(j, state):
        carry, col0 = state
        col1 = produce(2 * j + 1, 1)
        carry = consume(2 * j, 0, col0, carry)
        col0 = produce(2 * j + 2, 0)
        carry = consume(2 * j + 1, 1, col1, carry)
        return carry, col0

    init = tuple((jnp.full((1, nq), NEG_BIG, F32), jnp.zeros((1, nq), F32),
                  jnp.zeros((DSA_HEAD_DIM, nq), F32)) for _ in range(DSA_KV_HEADS))
    pairs = (n_chunks - 1) // 2
    carry, col0 = lax.fori_loop(0, pairs, phase3, (init, produce(0, 0)))
    last = 2 * pairs

    def tail_two(cr):
        col1 = produce(last + 1, 1)
        cr = consume(last, 0, col0, cr)
        return consume(last + 1, 1, col1, cr)

    res = lax.cond(n_chunks - 1 > last, tail_two, lambda cr: consume(last, 0, col0, cr), carry)
    for n in range(DSA_KV_HEADS):
        _, l, acc = res[n]
        o = (acc / l).T
        for g in range(DSA_GROUP):
            col = (n * DSA_GROUP + g) * LANES
            o_ref[:, col:col + LANES] = (o[g * qb:(g + 1) * qb, :]
                                         * _silu(gate_ref[:, col:col + LANES])).astype(o_ref.dtype)


def _dsa(qi4, ki4, wt, q, k, vt, h2d, batch, seq):
    m = batch * seq
    nq = seq // DSA_QBLK
    nck = seq // KEY_CHUNK
    kvw = DSA_KV_HEADS * DSA_HEAD_DIM
    topk = min(TOPK_MAX, seq // 4)
    seq_bits = int(np.log2(seq))
    assert 2 ** seq_bits == seq and nck % 2 == 0
    return pl.pallas_call(
        functools.partial(_dsa_kernel, topk=topk, seq_bits=seq_bits),
        grid=(batch, nq),
        in_specs=[
            pl.BlockSpec((None, None, IDX_HEADS // 2, 2 * DSA_QBLK, 4 * IDX_DIM),
                         lambda b, i: (b, i, 0, 0, 0)),
            pl.BlockSpec((seq, 4 * IDX_DIM), lambda b, i: (b, 0)),
            pl.BlockSpec((None, IDX_HEADS, DSA_QBLK), lambda b, i: (b, 0, i)),
            pl.BlockSpec((DSA_QBLK, GROUP_WIDTH), lambda b, i: (b * nq + i, 0)),
            pl.BlockSpec((seq, kvw), lambda b, i: (b, 0)),
            pl.BlockSpec((None, nck, kvw, KEY_CHUNK), lambda b, i: (b, 0, 0, 0)),
            pl.BlockSpec((DSA_QBLK, GROUP_WIDTH), lambda b, i: (b * nq + i, COL_BG // GROUP_WIDTH)),
        ],
        out_specs=pl.BlockSpec((DSA_QBLK, GROUP_WIDTH), lambda b, i: (b * nq + i, 0)),
        out_shape=jax.ShapeDtypeStruct((m, GROUP_WIDTH), BF16),
        scratch_shapes=[pltpu.VMEM((seq, DSA_QBLK), I32), pltpu.VMEM((seq, DSA_QBLK), I16),
                        pltpu.VMEM((seq, DSA_QBLK), I16), pltpu.VMEM((seq, DSA_QBLK), I16),
                        pltpu.VMEM((DSA_KV_HEADS, KEY_CHUNK, DSA_GROUP * DSA_QBLK), F32),
                        pltpu.VMEM((DSA_KV_HEADS, KEY_CHUNK, DSA_GROUP * DSA_QBLK), F32)],
        compiler_params=_params(32 << 20, 2),
        name="dsa",
    )(qi4, ki4, wt, q, k, vt, h2d)


def _rglru_kernel(u_ref, gate_ref, cw_ref, cb_ref, wa_ref, ba_ref, wx_ref, bx_ref, lam_ref,
                  o_ref, halo_ref, h_ref, a_scr, x_scr, y_scr, *, batch, tt):
    t = pl.program_id(0)

    @pl.when(t == 0)
    def _():
        halo_ref[...] = jnp.zeros_like(halo_ref)
        h_ref[...] = jnp.zeros_like(h_ref)

    lam = lam_ref[...]
    z = -lam
    softplus = jnp.maximum(z, 0.0) + jnp.log1p(jnp.exp(-jnp.abs(z)))
    row = lax.broadcasted_iota(I32, (tt, LRU_WIDTH), 0)
    for b in range(batch):
        u = u_ref[b]
        prev = halo_ref[b]
        p1, p2, p3 = prev[7:8, :], prev[6:7, :], prev[5:6, :]
        s1 = jnp.where(row == 0, p1, pltpu.roll(u, 1, 0))
        s2 = jnp.where(row == 0, p2, jnp.where(row == 1, p1, pltpu.roll(u, 2, 0)))
        s3 = jnp.where(row == 0, p3, jnp.where(row == 1, p2, jnp.where(row == 2, p1, pltpu.roll(u, 3, 0))))
        conv = (cw_ref[3:4, :] * u + cw_ref[2:3, :] * s1 + cw_ref[1:2, :] * s2 + cw_ref[0:1, :] * s3
                + cb_ref[...])
        halo_ref[b] = u[tt - 8:, :]
        cb16 = conv.astype(BF16)
        r = jax.nn.sigmoid(jnp.dot(cb16, wa_ref[...], preferred_element_type=F32) + ba_ref[...])
        ig = jax.nn.sigmoid(jnp.dot(cb16, wx_ref[...], preferred_element_type=F32) + bx_ref[...])
        log_a = -LRU_C * r * softplus
        a_scr[b] = jnp.exp(log_a)
        th = jnp.tanh(log_a)
        x_scr[b] = jnp.sqrt(-2.0 * th / (1.0 - th)) * (ig * conv)

    def step(s, h):
        h = a_scr[:, pl.ds(s, 1), :] * h + x_scr[:, pl.ds(s, 1), :]
        y_scr[:, pl.ds(s, 1), :] = h
        return h

    h_ref[...] = lax.fori_loop(0, tt, step, h_ref[...], unroll=8)
    o_ref[...] = (y_scr[...] * _silu(gate_ref[...])).astype(o_ref.dtype)


def _rglru(h3d, conv_w, conv_b, wa_bd, b_a, wx_bd, b_x, lam, batch, seq, tt):
    nt = seq // tt
    w = LRU_WIDTH

    def vec(r):
        return pl.BlockSpec((r, w), lambda t: (0, 0))

    return pl.pallas_call(
        functools.partial(_rglru_kernel, batch=batch, tt=tt),
        grid=(nt,),
        in_specs=[
            pl.BlockSpec((batch, tt, w), lambda t: (0, t, COL_U // w)),
            pl.BlockSpec((batch, tt, w), lambda t: (0, t, COL_CG // w)),
            vec(CONV_WIDTH), vec(1),
            pl.BlockSpec((w, w), lambda t: (0, 0)), vec(1),
            pl.BlockSpec((w, w), lambda t: (0, 0)), vec(1),
            vec(1),
        ],
        out_specs=pl.BlockSpec((batch, tt, w), lambda t: (0, t, 0)),
        out_shape=jax.ShapeDtypeStruct((batch, seq, w), BF16),
        scratch_shapes=[pltpu.VMEM((batch, 8, w), F32), pltpu.VMEM((batch, 1, w), F32),
                        pltpu.VMEM((batch, tt, w), F32), pltpu.VMEM((batch, tt, w), F32),
                        pltpu.VMEM((batch, tt, w), F32)],
        compiler_params=_params(32 << 20, 1),
        name="rglru",
    )(h3d, h3d, conv_w, conv_b, wa_bd, b_a, wx_bd, b_x, lam)


def _mem_attn_kernel(q_ref, gate_ref, km_ref, vm_ref, o_ref):
    scale = MEM_HEAD_DIM ** -0.5
    for h in range(MEM_HEADS):
        cols = slice(h * MEM_HEAD_DIM, (h + 1) * MEM_HEAD_DIM)
        qh = (q_ref[:, cols] * scale).astype(BF16)
        s = lax.dot_general(qh, km_ref[:, cols], _NT, preferred_element_type=F32)
        p = jnp.exp(s - jnp.max(s, axis=-1, keepdims=True))
        l = jnp.sum(p, axis=-1, keepdims=True)
        o = jnp.dot(p.astype(BF16), vm_ref[:, cols], preferred_element_type=F32) / l
        o_ref[:, cols] = (o * _silu(gate_ref[:, cols])).astype(o_ref.dtype)


def _mem_attn(h2d, km, vm, batch, seq, tm):
    m = batch * seq
    nt = seq // tm
    w = GROUP_WIDTH
    return pl.pallas_call(
        _mem_attn_kernel,
        grid=(batch, nt),
        in_specs=[
            pl.BlockSpec((tm, w), lambda b, i: (b * nt + i, COL_DQ // w)),
            pl.BlockSpec((tm, w), lambda b, i: (b * nt + i, COL_DG // w)),
            pl.BlockSpec((N_MEM, w), lambda b, i: (b, 0)),
            pl.BlockSpec((N_MEM, w), lambda b, i: (b, 0)),
        ],
        out_specs=pl.BlockSpec((tm, w), lambda b, i: (b * nt + i, 0)),
        out_shape=jax.ShapeDtypeStruct((m, w), BF16),
        compiler_params=_params(24 << 20, 2),
        name="mem_attn",
    )(h2d, h2d, km, vm)


def _deepnorm_kernel(x_ref, y_ref, g_ref, b_ref, o_ref, o16_ref):
    z = DEEPNORM_ALPHA * x_ref[...] + y_ref[...]
    mu = jnp.mean(z, axis=-1, keepdims=True)
    d = z - mu
    var = jnp.mean(d * d, axis=-1, keepdims=True)
    out = d * lax.rsqrt(var + LN_EPS) * g_ref[...] + b_ref[...]
    o_ref[...] = out
    o16_ref[...] = out.astype(BF16)


def _deepnorm(x, y, g, b, tm):
    m, d = x.shape
    return pl.pallas_call(
        _deepnorm_kernel,
        grid=(m // tm,),
        in_specs=[pl.BlockSpec((tm, d), lambda i: (i, 0)), pl.BlockSpec((tm, d), lambda i: (i, 0)),
                  pl.BlockSpec((1, d), lambda i: (0, 0)), pl.BlockSpec((1, d), lambda i: (0, 0))],
        out_specs=[pl.BlockSpec((tm, d), lambda i: (i, 0)), pl.BlockSpec((tm, d), lambda i: (i, 0))],
        out_shape=[jax.ShapeDtypeStruct((m, d), F32), jax.ShapeDtypeStruct((m, d), BF16)],
        compiler_params=_params(40 << 20, 1),
        name="deepnorm",
    )(x, y, g, b)


def _rope_tabs(positions):
    pos = positions.astype(F32)
    b, s = pos.shape

    def one(rot, period):
        inv = ROPE_THETA ** (-jnp.arange(0, rot, 2, dtype=F32) / rot)
        ang = pos[:, :, None] * inv
        c, sn = jnp.cos(ang), jnp.sin(ang)
        rest = period - rot
        cp = jnp.concatenate([c, c, jnp.ones((b, s, rest), F32)], axis=-1)
        sp = jnp.concatenate([-sn, sn, jnp.zeros((b, s, rest), F32)], axis=-1)
        reps = LANES // period
        return jnp.tile(cp, (1, 1, reps)), jnp.tile(sp, (1, 1, reps))

    parts = one(MLA_ROPE, LANES) + one(DSA_ROT, LANES) + one(IDX_ROT, IDX_DIM)
    return jnp.concatenate(parts, axis=-1).reshape(b * s, N_TABS * LANES)


def _pack_plan():
    src = dict(zip(("cq", "ckv", "kr", "ag", "bq", "bk", "bv", "qi", "ki", "wi", "bg", "u", "cg", "dq", "dg"),
                   [0] + [int(c) for c in np.cumsum(IN_SIZES)[:-1]]))
    size = dict(zip(src, IN_SIZES))
    whole = [("cq", COL_CQ), ("ag", COL_AG), ("bq", COL_BQ), ("qi", COL_QI), ("bg", COL_BG), ("u", COL_U),
             ("cg", COL_CG), ("dq", COL_DQ), ("dg", COL_DG), ("ckv", COL_CKV), ("bk", COL_BK), ("bv", COL_BV)]
    plan = [(src[n], dst, size[n]) for n, dst in whole]
    partial = [(src["kr"], COL_KR, size["kr"]), (src["ki"], COL_KI, size["ki"] + size["wi"])]
    return plan, partial


def _pack_w_in_kernel(w_ref, o_ref):
    plan, partial = _pack_plan()
    for s, d, n in plan:
        o_ref[:, d:d + n] = w_ref[:, s:s + n].astype(BF16)
    lane = lax.broadcasted_iota(I32, (w_ref.shape[0], LANES), 1)
    for s, d, n in partial:
        o_ref[:, d:d + LANES] = jnp.where(lane < n, w_ref[:, s:s + LANES], 0.0).astype(BF16)


def _pack_w_in(w_in):
    depth, k, n = w_in.shape
    tr = 256
    return pl.pallas_call(
        _pack_w_in_kernel,
        grid=(depth, k // tr),
        in_specs=[pl.BlockSpec((None, tr, n), lambda l, i: (l, i, 0))],
        out_specs=pl.BlockSpec((None, tr, N_PACKED), lambda l, i: (l, i, 0)),
        out_shape=jax.ShapeDtypeStruct((depth, k, N_PACKED), BF16),
        compiler_params=_params(2 * tr * (n * 4 + N_PACKED * 2) + (8 << 20), 2),
        name="pack_w_in",
    )(w_in)


def _block_diag(w):
    l = w.shape[0]
    eye = jnp.eye(LRU_BLOCKS, dtype=w.dtype)
    bd = jnp.einsum('lnde,nm->lndme', w, eye)
    return bd.reshape(l, LRU_WIDTH, LRU_WIDTH).astype(BF16)


def _layer(x, x16, mem16, tabs, lw, batch, seq):
    m = batch * seq
    h2d = _matmul([x16], lw["w_in"], F32, tm=min(2048, m), tn=256, name="in_proj")
    tile_a = min(512, seq)
    qa, ka, vta = _prep_mla(h2d, tabs, lw["g_cq"], lw["g_ckv"], lw["w_uq"], lw["w_uk"], lw["w_uvt"],
                            batch, seq, tile_a)
    ya = _mla_attn(qa, ka, vta, h2d, batch, seq, tile_a)
    qb, kb, vtb, qi4, ki4, wt = _prep_dsa(h2d, tabs, batch, seq)
    yb = _dsa(qi4, ki4, wt, qb, kb, vtb, h2d, batch, seq)
    yc = _rglru(h2d.reshape(batch, seq, N_PACKED), lw["conv_w"], lw["conv_b"], lw["w_rg_a"], lw["b_rg_a"],
                lw["w_rg_x"], lw["b_rg_x"], lw["lam"], batch, seq, min(256, seq))
    yc = yc.reshape(m, GROUP_WIDTH)
    km = _matmul([mem16], lw["w_mem_k"], BF16, tm=mem16.shape[0], tn=256, name="mem_proj_k")
    vm = _matmul([mem16], lw["w_mem_v"], BF16, tm=mem16.shape[0], tn=256, name="mem_proj_v")
    yd = _mem_attn(h2d, km, vm, batch, seq, min(512, seq))
    y = _matmul([ya, yb, yc, yd], lw["w_o"], F32, tm=min(1024, m), tn=256, name="out_proj")
    return _deepnorm(x, y, lw["ln_g"], lw["ln_b"], min(256, m))


def kernel(x, mem, positions, w_in, g_cq, g_ckv, w_uq, w_ukv, conv_w, conv_b, w_rg_a, b_rg_a, w_rg_x,
           b_rg_x, lru_lambda, w_mem_k, w_mem_v, w_o, ln_g, ln_b):
    batch, seq, d = x.shape
    depth = w_in.shape[0]
    tabs = _rope_tabs(positions)
    w_in_p = _pack_w_in(w_in)
    w_uq_p = jnp.pad(w_uq.reshape(depth, MLA_Q_LORA, MLA_HEADS, MLA_NOPE + MLA_ROPE),
                     [(0, 0), (0, 0), (0, 0), (0, 2 * LANES - MLA_NOPE - MLA_ROPE)])
    w_uq_p = w_uq_p.reshape(depth, MLA_Q_LORA, MLA_HEADS * 2 * LANES).astype(BF16)
    w_ukv4 = w_ukv.reshape(depth, MLA_KV_LORA, MLA_HEADS, MLA_NOPE + MLA_V)
    w_uk = w_ukv4[..., :MLA_NOPE].reshape(depth, MLA_KV_LORA, MLA_HEADS * MLA_NOPE).astype(BF16)
    w_uvt = w_ukv4[..., MLA_NOPE:].reshape(depth, MLA_KV_LORA, MLA_HEADS * MLA_V)
    w_uvt = w_uvt.transpose(0, 2, 1).astype(BF16)
    wa_bd = _block_diag(w_rg_a)
    wx_bd = _block_diag(w_rg_x)
    mem16 = mem.reshape(batch * mem.shape[1], d).astype(BF16)

    xf = x.reshape(batch * seq, d)
    x16 = xf.astype(BF16)
    for l in range(depth):
        lw = dict(w_in=w_in_p[l], g_cq=g_cq[l][None], g_ckv=g_ckv[l][None], w_uq=w_uq_p[l], w_uk=w_uk[l],
                  w_uvt=w_uvt[l], conv_w=conv_w[l], conv_b=conv_b[l][None], w_rg_a=wa_bd[l],
                  b_rg_a=b_rg_a[l][None], w_rg_x=wx_bd[l], b_rg_x=b_rg_x[l][None], lam=lru_lambda[l][None],
                  w_mem_k=w_mem_k[l], w_mem_v=w_mem_v[l], w_o=w_o[l], ln_g=ln_g[l][None], ln_b=ln_b[l][None])
        xf, x16 = _layer(xf, x16, mem16, tabs, lw, batch, seq)
    return xf.reshape(batch, seq, d)
```

```python
import functools

import numpy as np
import jax
import jax.numpy as jnp
from jax import lax
from jax.experimental import pallas as pl
from jax.experimental.pallas import tpu as pltpu

F32 = jnp.float32
BF16 = jnp.bfloat16
I32 = jnp.int32

DEPTH = 4
D_MODEL = 4096
N_MEM = 256
GROUP_WIDTH = D_MODEL // 4
ROPE_THETA = 500000.0
MLA_HEADS = 8
MLA_NOPE = 128
MLA_ROPE = 64
MLA_V = 128
MLA_Q_LORA = GROUP_WIDTH
MLA_KV_LORA = GROUP_WIDTH // 2
DSA_HEADS = 8
DSA_KV_HEADS = 2
DSA_GROUP = DSA_HEADS // DSA_KV_HEADS
DSA_HEAD_DIM = 128
DSA_ROT = DSA_HEAD_DIM // 4
IDX_HEADS = 16
IDX_DIM = 64
IDX_ROT = IDX_DIM // 4
TOPK_MAX = 256
LRU_WIDTH = GROUP_WIDTH
LRU_BLOCKS = 16
LRU_BLOCK_DIM = LRU_WIDTH // LRU_BLOCKS
CONV_WIDTH = 4
LRU_C = 8.0
MEM_HEADS = 4
MEM_HEAD_DIM = GROUP_WIDTH // MEM_HEADS
DEEPNORM_ALPHA = (2 * DEPTH) ** 0.25
LN_EPS = 1e-5
RMS_EPS = 1e-6

IN_SIZES = (
    MLA_Q_LORA, MLA_KV_LORA, MLA_ROPE, GROUP_WIDTH,
    GROUP_WIDTH, DSA_KV_HEADS * DSA_HEAD_DIM, DSA_KV_HEADS * DSA_HEAD_DIM,
    IDX_HEADS * IDX_DIM, IDX_DIM, IDX_HEADS, GROUP_WIDTH,
    LRU_WIDTH, GROUP_WIDTH,
    GROUP_WIDTH, GROUP_WIDTH,
)

LANES = 128
V7X_VMEM_BUDGET = 56 * 1024 * 1024

COL_CQ = 0
COL_AG = 1024
COL_BQ = 2048
COL_QI = 3072
COL_BG = 4096
COL_U = 5120
COL_CG = 6144
COL_DQ = 7168
COL_DG = 8192
COL_CKV = 9216
COL_BK = 9728
COL_BV = 9984
N_MAIN = 10240
SMALL_KR = 0
SMALL_KI = 128

TAB_MLA_C, TAB_MLA_S, TAB_DSA_C, TAB_DSA_S, TAB_IDX_C, TAB_IDX_S = range(6)
N_TABS = 6

NEG_BIG = -1e30
INT_MIN = -2 ** 31
LOG2E = 1.4426950408889634
KEY_CHUNK = 256
DSA_QBLK = 128
_NT = (((1,), (1,)), ((), ()))


def _vmem_limit(nbytes):
    return int(min(V7X_VMEM_BUDGET, max(nbytes, 16 * 1024 * 1024)))


def _params(nbytes, ndims):
    return pltpu.CompilerParams(dimension_semantics=("arbitrary",) * ndims,
                                vmem_limit_bytes=_vmem_limit(nbytes))


def _mm_kernel(*refs, n_lhs, kg):
    w_ref = refs[n_lhs]
    o_ref = refs[n_lhs + 1]
    acc = None
    for g in range(n_lhs):
        part = jnp.dot(refs[g][...], w_ref[g * kg:(g + 1) * kg, :].astype(BF16), preferred_element_type=F32)
        acc = part if acc is None else acc + part
    o_ref[...] = acc.astype(o_ref.dtype)


def _matmul(lhs_list, w, layer, out_dtype, tm, tn, name):
    m, kg = lhs_list[0].shape
    _, k, n = w.shape
    n_lhs = len(lhs_list)
    assert k == kg * n_lhs and m % tm == 0 and n % tn == 0
    est = 2 * (tm * k * 2 + k * tn * w.dtype.itemsize + tm * tn * jnp.dtype(out_dtype).itemsize) + tm * tn * 8
    return pl.pallas_call(
        functools.partial(_mm_kernel, n_lhs=n_lhs, kg=kg),
        grid=(m // tm, n // tn),
        in_specs=[pl.BlockSpec((tm, kg), lambda i, j: (i, 0)) for _ in range(n_lhs)]
        + [pl.BlockSpec((None, k, tn), lambda i, j: (layer, 0, j))],
        out_specs=pl.BlockSpec((tm, tn), lambda i, j: (i, j)),
        out_shape=jax.ShapeDtypeStruct((m, n), out_dtype),
        compiler_params=_params(est + (8 << 20), 2),
        name=name,
    )(*lhs_list, w)


def _in_proj_kernel(tab_ref, x_ref, w_ref, o_ref):
    del tab_ref
    o_ref[...] = lax.dot_general(x_ref[...], w_ref[...].astype(BF16), _NT, preferred_element_type=F32)


def _in_proj(x16, w_t, layer, row_offsets, tm, tn, name):
    m, k = x16.shape
    nblk = len(row_offsets)
    est = 2 * (tm * k * 2 + tn * k * 4 + tm * tn * 4) + tm * tn * 8
    grid_spec = pltpu.PrefetchScalarGridSpec(
        num_scalar_prefetch=1,
        grid=(m // tm, nblk),
        in_specs=[
            pl.BlockSpec((tm, k), lambda i, j, tab: (i, 0)),
            pl.BlockSpec((pl.Squeezed(), pl.Element(tn), pl.Element(k)),
                         lambda i, j, tab: (layer, pl.multiple_of(tab[j], 16), 0)),
        ],
        out_specs=pl.BlockSpec((tm, tn), lambda i, j, tab: (i, j)),
    )
    return pl.pallas_call(
        _in_proj_kernel,
        grid_spec=grid_spec,
        out_shape=jax.ShapeDtypeStruct((m, nblk * tn), F32),
        compiler_params=_params(est + (8 << 20), 2),
        name=name,
    )(jnp.asarray(row_offsets, I32), x16, w_t)


def _in_proj_row_tables(tn):
    names = ("cq", "ckv", "kr", "ag", "bq", "bk", "bv", "qi", "ki", "wi", "bg", "u", "cg", "dq", "dg")
    src = dict(zip(names, [0] + [int(c) for c in np.cumsum(IN_SIZES)[:-1]]))
    size = dict(zip(names, IN_SIZES))
    order = [("cq", COL_CQ), ("ag", COL_AG), ("bq", COL_BQ), ("qi", COL_QI), ("bg", COL_BG), ("u", COL_U),
             ("cg", COL_CG), ("dq", COL_DQ), ("dg", COL_DG), ("ckv", COL_CKV), ("bk", COL_BK), ("bv", COL_BV)]
    main = []
    for n, dst in order:
        assert dst == len(main) * tn and size[n] % tn == 0 and src[n] % 16 == 0
        main += [src[n] + t * tn for t in range(size[n] // tn)]
    assert len(main) * tn == N_MAIN and src["wi"] == src["ki"] + size["ki"]
    return np.asarray(main, np.int32), np.asarray([src["kr"], src["ki"]], np.int32)


def _rope(v, c_tab, s_tab, half, period):
    width = v.shape[-1]
    lane = lax.broadcasted_iota(I32, v.shape, 1) & (period - 1)
    swapped = jnp.where(lane < half, pltpu.roll(v, width - half, 1), pltpu.roll(v, half, 1))
    return v * c_tab + swapped * s_tab


def _tile_lanes(t, reps):
    return t if reps == 1 else jnp.concatenate([t] * reps, axis=1)


def _rms(x, g):
    return x * lax.rsqrt(jnp.mean(x * x, axis=-1, keepdims=True) + RMS_EPS) * g


def _silu(g):
    return g * jax.nn.sigmoid(g)


def _prep_mla_kernel(cq_ref, ckv_ref, kr_ref, tab_ref, gcq_ref, gckv_ref, wuq_ref, wk_ref, wvt_ref,
                     q_out, k_out, vt_out):
    scale = (MLA_NOPE + MLA_ROPE) ** -0.5 * LOG2E
    nq = _rms(cq_ref[...], gcq_ref[...]).astype(BF16)
    nkv = _rms(ckv_ref[...], gckv_ref[...]).astype(BF16)
    c_tab = tab_ref[:, TAB_MLA_C * LANES:(TAB_MLA_C + 1) * LANES]
    s_tab = tab_ref[:, TAB_MLA_S * LANES:(TAB_MLA_S + 1) * LANES]
    qf = jnp.dot(nq, wuq_ref[...], preferred_element_type=F32) * scale
    kn = jnp.dot(nkv, wk_ref[...], preferred_element_type=F32)
    lane = lax.broadcasted_iota(I32, kr_ref.shape, 1)
    kr = jnp.where(lane < MLA_ROPE, kr_ref[...], 0.0)
    kr = _rope(kr, c_tab, s_tab, MLA_ROPE // 2, LANES).astype(BF16)
    for h in range(MLA_HEADS):
        base = h * 2 * LANES
        q_out[:, base:base + LANES] = qf[:, base:base + LANES].astype(BF16)
        q_out[:, base + LANES:base + 2 * LANES] = _rope(
            qf[:, base + LANES:base + 2 * LANES], c_tab, s_tab, MLA_ROPE // 2, LANES).astype(BF16)
        k_out[:, base:base + LANES] = kn[:, h * LANES:(h + 1) * LANES].astype(BF16)
        k_out[:, base + LANES:base + 2 * LANES] = kr
    vt = lax.dot_general(wvt_ref[...], nkv, _NT, preferred_element_type=F32)
    half = vt.shape[1] // 2
    vt_out[0] = vt[:, :half].astype(BF16)
    vt_out[1] = vt[:, half:].astype(BF16)


def _prep_mla(h2d, hs, tabs, g_cq, g_ckv, wuq_p, wk, wvt, batch, seq, tm):
    m = batch * seq
    nt = seq // tm
    hd = MLA_HEADS * 2 * LANES
    return pl.pallas_call(
        _prep_mla_kernel,
        grid=(batch, nt),
        in_specs=[
            pl.BlockSpec((tm, MLA_Q_LORA), lambda b, i: (b * nt + i, COL_CQ // MLA_Q_LORA)),
            pl.BlockSpec((tm, MLA_KV_LORA), lambda b, i: (b * nt + i, COL_CKV // MLA_KV_LORA)),
            pl.BlockSpec((tm, LANES), lambda b, i: (b * nt + i, SMALL_KR // LANES)),
            pl.BlockSpec((tm, N_TABS * LANES), lambda b, i: (b * nt + i, 0)),
            pl.BlockSpec((1, MLA_Q_LORA), lambda b, i: (0, 0)),
            pl.BlockSpec((1, MLA_KV_LORA), lambda b, i: (0, 0)),
            pl.BlockSpec((MLA_Q_LORA, hd), lambda b, i: (0, 0)),
            pl.BlockSpec((MLA_KV_LORA, MLA_HEADS * MLA_NOPE), lambda b, i: (0, 0)),
            pl.BlockSpec((MLA_HEADS * MLA_V, MLA_KV_LORA), lambda b, i: (0, 0)),
        ],
        out_specs=[
            pl.BlockSpec((tm, hd), lambda b, i: (b * nt + i, 0)),
            pl.BlockSpec((tm, hd), lambda b, i: (b * nt + i, 0)),
            pl.BlockSpec((None, 2, MLA_HEADS * MLA_V, tm // 2), lambda b, i: (b, i, 0, 0)),
        ],
        out_shape=[
            jax.ShapeDtypeStruct((m, hd), BF16),
            jax.ShapeDtypeStruct((m, hd), BF16),
            jax.ShapeDtypeStruct((batch, 2 * nt, MLA_HEADS * MLA_V, tm // 2), BF16),
        ],
        compiler_params=_params(40 << 20, 2),
        name="prep_mla",
    )(h2d, h2d, hs, tabs, g_cq, g_ckv, wuq_p, wk, wvt)


def _softmax_step(s, col_max, vt_chunk, m, l, acc):
    m_new = jnp.maximum(m, col_max)
    alpha = jnp.exp2(m - m_new)
    p = jnp.exp2(s - m_new)
    l_new = alpha * l + jnp.sum(p, axis=0, keepdims=True)
    acc_new = alpha * acc + jnp.dot(vt_chunk, p.astype(BF16), preferred_element_type=F32)
    return m_new, l_new, acc_new


def _mla_attn_kernel(q_ref, k_ref, vt_ref, gate_ref, o_ref, s0_ref, s1_ref, *, tile, ch):
    i = pl.program_id(2)
    q = q_ref[...]
    slots = (s0_ref, s1_ref)

    def produce(c, slot, diag_offset=None):
        kc = k_ref[pl.ds(pl.multiple_of(c * ch, ch), ch), :]
        s = lax.dot_general(kc, q, _NT, preferred_element_type=F32)
        if diag_offset is not None:
            kidx = lax.broadcasted_iota(I32, (ch, tile), 0) + diag_offset
            qidx = lax.broadcasted_iota(I32, (ch, tile), 1)
            s = jnp.where(kidx <= qidx, s, NEG_BIG)
        slots[slot][...] = s
        return jnp.max(s, axis=0, keepdims=True)

    def consume(c, slot, col_max, carry):
        return _softmax_step(slots[slot][...], col_max, vt_ref[c], *carry)

    init = (jnp.full((1, tile), NEG_BIG, F32), jnp.zeros((1, tile), F32),
            jnp.zeros((MLA_V, tile), F32))
    diag = 2 * i
    col_a = produce(diag, 0, 0)
    col_b = produce(diag + 1, 1, ch)
    carry = consume(diag, 0, col_a, init)

    def pair(j, state):
        cr, col1, c1 = state
        col0 = produce(2 * j, 0)
        cr = consume(c1, 1, col1, cr)
        col1 = produce(2 * j + 1, 1)
        cr = consume(2 * j, 0, col0, cr)
        return cr, col1, 2 * j + 1

    state = lax.fori_loop(0, i // 2, lambda jj, st: pair(2 * jj + 1, pair(2 * jj, st)),
                          (carry, col_b, diag + 1))
    carry, col1, c1 = lax.fori_loop(0, i & 1, lambda _, st: pair(i - 1, st), state)
    _, l, acc = consume(c1, 1, col1, carry)
    o = (acc / l).T
    o_ref[...] = (o * _silu(gate_ref[...])).astype(o_ref.dtype)


def _mla_attn(q, k, vt, h2d, batch, seq, tile):
    m = batch * seq
    nt = seq // tile
    ch = tile // 2
    return pl.pallas_call(
        functools.partial(_mla_attn_kernel, tile=tile, ch=ch),
        grid=(batch, MLA_HEADS, nt),
        in_specs=[
            pl.BlockSpec((tile, 2 * LANES), lambda b, h, i: (b * nt + i, h)),
            pl.BlockSpec((seq, 2 * LANES), lambda b, h, i: (b, h)),
            pl.BlockSpec((None, 2 * nt, MLA_V, ch), lambda b, h, i: (b, 0, h, 0)),
            pl.BlockSpec((tile, LANES), lambda b, h, i: (b * nt + i, COL_AG // LANES + h)),
        ],
        out_specs=pl.BlockSpec((tile, LANES), lambda b, h, i: (b * nt + i, h)),
        out_shape=jax.ShapeDtypeStruct((m, GROUP_WIDTH), BF16),
        scratch_shapes=[pltpu.VMEM((ch, tile), F32), pltpu.VMEM((ch, tile), F32)],
        compiler_params=_params(32 << 20, 3),
        name="mla_attn",
    )(q, k, vt, h2d)


def _hi_lo(v):
    hi = v.astype(BF16).astype(F32)
    return hi, v - hi


def _prep_dsa_kernel(q_ref, k_ref, v_ref, qi_ref, ki_ref, tab_ref,
                     q_out, k_out, vt_out, qi4_out, ki4_out, wt_out):
    def tab(t):
        return tab_ref[:, t * LANES:(t + 1) * LANES]

    scale = DSA_HEAD_DIM ** -0.5 * LOG2E
    q = q_ref[...]
    q_out[...] = (_rope(q, _tile_lanes(tab(TAB_DSA_C), DSA_HEADS), _tile_lanes(tab(TAB_DSA_S), DSA_HEADS),
                        DSA_ROT // 2, LANES) * scale).astype(BF16)
    k = k_ref[...]
    k_out[...] = _rope(k, _tile_lanes(tab(TAB_DSA_C), DSA_KV_HEADS), _tile_lanes(tab(TAB_DSA_S), DSA_KV_HEADS),
                       DSA_ROT // 2, LANES).astype(BF16)
    vt_out[...] = v_ref[...].T.astype(BF16)
    reps = IDX_HEADS * IDX_DIM // LANES
    qi = _rope(qi_ref[...], _tile_lanes(tab(TAB_IDX_C), reps), _tile_lanes(tab(TAB_IDX_S), reps),
               IDX_ROT // 2, IDX_DIM)
    qb = DSA_QBLK
    first_half = lax.broadcasted_iota(I32, (qb, LANES), 1) < IDX_DIM
    for blk in range(q.shape[0] // qb):
        for pair in range(IDX_HEADS // 2):
            hi, lo = _hi_lo(qi[blk * qb:(blk + 1) * qb, pair * LANES:(pair + 1) * LANES])
            even = jnp.where(first_half, hi, pltpu.roll(lo, IDX_DIM, 1)).astype(BF16)
            odd = jnp.where(first_half, pltpu.roll(hi, IDX_DIM, 1), lo).astype(BF16)
            for half in range(2):
                qi4_out[blk, pair, 0:qb, half * LANES:(half + 1) * LANES] = even
                qi4_out[blk, pair, qb:2 * qb, half * LANES:(half + 1) * LANES] = odd
    small = ki_ref[...]
    lane = lax.broadcasted_iota(I32, small.shape, 1)
    c_ki = jnp.where(lane < IDX_DIM, tab(TAB_IDX_C), 1.0)
    s_ki = jnp.where(lane < IDX_DIM, tab(TAB_IDX_S), 0.0)
    hi, lo = _hi_lo(_rope(small, c_ki, s_ki, IDX_ROT // 2, IDX_DIM))
    ki4_out[:, 0:LANES] = jnp.where(lane < IDX_DIM, hi, pltpu.roll(hi, IDX_DIM, 1)).astype(BF16)
    ki4_out[:, LANES:2 * LANES] = jnp.where(lane < IDX_DIM, lo, pltpu.roll(lo, IDX_DIM, 1)).astype(BF16)
    wt = small.T[IDX_DIM:IDX_DIM + IDX_HEADS, :]
    wt_out[...] = wt * (IDX_HEADS ** -0.5 * IDX_DIM ** -0.5)


def _prep_dsa(h2d, hs, tabs, batch, seq):
    tm = KEY_CHUNK
    m = batch * seq
    nt = seq // tm
    qpt = tm // DSA_QBLK
    kvw = DSA_KV_HEADS * DSA_HEAD_DIM

    def row(b, i):
        return b * nt + i

    return pl.pallas_call(
        _prep_dsa_kernel,
        grid=(batch, nt),
        in_specs=[
            pl.BlockSpec((tm, GROUP_WIDTH), lambda b, i: (row(b, i), COL_BQ // GROUP_WIDTH)),
            pl.BlockSpec((tm, kvw), lambda b, i: (row(b, i), COL_BK // kvw)),
            pl.BlockSpec((tm, kvw), lambda b, i: (row(b, i), COL_BV // kvw)),
            pl.BlockSpec((tm, GROUP_WIDTH), lambda b, i: (row(b, i), COL_QI // GROUP_WIDTH)),
            pl.BlockSpec((tm, LANES), lambda b, i: (row(b, i), SMALL_KI // LANES)),
            pl.BlockSpec((tm, N_TABS * LANES), lambda b, i: (row(b, i), 0)),
        ],
        out_specs=[
            pl.BlockSpec((tm, GROUP_WIDTH), lambda b, i: (row(b, i), 0)),
            pl.BlockSpec((tm, kvw), lambda b, i: (row(b, i), 0)),
            pl.BlockSpec((None, None, kvw, tm), lambda b, i: (b, i, 0, 0)),
            pl.BlockSpec((None, qpt, IDX_HEADS // 2, 2 * DSA_QBLK, 4 * IDX_DIM), lambda b, i: (b, i, 0, 0, 0)),
            pl.BlockSpec((tm, 4 * IDX_DIM), lambda b, i: (row(b, i), 0)),
            pl.BlockSpec((None, IDX_HEADS, tm), lambda b, i: (b, 0, i)),
        ],
        out_shape=[
            jax.ShapeDtypeStruct((m, GROUP_WIDTH), BF16),
            jax.ShapeDtypeStruct((m, kvw), BF16),
            jax.ShapeDtypeStruct((batch, nt, kvw, tm), BF16),
            jax.ShapeDtypeStruct((batch, seq // DSA_QBLK, IDX_HEADS // 2, 2 * DSA_QBLK, 4 * IDX_DIM), BF16),
            jax.ShapeDtypeStruct((m, 4 * IDX_DIM), BF16),
            jax.ShapeDtypeStruct((batch, IDX_HEADS, seq), F32),
        ],
        compiler_params=_params(24 << 20, 2),
        name="prep_dsa",
    )(h2d, h2d, h2d, h2d, hs, tabs)


def _sortable_key(score):
    bits = lax.bitcast_convert_type(score, I32)
    key = jnp.where(bits < 0, bits ^ jnp.int32(0x7FFFFFFF), bits)
    return jnp.where(score == 0.0, 0, key)


def _dsa_kernel(qi4_ref, ki4_ref, wt_ref, q_ref, k_ref, vt_ref, gate_ref, o_ref, key_ref, s0_ref, s1_ref,
                *, topk, seq_bits):
    i = pl.program_id(1)
    n_vis = i // 2
    n_chunks = n_vis + 1
    wt = wt_ref[...]
    ch = KEY_CHUNK
    qb = DSA_QBLK

    def rows(c):
        return pl.ds(pl.multiple_of(c * ch, ch), ch)

    def score_chunk(c):
        kc = ki4_ref[rows(c), :]
        tot = None
        for p in range(IDX_HEADS // 2):
            s = lax.dot_general(kc, qi4_ref[p], _NT, preferred_element_type=F32)
            r = jnp.maximum(s, 0.0)
            t = r[:, :qb] * wt[2 * p:2 * p + 1, :] + r[:, qb:] * wt[2 * p + 1:2 * p + 2, :]
            tot = t if tot is None else tot + t
        return tot

    def phase1(c, carry):
        key_ref[rows(c), :] = _sortable_key(score_chunk(c))
        return carry

    lax.fori_loop(0, n_vis, phase1, 0)
    kidx = n_vis * ch + lax.broadcasted_iota(I32, (ch, qb), 0)
    qidx = i * qb + lax.broadcasted_iota(I32, (ch, qb), 1)
    key_ref[rows(n_vis), :] = jnp.where(kidx <= qidx, _sortable_key(score_chunk(n_vis)), INT_MIN)

    @pl.when((n_chunks & 1) == 1)
    def _():
        key_ref[rows(n_chunks), :] = jnp.full((ch, qb), INT_MIN, I32)

    n_pairs = (n_chunks + 1) // 2

    def count_ge(t):
        def body(p, acc):
            blk = key_ref[pl.ds(pl.multiple_of(p * 2 * ch, 2 * ch), 2 * ch), :]
            hit = (blk >= t).astype(I32)
            return acc + jnp.sum(hit.reshape(2 * ch // 8, 8, qb), axis=0)
        acc = lax.fori_loop(0, n_pairs, body, jnp.zeros((8, qb), I32))
        return jnp.sum(acc, axis=0, keepdims=True)

    thr = jnp.where(count_ge(jnp.zeros((1, qb), I32)) >= topk, 0, INT_MIN).astype(I32)

    def bit_step(b, t):
        cand = t + jnp.left_shift(jnp.int32(1), 30 - b)
        return jnp.where(count_ge(cand) >= topk, cand, t)

    thr = lax.fori_loop(0, 31, bit_step, thr)
    found = thr > INT_MIN
    c_ge = count_ge(jnp.maximum(thr, INT_MIN + 1))
    need = topk - count_ge(thr + 1)
    tied = jnp.logical_and(c_ge > topk, found)

    def count(pred):
        def body(c, acc):
            idx = c * ch + lax.broadcasted_iota(I32, (ch, qb), 0)
            hit = pred(key_ref[rows(c), :], idx).astype(I32)
            return acc + jnp.sum(hit.reshape(ch // 8, 8, qb), axis=0)
        acc = lax.fori_loop(0, n_chunks, body, jnp.zeros((8, qb), I32))
        return jnp.sum(acc, axis=0, keepdims=True)

    def tie_search():
        def step(b, lo):
            cand = lo + jnp.left_shift(jnp.int32(1), seq_bits - 1 - b)
            below = count(lambda keys, idx: jnp.logical_and(keys == thr, idx < cand))
            return jnp.where(below < need, cand, lo)
        return lax.fori_loop(0, seq_bits, step, jnp.zeros((1, qb), I32))

    last_eq = lax.cond(jnp.max(tied.astype(I32)) > 0, tie_search,
                       lambda: jnp.full((1, qb), 2 ** seq_bits, I32))
    thr_sel = jnp.maximum(thr, INT_MIN + 1)

    qs = [jnp.concatenate([q_ref[:, (n * DSA_GROUP + g) * LANES:(n * DSA_GROUP + g + 1) * LANES]
                           for g in range(DSA_GROUP)], axis=0) for n in range(DSA_KV_HEADS)]
    nq = DSA_GROUP * qb

    slots = (s0_ref, s1_ref)

    def produce(c, slot):
        keys = key_ref[rows(c), :]
        idx = c * ch + lax.broadcasted_iota(I32, (ch, qb), 0)
        drop = jnp.logical_and(keys == thr, idx > last_eq)
        sel = jnp.logical_and(keys >= thr_sel, jnp.logical_not(drop))
        neg = jnp.where(sel, 0.0, NEG_BIG)
        neg = jnp.concatenate([neg] * DSA_GROUP, axis=1)
        col_max = []
        for n in range(DSA_KV_HEADS):
            kc = k_ref[rows(c), n * LANES:(n + 1) * LANES]
            s = lax.dot_general(kc, qs[n], _NT, preferred_element_type=F32) + neg
            slots[slot][n] = s
            col_max.append(jnp.max(s, axis=0, keepdims=True))
        return tuple(col_max)

    def consume(c, slot, col_max, carry):
        return tuple(_softmax_step(slots[slot][n], col_max[n],
                                   vt_ref[c, n * LANES:(n + 1) * LANES, :], *carry[n])
                     for n in range(DSA_KV_HEADS))

    def phase3(j, state):
        carry, col0 = state
        col1 = produce(2 * j + 1, 1)
        carry = consume(2 * j, 0, col0, carry)
        col0 = produce(2 * j + 2, 0)
        carry = consume(2 * j + 1, 1, col1, carry)
        return carry, col0

    init = tuple((jnp.full((1, nq), NEG_BIG, F32), jnp.zeros((1, nq), F32),
                  jnp.zeros((DSA_HEAD_DIM, nq), F32)) for _ in range(DSA_KV_HEADS))
    pairs = (n_chunks - 1) // 2
    carry, col0 = lax.fori_loop(0, pairs, phase3, (init, produce(0, 0)))
    last = 2 * pairs

    def tail_two(cr):
        col1 = produce(last + 1, 1)
        cr = consume(last, 0, col0, cr)
        return consume(last + 1, 1, col1, cr)

    res = lax.cond(n_chunks - 1 > last, tail_two, lambda cr: consume(last, 0, col0, cr), carry)
    for n in range(DSA_KV_HEADS):
        _, l, acc = res[n]
        o = (acc / l).T
        for g in range(DSA_GROUP):
            col = (n * DSA_GROUP + g) * LANES
            o_ref[:, col:col + LANES] = (o[g * qb:(g + 1) * qb, :]
                                         * _silu(gate_ref[:, col:col + LANES])).astype(o_ref.dtype)


def _dsa(qi4, ki4, wt, q, k, vt, h2d, batch, seq):
    m = batch * seq
    nq = seq // DSA_QBLK
    nck = seq // KEY_CHUNK
    kvw = DSA_KV_HEADS * DSA_HEAD_DIM
    topk = min(TOPK_MAX, seq // 4)
    seq_bits = int(np.log2(seq))
    assert 2 ** seq_bits == seq and nck % 2 == 0
    return pl.pallas_call(
        functools.partial(_dsa_kernel, topk=topk, seq_bits=seq_bits),
        grid=(batch, nq),
        in_specs=[
            pl.BlockSpec((None, None, IDX_HEADS // 2, 2 * DSA_QBLK, 4 * IDX_DIM),
                         lambda b, i: (b, i, 0, 0, 0)),
            pl.BlockSpec((seq, 4 * IDX_DIM), lambda b, i: (b, 0)),
            pl.BlockSpec((None, IDX_HEADS, DSA_QBLK), lambda b, i: (b, 0, i)),
            pl.BlockSpec((DSA_QBLK, GROUP_WIDTH), lambda b, i: (b * nq + i, 0)),
            pl.BlockSpec((seq, kvw), lambda b, i: (b, 0)),
            pl.BlockSpec((None, nck, kvw, KEY_CHUNK), lambda b, i: (b, 0, 0, 0)),
            pl.BlockSpec((DSA_QBLK, GROUP_WIDTH), lambda b, i: (b * nq + i, COL_BG // GROUP_WIDTH)),
        ],
        out_specs=pl.BlockSpec((DSA_QBLK, GROUP_WIDTH), lambda b, i: (b * nq + i, 0)),
        out_shape=jax.ShapeDtypeStruct((m, GROUP_WIDTH), BF16),
        scratch_shapes=[pltpu.VMEM((seq, DSA_QBLK), I32),
                        pltpu.VMEM((DSA_KV_HEADS, KEY_CHUNK, DSA_GROUP * DSA_QBLK), F32),
                        pltpu.VMEM((DSA_KV_HEADS, KEY_CHUNK, DSA_GROUP * DSA_QBLK), F32)],
        compiler_params=_params(32 << 20, 2),
        name="dsa",
    )(qi4, ki4, wt, q, k, vt, h2d)


def _rglru_kernel(u_ref, gate_ref, cw_ref, cb_ref, wa_ref, ba_ref, wx_ref, bx_ref, lam_ref,
                  o_ref, halo_ref, h_ref, a_scr, x_scr, y_scr, *, batch, tt):
    t = pl.program_id(0)

    @pl.when(t == 0)
    def _():
        halo_ref[...] = jnp.zeros_like(halo_ref)
        h_ref[...] = jnp.zeros_like(h_ref)

    lam = lam_ref[...]
    z = -lam
    softplus = jnp.maximum(z, 0.0) + jnp.log1p(jnp.exp(-jnp.abs(z)))
    row = lax.broadcasted_iota(I32, (tt, LRU_WIDTH), 0)
    for b in range(batch):
        u = u_ref[b]
        prev = halo_ref[b]
        p1, p2, p3 = prev[7:8, :], prev[6:7, :], prev[5:6, :]
        s1 = jnp.where(row == 0, p1, pltpu.roll(u, 1, 0))
        s2 = jnp.where(row == 0, p2, jnp.where(row == 1, p1, pltpu.roll(u, 2, 0)))
        s3 = jnp.where(row == 0, p3, jnp.where(row == 1, p2, jnp.where(row == 2, p1, pltpu.roll(u, 3, 0))))
        conv = (cw_ref[3:4, :] * u + cw_ref[2:3, :] * s1 + cw_ref[1:2, :] * s2 + cw_ref[0:1, :] * s3
                + cb_ref[...])
        halo_ref[b] = u[tt - 8:, :]
        cb16 = conv.astype(BF16)
        r = jax.nn.sigmoid(jnp.dot(cb16, wa_ref[...], preferred_element_type=F32) + ba_ref[...])
        ig = jax.nn.sigmoid(jnp.dot(cb16, wx_ref[...], preferred_element_type=F32) + bx_ref[...])
        log_a = -LRU_C * r * softplus
        a_scr[b] = jnp.exp(log_a)
        th = jnp.tanh(log_a)
        x_scr[b] = jnp.sqrt(-2.0 * th / (1.0 - th)) * (ig * conv)

    def step(s, h):
        h = a_scr[:, pl.ds(s, 1), :] * h + x_scr[:, pl.ds(s, 1), :]
        y_scr[:, pl.ds(s, 1), :] = h
        return h

    h_ref[...] = lax.fori_loop(0, tt, step, h_ref[...], unroll=8)
    o_ref[...] = (y_scr[...] * _silu(gate_ref[...])).astype(o_ref.dtype)


def _rglru(h3d, conv_w, conv_b, wa_bd, b_a, wx_bd, b_x, lam, batch, seq, tt):
    nt = seq // tt
    w = LRU_WIDTH

    def vec(r):
        return pl.BlockSpec((r, w), lambda t: (0, 0))

    return pl.pallas_call(
        functools.partial(_rglru_kernel, batch=batch, tt=tt),
        grid=(nt,),
        in_specs=[
            pl.BlockSpec((batch, tt, w), lambda t: (0, t, COL_U // w)),
            pl.BlockSpec((batch, tt, w), lambda t: (0, t, COL_CG // w)),
            vec(CONV_WIDTH), vec(1),
            pl.BlockSpec((w, w), lambda t: (0, 0)), vec(1),
            pl.BlockSpec((w, w), lambda t: (0, 0)), vec(1),
            vec(1),
        ],
        out_specs=pl.BlockSpec((batch, tt, w), lambda t: (0, t, 0)),
        out_shape=jax.ShapeDtypeStruct((batch, seq, w), BF16),
        scratch_shapes=[pltpu.VMEM((batch, 8, w), F32), pltpu.VMEM((batch, 1, w), F32),
                        pltpu.VMEM((batch, tt, w), F32), pltpu.VMEM((batch, tt, w), F32),
                        pltpu.VMEM((batch, tt, w), F32)],
        compiler_params=_params(32 << 20, 1),
        name="rglru",
    )(h3d, h3d, conv_w, conv_b, wa_bd, b_a, wx_bd, b_x, lam)


def _mem_attn_kernel(q_ref, gate_ref, km_ref, vm_ref, o_ref):
    scale = MEM_HEAD_DIM ** -0.5
    for h in range(MEM_HEADS):
        cols = slice(h * MEM_HEAD_DIM, (h + 1) * MEM_HEAD_DIM)
        qh = (q_ref[:, cols] * scale).astype(BF16)
        s = lax.dot_general(qh, km_ref[:, cols], _NT, preferred_element_type=F32)
        p = jnp.exp(s - jnp.max(s, axis=-1, keepdims=True))
        l = jnp.sum(p, axis=-1, keepdims=True)
        o = jnp.dot(p.astype(BF16), vm_ref[:, cols], preferred_element_type=F32) / l
        o_ref[:, cols] = (o * _silu(gate_ref[:, cols])).astype(o_ref.dtype)


def _mem_attn(h2d, km, vm, batch, seq, tm):
    m = batch * seq
    nt = seq // tm
    w = GROUP_WIDTH
    return pl.pallas_call(
        _mem_attn_kernel,
        grid=(batch, nt),
        in_specs=[
            pl.BlockSpec((tm, w), lambda b, i: (b * nt + i, COL_DQ // w)),
            pl.BlockSpec((tm, w), lambda b, i: (b * nt + i, COL_DG // w)),
            pl.BlockSpec((N_MEM, w), lambda b, i: (b, 0)),
            pl.BlockSpec((N_MEM, w), lambda b, i: (b, 0)),
        ],
        out_specs=pl.BlockSpec((tm, w), lambda b, i: (b * nt + i, 0)),
        out_shape=jax.ShapeDtypeStruct((m, w), BF16),
        compiler_params=_params(24 << 20, 2),
        name="mem_attn",
    )(h2d, h2d, km, vm)


def _deepnorm_kernel(x_ref, y_ref, g_ref, b_ref, o_ref, o16_ref):
    z = DEEPNORM_ALPHA * x_ref[...] + y_ref[...]
    mu = jnp.mean(z, axis=-1, keepdims=True)
    d = z - mu
    var = jnp.mean(d * d, axis=-1, keepdims=True)
    out = d * lax.rsqrt(var + LN_EPS) * g_ref[...] + b_ref[...]
    o_ref[...] = out
    o16_ref[...] = out.astype(BF16)


def _deepnorm(x, y, g, b, tm):
    m, d = x.shape
    return pl.pallas_call(
        _deepnorm_kernel,
        grid=(m // tm,),
        in_specs=[pl.BlockSpec((tm, d), lambda i: (i, 0)), pl.BlockSpec((tm, d), lambda i: (i, 0)),
                  pl.BlockSpec((1, d), lambda i: (0, 0)), pl.BlockSpec((1, d), lambda i: (0, 0))],
        out_specs=[pl.BlockSpec((tm, d), lambda i: (i, 0)), pl.BlockSpec((tm, d), lambda i: (i, 0))],
        out_shape=[jax.ShapeDtypeStruct((m, d), F32), jax.ShapeDtypeStruct((m, d), BF16)],
        compiler_params=_params(40 << 20, 1),
        name="deepnorm",
    )(x, y, g, b)


def _rope_tabs(positions):
    pos = positions.astype(F32)
    b, s = pos.shape

    def one(rot, period):
        inv = ROPE_THETA ** (-jnp.arange(0, rot, 2, dtype=F32) / rot)
        ang = pos[:, :, None] * inv
        c, sn = jnp.cos(ang), jnp.sin(ang)
        rest = period - rot
        cp = jnp.concatenate([c, c, jnp.ones((b, s, rest), F32)], axis=-1)
        sp = jnp.concatenate([-sn, sn, jnp.zeros((b, s, rest), F32)], axis=-1)
        reps = LANES // period
        return jnp.tile(cp, (1, 1, reps)), jnp.tile(sp, (1, 1, reps))

    parts = one(MLA_ROPE, LANES) + one(DSA_ROT, LANES) + one(IDX_ROT, IDX_DIM)
    return jnp.concatenate(parts, axis=-1).reshape(b * s, N_TABS * LANES)


def _block_diag(w):
    l = w.shape[0]
    eye = jnp.eye(LRU_BLOCKS, dtype=w.dtype)
    bd = jnp.einsum('lnde,nm->lndme', w, eye)
    return bd.reshape(l, LRU_WIDTH, LRU_WIDTH).astype(BF16)


def _layer(x, x16, mem16, tabs, lw, l, batch, seq):
    m = batch * seq
    main_rows, small_rows = _in_proj_row_tables(256)
    h2d = _in_proj(x16, lw["w_in_t"], l, main_rows, tm=min(2048, m), tn=256, name="in_proj")
    hs = _in_proj(x16, lw["w_in_t"], l, small_rows, tm=min(2048, m), tn=LANES, name="in_proj_small")
    tile_a = min(512, seq)
    qa, ka, vta = _prep_mla(h2d, hs, tabs, lw["g_cq"], lw["g_ckv"], lw["w_uq"], lw["w_uk"], lw["w_uvt"],
                            batch, seq, tile_a)
    ya = _mla_attn(qa, ka, vta, h2d, batch, seq, tile_a)
    qb, kb, vtb, qi4, ki4, wt = _prep_dsa(h2d, hs, tabs, batch, seq)
    yb = _dsa(qi4, ki4, wt, qb, kb, vtb, h2d, batch, seq)
    yc = _rglru(h2d.reshape(batch, seq, N_MAIN), lw["conv_w"], lw["conv_b"], lw["w_rg_a"], lw["b_rg_a"],
                lw["w_rg_x"], lw["b_rg_x"], lw["lam"], batch, seq, min(256, seq))
    yc = yc.reshape(m, GROUP_WIDTH)
    km = _matmul([mem16], lw["w_mem_k"], l, BF16, tm=mem16.shape[0], tn=256, name="mem_proj_k")
    vm = _matmul([mem16], lw["w_mem_v"], l, BF16, tm=mem16.shape[0], tn=256, name="mem_proj_v")
    yd = _mem_attn(h2d, km, vm, batch, seq, min(512, seq))
    y = _matmul([ya, yb, yc, yd], lw["w_o"], l, F32, tm=min(1024, m), tn=256, name="out_proj")
    return _deepnorm(x, y, lw["ln_g"], lw["ln_b"], min(256, m))


def kernel(x, mem, positions, w_in, g_cq, g_ckv, w_uq, w_ukv, conv_w, conv_b, w_rg_a, b_rg_a, w_rg_x,
           b_rg_x, lru_lambda, w_mem_k, w_mem_v, w_o, ln_g, ln_b):
    batch, seq, d = x.shape
    depth = w_in.shape[0]
    tabs = _rope_tabs(positions)
    w_in_t = w_in.transpose(0, 2, 1)
    w_uq_p = jnp.pad(w_uq.reshape(depth, MLA_Q_LORA, MLA_HEADS, MLA_NOPE + MLA_ROPE),
                     [(0, 0), (0, 0), (0, 0), (0, 2 * LANES - MLA_NOPE - MLA_ROPE)])
    w_uq_p = w_uq_p.reshape(depth, MLA_Q_LORA, MLA_HEADS * 2 * LANES).astype(BF16)
    w_ukv4 = w_ukv.reshape(depth, MLA_KV_LORA, MLA_HEADS, MLA_NOPE + MLA_V)
    w_uk = w_ukv4[..., :MLA_NOPE].reshape(depth, MLA_KV_LORA, MLA_HEADS * MLA_NOPE).astype(BF16)
    w_uvt = w_ukv4[..., MLA_NOPE:].reshape(depth, MLA_KV_LORA, MLA_HEADS * MLA_V)
    w_uvt = w_uvt.transpose(0, 2, 1).astype(BF16)
    wa_bd = _block_diag(w_rg_a)
    wx_bd = _block_diag(w_rg_x)
    mem16 = mem.reshape(batch * mem.shape[1], d).astype(BF16)

    xf = x.reshape(batch * seq, d)
    x16 = xf.astype(BF16)
    for l in range(depth):
        lw = dict(w_in_t=w_in_t, g_cq=g_cq[l][None], g_ckv=g_ckv[l][None], w_uq=w_uq_p[l], w_uk=w_uk[l],
                  w_uvt=w_uvt[l], conv_w=conv_w[l], conv_b=conv_b[l][None], w_rg_a=wa_bd[l],
                  b_rg_a=b_rg_a[l][None], w_rg_x=wx_bd[l], b_rg_x=b_rg_x[l][None], lam=lru_lambda[l][None],
                  w_mem_k=w_mem_k, w_mem_v=w_mem_v, w_o=w_o, ln_g=ln_g[l][None], ln_b=ln_b[l][None])
        xf, x16 = _layer(xf, x16, mem16, tabs, lw, l, batch, seq)
    return xf.reshape(batch, seq, d)
```

```python
import functools

import numpy as np
import jax
import jax.numpy as jnp
from jax import lax
from jax.experimental import pallas as pl
from jax.experimental.pallas import tpu as pltpu

F32 = jnp.float32
BF16 = jnp.bfloat16
I32 = jnp.int32

DEPTH = 4
D_MODEL = 4096
N_MEM = 256
GROUP_WIDTH = D_MODEL // 4
ROPE_THETA = 500000.0
MLA_HEADS = 8
MLA_NOPE = 128
MLA_ROPE = 64
MLA_V = 128
MLA_Q_LORA = GROUP_WIDTH
MLA_KV_LORA = GROUP_WIDTH // 2
DSA_HEADS = 8
DSA_KV_HEADS = 2
DSA_GROUP = DSA_HEADS // DSA_KV_HEADS
DSA_HEAD_DIM = 128
DSA_ROT = DSA_HEAD_DIM // 4
IDX_HEADS = 16
IDX_DIM = 64
IDX_ROT = IDX_DIM // 4
TOPK_MAX = 256
LRU_WIDTH = GROUP_WIDTH
LRU_BLOCKS = 16
LRU_BLOCK_DIM = LRU_WIDTH // LRU_BLOCKS
CONV_WIDTH = 4
LRU_C = 8.0
MEM_HEADS = 4
MEM_HEAD_DIM = GROUP_WIDTH // MEM_HEADS
DEEPNORM_ALPHA = (2 * DEPTH) ** 0.25
LN_EPS = 1e-5
RMS_EPS = 1e-6

IN_SIZES = (
    MLA_Q_LORA, MLA_KV_LORA, MLA_ROPE, GROUP_WIDTH,
    GROUP_WIDTH, DSA_KV_HEADS * DSA_HEAD_DIM, DSA_KV_HEADS * DSA_HEAD_DIM,
    IDX_HEADS * IDX_DIM, IDX_DIM, IDX_HEADS, GROUP_WIDTH,
    LRU_WIDTH, GROUP_WIDTH,
    GROUP_WIDTH, GROUP_WIDTH,
)

LANES = 128
V7X_VMEM_BUDGET = 60 * 1024 * 1024

COL_CQ = 0
COL_AG = 1024
COL_BQ = 2048
COL_QI = 3072
COL_BG = 4096
COL_U = 5120
COL_CG = 6144
COL_DQ = 7168
COL_DG = 8192
COL_CKV = 9216
COL_BK = 9728
COL_BV = 9984
N_MAIN = 10240
SMALL_KR = 0
SMALL_KI = 128

TAB_MLA_C, TAB_MLA_S, TAB_DSA_C, TAB_DSA_S, TAB_IDX_C, TAB_IDX_S = range(6)
N_TABS = 6

NEG_BIG = -1e30
INT_MIN = -2 ** 31
LOG2E = 1.4426950408889634
KEY_CHUNK = 256
DSA_QBLK = 128
VT_ROWS = 128 + 16
_NT = (((1,), (1,)), ((), ()))


def _vmem_limit(nbytes):
    return int(min(V7X_VMEM_BUDGET, max(nbytes, 16 * 1024 * 1024)))


def _params(nbytes, ndims):
    return pltpu.CompilerParams(dimension_semantics=("arbitrary",) * ndims,
                                vmem_limit_bytes=_vmem_limit(nbytes))


def _mm_kernel(*refs, n_lhs, kg):
    w_ref = refs[n_lhs]
    o_ref = refs[n_lhs + 1]
    acc = None
    for g in range(n_lhs):
        part = jnp.dot(refs[g][...], w_ref[g * kg:(g + 1) * kg, :].astype(BF16), preferred_element_type=F32)
        acc = part if acc is None else acc + part
    o_ref[...] = acc.astype(o_ref.dtype)


def _matmul(lhs_list, w, layer, out_dtype, tm, tn, name):
    m, kg = lhs_list[0].shape
    _, k, n = w.shape
    n_lhs = len(lhs_list)
    assert k == kg * n_lhs and m % tm == 0 and n % tn == 0
    est = 2 * (tm * k * 2 + k * tn * w.dtype.itemsize + tm * tn * jnp.dtype(out_dtype).itemsize) + tm * tn * 8
    return pl.pallas_call(
        functools.partial(_mm_kernel, n_lhs=n_lhs, kg=kg),
        grid=(m // tm, n // tn),
        in_specs=[pl.BlockSpec((tm, kg), lambda i, j: (i, 0)) for _ in range(n_lhs)]
        + [pl.BlockSpec((None, k, tn), lambda i, j: (layer, 0, j))],
        out_specs=pl.BlockSpec((tm, tn), lambda i, j: (i, j)),
        out_shape=jax.ShapeDtypeStruct((m, n), out_dtype),
        compiler_params=_params(est + (8 << 20), 2),
        name=name,
    )(*lhs_list, w)


def _in_proj_kernel(tab_ref, x_ref, w_ref, o_ref):
    del tab_ref
    o_ref[...] = lax.dot_general(x_ref[...], w_ref[...].astype(BF16), _NT, preferred_element_type=F32)


def _in_proj(x16, w_t, layer, row_offsets, tm, tn, name):
    m, k = x16.shape
    nblk = len(row_offsets)
    est = 2 * (tm * k * 2 + tn * k * 4 + tm * tn * 4) + tm * tn * 8
    grid_spec = pltpu.PrefetchScalarGridSpec(
        num_scalar_prefetch=1,
        grid=(m // tm, nblk),
        in_specs=[
            pl.BlockSpec((tm, k), lambda i, j, tab: (i, 0)),
            pl.BlockSpec((pl.Squeezed(), pl.Element(tn), pl.Element(k)),
                         lambda i, j, tab: (layer, pl.multiple_of(tab[j], 16), 0)),
        ],
        out_specs=pl.BlockSpec((tm, tn), lambda i, j, tab: (i, j)),
    )
    return pl.pallas_call(
        _in_proj_kernel,
        grid_spec=grid_spec,
        out_shape=jax.ShapeDtypeStruct((m, nblk * tn), F32),
        compiler_params=_params(est + (8 << 20), 2),
        name=name,
    )(jnp.asarray(row_offsets, I32), x16, w_t)


def _in_proj_row_tables(tn):
    names = ("cq", "ckv", "kr", "ag", "bq", "bk", "bv", "qi", "ki", "wi", "bg", "u", "cg", "dq", "dg")
    src = dict(zip(names, [0] + [int(c) for c in np.cumsum(IN_SIZES)[:-1]]))
    size = dict(zip(names, IN_SIZES))
    order = [("cq", COL_CQ), ("ag", COL_AG), ("bq", COL_BQ), ("qi", COL_QI), ("bg", COL_BG), ("u", COL_U),
             ("cg", COL_CG), ("dq", COL_DQ), ("dg", COL_DG), ("ckv", COL_CKV), ("bk", COL_BK), ("bv", COL_BV)]
    main = []
    for n, dst in order:
        assert dst == len(main) * tn and size[n] % tn == 0 and src[n] % 16 == 0
        main += [src[n] + t * tn for t in range(size[n] // tn)]
    assert len(main) * tn == N_MAIN and src["wi"] == src["ki"] + size["ki"]
    return np.asarray(main, np.int32), np.asarray([src["kr"], src["ki"]], np.int32)


def _rope(v, c_tab, s_tab, half, period):
    width = v.shape[-1]
    lane = lax.broadcasted_iota(I32, v.shape, 1) & (period - 1)
    swapped = jnp.where(lane < half, pltpu.roll(v, width - half, 1), pltpu.roll(v, half, 1))
    return v * c_tab + swapped * s_tab


def _tile_lanes(t, reps):
    return t if reps == 1 else jnp.concatenate([t] * reps, axis=1)


def _rms(x, g):
    return x * lax.rsqrt(jnp.mean(x * x, axis=-1, keepdims=True) + RMS_EPS) * g


def _silu(g):
    return g * jax.nn.sigmoid(g)


def _prep_mla_kernel(cq_ref, ckv_ref, kr_ref, tab_ref, gcq_ref, gckv_ref, wuq_ref, wk_ref, wvt_ref,
                     q_out, k_out, vt_out):
    scale = (MLA_NOPE + MLA_ROPE) ** -0.5 * LOG2E
    nq = _rms(cq_ref[...], gcq_ref[...]).astype(BF16)
    nkv = _rms(ckv_ref[...], gckv_ref[...]).astype(BF16)
    c_tab = tab_ref[:, TAB_MLA_C * LANES:(TAB_MLA_C + 1) * LANES]
    s_tab = tab_ref[:, TAB_MLA_S * LANES:(TAB_MLA_S + 1) * LANES]
    qf = jnp.dot(nq, wuq_ref[...], preferred_element_type=F32) * scale
    kn = jnp.dot(nkv, wk_ref[...], preferred_element_type=F32)
    lane = lax.broadcasted_iota(I32, kr_ref.shape, 1)
    kr = jnp.where(lane < MLA_ROPE, kr_ref[...], 0.0)
    kr = _rope(kr, c_tab, s_tab, MLA_ROPE // 2, LANES).astype(BF16)
    for h in range(MLA_HEADS):
        base = h * 2 * LANES
        q_out[:, base:base + LANES] = qf[:, base:base + LANES].astype(BF16)
        q_out[:, base + LANES:base + 2 * LANES] = _rope(
            qf[:, base + LANES:base + 2 * LANES], c_tab, s_tab, MLA_ROPE // 2, LANES).astype(BF16)
        k_out[:, base:base + LANES] = kn[:, h * LANES:(h + 1) * LANES].astype(BF16)
        k_out[:, base + LANES:base + 2 * LANES] = kr
    vt = lax.dot_general(wvt_ref[...], nkv, _NT, preferred_element_type=F32)
    half = vt.shape[1] // 2
    vt_out[0] = _vt_with_ones(vt[:, :half], MLA_HEADS, MLA_V)
    vt_out[1] = _vt_with_ones(vt[:, half:], MLA_HEADS, MLA_V)


def _prep_mla(h2d, hs, tabs, g_cq, g_ckv, wuq_p, wk, wvt, batch, seq, tm):
    m = batch * seq
    nt = seq // tm
    hd = MLA_HEADS * 2 * LANES
    return pl.pallas_call(
        _prep_mla_kernel,
        grid=(batch, nt),
        in_specs=[
            pl.BlockSpec((tm, MLA_Q_LORA), lambda b, i: (b * nt + i, COL_CQ // MLA_Q_LORA)),
            pl.BlockSpec((tm, MLA_KV_LORA), lambda b, i: (b * nt + i, COL_CKV // MLA_KV_LORA)),
            pl.BlockSpec((tm, LANES), lambda b, i: (b * nt + i, SMALL_KR // LANES)),
            pl.BlockSpec((tm, N_TABS * LANES), lambda b, i: (b * nt + i, 0)),
            pl.BlockSpec((1, MLA_Q_LORA), lambda b, i: (0, 0)),
            pl.BlockSpec((1, MLA_KV_LORA), lambda b, i: (0, 0)),
            pl.BlockSpec((MLA_Q_LORA, hd), lambda b, i: (0, 0)),
            pl.BlockSpec((MLA_KV_LORA, MLA_HEADS * MLA_NOPE), lambda b, i: (0, 0)),
            pl.BlockSpec((MLA_HEADS * MLA_V, MLA_KV_LORA), lambda b, i: (0, 0)),
        ],
        out_specs=[
            pl.BlockSpec((tm, hd), lambda b, i: (b * nt + i, 0)),
            pl.BlockSpec((tm, hd), lambda b, i: (b * nt + i, 0)),
            pl.BlockSpec((None, 2, MLA_HEADS * VT_ROWS, tm // 2), lambda b, i: (b, i, 0, 0)),
        ],
        out_shape=[
            jax.ShapeDtypeStruct((m, hd), BF16),
            jax.ShapeDtypeStruct((m, hd), BF16),
            jax.ShapeDtypeStruct((batch, 2 * nt, MLA_HEADS * VT_ROWS, tm // 2), BF16),
        ],
        compiler_params=_params(40 << 20, 2),
        name="prep_mla",
    )(h2d, h2d, hs, tabs, g_cq, g_ckv, wuq_p, wk, wvt)


def _softmax_step(s, col_max, vt_chunk, m, acc):
    m_new = jnp.maximum(m, col_max)
    alpha = jnp.exp2(m - m_new)
    p = jnp.exp2(s - m_new)
    acc_new = alpha * acc + jnp.dot(vt_chunk, p.astype(BF16), preferred_element_type=F32)
    return m_new, acc_new


def _normalised_t(acc, dv):
    return (acc[:dv, :] / acc[dv:dv + 1, :]).T


def _vt_with_ones(vt, heads, dv):
    ones = jnp.ones((VT_ROWS - dv, vt.shape[1]), BF16)
    parts = []
    for h in range(heads):
        parts += [vt[h * dv:(h + 1) * dv, :].astype(BF16), ones]
    return jnp.concatenate(parts, axis=0)


def _mla_attn_kernel(q_ref, k_ref, vt_ref, gate_ref, o_ref, s0_ref, s1_ref, *, tile, ch):
    i = pl.program_id(2)
    q = q_ref[...]
    slots = (s0_ref, s1_ref)

    def produce(c, slot, diag_offset=None):
        kc = k_ref[pl.ds(pl.multiple_of(c * ch, ch), ch), :]
        s = lax.dot_general(kc, q, _NT, preferred_element_type=F32)
        if diag_offset is not None:
            kidx = lax.broadcasted_iota(I32, (ch, tile), 0) + diag_offset
            qidx = lax.broadcasted_iota(I32, (ch, tile), 1)
            s = jnp.where(kidx <= qidx, s, NEG_BIG)
        slots[slot][...] = s
        return jnp.max(s, axis=0, keepdims=True)

    def consume(c, slot, col_max, carry):
        return _softmax_step(slots[slot][...], col_max, vt_ref[c], *carry)

    init = (jnp.full((1, tile), NEG_BIG, F32), jnp.zeros((VT_ROWS, tile), F32))
    diag = 2 * i
    col_a = produce(diag, 0, 0)
    col_b = produce(diag + 1, 1, ch)
    carry = consume(diag, 0, col_a, init)

    def pair(j, state):
        cr, col1, c1 = state
        col0 = produce(2 * j, 0)
        cr = consume(c1, 1, col1, cr)
        col1 = produce(2 * j + 1, 1)
        cr = consume(2 * j, 0, col0, cr)
        return cr, col1, 2 * j + 1

    state = lax.fori_loop(0, i // 2, lambda jj, st: pair(2 * jj + 1, pair(2 * jj, st)),
                          (carry, col_b, diag + 1))
    carry, col1, c1 = lax.fori_loop(0, i & 1, lambda _, st: pair(i - 1, st), state)
    _, acc = consume(c1, 1, col1, carry)
    o = _normalised_t(acc, MLA_V)
    o_ref[...] = (o * _silu(gate_ref[...])).astype(o_ref.dtype)


def _mla_attn(q, k, vt, h2d, batch, seq, tile):
    m = batch * seq
    nt = seq // tile
    ch = tile // 2
    return pl.pallas_call(
        functools.partial(_mla_attn_kernel, tile=tile, ch=ch),
        grid=(batch, MLA_HEADS, nt),
        in_specs=[
            pl.BlockSpec((tile, 2 * LANES), lambda b, h, i: (b * nt + i, h)),
            pl.BlockSpec((seq, 2 * LANES), lambda b, h, i: (b, h)),
            pl.BlockSpec((None, 2 * nt, VT_ROWS, ch), lambda b, h, i: (b, 0, h, 0)),
            pl.BlockSpec((tile, LANES), lambda b, h, i: (b * nt + i, COL_AG // LANES + h)),
        ],
        out_specs=pl.BlockSpec((tile, LANES), lambda b, h, i: (b * nt + i, h)),
        out_shape=jax.ShapeDtypeStruct((m, GROUP_WIDTH), BF16),
        scratch_shapes=[pltpu.VMEM((ch, tile), F32), pltpu.VMEM((ch, tile), F32)],
        compiler_params=_params(32 << 20, 3),
        name="mla_attn",
    )(q, k, vt, h2d)


def _hi_lo(v):
    hi = v.astype(BF16).astype(F32)
    return hi, v - hi


def _prep_dsa_kernel(q_ref, k_ref, v_ref, qi_ref, ki_ref, tab_ref,
                     q_out, k_out, vt_out, qi4_out, ki4_out, wt_out):
    def tab(t):
        return tab_ref[:, t * LANES:(t + 1) * LANES]

    scale = DSA_HEAD_DIM ** -0.5 * LOG2E
    q = q_ref[...]
    q_out[...] = (_rope(q, _tile_lanes(tab(TAB_DSA_C), DSA_HEADS), _tile_lanes(tab(TAB_DSA_S), DSA_HEADS),
                        DSA_ROT // 2, LANES) * scale).astype(BF16)
    k = k_ref[...]
    k_out[...] = _rope(k, _tile_lanes(tab(TAB_DSA_C), DSA_KV_HEADS), _tile_lanes(tab(TAB_DSA_S), DSA_KV_HEADS),
                       DSA_ROT // 2, LANES).astype(BF16)
    vt_out[...] = _vt_with_ones(v_ref[...].T, DSA_KV_HEADS, DSA_HEAD_DIM)
    reps = IDX_HEADS * IDX_DIM // LANES
    qi = _rope(qi_ref[...], _tile_lanes(tab(TAB_IDX_C), reps), _tile_lanes(tab(TAB_IDX_S), reps),
               IDX_ROT // 2, IDX_DIM)
    qb = DSA_QBLK
    first_half = lax.broadcasted_iota(I32, (qb, LANES), 1) < IDX_DIM
    for blk in range(q.shape[0] // qb):
        for pair in range(IDX_HEADS // 2):
            hi, lo = _hi_lo(qi[blk * qb:(blk + 1) * qb, pair * LANES:(pair + 1) * LANES])
            even = jnp.where(first_half, hi, pltpu.roll(lo, IDX_DIM, 1)).astype(BF16)
            odd = jnp.where(first_half, pltpu.roll(hi, IDX_DIM, 1), lo).astype(BF16)
            for half in range(2):
                qi4_out[blk, pair, 0:qb, half * LANES:(half + 1) * LANES] = even
                qi4_out[blk, pair, qb:2 * qb, half * LANES:(half + 1) * LANES] = odd
    small = ki_ref[...]
    lane = lax.broadcasted_iota(I32, small.shape, 1)
    c_ki = jnp.where(lane < IDX_DIM, tab(TAB_IDX_C), 1.0)
    s_ki = jnp.where(lane < IDX_DIM, tab(TAB_IDX_S), 0.0)
    hi, lo = _hi_lo(_rope(small, c_ki, s_ki, IDX_ROT // 2, IDX_DIM))
    ki4_out[:, 0:LANES] = jnp.where(lane < IDX_DIM, hi, pltpu.roll(hi, IDX_DIM, 1)).astype(BF16)
    ki4_out[:, LANES:2 * LANES] = jnp.where(lane < IDX_DIM, lo, pltpu.roll(lo, IDX_DIM, 1)).astype(BF16)
    wt = small.T[IDX_DIM:IDX_DIM + IDX_HEADS, :]
    wt_out[...] = wt * (IDX_HEADS ** -0.5 * IDX_DIM ** -0.5)


def _prep_dsa(h2d, hs, tabs, batch, seq):
    tm = KEY_CHUNK
    m = batch * seq
    nt = seq // tm
    qpt = tm // DSA_QBLK
    kvw = DSA_KV_HEADS * DSA_HEAD_DIM

    def row(b, i):
        return b * nt + i

    return pl.pallas_call(
        _prep_dsa_kernel,
        grid=(batch, nt),
        in_specs=[
            pl.BlockSpec((tm, GROUP_WIDTH), lambda b, i: (row(b, i), COL_BQ // GROUP_WIDTH)),
            pl.BlockSpec((tm, kvw), lambda b, i: (row(b, i), COL_BK // kvw)),
            pl.BlockSpec((tm, kvw), lambda b, i: (row(b, i), COL_BV // kvw)),
            pl.BlockSpec((tm, GROUP_WIDTH), lambda b, i: (row(b, i), COL_QI // GROUP_WIDTH)),
            pl.BlockSpec((tm, LANES), lambda b, i: (row(b, i), SMALL_KI // LANES)),
            pl.BlockSpec((tm, N_TABS * LANES), lambda b, i: (row(b, i), 0)),
        ],
        out_specs=[
            pl.BlockSpec((tm, GROUP_WIDTH), lambda b, i: (row(b, i), 0)),
            pl.BlockSpec((tm, kvw), lambda b, i: (row(b, i), 0)),
            pl.BlockSpec((None, None, DSA_KV_HEADS * VT_ROWS, tm), lambda b, i: (b, i, 0, 0)),
            pl.BlockSpec((None, qpt, IDX_HEADS // 2, 2 * DSA_QBLK, 4 * IDX_DIM), lambda b, i: (b, i, 0, 0, 0)),
            pl.BlockSpec((tm, 4 * IDX_DIM), lambda b, i: (row(b, i), 0)),
            pl.BlockSpec((None, IDX_HEADS, tm), lambda b, i: (b, 0, i)),
        ],
        out_shape=[
            jax.ShapeDtypeStruct((m, GROUP_WIDTH), BF16),
            jax.ShapeDtypeStruct((m, kvw), BF16),
            jax.ShapeDtypeStruct((batch, nt, DSA_KV_HEADS * VT_ROWS, tm), BF16),
            jax.ShapeDtypeStruct((batch, seq // DSA_QBLK, IDX_HEADS // 2, 2 * DSA_QBLK, 4 * IDX_DIM), BF16),
            jax.ShapeDtypeStruct((m, 4 * IDX_DIM), BF16),
            jax.ShapeDtypeStruct((batch, IDX_HEADS, seq), F32),
        ],
        compiler_params=_params(24 << 20, 2),
        name="prep_dsa",
    )(h2d, h2d, h2d, h2d, hs, tabs)


def _sortable_key(score):
    bits = lax.bitcast_convert_type(score, I32)
    key = jnp.where(bits < 0, bits ^ jnp.int32(0x7FFFFFFF), bits)
    return jnp.where(score == 0.0, 0, key)


def _dsa_kernel(qi4_ref, ki4_ref, wt_ref, q_ref, k_ref, vt_ref, gate_ref, o_ref, key_ref, s0_ref, s1_ref,
                *, topk, seq_bits):
    i = pl.program_id(1)
    n_vis = i // 2
    n_chunks = n_vis + 1
    wt = wt_ref[...]
    ch = KEY_CHUNK
    qb = DSA_QBLK

    def rows(c):
        return pl.ds(pl.multiple_of(c * ch, ch), ch)

    def score_chunk(c):
        kc = ki4_ref[rows(c), :]
        tot = None
        for p in range(IDX_HEADS // 2):
            s = lax.dot_general(kc, qi4_ref[p], _NT, preferred_element_type=F32)
            r = jnp.maximum(s, 0.0)
            t = r[:, :qb] * wt[2 * p:2 * p + 1, :] + r[:, qb:] * wt[2 * p + 1:2 * p + 2, :]
            tot = t if tot is None else tot + t
        return tot

    def phase1(c, carry):
        key_ref[rows(c), :] = _sortable_key(score_chunk(c))
        return carry

    lax.fori_loop(0, n_vis, phase1, 0)
    kidx = n_vis * ch + lax.broadcasted_iota(I32, (ch, qb), 0)
    qidx = i * qb + lax.broadcasted_iota(I32, (ch, qb), 1)
    key_ref[rows(n_vis), :] = jnp.where(kidx <= qidx, _sortable_key(score_chunk(n_vis)), INT_MIN)

    @pl.when((n_chunks & 1) == 1)
    def _():
        key_ref[rows(n_chunks), :] = jnp.full((ch, qb), INT_MIN, I32)

    n_pairs = (n_chunks + 1) // 2

    def count_ge(t):
        def body(p, acc):
            blk = key_ref[pl.ds(pl.multiple_of(p * 2 * ch, 2 * ch), 2 * ch), :]
            hit = (blk >= t).astype(I32)
            return acc + jnp.sum(hit.reshape(2 * ch // 8, 8, qb), axis=0)
        acc = lax.fori_loop(0, n_pairs, body, jnp.zeros((8, qb), I32))
        return jnp.sum(acc, axis=0, keepdims=True)

    thr = jnp.where(count_ge(jnp.zeros((1, qb), I32)) >= topk, 0, INT_MIN).astype(I32)

    def bit_step(b, t):
        cand = t + jnp.left_shift(jnp.int32(1), 30 - b)
        return jnp.where(count_ge(cand) >= topk, cand, t)

    thr = lax.fori_loop(0, 31, bit_step, thr)
    found = thr > INT_MIN
    c_ge = count_ge(jnp.maximum(thr, INT_MIN + 1))
    need = topk - count_ge(thr + 1)
    tied = jnp.logical_and(c_ge > topk, found)

    def count(pred):
        def body(c, acc):
            idx = c * ch + lax.broadcasted_iota(I32, (ch, qb), 0)
            hit = pred(key_ref[rows(c), :], idx).astype(I32)
            return acc + jnp.sum(hit.reshape(ch // 8, 8, qb), axis=0)
        acc = lax.fori_loop(0, n_chunks, body, jnp.zeros((8, qb), I32))
        return jnp.sum(acc, axis=0, keepdims=True)

    def tie_search():
        def step(b, lo):
            cand = lo + jnp.left_shift(jnp.int32(1), seq_bits - 1 - b)
            below = count(lambda keys, idx: jnp.logical_and(keys == thr, idx < cand))
            return jnp.where(below < need, cand, lo)
        return lax.fori_loop(0, seq_bits, step, jnp.zeros((1, qb), I32))

    last_eq = lax.cond(jnp.max(tied.astype(I32)) > 0, tie_search,
                       lambda: jnp.full((1, qb), 2 ** seq_bits, I32))
    thr_sel = jnp.maximum(thr, INT_MIN + 1)

    nq = DSA_GROUP * qb
    one_hot = (lax.broadcasted_iota(I32, (nq, qb), 0) % qb == lax.broadcasted_iota(I32, (nq, qb), 1))
    one_hot = jnp.where(one_hot, 1.0, 0.0).astype(BF16)
    qs = [jnp.concatenate([jnp.concatenate([q_ref[:, (n * DSA_GROUP + g) * LANES:(n * DSA_GROUP + g + 1) * LANES]
                                            for g in range(DSA_GROUP)], axis=0), one_hot], axis=1)
          for n in range(DSA_KV_HEADS)]

    slots = (s0_ref, s1_ref)

    def produce(c, slot):
        keys = key_ref[rows(c), :]
        idx = c * ch + lax.broadcasted_iota(I32, (ch, qb), 0)
        drop = jnp.logical_and(keys == thr, idx > last_eq)
        sel = jnp.logical_and(keys >= thr_sel, jnp.logical_not(drop))
        neg = jnp.where(sel, 0.0, NEG_BIG).astype(BF16)
        col_max = []
        for n in range(DSA_KV_HEADS):
            kc = jnp.concatenate([k_ref[rows(c), n * LANES:(n + 1) * LANES], neg], axis=1)
            s = lax.dot_general(kc, qs[n], _NT, preferred_element_type=F32)
            slots[slot][n] = s
            col_max.append(jnp.max(s, axis=0, keepdims=True))
        return tuple(col_max)

    def consume(c, slot, col_max, carry):
        return tuple(_softmax_step(slots[slot][n], col_max[n],
                                   vt_ref[c, n * VT_ROWS:(n + 1) * VT_ROWS, :], *carry[n])
                     for n in range(DSA_KV_HEADS))

    def phase3(j, state):
        carry, col0 = state
        col1 = produce(2 * j + 1, 1)
        carry = consume(2 * j, 0, col0, carry)
        col0 = produce(2 * j + 2, 0)
        carry = consume(2 * j + 1, 1, col1, carry)
        return carry, col0

    init = tuple((jnp.full((1, nq), NEG_BIG, F32), jnp.zeros((VT_ROWS, nq), F32))
                 for _ in range(DSA_KV_HEADS))
    pairs = (n_chunks - 1) // 2
    carry, col0 = lax.fori_loop(0, pairs, phase3, (init, produce(0, 0)))
    last = 2 * pairs

    def tail_two(cr):
        col1 = produce(last + 1, 1)
        cr = consume(last, 0, col0, cr)
        return consume(last + 1, 1, col1, cr)

    res = lax.cond(n_chunks - 1 > last, tail_two, lambda cr: consume(last, 0, col0, cr), carry)
    for n in range(DSA_KV_HEADS):
        o = _normalised_t(res[n][1], DSA_HEAD_DIM)
        for g in range(DSA_GROUP):
            col = (n * DSA_GROUP + g) * LANES
            o_ref[:, col:col + LANES] = (o[g * qb:(g + 1) * qb, :]
                                         * _silu(gate_ref[:, col:col + LANES])).astype(o_ref.dtype)


def _dsa(qi4, ki4, wt, q, k, vt, h2d, batch, seq):
    m = batch * seq
    nq = seq // DSA_QBLK
    nck = seq // KEY_CHUNK
    kvw = DSA_KV_HEADS * DSA_HEAD_DIM
    topk = min(TOPK_MAX, seq // 4)
    seq_bits = int(np.log2(seq))
    assert 2 ** seq_bits == seq and nck % 2 == 0
    return pl.pallas_call(
        functools.partial(_dsa_kernel, topk=topk, seq_bits=seq_bits),
        grid=(batch, nq),
        in_specs=[
            pl.BlockSpec((None, None, IDX_HEADS // 2, 2 * DSA_QBLK, 4 * IDX_DIM),
                         lambda b, i: (b, i, 0, 0, 0)),
            pl.BlockSpec((seq, 4 * IDX_DIM), lambda b, i: (b, 0)),
            pl.BlockSpec((None, IDX_HEADS, DSA_QBLK), lambda b, i: (b, 0, i)),
            pl.BlockSpec((DSA_QBLK, GROUP_WIDTH), lambda b, i: (b * nq + i, 0)),
            pl.BlockSpec((seq, kvw), lambda b, i: (b, 0)),
            pl.BlockSpec((None, nck, DSA_KV_HEADS * VT_ROWS, KEY_CHUNK), lambda b, i: (b, 0, 0, 0)),
            pl.BlockSpec((DSA_QBLK, GROUP_WIDTH), lambda b, i: (b * nq + i, COL_BG // GROUP_WIDTH)),
        ],
        out_specs=pl.BlockSpec((DSA_QBLK, GROUP_WIDTH), lambda b, i: (b * nq + i, 0)),
        out_shape=jax.ShapeDtypeStruct((m, GROUP_WIDTH), BF16),
        scratch_shapes=[pltpu.VMEM((seq, DSA_QBLK), I32),
                        pltpu.VMEM((DSA_KV_HEADS, KEY_CHUNK, DSA_GROUP * DSA_QBLK), F32),
                        pltpu.VMEM((DSA_KV_HEADS, KEY_CHUNK, DSA_GROUP * DSA_QBLK), F32)],
        compiler_params=_params(32 << 20, 2),
        name="dsa",
    )(qi4, ki4, wt, q, k, vt, h2d)


def _rglru_kernel(u_ref, gate_ref, cw_ref, cb_ref, wa_ref, ba_ref, wx_ref, bx_ref, lam_ref,
                  o_ref, halo_ref, h_ref, a_scr, x_scr, y_scr, *, batch, tt):
    t = pl.program_id(0)

    @pl.when(t == 0)
    def _():
        halo_ref[...] = jnp.zeros_like(halo_ref)
        h_ref[...] = jnp.zeros_like(h_ref)

    lam = lam_ref[...]
    z = -lam
    softplus = jnp.maximum(z, 0.0) + jnp.log1p(jnp.exp(-jnp.abs(z)))
    row = lax.broadcasted_iota(I32, (tt, LRU_WIDTH), 0)
    for b in range(batch):
        u = u_ref[b]
        prev = halo_ref[b]
        p1, p2, p3 = prev[7:8, :], prev[6:7, :], prev[5:6, :]
        s1 = jnp.where(row == 0, p1, pltpu.roll(u, 1, 0))
        s2 = jnp.where(row == 0, p2, jnp.where(row == 1, p1, pltpu.roll(u, 2, 0)))
        s3 = jnp.where(row == 0, p3, jnp.where(row == 1, p2, jnp.where(row == 2, p1, pltpu.roll(u, 3, 0))))
        conv = (cw_ref[3:4, :] * u + cw_ref[2:3, :] * s1 + cw_ref[1:2, :] * s2 + cw_ref[0:1, :] * s3
                + cb_ref[...])
        halo_ref[b] = u[tt - 8:, :]
        cb16 = conv.astype(BF16)
        r = jax.nn.sigmoid(jnp.dot(cb16, wa_ref[...], preferred_element_type=F32) + ba_ref[...])
        ig = jax.nn.sigmoid(jnp.dot(cb16, wx_ref[...], preferred_element_type=F32) + bx_ref[...])
        log_a = -LRU_C * r * softplus
        a_scr[b] = jnp.exp(log_a)
        th = jnp.tanh(log_a)
        x_scr[b] = jnp.sqrt(-2.0 * th / (1.0 - th)) * (ig * conv)

    def step(s, h):
        h = a_scr[:, pl.ds(s, 1), :] * h + x_scr[:, pl.ds(s, 1), :]
        y_scr[:, pl.ds(s, 1), :] = h
        return h

    h_ref[...] = lax.fori_loop(0, tt, step, h_ref[...], unroll=8)
    o_ref[...] = (y_scr[...] * _silu(gate_ref[...])).astype(o_ref.dtype)


def _rglru(h3d, conv_w, conv_b, wa_bd, b_a, wx_bd, b_x, lam, batch, seq, tt):
    nt = seq // tt
    w = LRU_WIDTH

    def vec(r):
        return pl.BlockSpec((r, w), lambda t: (0, 0))

    return pl.pallas_call(
        functools.partial(_rglru_kernel, batch=batch, tt=tt),
        grid=(nt,),
        in_specs=[
            pl.BlockSpec((batch, tt, w), lambda t: (0, t, COL_U // w)),
            pl.BlockSpec((batch, tt, w), lambda t: (0, t, COL_CG // w)),
            vec(CONV_WIDTH), vec(1),
            pl.BlockSpec((w, w), lambda t: (0, 0)), vec(1),
            pl.BlockSpec((w, w), lambda t: (0, 0)), vec(1),
            vec(1),
        ],
        out_specs=pl.BlockSpec((batch, tt, w), lambda t: (0, t, 0)),
        out_shape=jax.ShapeDtypeStruct((batch, seq, w), BF16),
        scratch_shapes=[pltpu.VMEM((batch, 8, w), F32), pltpu.VMEM((batch, 1, w), F32),
                        pltpu.VMEM((batch, tt, w), F32), pltpu.VMEM((batch, tt, w), F32),
                        pltpu.VMEM((batch, tt, w), F32)],
        compiler_params=_params(32 << 20, 1),
        name="rglru",
    )(h3d, h3d, conv_w, conv_b, wa_bd, b_a, wx_bd, b_x, lam)


def _mem_attn_kernel(q_ref, gate_ref, km_ref, vm_ref, o_ref):
    scale = MEM_HEAD_DIM ** -0.5
    for h in range(MEM_HEADS):
        cols = slice(h * MEM_HEAD_DIM, (h + 1) * MEM_HEAD_DIM)
        qh = (q_ref[:, cols] * scale).astype(BF16)
        s = lax.dot_general(qh, km_ref[:, cols], _NT, preferred_element_type=F32)
        p = jnp.exp(s - jnp.max(s, axis=-1, keepdims=True))
        l = jnp.sum(p, axis=-1, keepdims=True)
        o = jnp.dot(p.astype(BF16), vm_ref[:, cols], preferred_element_type=F32) / l
        o_ref[:, cols] = (o * _silu(gate_ref[:, cols])).astype(o_ref.dtype)


def _mem_attn(h2d, km, vm, batch, seq, tm):
    m = batch * seq
    nt = seq // tm
    w = GROUP_WIDTH
    return pl.pallas_call(
        _mem_attn_kernel,
        grid=(batch, nt),
        in_specs=[
            pl.BlockSpec((tm, w), lambda b, i: (b * nt + i, COL_DQ // w)),
            pl.BlockSpec((tm, w), lambda b, i: (b * nt + i, COL_DG // w)),
            pl.BlockSpec((N_MEM, w), lambda b, i: (b, 0)),
            pl.BlockSpec((N_MEM, w), lambda b, i: (b, 0)),
        ],
        out_specs=pl.BlockSpec((tm, w), lambda b, i: (b * nt + i, 0)),
        out_shape=jax.ShapeDtypeStruct((m, w), BF16),
        compiler_params=_params(24 << 20, 2),
        name="mem_attn",
    )(h2d, h2d, km, vm)


def _out_ln_kernel(*refs, n_lhs, kg, nj, per_emit, d_model):
    lhs = refs[:n_lhs]
    w_ref, x_ref, g_ref, b_ref, o_ref, o16_ref, z_ref, mu_ref, rs_ref = refs[n_lhs:]
    j = pl.program_id(1)

    @pl.when(j < nj)
    def _():
        acc = None
        for g in range(n_lhs):
            part = jnp.dot(lhs[g][...], w_ref[g * kg:(g + 1) * kg, :].astype(BF16), preferred_element_type=F32)
            acc = part if acc is None else acc + part
        z_ref[j] = DEEPNORM_ALPHA * x_ref[...] + acc

    @pl.when(j == nj)
    def _():
        def lane_tiles(v):
            tiles = [v[:, t * LANES:(t + 1) * LANES] for t in range(v.shape[1] // LANES)]
            return functools.reduce(lambda a, b: a + b, tiles)

        tot = functools.reduce(lambda a, b: a + b, [lane_tiles(z_ref[jj]) for jj in range(nj)])
        mu = jnp.sum(tot, axis=-1, keepdims=True) * (1.0 / d_model)
        sq = None
        for jj in range(nj):
            dlt = z_ref[jj] - mu
            part = lane_tiles(dlt * dlt)
            sq = part if sq is None else sq + part
        mu_ref[...] = mu
        rs_ref[...] = lax.rsqrt(jnp.sum(sq, axis=-1, keepdims=True) * (1.0 / d_model) + LN_EPS)

    @pl.when(j >= nj)
    def _():
        e = j - nj
        tn = z_ref.shape[2]
        for t in range(per_emit):
            cols = slice(t * tn, (t + 1) * tn)
            out = (z_ref[e * per_emit + t] - mu_ref[...]) * rs_ref[...] * g_ref[:, cols] + b_ref[:, cols]
            o_ref[:, cols] = out
            o16_ref[:, cols] = out.astype(BF16)


def _out_proj_deepnorm(lhs_list, w, layer, x, ln_g, ln_b, tm, tn, te):
    m, kg = lhs_list[0].shape
    _, k, n = w.shape
    n_lhs = len(lhs_list)
    nj, ne, per_emit = n // tn, n // te, te // tn
    assert k == kg * n_lhs and m % tm == 0 and n % te == 0 and te % tn == 0

    def col(j):
        return jnp.minimum(j, nj - 1)

    def emit(j):
        return jnp.maximum(j - nj, 0)

    est = (nj * tm * tn * 4 + 2 * (tm * k * 2 + k * tn * 4 + tm * tn * 4 + tm * te * 6) + tm * tn * 8)
    return pl.pallas_call(
        functools.partial(_out_ln_kernel, n_lhs=n_lhs, kg=kg, nj=nj, per_emit=per_emit, d_model=n),
        grid=(m // tm, nj + ne),
        in_specs=[pl.BlockSpec((tm, kg), lambda i, j: (i, 0)) for _ in range(n_lhs)]
        + [pl.BlockSpec((None, k, tn), lambda i, j: (layer, 0, col(j))),
           pl.BlockSpec((tm, tn), lambda i, j: (i, col(j))),
           pl.BlockSpec((1, te), lambda i, j: (0, emit(j))),
           pl.BlockSpec((1, te), lambda i, j: (0, emit(j)))],
        out_specs=[pl.BlockSpec((tm, te), lambda i, j: (i, emit(j))),
                   pl.BlockSpec((tm, te), lambda i, j: (i, emit(j)))],
        out_shape=[jax.ShapeDtypeStruct((m, n), F32), jax.ShapeDtypeStruct((m, n), BF16)],
        scratch_shapes=[pltpu.VMEM((nj, tm, tn), F32), pltpu.VMEM((tm, 1), F32), pltpu.VMEM((tm, 1), F32)],
        compiler_params=_params(est + (10 << 20), 2),
        name="out_proj_deepnorm",
    )(*lhs_list, w, x, ln_g, ln_b)


def _rope_tabs(positions):
    pos = positions.astype(F32)
    b, s = pos.shape

    def one(rot, period):
        inv = ROPE_THETA ** (-jnp.arange(0, rot, 2, dtype=F32) / rot)
        ang = pos[:, :, None] * inv
        c, sn = jnp.cos(ang), jnp.sin(ang)
        rest = period - rot
        cp = jnp.concatenate([c, c, jnp.ones((b, s, rest), F32)], axis=-1)
        sp = jnp.concatenate([-sn, sn, jnp.zeros((b, s, rest), F32)], axis=-1)
        reps = LANES // period
        return jnp.tile(cp, (1, 1, reps)), jnp.tile(sp, (1, 1, reps))

    parts = one(MLA_ROPE, LANES) + one(DSA_ROT, LANES) + one(IDX_ROT, IDX_DIM)
    return jnp.concatenate(parts, axis=-1).reshape(b * s, N_TABS * LANES)


def _block_diag(w):
    l = w.shape[0]
    eye = jnp.eye(LRU_BLOCKS, dtype=w.dtype)
    bd = jnp.einsum('lnde,nm->lndme', w, eye)
    return bd.reshape(l, LRU_WIDTH, LRU_WIDTH).astype(BF16)


def _layer(x, x16, mem16, tabs, lw, l, batch, seq):
    m = batch * seq
    main_rows, small_rows = _in_proj_row_tables(256)
    h2d = _in_proj(x16, lw["w_in_t"], l, main_rows, tm=min(2048, m), tn=256, name="in_proj")
    hs = _in_proj(x16, lw["w_in_t"], l, small_rows, tm=min(2048, m), tn=LANES, name="in_proj_small")
    tile_a = min(512, seq)
    qa, ka, vta = _prep_mla(h2d, hs, tabs, lw["g_cq"], lw["g_ckv"], lw["w_uq"], lw["w_uk"], lw["w_uvt"],
                            batch, seq, tile_a)
    ya = _mla_attn(qa, ka, vta, h2d, batch, seq, tile_a)
    qb, kb, vtb, qi4, ki4, wt = _prep_dsa(h2d, hs, tabs, batch, seq)
    yb = _dsa(qi4, ki4, wt, qb, kb, vtb, h2d, batch, seq)
    yc = _rglru(h2d.reshape(batch, seq, N_MAIN), lw["conv_w"], lw["conv_b"], lw["w_rg_a"], lw["b_rg_a"],
                lw["w_rg_x"], lw["b_rg_x"], lw["lam"], batch, seq, min(256, seq))
    yc = yc.reshape(m, GROUP_WIDTH)
    km = _matmul([mem16], lw["w_mem_k"], l, BF16, tm=mem16.shape[0], tn=256, name="mem_proj_k")
    vm = _matmul([mem16], lw["w_mem_v"], l, BF16, tm=mem16.shape[0], tn=256, name="mem_proj_v")
    yd = _mem_attn(h2d, km, vm, batch, seq, min(512, seq))
    return _out_proj_deepnorm([ya, yb, yc, yd], lw["w_o"], l, x, lw["ln_g"], lw["ln_b"],
                              tm=min(1024, m), tn=256, te=512)


def kernel(x, mem, positions, w_in, g_cq, g_ckv, w_uq, w_ukv, conv_w, conv_b, w_rg_a, b_rg_a, w_rg_x,
           b_rg_x, lru_lambda, w_mem_k, w_mem_v, w_o, ln_g, ln_b):
    batch, seq, d = x.shape
    depth = w_in.shape[0]
    tabs = _rope_tabs(positions)
    w_in_t = w_in.transpose(0, 2, 1)
    w_uq_p = jnp.pad(w_uq.reshape(depth, MLA_Q_LORA, MLA_HEADS, MLA_NOPE + MLA_ROPE),
                     [(0, 0), (0, 0), (0, 0), (0, 2 * LANES - MLA_NOPE - MLA_ROPE)])
    w_uq_p = w_uq_p.reshape(depth, MLA_Q_LORA, MLA_HEADS * 2 * LANES).astype(BF16)
    w_ukv4 = w_ukv.reshape(depth, MLA_KV_LORA, MLA_HEADS, MLA_NOPE + MLA_V)
    w_uk = w_ukv4[..., :MLA_NOPE].reshape(depth, MLA_KV_LORA, MLA_HEADS * MLA_NOPE).astype(BF16)
    w_uvt = w_ukv4[..., MLA_NOPE:].reshape(depth, MLA_KV_LORA, MLA_HEADS * MLA_V)
    w_uvt = w_uvt.transpose(0, 2, 1).astype(BF16)
    wa_bd = _block_diag(w_rg_a)
    wx_bd = _block_diag(w_rg_x)
    mem16 = mem.reshape(batch * mem.shape[1], d).astype(BF16)

    xf = x.reshape(batch * seq, d)
    x16 = xf.astype(BF16)
    for l in range(depth):
        lw = dict(w_in_t=w_in_t, g_cq=g_cq[l][None], g_ckv=g_ckv[l][None], w_uq=w_uq_p[l], w_uk=w_uk[l],
                  w_uvt=w_uvt[l], conv_w=conv_w[l], conv_b=conv_b[l][None], w_rg_a=wa_bd[l],
                  b_rg_a=b_rg_a[l][None], w_rg_x=wx_bd[l], b_rg_x=b_rg_x[l][None], lam=lru_lambda[l][None],
                  w_mem_k=w_mem_k, w_mem_v=w_mem_v, w_o=w_o, ln_g=ln_g[l][None], ln_b=ln_b[l][None])
        xf, x16 = _layer(xf, x16, mem16, tabs, lw, l, batch, seq)
    return xf.reshape(batch, seq, d)
```

```python
import functools

import numpy as np
import jax
import jax.numpy as jnp
from jax import lax
from jax.experimental import pallas as pl
from jax.experimental.pallas import tpu as pltpu

F32 = jnp.float32
BF16 = jnp.bfloat16
I32 = jnp.int32

DEPTH = 4
D_MODEL = 4096
N_MEM = 256
GROUP_WIDTH = D_MODEL // 4
ROPE_THETA = 500000.0
MLA_HEADS = 8
MLA_NOPE = 128
MLA_ROPE = 64
MLA_V = 128
MLA_Q_LORA = GROUP_WIDTH
MLA_KV_LORA = GROUP_WIDTH // 2
DSA_HEADS = 8
DSA_KV_HEADS = 2
DSA_GROUP = DSA_HEADS // DSA_KV_HEADS
DSA_HEAD_DIM = 128
DSA_ROT = DSA_HEAD_DIM // 4
IDX_HEADS = 16
IDX_DIM = 64
IDX_ROT = IDX_DIM // 4
TOPK_MAX = 256
LRU_WIDTH = GROUP_WIDTH
LRU_BLOCKS = 16
LRU_BLOCK_DIM = LRU_WIDTH // LRU_BLOCKS
CONV_WIDTH = 4
LRU_C = 8.0
MEM_HEADS = 4
MEM_HEAD_DIM = GROUP_WIDTH // MEM_HEADS
DEEPNORM_ALPHA = (2 * DEPTH) ** 0.25
LN_EPS = 1e-5
RMS_EPS = 1e-6

IN_SIZES = (
    MLA_Q_LORA, MLA_KV_LORA, MLA_ROPE, GROUP_WIDTH,
    GROUP_WIDTH, DSA_KV_HEADS * DSA_HEAD_DIM, DSA_KV_HEADS * DSA_HEAD_DIM,
    IDX_HEADS * IDX_DIM, IDX_DIM, IDX_HEADS, GROUP_WIDTH,
    LRU_WIDTH, GROUP_WIDTH,
    GROUP_WIDTH, GROUP_WIDTH,
)

LANES = 128
V7X_VMEM_BUDGET = 60 * 1024 * 1024

COL_CQ = 0
COL_AG = 1024
COL_BQ = 2048
COL_QI = 3072
COL_BG = 4096
COL_U = 5120
COL_CG = 6144
COL_DQ = 7168
COL_DG = 8192
COL_CKV = 9216
COL_BK = 9728
COL_BV = 9984
N_MAIN = 10240
SMALL_KR = 0
SMALL_KI = 128

TAB_MLA_C, TAB_MLA_S, TAB_DSA_C, TAB_DSA_S, TAB_IDX_C, TAB_IDX_S = range(6)
N_TABS = 6

NEG_BIG = -1e30
INT_MIN = -2 ** 31
LOG2E = 1.4426950408889634
KEY_CHUNK = 256
DSA_QBLK = 128
VT_ROWS = 128 + 16
_NT = (((1,), (1,)), ((), ()))


def _vmem_limit(nbytes):
    return int(min(V7X_VMEM_BUDGET, max(nbytes, 16 * 1024 * 1024)))


def _params(nbytes, ndims):
    return pltpu.CompilerParams(dimension_semantics=("arbitrary",) * ndims,
                                vmem_limit_bytes=_vmem_limit(nbytes))


def _mm_kernel(*refs, n_lhs, kg):
    w_ref = refs[n_lhs]
    o_ref = refs[n_lhs + 1]
    acc = None
    for g in range(n_lhs):
        part = jnp.dot(refs[g][...], w_ref[g * kg:(g + 1) * kg, :].astype(BF16), preferred_element_type=F32)
        acc = part if acc is None else acc + part
    o_ref[...] = acc.astype(o_ref.dtype)


def _matmul(lhs_list, w, layer, out_dtype, tm, tn, name):
    m, kg = lhs_list[0].shape
    _, k, n = w.shape
    n_lhs = len(lhs_list)
    assert k == kg * n_lhs and m % tm == 0 and n % tn == 0
    est = 2 * (tm * k * 2 + k * tn * w.dtype.itemsize + tm * tn * jnp.dtype(out_dtype).itemsize) + tm * tn * 8
    return pl.pallas_call(
        functools.partial(_mm_kernel, n_lhs=n_lhs, kg=kg),
        grid=(m // tm, n // tn),
        in_specs=[pl.BlockSpec((tm, kg), lambda i, j: (i, 0)) for _ in range(n_lhs)]
        + [pl.BlockSpec((None, k, tn), lambda i, j: (layer, 0, j))],
        out_specs=pl.BlockSpec((tm, tn), lambda i, j: (i, j)),
        out_shape=jax.ShapeDtypeStruct((m, n), out_dtype),
        compiler_params=_params(est + (8 << 20), 2),
        name=name,
    )(*lhs_list, w)


def _in_proj_kernel(tab_ref, x_ref, w_ref, o_ref):
    del tab_ref
    o_ref[...] = lax.dot_general(x_ref[...], w_ref[...].astype(BF16), _NT, preferred_element_type=F32)


def _in_proj(x16, w_t, layer, row_offsets, tm, tn, name):
    m, k = x16.shape
    nblk = len(row_offsets)
    est = 2 * (tm * k * 2 + tn * k * 4 + tm * tn * 4) + tm * tn * 8
    grid_spec = pltpu.PrefetchScalarGridSpec(
        num_scalar_prefetch=1,
        grid=(m // tm, nblk),
        in_specs=[
            pl.BlockSpec((tm, k), lambda i, j, tab: (i, 0)),
            pl.BlockSpec((pl.Squeezed(), pl.Element(tn), pl.Element(k)),
                         lambda i, j, tab: (layer, pl.multiple_of(tab[j], 16), 0)),
        ],
        out_specs=pl.BlockSpec((tm, tn), lambda i, j, tab: (i, j)),
    )
    return pl.pallas_call(
        _in_proj_kernel,
        grid_spec=grid_spec,
        out_shape=jax.ShapeDtypeStruct((m, nblk * tn), F32),
        compiler_params=_params(est + (8 << 20), 2),
        name=name,
    )(jnp.asarray(row_offsets, I32), x16, w_t)


def _in_proj_row_tables(tn):
    names = ("cq", "ckv", "kr", "ag", "bq", "bk", "bv", "qi", "ki", "wi", "bg", "u", "cg", "dq", "dg")
    src = dict(zip(names, [0] + [int(c) for c in np.cumsum(IN_SIZES)[:-1]]))
    size = dict(zip(names, IN_SIZES))
    order = [("cq", COL_CQ), ("ag", COL_AG), ("bq", COL_BQ), ("qi", COL_QI), ("bg", COL_BG), ("u", COL_U),
             ("cg", COL_CG), ("dq", COL_DQ), ("dg", COL_DG), ("ckv", COL_CKV), ("bk", COL_BK), ("bv", COL_BV)]
    main = []
    for n, dst in order:
        assert dst == len(main) * tn and size[n] % tn == 0 and src[n] % 16 == 0
        main += [src[n] + t * tn for t in range(size[n] // tn)]
    assert len(main) * tn == N_MAIN and src["wi"] == src["ki"] + size["ki"]
    return np.asarray(main, np.int32), np.asarray([src["kr"], src["ki"]], np.int32)


def _rope(v, c_tab, s_tab, half, period):
    width = v.shape[-1]
    lane = lax.broadcasted_iota(I32, v.shape, 1) & (period - 1)
    swapped = jnp.where(lane < half, pltpu.roll(v, width - half, 1), pltpu.roll(v, half, 1))
    return v * c_tab + swapped * s_tab


def _tile_lanes(t, reps):
    return t if reps == 1 else jnp.concatenate([t] * reps, axis=1)


def _rms(x, g):
    return x * lax.rsqrt(jnp.mean(x * x, axis=-1, keepdims=True) + RMS_EPS) * g


def _silu(g):
    return g * jax.nn.sigmoid(g)


def _prep_mla_kernel(cq_ref, ckv_ref, kr_ref, tab_ref, gcq_ref, gckv_ref, wuq_ref, wk_ref, wvt_ref,
                     q_out, k_out, vt_out):
    scale = (MLA_NOPE + MLA_ROPE) ** -0.5 * LOG2E
    nq = _rms(cq_ref[...], gcq_ref[...]).astype(BF16)
    nkv = _rms(ckv_ref[...], gckv_ref[...]).astype(BF16)
    c_tab = tab_ref[:, TAB_MLA_C * LANES:(TAB_MLA_C + 1) * LANES]
    s_tab = tab_ref[:, TAB_MLA_S * LANES:(TAB_MLA_S + 1) * LANES]
    qf = jnp.dot(nq, wuq_ref[...], preferred_element_type=F32) * scale
    kn = jnp.dot(nkv, wk_ref[...], preferred_element_type=F32)
    lane = lax.broadcasted_iota(I32, kr_ref.shape, 1)
    kr = jnp.where(lane < MLA_ROPE, kr_ref[...], 0.0)
    kr = _rope(kr, c_tab, s_tab, MLA_ROPE // 2, LANES).astype(BF16)
    for h in range(MLA_HEADS):
        base = h * 2 * LANES
        q_out[:, base:base + LANES] = qf[:, base:base + LANES].astype(BF16)
        q_out[:, base + LANES:base + 2 * LANES] = _rope(
            qf[:, base + LANES:base + 2 * LANES], c_tab, s_tab, MLA_ROPE // 2, LANES).astype(BF16)
        k_out[:, base:base + LANES] = kn[:, h * LANES:(h + 1) * LANES].astype(BF16)
        k_out[:, base + LANES:base + 2 * LANES] = kr
    vt = lax.dot_general(wvt_ref[...], nkv, _NT, preferred_element_type=F32)
    half = vt.shape[1] // 2
    vt_out[0] = _vt_with_ones(vt[:, :half], MLA_HEADS, MLA_V)
    vt_out[1] = _vt_with_ones(vt[:, half:], MLA_HEADS, MLA_V)


def _prep_mla(h2d, hs, tabs, g_cq, g_ckv, wuq_p, wk, wvt, batch, seq, tm):
    m = batch * seq
    nt = seq // tm
    hd = MLA_HEADS * 2 * LANES
    return pl.pallas_call(
        _prep_mla_kernel,
        grid=(batch, nt),
        in_specs=[
            pl.BlockSpec((tm, MLA_Q_LORA), lambda b, i: (b * nt + i, COL_CQ // MLA_Q_LORA)),
            pl.BlockSpec((tm, MLA_KV_LORA), lambda b, i: (b * nt + i, COL_CKV // MLA_KV_LORA)),
            pl.BlockSpec((tm, LANES), lambda b, i: (b * nt + i, SMALL_KR // LANES)),
            pl.BlockSpec((tm, N_TABS * LANES), lambda b, i: (b * nt + i, 0)),
            pl.BlockSpec((1, MLA_Q_LORA), lambda b, i: (0, 0)),
            pl.BlockSpec((1, MLA_KV_LORA), lambda b, i: (0, 0)),
            pl.BlockSpec((MLA_Q_LORA, hd), lambda b, i: (0, 0)),
            pl.BlockSpec((MLA_KV_LORA, MLA_HEADS * MLA_NOPE), lambda b, i: (0, 0)),
            pl.BlockSpec((MLA_HEADS * MLA_V, MLA_KV_LORA), lambda b, i: (0, 0)),
        ],
        out_specs=[
            pl.BlockSpec((tm, hd), lambda b, i: (b * nt + i, 0)),
            pl.BlockSpec((tm, hd), lambda b, i: (b * nt + i, 0)),
            pl.BlockSpec((None, 2, MLA_HEADS * VT_ROWS, tm // 2), lambda b, i: (b, i, 0, 0)),
        ],
        out_shape=[
            jax.ShapeDtypeStruct((m, hd), BF16),
            jax.ShapeDtypeStruct((m, hd), BF16),
            jax.ShapeDtypeStruct((batch, 2 * nt, MLA_HEADS * VT_ROWS, tm // 2), BF16),
        ],
        compiler_params=_params(40 << 20, 2),
        name="prep_mla",
    )(h2d, h2d, hs, tabs, g_cq, g_ckv, wuq_p, wk, wvt)


def _softmax_step(s, col_max, vt_chunk, m, acc):
    m_new = jnp.maximum(m, col_max)
    alpha = jnp.exp2(m - m_new)
    p = jnp.exp2(s - m_new)
    acc_new = alpha * acc + jnp.dot(vt_chunk, p.astype(BF16), preferred_element_type=F32)
    return m_new, acc_new


def _normalised_t(acc, dv):
    return (acc[:dv, :] / acc[dv:dv + 1, :]).T


def _vt_with_ones(vt, heads, dv):
    ones = jnp.ones((VT_ROWS - dv, vt.shape[1]), BF16)
    parts = []
    for h in range(heads):
        parts += [vt[h * dv:(h + 1) * dv, :].astype(BF16), ones]
    return jnp.concatenate(parts, axis=0)


def _mla_attn_kernel(q_ref, k_ref, vt_ref, gate_ref, o_ref, s0_ref, s1_ref, *, tile, ch):
    i = pl.program_id(2)
    q = q_ref[...]
    slots = (s0_ref, s1_ref)

    def produce(c, slot, diag_offset=None):
        kc = k_ref[pl.ds(pl.multiple_of(c * ch, ch), ch), :]
        s = lax.dot_general(kc, q, _NT, preferred_element_type=F32)
        if diag_offset is not None:
            kidx = lax.broadcasted_iota(I32, (ch, tile), 0) + diag_offset
            qidx = lax.broadcasted_iota(I32, (ch, tile), 1)
            s = jnp.where(kidx <= qidx, s, NEG_BIG)
        slots[slot][...] = s
        return jnp.max(s, axis=0, keepdims=True)

    def consume(c, slot, col_max, carry):
        return _softmax_step(slots[slot][...], col_max, vt_ref[c], *carry)

    init = (jnp.full((1, tile), NEG_BIG, F32), jnp.zeros((VT_ROWS, tile), F32))
    diag = 2 * i
    col_a = produce(diag, 0, 0)
    col_b = produce(diag + 1, 1, ch)
    carry = consume(diag, 0, col_a, init)

    def pair(j, state):
        cr, col1, c1 = state
        col0 = produce(2 * j, 0)
        cr = consume(c1, 1, col1, cr)
        col1 = produce(2 * j + 1, 1)
        cr = consume(2 * j, 0, col0, cr)
        return cr, col1, 2 * j + 1

    state = lax.fori_loop(0, i // 2, lambda jj, st: pair(2 * jj + 1, pair(2 * jj, st)),
                          (carry, col_b, diag + 1))
    carry, col1, c1 = lax.fori_loop(0, i & 1, lambda _, st: pair(i - 1, st), state)
    _, acc = consume(c1, 1, col1, carry)
    o = _normalised_t(acc, MLA_V)
    o_ref[...] = (o * _silu(gate_ref[...])).astype(o_ref.dtype)


def _mla_attn(q, k, vt, h2d, batch, seq, tile):
    m = batch * seq
    nt = seq // tile
    ch = tile // 2
    return pl.pallas_call(
        functools.partial(_mla_attn_kernel, tile=tile, ch=ch),
        grid=(batch, MLA_HEADS, nt),
        in_specs=[
            pl.BlockSpec((tile, 2 * LANES), lambda b, h, i: (b * nt + i, h)),
            pl.BlockSpec((seq, 2 * LANES), lambda b, h, i: (b, h)),
            pl.BlockSpec((None, 2 * nt, VT_ROWS, ch), lambda b, h, i: (b, 0, h, 0)),
            pl.BlockSpec((tile, LANES), lambda b, h, i: (b * nt + i, COL_AG // LANES + h)),
        ],
        out_specs=pl.BlockSpec((tile, LANES), lambda b, h, i: (b * nt + i, h)),
        out_shape=jax.ShapeDtypeStruct((m, GROUP_WIDTH), BF16),
        scratch_shapes=[pltpu.VMEM((ch, tile), F32), pltpu.VMEM((ch, tile), F32)],
        compiler_params=_params(32 << 20, 3),
        name="mla_attn",
    )(q, k, vt, h2d)


def _hi_lo(v):
    hi = v.astype(BF16).astype(F32)
    return hi, v - hi


def _prep_dsa_kernel(q_ref, k_ref, v_ref, qi_ref, ki_ref, tab_ref,
                     q_out, k_out, vt_out, qi4_out, ki4_out, wt_out):
    def tab(t):
        return tab_ref[:, t * LANES:(t + 1) * LANES]

    scale = DSA_HEAD_DIM ** -0.5 * LOG2E
    q = q_ref[...]
    q_out[...] = (_rope(q, _tile_lanes(tab(TAB_DSA_C), DSA_HEADS), _tile_lanes(tab(TAB_DSA_S), DSA_HEADS),
                        DSA_ROT // 2, LANES) * scale).astype(BF16)
    k = k_ref[...]
    k_out[...] = _rope(k, _tile_lanes(tab(TAB_DSA_C), DSA_KV_HEADS), _tile_lanes(tab(TAB_DSA_S), DSA_KV_HEADS),
                       DSA_ROT // 2, LANES).astype(BF16)
    vt_out[...] = _vt_with_ones(v_ref[...].T, DSA_KV_HEADS, DSA_HEAD_DIM)
    reps = IDX_HEADS * IDX_DIM // LANES
    qi = _rope(qi_ref[...], _tile_lanes(tab(TAB_IDX_C), reps), _tile_lanes(tab(TAB_IDX_S), reps),
               IDX_ROT // 2, IDX_DIM)
    qb = DSA_QBLK
    first_half = lax.broadcasted_iota(I32, (qb, LANES), 1) < IDX_DIM
    for blk in range(q.shape[0] // qb):
        for pair in range(IDX_HEADS // 2):
            hi, lo = _hi_lo(qi[blk * qb:(blk + 1) * qb, pair * LANES:(pair + 1) * LANES])
            even = jnp.where(first_half, hi, pltpu.roll(lo, IDX_DIM, 1)).astype(BF16)
            odd = jnp.where(first_half, pltpu.roll(hi, IDX_DIM, 1), lo).astype(BF16)
            for half in range(2):
                qi4_out[blk, pair, 0:qb, half * LANES:(half + 1) * LANES] = even
                qi4_out[blk, pair, qb:2 * qb, half * LANES:(half + 1) * LANES] = odd
    small = ki_ref[...]
    lane = lax.broadcasted_iota(I32, small.shape, 1)
    c_ki = jnp.where(lane < IDX_DIM, tab(TAB_IDX_C), 1.0)
    s_ki = jnp.where(lane < IDX_DIM, tab(TAB_IDX_S), 0.0)
    hi, lo = _hi_lo(_rope(small, c_ki, s_ki, IDX_ROT // 2, IDX_DIM))
    ki4_out[:, 0:LANES] = jnp.where(lane < IDX_DIM, hi, pltpu.roll(hi, IDX_DIM, 1)).astype(BF16)
    ki4_out[:, LANES:2 * LANES] = jnp.where(lane < IDX_DIM, lo, pltpu.roll(lo, IDX_DIM, 1)).astype(BF16)
    wt = small.T[IDX_DIM:IDX_DIM + IDX_HEADS, :]
    wt_out[...] = wt * (IDX_HEADS ** -0.5 * IDX_DIM ** -0.5)


def _prep_dsa(h2d, hs, tabs, batch, seq):
    tm = KEY_CHUNK
    m = batch * seq
    nt = seq // tm
    qpt = tm // DSA_QBLK
    kvw = DSA_KV_HEADS * DSA_HEAD_DIM

    def row(b, i):
        return b * nt + i

    return pl.pallas_call(
        _prep_dsa_kernel,
        grid=(batch, nt),
        in_specs=[
            pl.BlockSpec((tm, GROUP_WIDTH), lambda b, i: (row(b, i), COL_BQ // GROUP_WIDTH)),
            pl.BlockSpec((tm, kvw), lambda b, i: (row(b, i), COL_BK // kvw)),
            pl.BlockSpec((tm, kvw), lambda b, i: (row(b, i), COL_BV // kvw)),
            pl.BlockSpec((tm, GROUP_WIDTH), lambda b, i: (row(b, i), COL_QI // GROUP_WIDTH)),
            pl.BlockSpec((tm, LANES), lambda b, i: (row(b, i), SMALL_KI // LANES)),
            pl.BlockSpec((tm, N_TABS * LANES), lambda b, i: (row(b, i), 0)),
        ],
        out_specs=[
            pl.BlockSpec((tm, GROUP_WIDTH), lambda b, i: (row(b, i), 0)),
            pl.BlockSpec((tm, kvw), lambda b, i: (row(b, i), 0)),
            pl.BlockSpec((None, None, DSA_KV_HEADS * VT_ROWS, tm), lambda b, i: (b, i, 0, 0)),
            pl.BlockSpec((None, qpt, IDX_HEADS // 2, 2 * DSA_QBLK, 4 * IDX_DIM), lambda b, i: (b, i, 0, 0, 0)),
            pl.BlockSpec((tm, 4 * IDX_DIM), lambda b, i: (row(b, i), 0)),
            pl.BlockSpec((None, IDX_HEADS, tm), lambda b, i: (b, 0, i)),
        ],
        out_shape=[
            jax.ShapeDtypeStruct((m, GROUP_WIDTH), BF16),
            jax.ShapeDtypeStruct((m, kvw), BF16),
            jax.ShapeDtypeStruct((batch, nt, DSA_KV_HEADS * VT_ROWS, tm), BF16),
            jax.ShapeDtypeStruct((batch, seq // DSA_QBLK, IDX_HEADS // 2, 2 * DSA_QBLK, 4 * IDX_DIM), BF16),
            jax.ShapeDtypeStruct((m, 4 * IDX_DIM), BF16),
            jax.ShapeDtypeStruct((batch, IDX_HEADS, seq), F32),
        ],
        compiler_params=_params(24 << 20, 2),
        name="prep_dsa",
    )(h2d, h2d, h2d, h2d, hs, tabs)


def _sortable_key(score):
    bits = lax.bitcast_convert_type(score, I32)
    key = jnp.where(bits < 0, bits ^ jnp.int32(0x7FFFFFFF), bits)
    return jnp.where(score == 0.0, 0, key)


def _dsa_kernel(qi4_ref, ki4_ref, wt_ref, q_ref, k_ref, vt_ref, gate_ref, o_ref, key_ref, s0_ref, s1_ref,
                *, topk, seq_bits):
    i = pl.program_id(1)
    n_vis = i // 2
    n_chunks = n_vis + 1
    wt = wt_ref[...]
    ch = KEY_CHUNK
    qb = DSA_QBLK

    def rows(c):
        return pl.ds(pl.multiple_of(c * ch, ch), ch)

    def score_chunk(c):
        kc = ki4_ref[rows(c), :]
        tot = None
        for p in range(IDX_HEADS // 2):
            s = lax.dot_general(kc, qi4_ref[p], _NT, preferred_element_type=F32)
            r = jnp.maximum(s, 0.0)
            t = r[:, :qb] * wt[2 * p:2 * p + 1, :] + r[:, qb:] * wt[2 * p + 1:2 * p + 2, :]
            tot = t if tot is None else tot + t
        return tot

    def phase1(c, carry):
        key_ref[rows(c), :] = _sortable_key(score_chunk(c))
        return carry

    lax.fori_loop(0, n_vis, phase1, 0)
    kidx = n_vis * ch + lax.broadcasted_iota(I32, (ch, qb), 0)
    qidx = i * qb + lax.broadcasted_iota(I32, (ch, qb), 1)
    key_ref[rows(n_vis), :] = jnp.where(kidx <= qidx, _sortable_key(score_chunk(n_vis)), INT_MIN)

    @pl.when((n_chunks & 1) == 1)
    def _():
        key_ref[rows(n_chunks), :] = jnp.full((ch, qb), INT_MIN, I32)

    n_pairs = (n_chunks + 1) // 2

    def count_ge(t):
        def body(p, acc):
            blk = key_ref[pl.ds(pl.multiple_of(p * 2 * ch, 2 * ch), 2 * ch), :]
            hit = (blk >= t).astype(I32)
            return acc + jnp.sum(hit.reshape(2 * ch // 8, 8, qb), axis=0)
        acc = lax.fori_loop(0, n_pairs, body, jnp.zeros((8, qb), I32))
        return jnp.sum(acc, axis=0, keepdims=True)

    def group_max(p, g):
        blk = key_ref[pl.ds(pl.multiple_of(p * 2 * ch, 2 * ch), 2 * ch), :]
        return jnp.maximum(g, jnp.maximum(blk[:ch, :], blk[ch:, :]))

    gmax = lax.fori_loop(0, n_pairs, group_max, jnp.full((ch, qb), INT_MIN, I32))
    hi_u = jnp.max(gmax, axis=0, keepdims=True) ^ INT_MIN
    lo_u = jnp.min(gmax, axis=0, keepdims=True) ^ INT_MIN
    top_bit = 31 - lax.clz(lo_u ^ hi_u)
    low_mask = jnp.where(top_bit >= 31, -1, jnp.left_shift(jnp.int32(1), top_bit + 1) - 1)
    thr_u = hi_u & ~low_mask
    n_bits = jnp.max(top_bit) + 1

    def bit_step(it, t_u):
        cand_u = t_u | jnp.left_shift(jnp.int32(1), n_bits - 1 - it)
        return jnp.where(count_ge(cand_u ^ INT_MIN) >= topk, cand_u, t_u)

    thr = lax.fori_loop(0, n_bits, bit_step, thr_u) ^ INT_MIN
    found = thr > INT_MIN
    c_ge = count_ge(jnp.maximum(thr, INT_MIN + 1))
    need = topk - count_ge(thr + 1)
    tied = jnp.logical_and(c_ge > topk, found)

    def count(pred):
        def body(c, acc):
            idx = c * ch + lax.broadcasted_iota(I32, (ch, qb), 0)
            hit = pred(key_ref[rows(c), :], idx).astype(I32)
            return acc + jnp.sum(hit.reshape(ch // 8, 8, qb), axis=0)
        acc = lax.fori_loop(0, n_chunks, body, jnp.zeros((8, qb), I32))
        return jnp.sum(acc, axis=0, keepdims=True)

    def tie_search():
        def step(b, lo):
            cand = lo + jnp.left_shift(jnp.int32(1), seq_bits - 1 - b)
            below = count(lambda keys, idx: jnp.logical_and(keys == thr, idx < cand))
            return jnp.where(below < need, cand, lo)
        return lax.fori_loop(0, seq_bits, step, jnp.zeros((1, qb), I32))

    last_eq = lax.cond(jnp.max(tied.astype(I32)) > 0, tie_search,
                       lambda: jnp.full((1, qb), 2 ** seq_bits, I32))
    thr_sel = jnp.maximum(thr, INT_MIN + 1)

    nq = DSA_GROUP * qb
    one_hot = (lax.broadcasted_iota(I32, (nq, qb), 0) % qb == lax.broadcasted_iota(I32, (nq, qb), 1))
    one_hot = jnp.where(one_hot, 1.0, 0.0).astype(BF16)
    qs = [jnp.concatenate([jnp.concatenate([q_ref[:, (n * DSA_GROUP + g) * LANES:(n * DSA_GROUP + g + 1) * LANES]
                                            for g in range(DSA_GROUP)], axis=0), one_hot], axis=1)
          for n in range(DSA_KV_HEADS)]

    slots = (s0_ref, s1_ref)

    def produce(c, slot):
        keys = key_ref[rows(c), :]
        idx = c * ch + lax.broadcasted_iota(I32, (ch, qb), 0)
        drop = jnp.logical_and(keys == thr, idx > last_eq)
        sel = jnp.logical_and(keys >= thr_sel, jnp.logical_not(drop))
        neg = jnp.where(sel, 0.0, NEG_BIG).astype(BF16)
        col_max = []
        for n in range(DSA_KV_HEADS):
            kc = jnp.concatenate([k_ref[rows(c), n * LANES:(n + 1) * LANES], neg], axis=1)
            s = lax.dot_general(kc, qs[n], _NT, preferred_element_type=F32)
            slots[slot][n] = s
            col_max.append(jnp.max(s, axis=0, keepdims=True))
        return tuple(col_max)

    def consume(c, slot, col_max, carry):
        return tuple(_softmax_step(slots[slot][n], col_max[n],
                                   vt_ref[c, n * VT_ROWS:(n + 1) * VT_ROWS, :], *carry[n])
                     for n in range(DSA_KV_HEADS))

    def phase3(j, state):
        carry, col0 = state
        col1 = produce(2 * j + 1, 1)
        carry = consume(2 * j, 0, col0, carry)
        col0 = produce(2 * j + 2, 0)
        carry = consume(2 * j + 1, 1, col1, carry)
        return carry, col0

    init = tuple((jnp.full((1, nq), NEG_BIG, F32), jnp.zeros((VT_ROWS, nq), F32))
                 for _ in range(DSA_KV_HEADS))
    pairs = (n_chunks - 1) // 2
    carry, col0 = lax.fori_loop(0, pairs, phase3, (init, produce(0, 0)))
    last = 2 * pairs

    def tail_two(cr):
        col1 = produce(last + 1, 1)
        cr = consume(last, 0, col0, cr)
        return consume(last + 1, 1, col1, cr)

    res = lax.cond(n_chunks - 1 > last, tail_two, lambda cr: consume(last, 0, col0, cr), carry)
    for n in range(DSA_KV_HEADS):
        o = _normalised_t(res[n][1], DSA_HEAD_DIM)
        for g in range(DSA_GROUP):
            col = (n * DSA_GROUP + g) * LANES
            o_ref[:, col:col + LANES] = (o[g * qb:(g + 1) * qb, :]
                                         * _silu(gate_ref[:, col:col + LANES])).astype(o_ref.dtype)


def _dsa(qi4, ki4, wt, q, k, vt, h2d, batch, seq):
    m = batch * seq
    nq = seq // DSA_QBLK
    nck = seq // KEY_CHUNK
    kvw = DSA_KV_HEADS * DSA_HEAD_DIM
    topk = min(TOPK_MAX, seq // 4)
    seq_bits = int(np.log2(seq))
    assert 2 ** seq_bits == seq and nck % 2 == 0 and topk <= KEY_CHUNK
    return pl.pallas_call(
        functools.partial(_dsa_kernel, topk=topk, seq_bits=seq_bits),
        grid=(batch, nq),
        in_specs=[
            pl.BlockSpec((None, None, IDX_HEADS // 2, 2 * DSA_QBLK, 4 * IDX_DIM),
                         lambda b, i: (b, i, 0, 0, 0)),
            pl.BlockSpec((seq, 4 * IDX_DIM), lambda b, i: (b, 0)),
            pl.BlockSpec((None, IDX_HEADS, DSA_QBLK), lambda b, i: (b, 0, i)),
            pl.BlockSpec((DSA_QBLK, GROUP_WIDTH), lambda b, i: (b * nq + i, 0)),
            pl.BlockSpec((seq, kvw), lambda b, i: (b, 0)),
            pl.BlockSpec((None, nck, DSA_KV_HEADS * VT_ROWS, KEY_CHUNK), lambda b, i: (b, 0, 0, 0)),
            pl.BlockSpec((DSA_QBLK, GROUP_WIDTH), lambda b, i: (b * nq + i, COL_BG // GROUP_WIDTH)),
        ],
        out_specs=pl.BlockSpec((DSA_QBLK, GROUP_WIDTH), lambda b, i: (b * nq + i, 0)),
        out_shape=jax.ShapeDtypeStruct((m, GROUP_WIDTH), BF16),
        scratch_shapes=[pltpu.VMEM((seq, DSA_QBLK), I32),
                        pltpu.VMEM((DSA_KV_HEADS, KEY_CHUNK, DSA_GROUP * DSA_QBLK), F32),
                        pltpu.VMEM((DSA_KV_HEADS, KEY_CHUNK, DSA_GROUP * DSA_QBLK), F32)],
        compiler_params=_params(32 << 20, 2),
        name="dsa",
    )(qi4, ki4, wt, q, k, vt, h2d)


def _rglru_kernel(u_ref, gate_ref, cw_ref, cb_ref, wa_ref, ba_ref, wx_ref, bx_ref, lam_ref,
                  o_ref, halo_ref, h_ref, a_scr, x_scr, y_scr, *, batch, tt):
    t = pl.program_id(0)

    @pl.when(t == 0)
    def _():
        halo_ref[...] = jnp.zeros_like(halo_ref)
        h_ref[...] = jnp.zeros_like(h_ref)

    lam = lam_ref[...]
    z = -lam
    softplus = jnp.maximum(z, 0.0) + jnp.log1p(jnp.exp(-jnp.abs(z)))
    row = lax.broadcasted_iota(I32, (tt, LRU_WIDTH), 0)
    for b in range(batch):
        u = u_ref[b]
        prev = halo_ref[b]
        p1, p2, p3 = prev[7:8, :], prev[6:7, :], prev[5:6, :]
        s1 = jnp.where(row == 0, p1, pltpu.roll(u, 1, 0))
        s2 = jnp.where(row == 0, p2, jnp.where(row == 1, p1, pltpu.roll(u, 2, 0)))
        s3 = jnp.where(row == 0, p3, jnp.where(row == 1, p2, jnp.where(row == 2, p1, pltpu.roll(u, 3, 0))))
        conv = (cw_ref[3:4, :] * u + cw_ref[2:3, :] * s1 + cw_ref[1:2, :] * s2 + cw_ref[0:1, :] * s3
                + cb_ref[...])
        halo_ref[b] = u[tt - 8:, :]
        cb16 = conv.astype(BF16)
        r = jax.nn.sigmoid(jnp.dot(cb16, wa_ref[...], preferred_element_type=F32) + ba_ref[...])
        ig = jax.nn.sigmoid(jnp.dot(cb16, wx_ref[...], preferred_element_type=F32) + bx_ref[...])
        log_a = -LRU_C * r * softplus
        a_scr[b] = jnp.exp(log_a)
        th = jnp.tanh(log_a)
        x_scr[b] = jnp.sqrt(-2.0 * th / (1.0 - th)) * (ig * conv)

    def step(s, h):
        h = a_scr[:, pl.ds(s, 1), :] * h + x_scr[:, pl.ds(s, 1), :]
        y_scr[:, pl.ds(s, 1), :] = h
        return h

    h_ref[...] = lax.fori_loop(0, tt, step, h_ref[...], unroll=8)
    o_ref[...] = (y_scr[...] * _silu(gate_ref[...])).astype(o_ref.dtype)


def _rglru(h3d, conv_w, conv_b, wa_bd, b_a, wx_bd, b_x, lam, batch, seq, tt):
    nt = seq // tt
    w = LRU_WIDTH

    def vec(r):
        return pl.BlockSpec((r, w), lambda t: (0, 0))

    return pl.pallas_call(
        functools.partial(_rglru_kernel, batch=batch, tt=tt),
        grid=(nt,),
        in_specs=[
            pl.BlockSpec((batch, tt, w), lambda t: (0, t, COL_U // w)),
            pl.BlockSpec((batch, tt, w), lambda t: (0, t, COL_CG // w)),
            vec(CONV_WIDTH), vec(1),
            pl.BlockSpec((w, w), lambda t: (0, 0)), vec(1),
            pl.BlockSpec((w, w), lambda t: (0, 0)), vec(1),
            vec(1),
        ],
        out_specs=pl.BlockSpec((batch, tt, w), lambda t: (0, t, 0)),
        out_shape=jax.ShapeDtypeStruct((batch, seq, w), BF16),
        scratch_shapes=[pltpu.VMEM((batch, 8, w), F32), pltpu.VMEM((batch, 1, w), F32),
                        pltpu.VMEM((batch, tt, w), F32), pltpu.VMEM((batch, tt, w), F32),
                        pltpu.VMEM((batch, tt, w), F32)],
        compiler_params=_params(32 << 20, 1),
        name="rglru",
    )(h3d, h3d, conv_w, conv_b, wa_bd, b_a, wx_bd, b_x, lam)


def _mem_attn_kernel(q_ref, gate_ref, km_ref, vm_ref, o_ref):
    scale = MEM_HEAD_DIM ** -0.5
    for h in range(MEM_HEADS):
        cols = slice(h * MEM_HEAD_DIM, (h + 1) * MEM_HEAD_DIM)
        qh = (q_ref[:, cols] * scale).astype(BF16)
        s = lax.dot_general(qh, km_ref[:, cols], _NT, preferred_element_type=F32)
        p = jnp.exp(s - jnp.max(s, axis=-1, keepdims=True))
        l = jnp.sum(p, axis=-1, keepdims=True)
        o = jnp.dot(p.astype(BF16), vm_ref[:, cols], preferred_element_type=F32) / l
        o_ref[:, cols] = (o * _silu(gate_ref[:, cols])).astype(o_ref.dtype)


def _mem_attn(h2d, km, vm, batch, seq, tm):
    m = batch * seq
    nt = seq // tm
    w = GROUP_WIDTH
    return pl.pallas_call(
        _mem_attn_kernel,
        grid=(batch, nt),
        in_specs=[
            pl.BlockSpec((tm, w), lambda b, i: (b * nt + i, COL_DQ // w)),
            pl.BlockSpec((tm, w), lambda b, i: (b * nt + i, COL_DG // w)),
            pl.BlockSpec((N_MEM, w), lambda b, i: (b, 0)),
            pl.BlockSpec((N_MEM, w), lambda b, i: (b, 0)),
        ],
        out_specs=pl.BlockSpec((tm, w), lambda b, i: (b * nt + i, 0)),
        out_shape=jax.ShapeDtypeStruct((m, w), BF16),
        compiler_params=_params(24 << 20, 2),
        name="mem_attn",
    )(h2d, h2d, km, vm)


def _out_ln_kernel(*refs, n_lhs, kg, nj, nt, d_model):
    lhs = refs[:n_lhs]
    w_ref, x_ref, g_ref, b_ref, o_ref, o16_ref, z_ref, mu_ref, rs_ref = refs[n_lhs:]
    i = pl.program_id(0)
    j = pl.program_id(1)
    cur = (i & 1) * nj
    prev = nj - cur

    @pl.when(i < nt)
    def _():
        acc = None
        for g in range(n_lhs):
            part = jnp.dot(lhs[g][...], w_ref[g * kg:(g + 1) * kg, :], preferred_element_type=F32)
            acc = part if acc is None else acc + part
        z_ref[cur + j] = DEEPNORM_ALPHA * x_ref[...] + acc

    @pl.when(jnp.logical_and(i > 0, j == 0))
    def _():
        def lane_tiles(v):
            tiles = [v[:, t * LANES:(t + 1) * LANES] for t in range(v.shape[1] // LANES)]
            return functools.reduce(lambda a, b: a + b, tiles)

        tot = functools.reduce(lambda a, b: a + b, [lane_tiles(z_ref[prev + jj]) for jj in range(nj)])
        mu = jnp.sum(tot, axis=-1, keepdims=True) * (1.0 / d_model)
        sq = None
        for jj in range(nj):
            dlt = z_ref[prev + jj] - mu
            part = lane_tiles(dlt * dlt)
            sq = part if sq is None else sq + part
        mu_ref[...] = mu
        rs_ref[...] = lax.rsqrt(jnp.sum(sq, axis=-1, keepdims=True) * (1.0 / d_model) + LN_EPS)

    @pl.when(i > 0)
    def _():
        out = (z_ref[prev + j] - mu_ref[...]) * rs_ref[...] * g_ref[...] + b_ref[...]
        o_ref[...] = out
        o16_ref[...] = out.astype(BF16)


def _out_proj_deepnorm(lhs_list, w16, layer, x, ln_g, ln_b, tm, tn):
    m, kg = lhs_list[0].shape
    _, k, n = w16.shape
    n_lhs = len(lhs_list)
    nj, nt = n // tn, m // tm
    assert k == kg * n_lhs and m % tm == 0 and n % tn == 0

    def row(i):
        return jnp.minimum(i, nt - 1)

    def col(i, j):
        return jnp.where(i < nt, j, nj - 1)

    def out_block(i, j):
        return jnp.maximum(i - 1, 0), jnp.where(i == 0, 0, j)

    est = 2 * nj * tm * tn * 4 + 2 * (tm * k * 2 + k * tn * 2 + tm * tn * 4 + tm * tn * 6) + tm * tn * 8
    return pl.pallas_call(
        functools.partial(_out_ln_kernel, n_lhs=n_lhs, kg=kg, nj=nj, nt=nt, d_model=n),
        grid=(nt + 1, nj),
        in_specs=[pl.BlockSpec((tm, kg), lambda i, j: (row(i), 0)) for _ in range(n_lhs)]
        + [pl.BlockSpec((None, k, tn), lambda i, j: (layer, 0, col(i, j))),
           pl.BlockSpec((tm, tn), lambda i, j: (row(i), col(i, j))),
           pl.BlockSpec((1, tn), lambda i, j: (0, j)),
           pl.BlockSpec((1, tn), lambda i, j: (0, j))],
        out_specs=[pl.BlockSpec((tm, tn), out_block), pl.BlockSpec((tm, tn), out_block)],
        out_shape=[jax.ShapeDtypeStruct((m, n), F32), jax.ShapeDtypeStruct((m, n), BF16)],
        scratch_shapes=[pltpu.VMEM((2 * nj, tm, tn), F32), pltpu.VMEM((tm, 1), F32), pltpu.VMEM((tm, 1), F32)],
        compiler_params=_params(est + (10 << 20), 2),
        name="out_proj_deepnorm",
    )(*lhs_list, w16, x, ln_g, ln_b)


def _rope_tabs(positions):
    pos = positions.astype(F32)
    b, s = pos.shape

    def one(rot, period):
        inv = ROPE_THETA ** (-jnp.arange(0, rot, 2, dtype=F32) / rot)
        ang = pos[:, :, None] * inv
        c, sn = jnp.cos(ang), jnp.sin(ang)
        rest = period - rot
        cp = jnp.concatenate([c, c, jnp.ones((b, s, rest), F32)], axis=-1)
        sp = jnp.concatenate([-sn, sn, jnp.zeros((b, s, rest), F32)], axis=-1)
        reps = LANES // period
        return jnp.tile(cp, (1, 1, reps)), jnp.tile(sp, (1, 1, reps))

    parts = one(MLA_ROPE, LANES) + one(DSA_ROT, LANES) + one(IDX_ROT, IDX_DIM)
    return jnp.concatenate(parts, axis=-1).reshape(b * s, N_TABS * LANES)


def _block_diag(w):
    l = w.shape[0]
    eye = jnp.eye(LRU_BLOCKS, dtype=w.dtype)
    bd = jnp.einsum('lnde,nm->lndme', w, eye)
    return bd.reshape(l, LRU_WIDTH, LRU_WIDTH).astype(BF16)


def _layer(x, x16, mem16, tabs, lw, l, batch, seq):
    m = batch * seq
    main_rows, small_rows = _in_proj_row_tables(256)
    h2d = _in_proj(x16, lw["w_in_t"], l, main_rows, tm=min(2048, m), tn=256, name="in_proj")
    hs = _in_proj(x16, lw["w_in_t"], l, small_rows, tm=min(2048, m), tn=LANES, name="in_proj_small")
    tile_a = min(512, seq)
    qa, ka, vta = _prep_mla(h2d, hs, tabs, lw["g_cq"], lw["g_ckv"], lw["w_uq"], lw["w_uk"], lw["w_uvt"],
                            batch, seq, tile_a)
    ya = _mla_attn(qa, ka, vta, h2d, batch, seq, tile_a)
    qb, kb, vtb, qi4, ki4, wt = _prep_dsa(h2d, hs, tabs, batch, seq)
    yb = _dsa(qi4, ki4, wt, qb, kb, vtb, h2d, batch, seq)
    yc = _rglru(h2d.reshape(batch, seq, N_MAIN), lw["conv_w"], lw["conv_b"], lw["w_rg_a"], lw["b_rg_a"],
                lw["w_rg_x"], lw["b_rg_x"], lw["lam"], batch, seq, min(256, seq))
    yc = yc.reshape(m, GROUP_WIDTH)
    km = _matmul([mem16], lw["w_mem_k"], l, BF16, tm=mem16.shape[0], tn=256, name="mem_proj_k")
    vm = _matmul([mem16], lw["w_mem_v"], l, BF16, tm=mem16.shape[0], tn=256, name="mem_proj_v")
    yd = _mem_attn(h2d, km, vm, batch, seq, min(512, seq))
    return _out_proj_deepnorm([ya, yb, yc, yd], lw["w_o"], l, x, lw["ln_g"], lw["ln_b"],
                              tm=min(512, m), tn=256)


def kernel(x, mem, positions, w_in, g_cq, g_ckv, w_uq, w_ukv, conv_w, conv_b, w_rg_a, b_rg_a, w_rg_x,
           b_rg_x, lru_lambda, w_mem_k, w_mem_v, w_o, ln_g, ln_b):
    batch, seq, d = x.shape
    depth = w_in.shape[0]
    tabs = _rope_tabs(positions)
    w_in_t = w_in.transpose(0, 2, 1)
    w_uq_p = jnp.pad(w_uq.reshape(depth, MLA_Q_LORA, MLA_HEADS, MLA_NOPE + MLA_ROPE),
                     [(0, 0), (0, 0), (0, 0), (0, 2 * LANES - MLA_NOPE - MLA_ROPE)])
    w_uq_p = w_uq_p.reshape(depth, MLA_Q_LORA, MLA_HEADS * 2 * LANES).astype(BF16)
    w_ukv4 = w_ukv.reshape(depth, MLA_KV_LORA, MLA_HEADS, MLA_NOPE + MLA_V)
    w_uk = w_ukv4[..., :MLA_NOPE].reshape(depth, MLA_KV_LORA, MLA_HEADS * MLA_NOPE).astype(BF16)
    w_uvt = w_ukv4[..., MLA_NOPE:].reshape(depth, MLA_KV_LORA, MLA_HEADS * MLA_V)
    w_uvt = w_uvt.transpose(0, 2, 1).astype(BF16)
    wa_bd = _block_diag(w_rg_a)
    wx_bd = _block_diag(w_rg_x)
    mem16 = mem.reshape(batch * mem.shape[1], d).astype(BF16)
    w_o16 = w_o.astype(BF16)

    xf = x.reshape(batch * seq, d)
    x16 = xf.astype(BF16)
    for l in range(depth):
        lw = dict(w_in_t=w_in_t, g_cq=g_cq[l][None], g_ckv=g_ckv[l][None], w_uq=w_uq_p[l], w_uk=w_uk[l],
                  w_uvt=w_uvt[l], conv_w=conv_w[l], conv_b=conv_b[l][None], w_rg_a=wa_bd[l],
                  b_rg_a=b_rg_a[l][None], w_rg_x=wx_bd[l], b_rg_x=b_rg_x[l][None], lam=lru_lambda[l][None],
                  w_mem_k=w_mem_k, w_mem_v=w_mem_v, w_o=w_o16, ln_g=ln_g[l][None], ln_b=ln_b[l][None])
        xf, x16 = _layer(xf, x16, mem16, tabs, lw, l, batch, seq)
    return xf.reshape(batch, seq, d)
```

```python
import functools

import numpy as np
import jax
import jax.numpy as jnp
from jax import lax
from jax.experimental import pallas as pl
from jax.experimental.pallas import tpu as pltpu

F32 = jnp.float32
BF16 = jnp.bfloat16
I32 = jnp.int32

DEPTH = 4
D_MODEL = 4096
N_MEM = 256
GROUP_WIDTH = D_MODEL // 4
ROPE_THETA = 500000.0
MLA_HEADS = 8
MLA_NOPE = 128
MLA_ROPE = 64
MLA_V = 128
MLA_Q_LORA = GROUP_WIDTH
MLA_KV_LORA = GROUP_WIDTH // 2
DSA_HEADS = 8
DSA_KV_HEADS = 2
DSA_GROUP = DSA_HEADS // DSA_KV_HEADS
DSA_HEAD_DIM = 128
DSA_ROT = DSA_HEAD_DIM // 4
IDX_HEADS = 16
IDX_DIM = 64
IDX_ROT = IDX_DIM // 4
TOPK_MAX = 256
LRU_WIDTH = GROUP_WIDTH
LRU_BLOCKS = 16
LRU_BLOCK_DIM = LRU_WIDTH // LRU_BLOCKS
CONV_WIDTH = 4
LRU_C = 8.0
MEM_HEADS = 4
MEM_HEAD_DIM = GROUP_WIDTH // MEM_HEADS
DEEPNORM_ALPHA = (2 * DEPTH) ** 0.25
LN_EPS = 1e-5
RMS_EPS = 1e-6

IN_SIZES = (
    MLA_Q_LORA, MLA_KV_LORA, MLA_ROPE, GROUP_WIDTH,
    GROUP_WIDTH, DSA_KV_HEADS * DSA_HEAD_DIM, DSA_KV_HEADS * DSA_HEAD_DIM,
    IDX_HEADS * IDX_DIM, IDX_DIM, IDX_HEADS, GROUP_WIDTH,
    LRU_WIDTH, GROUP_WIDTH,
    GROUP_WIDTH, GROUP_WIDTH,
)

LANES = 128
V7X_VMEM_BUDGET = 60 * 1024 * 1024

COL_CQ = 0
COL_AG = 1024
COL_BQ = 2048
COL_QI = 3072
COL_BG = 4096
COL_U = 5120
COL_CG = 6144
COL_DQ = 7168
COL_DG = 8192
COL_CKV = 9216
COL_BK = 9728
COL_BV = 9984
N_MAIN = 10240
SMALL_KR = 0
SMALL_KI = 128

TAB_MLA_C, TAB_MLA_S, TAB_DSA_C, TAB_DSA_S, TAB_IDX_C, TAB_IDX_S = range(6)
N_TABS = 6

NEG_BIG = -1e30
INT_MIN = -2 ** 31
LOG2E = 1.4426950408889634
KEY_CHUNK = 256
DSA_QBLK = 128
VT_ROWS = 128 + 16
_NT = (((1,), (1,)), ((), ()))


def _vmem_limit(nbytes):
    return int(min(V7X_VMEM_BUDGET, max(nbytes, 16 * 1024 * 1024)))


def _params(nbytes, ndims):
    return pltpu.CompilerParams(dimension_semantics=("arbitrary",) * ndims,
                                vmem_limit_bytes=_vmem_limit(nbytes))


def _mm_kernel(*refs, n_lhs, kg):
    w_ref = refs[n_lhs]
    o_ref = refs[n_lhs + 1]
    acc = None
    for g in range(n_lhs):
        part = jnp.dot(refs[g][...], w_ref[g * kg:(g + 1) * kg, :].astype(BF16), preferred_element_type=F32)
        acc = part if acc is None else acc + part
    o_ref[...] = acc.astype(o_ref.dtype)


def _matmul(lhs_list, w, layer, out_dtype, tm, tn, name):
    m, kg = lhs_list[0].shape
    _, k, n = w.shape
    n_lhs = len(lhs_list)
    assert k == kg * n_lhs and m % tm == 0 and n % tn == 0
    est = 2 * (tm * k * 2 + k * tn * w.dtype.itemsize + tm * tn * jnp.dtype(out_dtype).itemsize) + tm * tn * 8
    return pl.pallas_call(
        functools.partial(_mm_kernel, n_lhs=n_lhs, kg=kg),
        grid=(m // tm, n // tn),
        in_specs=[pl.BlockSpec((tm, kg), lambda i, j: (i, 0)) for _ in range(n_lhs)]
        + [pl.BlockSpec((None, k, tn), lambda i, j: (layer, 0, j))],
        out_specs=pl.BlockSpec((tm, tn), lambda i, j: (i, j)),
        out_shape=jax.ShapeDtypeStruct((m, n), out_dtype),
        compiler_params=_params(est + (8 << 20), 2),
        name=name,
    )(*lhs_list, w)


def _in_proj_kernel(tab_ref, x_ref, w_ref, o_ref):
    del tab_ref
    o_ref[...] = lax.dot_general(x_ref[...], w_ref[...].astype(BF16), _NT, preferred_element_type=F32)


def _in_proj(x16, w_t, layer, row_offsets, tm, tn, name):
    m, k = x16.shape
    nblk = len(row_offsets)
    est = 2 * (tm * k * 2 + tn * k * 4 + tm * tn * 4) + tm * tn * 8
    grid_spec = pltpu.PrefetchScalarGridSpec(
        num_scalar_prefetch=1,
        grid=(m // tm, nblk),
        in_specs=[
            pl.BlockSpec((tm, k), lambda i, j, tab: (i, 0)),
            pl.BlockSpec((pl.Squeezed(), pl.Element(tn), pl.Element(k)),
                         lambda i, j, tab: (layer, pl.multiple_of(tab[j], 16), 0)),
        ],
        out_specs=pl.BlockSpec((tm, tn), lambda i, j, tab: (i, j)),
    )
    return pl.pallas_call(
        _in_proj_kernel,
        grid_spec=grid_spec,
        out_shape=jax.ShapeDtypeStruct((m, nblk * tn), F32),
        compiler_params=_params(est + (8 << 20), 2),
        name=name,
    )(jnp.asarray(row_offsets, I32), x16, w_t)


def _in_proj_row_tables(tn):
    names = ("cq", "ckv", "kr", "ag", "bq", "bk", "bv", "qi", "ki", "wi", "bg", "u", "cg", "dq", "dg")
    src = dict(zip(names, [0] + [int(c) for c in np.cumsum(IN_SIZES)[:-1]]))
    size = dict(zip(names, IN_SIZES))
    order = [("cq", COL_CQ), ("ag", COL_AG), ("bq", COL_BQ), ("qi", COL_QI), ("bg", COL_BG), ("u", COL_U),
             ("cg", COL_CG), ("dq", COL_DQ), ("dg", COL_DG), ("ckv", COL_CKV), ("bk", COL_BK), ("bv", COL_BV)]
    main = []
    for n, dst in order:
        assert dst == len(main) * tn and size[n] % tn == 0 and src[n] % 16 == 0
        main += [src[n] + t * tn for t in range(size[n] // tn)]
    assert len(main) * tn == N_MAIN and src["wi"] == src["ki"] + size["ki"]
    return np.asarray(main, np.int32), np.asarray([src["kr"], src["ki"]], np.int32)


def _rope(v, c_tab, s_tab, half, period):
    width = v.shape[-1]
    lane = lax.broadcasted_iota(I32, v.shape, 1) & (period - 1)
    swapped = jnp.where(lane < half, pltpu.roll(v, width - half, 1), pltpu.roll(v, half, 1))
    return v * c_tab + swapped * s_tab


def _tile_lanes(t, reps):
    return t if reps == 1 else jnp.concatenate([t] * reps, axis=1)


def _rms(x, g):
    return x * lax.rsqrt(jnp.mean(x * x, axis=-1, keepdims=True) + RMS_EPS) * g


def _silu(g):
    return g * jax.nn.sigmoid(g)


def _prep_mla_kernel(cq_ref, ckv_ref, kr_ref, tab_ref, gcq_ref, gckv_ref, wuq_ref, wk_ref, wvt_ref,
                     q_out, k_out, vt_out):
    scale = (MLA_NOPE + MLA_ROPE) ** -0.5 * LOG2E
    nq = _rms(cq_ref[...], gcq_ref[...]).astype(BF16)
    nkv = _rms(ckv_ref[...], gckv_ref[...]).astype(BF16)
    c_tab = tab_ref[:, TAB_MLA_C * LANES:(TAB_MLA_C + 1) * LANES]
    s_tab = tab_ref[:, TAB_MLA_S * LANES:(TAB_MLA_S + 1) * LANES]
    qf = jnp.dot(nq, wuq_ref[...], preferred_element_type=F32) * scale
    kn = jnp.dot(nkv, wk_ref[...], preferred_element_type=F32)
    lane = lax.broadcasted_iota(I32, kr_ref.shape, 1)
    kr = jnp.where(lane < MLA_ROPE, kr_ref[...], 0.0)
    kr = _rope(kr, c_tab, s_tab, MLA_ROPE // 2, LANES).astype(BF16)
    for h in range(MLA_HEADS):
        base = h * 2 * LANES
        q_out[:, base:base + LANES] = qf[:, base:base + LANES].astype(BF16)
        q_out[:, base + LANES:base + 2 * LANES] = _rope(
            qf[:, base + LANES:base + 2 * LANES], c_tab, s_tab, MLA_ROPE // 2, LANES).astype(BF16)
        k_out[:, base:base + LANES] = kn[:, h * LANES:(h + 1) * LANES].astype(BF16)
        k_out[:, base + LANES:base + 2 * LANES] = kr
    vt = lax.dot_general(wvt_ref[...], nkv, _NT, preferred_element_type=F32)
    half = vt.shape[1] // 2
    vt_out[0] = _vt_with_ones(vt[:, :half], MLA_HEADS, MLA_V)
    vt_out[1] = _vt_with_ones(vt[:, half:], MLA_HEADS, MLA_V)


def _prep_mla(h2d, hs, tabs, g_cq, g_ckv, wuq_p, wk, wvt, batch, seq, tm):
    m = batch * seq
    nt = seq // tm
    hd = MLA_HEADS * 2 * LANES
    return pl.pallas_call(
        _prep_mla_kernel,
        grid=(batch, nt),
        in_specs=[
            pl.BlockSpec((tm, MLA_Q_LORA), lambda b, i: (b * nt + i, COL_CQ // MLA_Q_LORA)),
            pl.BlockSpec((tm, MLA_KV_LORA), lambda b, i: (b * nt + i, COL_CKV // MLA_KV_LORA)),
            pl.BlockSpec((tm, LANES), lambda b, i: (b * nt + i, SMALL_KR // LANES)),
            pl.BlockSpec((tm, N_TABS * LANES), lambda b, i: (b * nt + i, 0)),
            pl.BlockSpec((1, MLA_Q_LORA), lambda b, i: (0, 0)),
            pl.BlockSpec((1, MLA_KV_LORA), lambda b, i: (0, 0)),
            pl.BlockSpec((MLA_Q_LORA, hd), lambda b, i: (0, 0)),
            pl.BlockSpec((MLA_KV_LORA, MLA_HEADS * MLA_NOPE), lambda b, i: (0, 0)),
            pl.BlockSpec((MLA_HEADS * MLA_V, MLA_KV_LORA), lambda b, i: (0, 0)),
        ],
        out_specs=[
            pl.BlockSpec((tm, hd), lambda b, i: (b * nt + i, 0)),
            pl.BlockSpec((tm, hd), lambda b, i: (b * nt + i, 0)),
            pl.BlockSpec((None, 2, MLA_HEADS * VT_ROWS, tm // 2), lambda b, i: (b, i, 0, 0)),
        ],
        out_shape=[
            jax.ShapeDtypeStruct((m, hd), BF16),
            jax.ShapeDtypeStruct((m, hd), BF16),
            jax.ShapeDtypeStruct((batch, 2 * nt, MLA_HEADS * VT_ROWS, tm // 2), BF16),
        ],
        compiler_params=_params(40 << 20, 2),
        name="prep_mla",
    )(h2d, h2d, hs, tabs, g_cq, g_ckv, wuq_p, wk, wvt)


def _softmax_step(s, col_max, vt_chunk, m, acc):
    m_new = jnp.maximum(m, col_max)
    alpha = jnp.exp2(m - m_new)
    p = jnp.exp2(s - m_new)
    acc_new = alpha * acc + jnp.dot(vt_chunk, p.astype(BF16), preferred_element_type=F32)
    return m_new, acc_new


def _normalised_t(acc, dv):
    return (acc[:dv, :] / acc[dv:dv + 1, :]).T


def _vt_with_ones(vt, heads, dv):
    ones = jnp.ones((VT_ROWS - dv, vt.shape[1]), BF16)
    parts = []
    for h in range(heads):
        parts += [vt[h * dv:(h + 1) * dv, :].astype(BF16), ones]
    return jnp.concatenate(parts, axis=0)


def _mla_attn_kernel(q_ref, k_ref, vt_ref, gate_ref, o_ref, s0_ref, s1_ref, *, tile, ch):
    i = pl.program_id(2)
    q = q_ref[...]
    slots = (s0_ref, s1_ref)

    def produce(c, slot, diag_offset=None):
        kc = k_ref[pl.ds(pl.multiple_of(c * ch, ch), ch), :]
        s = lax.dot_general(kc, q, _NT, preferred_element_type=F32)
        if diag_offset is not None:
            kidx = lax.broadcasted_iota(I32, (ch, tile), 0) + diag_offset
            qidx = lax.broadcasted_iota(I32, (ch, tile), 1)
            s = jnp.where(kidx <= qidx, s, NEG_BIG)
        slots[slot][...] = s
        return jnp.max(s, axis=0, keepdims=True)

    def consume(c, slot, col_max, carry):
        return _softmax_step(slots[slot][...], col_max, vt_ref[c], *carry)

    init = (jnp.full((1, tile), NEG_BIG, F32), jnp.zeros((VT_ROWS, tile), F32))
    diag = 2 * i
    col_a = produce(diag, 0, 0)
    col_b = produce(diag + 1, 1, ch)
    carry = consume(diag, 0, col_a, init)

    def pair(j, state):
        cr, col1, c1 = state
        col0 = produce(2 * j, 0)
        cr = consume(c1, 1, col1, cr)
        col1 = produce(2 * j + 1, 1)
        cr = consume(2 * j, 0, col0, cr)
        return cr, col1, 2 * j + 1

    state = lax.fori_loop(0, i // 2, lambda jj, st: pair(2 * jj + 1, pair(2 * jj, st)),
                          (carry, col_b, diag + 1))
    carry, col1, c1 = lax.fori_loop(0, i & 1, lambda _, st: pair(i - 1, st), state)
    _, acc = consume(c1, 1, col1, carry)
    o = _normalised_t(acc, MLA_V)
    o_ref[...] = (o * _silu(gate_ref[...])).astype(o_ref.dtype)


def _mla_attn(q, k, vt, h2d, batch, seq, tile):
    m = batch * seq
    nt = seq // tile
    ch = tile // 2
    return pl.pallas_call(
        functools.partial(_mla_attn_kernel, tile=tile, ch=ch),
        grid=(batch, MLA_HEADS, nt),
        in_specs=[
            pl.BlockSpec((tile, 2 * LANES), lambda b, h, i: (b * nt + i, h)),
            pl.BlockSpec((seq, 2 * LANES), lambda b, h, i: (b, h)),
            pl.BlockSpec((None, 2 * nt, VT_ROWS, ch), lambda b, h, i: (b, 0, h, 0)),
            pl.BlockSpec((tile, LANES), lambda b, h, i: (b * nt + i, COL_AG // LANES + h)),
        ],
        out_specs=pl.BlockSpec((tile, LANES), lambda b, h, i: (b * nt + i, h)),
        out_shape=jax.ShapeDtypeStruct((m, GROUP_WIDTH), BF16),
        scratch_shapes=[pltpu.VMEM((ch, tile), F32), pltpu.VMEM((ch, tile), F32)],
        compiler_params=_params(32 << 20, 3),
        name="mla_attn",
    )(q, k, vt, h2d)


def _hi_lo(v):
    hi = v.astype(BF16).astype(F32)
    return hi, v - hi


def _prep_dsa_kernel(q_ref, k_ref, v_ref, qi_ref, ki_ref, tab_ref,
                     q_out, k_out, vt_out, qi4_out, ki4_out, wt_out):
    def tab(t):
        return tab_ref[:, t * LANES:(t + 1) * LANES]

    scale = DSA_HEAD_DIM ** -0.5 * LOG2E
    q = q_ref[...]
    q_out[...] = (_rope(q, _tile_lanes(tab(TAB_DSA_C), DSA_HEADS), _tile_lanes(tab(TAB_DSA_S), DSA_HEADS),
                        DSA_ROT // 2, LANES) * scale).astype(BF16)
    k = k_ref[...]
    k_out[...] = _rope(k, _tile_lanes(tab(TAB_DSA_C), DSA_KV_HEADS), _tile_lanes(tab(TAB_DSA_S), DSA_KV_HEADS),
                       DSA_ROT // 2, LANES).astype(BF16)
    vt_out[...] = _vt_with_ones(v_ref[...].T, DSA_KV_HEADS, DSA_HEAD_DIM)
    reps = IDX_HEADS * IDX_DIM // LANES
    qi = _rope(qi_ref[...], _tile_lanes(tab(TAB_IDX_C), reps), _tile_lanes(tab(TAB_IDX_S), reps),
               IDX_ROT // 2, IDX_DIM)
    qb = DSA_QBLK
    first_half = lax.broadcasted_iota(I32, (qb, LANES), 1) < IDX_DIM
    for blk in range(q.shape[0] // qb):
        for pair in range(IDX_HEADS // 2):
            hi, lo = _hi_lo(qi[blk * qb:(blk + 1) * qb, pair * LANES:(pair + 1) * LANES])
            even = jnp.where(first_half, hi, pltpu.roll(lo, IDX_DIM, 1)).astype(BF16)
            odd = jnp.where(first_half, pltpu.roll(hi, IDX_DIM, 1), lo).astype(BF16)
            for half in range(2):
                qi4_out[blk, pair, 0:qb, half * LANES:(half + 1) * LANES] = even
                qi4_out[blk, pair, qb:2 * qb, half * LANES:(half + 1) * LANES] = odd
    small = ki_ref[...]
    lane = lax.broadcasted_iota(I32, small.shape, 1)
    c_ki = jnp.where(lane < IDX_DIM, tab(TAB_IDX_C), 1.0)
    s_ki = jnp.where(lane < IDX_DIM, tab(TAB_IDX_S), 0.0)
    hi, lo = _hi_lo(_rope(small, c_ki, s_ki, IDX_ROT // 2, IDX_DIM))
    ki4_out[:, 0:LANES] = jnp.where(lane < IDX_DIM, hi, pltpu.roll(hi, IDX_DIM, 1)).astype(BF16)
    ki4_out[:, LANES:2 * LANES] = jnp.where(lane < IDX_DIM, lo, pltpu.roll(lo, IDX_DIM, 1)).astype(BF16)
    wt = small.T[IDX_DIM:IDX_DIM + IDX_HEADS, :]
    wt_out[...] = wt * (IDX_HEADS ** -0.5 * IDX_DIM ** -0.5)


def _prep_dsa(h2d, hs, tabs, batch, seq):
    tm = KEY_CHUNK
    m = batch * seq
    nt = seq // tm
    qpt = tm // DSA_QBLK
    kvw = DSA_KV_HEADS * DSA_HEAD_DIM

    def row(b, i):
        return b * nt + i

    return pl.pallas_call(
        _prep_dsa_kernel,
        grid=(batch, nt),
        in_specs=[
            pl.BlockSpec((tm, GROUP_WIDTH), lambda b, i: (row(b, i), COL_BQ // GROUP_WIDTH)),
            pl.BlockSpec((tm, kvw), lambda b, i: (row(b, i), COL_BK // kvw)),
            pl.BlockSpec((tm, kvw), lambda b, i: (row(b, i), COL_BV // kvw)),
            pl.BlockSpec((tm, GROUP_WIDTH), lambda b, i: (row(b, i), COL_QI // GROUP_WIDTH)),
            pl.BlockSpec((tm, LANES), lambda b, i: (row(b, i), SMALL_KI // LANES)),
            pl.BlockSpec((tm, N_TABS * LANES), lambda b, i: (row(b, i), 0)),
        ],
        out_specs=[
            pl.BlockSpec((tm, GROUP_WIDTH), lambda b, i: (row(b, i), 0)),
            pl.BlockSpec((tm, kvw), lambda b, i: (row(b, i), 0)),
            pl.BlockSpec((None, None, DSA_KV_HEADS * VT_ROWS, tm), lambda b, i: (b, i, 0, 0)),
            pl.BlockSpec((None, qpt, IDX_HEADS // 2, 2 * DSA_QBLK, 4 * IDX_DIM), lambda b, i: (b, i, 0, 0, 0)),
            pl.BlockSpec((tm, 4 * IDX_DIM), lambda b, i: (row(b, i), 0)),
            pl.BlockSpec((None, IDX_HEADS, tm), lambda b, i: (b, 0, i)),
        ],
        out_shape=[
            jax.ShapeDtypeStruct((m, GROUP_WIDTH), BF16),
            jax.ShapeDtypeStruct((m, kvw), BF16),
            jax.ShapeDtypeStruct((batch, nt, DSA_KV_HEADS * VT_ROWS, tm), BF16),
            jax.ShapeDtypeStruct((batch, seq // DSA_QBLK, IDX_HEADS // 2, 2 * DSA_QBLK, 4 * IDX_DIM), BF16),
            jax.ShapeDtypeStruct((m, 4 * IDX_DIM), BF16),
            jax.ShapeDtypeStruct((batch, IDX_HEADS, seq), F32),
        ],
        compiler_params=_params(24 << 20, 2),
        name="prep_dsa",
    )(h2d, h2d, h2d, h2d, hs, tabs)


def _sortable_key(score):
    bits = lax.bitcast_convert_type(score, I32)
    key = jnp.where(bits < 0, bits ^ jnp.int32(0x7FFFFFFF), bits)
    return jnp.where(score == 0.0, 0, key)


def _dsa_kernel(qi4_ref, ki4_ref, wt_ref, q_ref, k_ref, vt_ref, gate_ref, o_ref, key_ref, s0_ref, s1_ref,
                *, topk, seq_bits, grp):
    i = pl.program_id(1)
    n_vis = i // 2
    n_chunks = n_vis + 1
    wt = wt_ref[...]
    ch = KEY_CHUNK
    qb = DSA_QBLK

    def rows(c):
        return pl.ds(pl.multiple_of(c * ch, ch), ch)

    def score_chunk(c):
        kc = ki4_ref[rows(c), :]
        tot = None
        for p in range(IDX_HEADS // 2):
            s = lax.dot_general(kc, qi4_ref[p], _NT, preferred_element_type=F32)
            r = jnp.maximum(s, 0.0)
            t = r[:, :qb] * wt[2 * p:2 * p + 1, :] + r[:, qb:] * wt[2 * p + 1:2 * p + 2, :]
            tot = t if tot is None else tot + t
        return tot

    def phase1(c, carry):
        key_ref[rows(c), :] = _sortable_key(score_chunk(c))
        return carry

    lax.fori_loop(0, n_vis // 2, lambda p, cr: phase1(2 * p + 1, phase1(2 * p, cr)), 0)
    lax.fori_loop(0, n_vis & 1, lambda _, cr: phase1(n_vis - 1, cr), 0)
    kidx = n_vis * ch + lax.broadcasted_iota(I32, (ch, qb), 0)
    qidx = i * qb + lax.broadcasted_iota(I32, (ch, qb), 1)
    key_ref[rows(n_vis), :] = jnp.where(kidx <= qidx, _sortable_key(score_chunk(n_vis)), INT_MIN)

    n_groups = (n_chunks + grp - 1) // grp

    def pad_chunk(c, carry):
        key_ref[rows(c), :] = jnp.full((ch, qb), INT_MIN, I32)
        return carry

    lax.fori_loop(n_chunks, n_groups * grp, pad_chunk, 0)

    def count_ge(t):
        def body(p, acc):
            blk = key_ref[pl.ds(pl.multiple_of(p * grp * ch, grp * ch), grp * ch), :]
            hit = (blk >= t).astype(I32)
            return acc + jnp.sum(hit.reshape(grp * ch // 8, 8, qb), axis=0)
        acc = lax.fori_loop(0, n_groups, body, jnp.zeros((8, qb), I32))
        return jnp.sum(acc, axis=0, keepdims=True)

    thr = jnp.where(count_ge(jnp.zeros((1, qb), I32)) >= topk, 0, INT_MIN).astype(I32)

    def bit_step(b, t):
        cand = t + jnp.left_shift(jnp.int32(1), 30 - b)
        return jnp.where(count_ge(cand) >= topk, cand, t)

    thr = lax.fori_loop(0, 31, bit_step, thr)
    found = thr > INT_MIN
    c_ge = count_ge(jnp.maximum(thr, INT_MIN + 1))
    need = topk - count_ge(thr + 1)
    tied = jnp.logical_and(c_ge > topk, found)

    def count(pred):
        def body(c, acc):
            idx = c * ch + lax.broadcasted_iota(I32, (ch, qb), 0)
            hit = pred(key_ref[rows(c), :], idx).astype(I32)
            return acc + jnp.sum(hit.reshape(ch // 8, 8, qb), axis=0)
        acc = lax.fori_loop(0, n_chunks, body, jnp.zeros((8, qb), I32))
        return jnp.sum(acc, axis=0, keepdims=True)

    def tie_search():
        def step(b, lo):
            cand = lo + jnp.left_shift(jnp.int32(1), seq_bits - 1 - b)
            below = count(lambda keys, idx: jnp.logical_and(keys == thr, idx < cand))
            return jnp.where(below < need, cand, lo)
        return lax.fori_loop(0, seq_bits, step, jnp.zeros((1, qb), I32))

    last_eq = lax.cond(jnp.max(tied.astype(I32)) > 0, tie_search,
                       lambda: jnp.full((1, qb), 2 ** seq_bits, I32))
    thr_sel = jnp.maximum(thr, INT_MIN + 1)

    nq = DSA_GROUP * qb
    one_hot = (lax.broadcasted_iota(I32, (nq, qb), 0) % qb == lax.broadcasted_iota(I32, (nq, qb), 1))
    one_hot = jnp.where(one_hot, 1.0, 0.0).astype(BF16)
    qs = [jnp.concatenate([jnp.concatenate([q_ref[:, (n * DSA_GROUP + g) * LANES:(n * DSA_GROUP + g + 1) * LANES]
                                            for g in range(DSA_GROUP)], axis=0), one_hot], axis=1)
          for n in range(DSA_KV_HEADS)]

    slots = (s0_ref, s1_ref)

    def produce(c, slot):
        keys = key_ref[rows(c), :]
        idx = c * ch + lax.broadcasted_iota(I32, (ch, qb), 0)
        drop = jnp.logical_and(keys == thr, idx > last_eq)
        sel = jnp.logical_and(keys >= thr_sel, jnp.logical_not(drop))
        neg = jnp.where(sel, 0.0, NEG_BIG).astype(BF16)
        col_max = []
        for n in range(DSA_KV_HEADS):
            kc = jnp.concatenate([k_ref[rows(c), n * LANES:(n + 1) * LANES], neg], axis=1)
            s = lax.dot_general(kc, qs[n], _NT, preferred_element_type=F32)
            slots[slot][n] = s
            col_max.append(jnp.max(s, axis=0, keepdims=True))
        return tuple(col_max)

    def consume(c, slot, col_max, carry):
        return tuple(_softmax_step(slots[slot][n], col_max[n],
                                   vt_ref[c, n * VT_ROWS:(n + 1) * VT_ROWS, :], *carry[n])
                     for n in range(DSA_KV_HEADS))

    def phase3(j, state):
        carry, col0 = state
        col1 = produce(2 * j + 1, 1)
        carry = consume(2 * j, 0, col0, carry)
        col0 = produce(2 * j + 2, 0)
        carry = consume(2 * j + 1, 1, col1, carry)
        return carry, col0

    init = tuple((jnp.full((1, nq), NEG_BIG, F32), jnp.zeros((VT_ROWS, nq), F32))
                 for _ in range(DSA_KV_HEADS))
    pairs = (n_chunks - 1) // 2
    state = lax.fori_loop(0, pairs // 2, lambda jj, st: phase3(2 * jj + 1, phase3(2 * jj, st)),
                          (init, produce(0, 0)))
    carry, col0 = lax.fori_loop(0, pairs & 1, lambda _, st: phase3(pairs - 1, st), state)
    last = 2 * pairs

    def tail_two(cr):
        col1 = produce(last + 1, 1)
        cr = consume(last, 0, col0, cr)
        return consume(last + 1, 1, col1, cr)

    res = lax.cond(n_chunks - 1 > last, tail_two, lambda cr: consume(last, 0, col0, cr), carry)
    for n in range(DSA_KV_HEADS):
        o = _normalised_t(res[n][1], DSA_HEAD_DIM)
        for g in range(DSA_GROUP):
            col = (n * DSA_GROUP + g) * LANES
            o_ref[:, col:col + LANES] = (o[g * qb:(g + 1) * qb, :]
                                         * _silu(gate_ref[:, col:col + LANES])).astype(o_ref.dtype)


def _dsa(qi4, ki4, wt, q, k, vt, h2d, batch, seq):
    m = batch * seq
    nq = seq // DSA_QBLK
    nck = seq // KEY_CHUNK
    kvw = DSA_KV_HEADS * DSA_HEAD_DIM
    topk = min(TOPK_MAX, seq // 4)
    seq_bits = int(np.log2(seq))
    assert 2 ** seq_bits == seq and nck % 2 == 0 and topk <= KEY_CHUNK
    return pl.pallas_call(
        functools.partial(_dsa_kernel, topk=topk, seq_bits=seq_bits, grp=2),
        grid=(batch, nq),
        in_specs=[
            pl.BlockSpec((None, None, IDX_HEADS // 2, 2 * DSA_QBLK, 4 * IDX_DIM),
                         lambda b, i: (b, i, 0, 0, 0)),
            pl.BlockSpec((seq, 4 * IDX_DIM), lambda b, i: (b, 0)),
            pl.BlockSpec((None, IDX_HEADS, DSA_QBLK), lambda b, i: (b, 0, i)),
            pl.BlockSpec((DSA_QBLK, GROUP_WIDTH), lambda b, i: (b * nq + i, 0)),
            pl.BlockSpec((seq, kvw), lambda b, i: (b, 0)),
            pl.BlockSpec((None, nck, DSA_KV_HEADS * VT_ROWS, KEY_CHUNK), lambda b, i: (b, 0, 0, 0)),
            pl.BlockSpec((DSA_QBLK, GROUP_WIDTH), lambda b, i: (b * nq + i, COL_BG // GROUP_WIDTH)),
        ],
        out_specs=pl.BlockSpec((DSA_QBLK, GROUP_WIDTH), lambda b, i: (b * nq + i, 0)),
        out_shape=jax.ShapeDtypeStruct((m, GROUP_WIDTH), BF16),
        scratch_shapes=[pltpu.VMEM((seq, DSA_QBLK), I32),
                        pltpu.VMEM((DSA_KV_HEADS, KEY_CHUNK, DSA_GROUP * DSA_QBLK), F32),
                        pltpu.VMEM((DSA_KV_HEADS, KEY_CHUNK, DSA_GROUP * DSA_QBLK), F32)],
        compiler_params=_params(32 << 20, 2),
        name="dsa",
    )(qi4, ki4, wt, q, k, vt, h2d)


def _rglru_kernel(u_ref, gate_ref, cw_ref, cb_ref, wa_ref, ba_ref, wx_ref, bx_ref, lam_ref,
                  o_ref, halo_ref, h_ref, a_scr, x_scr, y_scr, *, batch, tt):
    t = pl.program_id(0)

    @pl.when(t == 0)
    def _():
        halo_ref[...] = jnp.zeros_like(halo_ref)
        h_ref[...] = jnp.zeros_like(h_ref)

    lam = lam_ref[...]
    z = -lam
    softplus = jnp.maximum(z, 0.0) + jnp.log1p(jnp.exp(-jnp.abs(z)))
    row = lax.broadcasted_iota(I32, (tt, LRU_WIDTH), 0)
    for b in range(batch):
        u = u_ref[b]
        prev = halo_ref[b]
        p1, p2, p3 = prev[7:8, :], prev[6:7, :], prev[5:6, :]
        s1 = jnp.where(row == 0, p1, pltpu.roll(u, 1, 0))
        s2 = jnp.where(row == 0, p2, jnp.where(row == 1, p1, pltpu.roll(u, 2, 0)))
        s3 = jnp.where(row == 0, p3, jnp.where(row == 1, p2, jnp.where(row == 2, p1, pltpu.roll(u, 3, 0))))
        conv = (cw_ref[3:4, :] * u + cw_ref[2:3, :] * s1 + cw_ref[1:2, :] * s2 + cw_ref[0:1, :] * s3
                + cb_ref[...])
        halo_ref[b] = u[tt - 8:, :]
        cb16 = conv.astype(BF16)
        r = jax.nn.sigmoid(jnp.dot(cb16, wa_ref[...], preferred_element_type=F32) + ba_ref[...])
        ig = jax.nn.sigmoid(jnp.dot(cb16, wx_ref[...], preferred_element_type=F32) + bx_ref[...])
        log_a = -LRU_C * r * softplus
        a_scr[b] = jnp.exp(log_a)
        th = jnp.tanh(log_a)
        x_scr[b] = jnp.sqrt(-2.0 * th / (1.0 - th)) * (ig * conv)

    def step(s, h):
        h = a_scr[:, pl.ds(s, 1), :] * h + x_scr[:, pl.ds(s, 1), :]
        y_scr[:, pl.ds(s, 1), :] = h
        return h

    h_ref[...] = lax.fori_loop(0, tt, step, h_ref[...], unroll=8)
    o_ref[...] = (y_scr[...] * _silu(gate_ref[...])).astype(o_ref.dtype)


def _rglru(h3d, conv_w, conv_b, wa_bd, b_a, wx_bd, b_x, lam, batch, seq, tt):
    nt = seq // tt
    w = LRU_WIDTH

    def vec(r):
        return pl.BlockSpec((r, w), lambda t: (0, 0))

    return pl.pallas_call(
        functools.partial(_rglru_kernel, batch=batch, tt=tt),
        grid=(nt,),
        in_specs=[
            pl.BlockSpec((batch, tt, w), lambda t: (0, t, COL_U // w)),
            pl.BlockSpec((batch, tt, w), lambda t: (0, t, COL_CG // w)),
            vec(CONV_WIDTH), vec(1),
            pl.BlockSpec((w, w), lambda t: (0, 0)), vec(1),
            pl.BlockSpec((w, w), lambda t: (0, 0)), vec(1),
            vec(1),
        ],
        out_specs=pl.BlockSpec((batch, tt, w), lambda t: (0, t, 0)),
        out_shape=jax.ShapeDtypeStruct((batch, seq, w), BF16),
        scratch_shapes=[pltpu.VMEM((batch, 8, w), F32), pltpu.VMEM((batch, 1, w), F32),
                        pltpu.VMEM((batch, tt, w), F32), pltpu.VMEM((batch, tt, w), F32),
                        pltpu.VMEM((batch, tt, w), F32)],
        compiler_params=_params(32 << 20, 1),
        name="rglru",
    )(h3d, h3d, conv_w, conv_b, wa_bd, b_a, wx_bd, b_x, lam)


def _mem_attn_kernel(q_ref, gate_ref, km_ref, vm_ref, o_ref):
    scale = MEM_HEAD_DIM ** -0.5
    for h in range(MEM_HEADS):
        cols = slice(h * MEM_HEAD_DIM, (h + 1) * MEM_HEAD_DIM)
        qh = (q_ref[:, cols] * scale).astype(BF16)
        s = lax.dot_general(qh, km_ref[:, cols], _NT, preferred_element_type=F32)
        p = jnp.exp(s - jnp.max(s, axis=-1, keepdims=True))
        l = jnp.sum(p, axis=-1, keepdims=True)
        o = jnp.dot(p.astype(BF16), vm_ref[:, cols], preferred_element_type=F32) / l
        o_ref[:, cols] = (o * _silu(gate_ref[:, cols])).astype(o_ref.dtype)


def _mem_attn(h2d, km, vm, batch, seq, tm):
    m = batch * seq
    nt = seq // tm
    w = GROUP_WIDTH
    return pl.pallas_call(
        _mem_attn_kernel,
        grid=(batch, nt),
        in_specs=[
            pl.BlockSpec((tm, w), lambda b, i: (b * nt + i, COL_DQ // w)),
            pl.BlockSpec((tm, w), lambda b, i: (b * nt + i, COL_DG // w)),
            pl.BlockSpec((N_MEM, w), lambda b, i: (b, 0)),
            pl.BlockSpec((N_MEM, w), lambda b, i: (b, 0)),
        ],
        out_specs=pl.BlockSpec((tm, w), lambda b, i: (b * nt + i, 0)),
        out_shape=jax.ShapeDtypeStruct((m, w), BF16),
        compiler_params=_params(24 << 20, 2),
        name="mem_attn",
    )(h2d, h2d, km, vm)


def _out_ln_kernel(*refs, n_lhs, kg, nj, per_emit, d_model):
    lhs = refs[:n_lhs]
    w_ref, x_ref, g_ref, b_ref, o_ref, o16_ref, z_ref, mu_ref, rs_ref = refs[n_lhs:]
    j = pl.program_id(1)

    @pl.when(j < nj)
    def _():
        acc = None
        for g in range(n_lhs):
            part = jnp.dot(lhs[g][...], w_ref[g * kg:(g + 1) * kg, :].astype(BF16), preferred_element_type=F32)
            acc = part if acc is None else acc + part
        z_ref[j] = DEEPNORM_ALPHA * x_ref[...] + acc

    @pl.when(j == nj)
    def _():
        def lane_tiles(v):
            tiles = [v[:, t * LANES:(t + 1) * LANES] for t in range(v.shape[1] // LANES)]
            return functools.reduce(lambda a, b: a + b, tiles)

        tot = functools.reduce(lambda a, b: a + b, [lane_tiles(z_ref[jj]) for jj in range(nj)])
        mu = jnp.sum(tot, axis=-1, keepdims=True) * (1.0 / d_model)
        sq = None
        for jj in range(nj):
            dlt = z_ref[jj] - mu
            part = lane_tiles(dlt * dlt)
            sq = part if sq is None else sq + part
        mu_ref[...] = mu
        rs_ref[...] = lax.rsqrt(jnp.sum(sq, axis=-1, keepdims=True) * (1.0 / d_model) + LN_EPS)

    @pl.when(j >= nj)
    def _():
        e = j - nj
        tn = z_ref.shape[2]
        for t in range(per_emit):
            cols = slice(t * tn, (t + 1) * tn)
            out = (z_ref[e * per_emit + t] - mu_ref[...]) * rs_ref[...] * g_ref[:, cols] + b_ref[:, cols]
            o_ref[:, cols] = out
            o16_ref[:, cols] = out.astype(BF16)


def _out_proj_deepnorm(lhs_list, w, layer, x, ln_g, ln_b, tm, tn, te):
    m, kg = lhs_list[0].shape
    _, k, n = w.shape
    n_lhs = len(lhs_list)
    nj, ne, per_emit = n // tn, n // te, te // tn
    assert k == kg * n_lhs and m % tm == 0 and n % te == 0 and te % tn == 0

    def col(j):
        return jnp.minimum(j, nj - 1)

    def emit(j):
        return jnp.maximum(j - nj, 0)

    est = (nj * tm * tn * 4 + 2 * (tm * k * 2 + k * tn * 4 + tm * tn * 4 + tm * te * 6) + tm * tn * 8)
    return pl.pallas_call(
        functools.partial(_out_ln_kernel, n_lhs=n_lhs, kg=kg, nj=nj, per_emit=per_emit, d_model=n),
        grid=(m // tm, nj + ne),
        in_specs=[pl.BlockSpec((tm, kg), lambda i, j: (i, 0)) for _ in range(n_lhs)]
        + [pl.BlockSpec((None, k, tn), lambda i, j: (layer, 0, col(j))),
           pl.BlockSpec((tm, tn), lambda i, j: (i, col(j))),
           pl.BlockSpec((1, te), lambda i, j: (0, emit(j))),
           pl.BlockSpec((1, te), lambda i, j: (0, emit(j)))],
        out_specs=[pl.BlockSpec((tm, te), lambda i, j: (i, emit(j))),
                   pl.BlockSpec((tm, te), lambda i, j: (i, emit(j)))],
        out_shape=[jax.ShapeDtypeStruct((m, n), F32), jax.ShapeDtypeStruct((m, n), BF16)],
        scratch_shapes=[pltpu.VMEM((nj, tm, tn), F32), pltpu.VMEM((tm, 1), F32), pltpu.VMEM((tm, 1), F32)],
        compiler_params=_params(est + (10 << 20), 2),
        name="out_proj_deepnorm",
    )(*lhs_list, w, x, ln_g, ln_b)


def _rope_tabs(positions):
    pos = positions.astype(F32)
    b, s = pos.shape

    def one(rot, period):
        inv = ROPE_THETA ** (-jnp.arange(0, rot, 2, dtype=F32) / rot)
        ang = pos[:, :, None] * inv
        c, sn = jnp.cos(ang), jnp.sin(ang)
        rest = period - rot
        cp = jnp.concatenate([c, c, jnp.ones((b, s, rest), F32)], axis=-1)
        sp = jnp.concatenate([-sn, sn, jnp.zeros((b, s, rest), F32)], axis=-1)
        reps = LANES // period
        return jnp.tile(cp, (1, 1, reps)), jnp.tile(sp, (1, 1, reps))

    parts = one(MLA_ROPE, LANES) + one(DSA_ROT, LANES) + one(IDX_ROT, IDX_DIM)
    return jnp.concatenate(parts, axis=-1).reshape(b * s, N_TABS * LANES)


def _block_diag(w):
    l = w.shape[0]
    eye = jnp.eye(LRU_BLOCKS, dtype=w.dtype)
    bd = jnp.einsum('lnde,nm->lndme', w, eye)
    return bd.reshape(l, LRU_WIDTH, LRU_WIDTH).astype(BF16)


def _layer(x, x16, mem16, tabs, lw, l, batch, seq):
    m = batch * seq
    main_rows, small_rows = _in_proj_row_tables(256)
    h2d = _in_proj(x16, lw["w_in_t"], l, main_rows, tm=min(2048, m), tn=256, name="in_proj")
    hs = _in_proj(x16, lw["w_in_t"], l, small_rows, tm=min(512, m), tn=LANES, name="in_proj_small")
    tile_a = min(512, seq)
    qa, ka, vta = _prep_mla(h2d, hs, tabs, lw["g_cq"], lw["g_ckv"], lw["w_uq"], lw["w_uk"], lw["w_uvt"],
                            batch, seq, tile_a)
    ya = _mla_attn(qa, ka, vta, h2d, batch, seq, tile_a)
    qb, kb, vtb, qi4, ki4, wt = _prep_dsa(h2d, hs, tabs, batch, seq)
    yb = _dsa(qi4, ki4, wt, qb, kb, vtb, h2d, batch, seq)
    yc = _rglru(h2d.reshape(batch, seq, N_MAIN), lw["conv_w"], lw["conv_b"], lw["w_rg_a"], lw["b_rg_a"],
                lw["w_rg_x"], lw["b_rg_x"], lw["lam"], batch, seq, min(256, seq))
    yc = yc.reshape(m, GROUP_WIDTH)
    km = _matmul([mem16], lw["w_mem_k"], l, BF16, tm=mem16.shape[0], tn=256, name="mem_proj_k")
    vm = _matmul([mem16], lw["w_mem_v"], l, BF16, tm=mem16.shape[0], tn=256, name="mem_proj_v")
    yd = _mem_attn(h2d, km, vm, batch, seq, min(512, seq))
    return _out_proj_deepnorm([ya, yb, yc, yd], lw["w_o"], l, x, lw["ln_g"], lw["ln_b"],
                              tm=min(1024, m), tn=256, te=512)


def kernel(x, mem, positions, w_in, g_cq, g_ckv, w_uq, w_ukv, conv_w, conv_b, w_rg_a, b_rg_a, w_rg_x,
           b_rg_x, lru_lambda, w_mem_k, w_mem_v, w_o, ln_g, ln_b):
    batch, seq, d = x.shape
    depth = w_in.shape[0]
    tabs = _rope_tabs(positions)
    w_in_t = w_in.transpose(0, 2, 1)
    w_uq_p = jnp.pad(w_uq.reshape(depth, MLA_Q_LORA, MLA_HEADS, MLA_NOPE + MLA_ROPE),
                     [(0, 0), (0, 0), (0, 0), (0, 2 * LANES - MLA_NOPE - MLA_ROPE)])
    w_uq_p = w_uq_p.reshape(depth, MLA_Q_LORA, MLA_HEADS * 2 * LANES).astype(BF16)
    w_ukv4 = w_ukv.reshape(depth, MLA_KV_LORA, MLA_HEADS, MLA_NOPE + MLA_V)
    w_uk = w_ukv4[..., :MLA_NOPE].reshape(depth, MLA_KV_LORA, MLA_HEADS * MLA_NOPE).astype(BF16)
    w_uvt = w_ukv4[..., MLA_NOPE:].reshape(depth, MLA_KV_LORA, MLA_HEADS * MLA_V)
    w_uvt = w_uvt.transpose(0, 2, 1).astype(BF16)
    wa_bd = _block_diag(w_rg_a)
    wx_bd = _block_diag(w_rg_x)
    mem16 = mem.reshape(batch * mem.shape[1], d).astype(BF16)

    xf = x.reshape(batch * seq, d)
    x16 = xf.astype(BF16)
    for l in range(depth):
        lw = dict(w_in_t=w_in_t, g_cq=g_cq[l][None], g_ckv=g_ckv[l][None], w_uq=w_uq_p[l], w_uk=w_uk[l],
                  w_uvt=w_uvt[l], conv_w=conv_w[l], conv_b=conv_b[l][None], w_rg_a=wa_bd[l],
                  b_rg_a=b_rg_a[l][None], w_rg_x=wx_bd[l], b_rg_x=b_rg_x[l][None], lam=lru_lambda[l][None],
                  w_mem_k=w_mem_k, w_mem_v=w_mem_v, w_o=w_o, ln_g=ln_g[l][None], ln_b=ln_b[l][None])
        xf, x16 = _layer(xf, x16, mem16, tabs, lw, l, batch, seq)
    return xf.reshape(batch, seq, d)
```

```python
import functools

import numpy as np
import jax
import jax.numpy as jnp
from jax import lax
from jax.experimental import pallas as pl
from jax.experimental.pallas import tpu as pltpu

F32 = jnp.float32
BF16 = jnp.bfloat16
I32 = jnp.int32

DEPTH = 4
D_MODEL = 4096
N_MEM = 256
GROUP_WIDTH = D_MODEL // 4
ROPE_THETA = 500000.0
MLA_HEADS = 8
MLA_NOPE = 128
MLA_ROPE = 64
MLA_V = 128
MLA_Q_LORA = GROUP_WIDTH
MLA_KV_LORA = GROUP_WIDTH // 2
DSA_HEADS = 8
DSA_KV_HEADS = 2
DSA_GROUP = DSA_HEADS // DSA_KV_HEADS
DSA_HEAD_DIM = 128
DSA_ROT = DSA_HEAD_DIM // 4
IDX_HEADS = 16
IDX_DIM = 64
IDX_ROT = IDX_DIM // 4
TOPK_MAX = 256
LRU_WIDTH = GROUP_WIDTH
LRU_BLOCKS = 16
LRU_BLOCK_DIM = LRU_WIDTH // LRU_BLOCKS
CONV_WIDTH = 4
LRU_C = 8.0
MEM_HEADS = 4
MEM_HEAD_DIM = GROUP_WIDTH // MEM_HEADS
DEEPNORM_ALPHA = (2 * DEPTH) ** 0.25
LN_EPS = 1e-5
RMS_EPS = 1e-6

IN_SIZES = (
    MLA_Q_LORA, MLA_KV_LORA, MLA_ROPE, GROUP_WIDTH,
    GROUP_WIDTH, DSA_KV_HEADS * DSA_HEAD_DIM, DSA_KV_HEADS * DSA_HEAD_DIM,
    IDX_HEADS * IDX_DIM, IDX_DIM, IDX_HEADS, GROUP_WIDTH,
    LRU_WIDTH, GROUP_WIDTH,
    GROUP_WIDTH, GROUP_WIDTH,
)

LANES = 128
V7X_VMEM_BUDGET = 60 * 1024 * 1024

COL_CQ = 0
COL_AG = 1024
COL_BQ = 2048
COL_QI = 3072
COL_BG = 4096
COL_U = 5120
COL_CG = 6144
COL_DQ = 7168
COL_DG = 8192
COL_CKV = 9216
COL_BK = 9728
COL_BV = 9984
N_MAIN = 10240
SMALL_KR = 0
SMALL_KI = 128

TAB_MLA_C, TAB_MLA_S, TAB_DSA_C, TAB_DSA_S, TAB_IDX_C, TAB_IDX_S = range(6)
N_TABS = 6

NEG_BIG = -1e30
INT_MIN = -2 ** 31
LOG2E = 1.4426950408889634
KEY_CHUNK = 256
DSA_QBLK = 128
VT_ROWS = 128 + 16
_NT = (((1,), (1,)), ((), ()))


def _vmem_limit(nbytes):
    return int(min(V7X_VMEM_BUDGET, max(nbytes, 16 * 1024 * 1024)))


def _params(nbytes, ndims):
    return pltpu.CompilerParams(dimension_semantics=("arbitrary",) * ndims,
                                vmem_limit_bytes=_vmem_limit(nbytes))


def _mm_kernel(*refs, n_lhs, kg):
    w_ref = refs[n_lhs]
    o_ref = refs[n_lhs + 1]
    acc = None
    for g in range(n_lhs):
        part = jnp.dot(refs[g][...], w_ref[g * kg:(g + 1) * kg, :].astype(BF16), preferred_element_type=F32)
        acc = part if acc is None else acc + part
    o_ref[...] = acc.astype(o_ref.dtype)


def _matmul(lhs_list, w, layer, out_dtype, tm, tn, name):
    m, kg = lhs_list[0].shape
    _, k, n = w.shape
    n_lhs = len(lhs_list)
    assert k == kg * n_lhs and m % tm == 0 and n % tn == 0
    est = 2 * (tm * k * 2 + k * tn * w.dtype.itemsize + tm * tn * jnp.dtype(out_dtype).itemsize) + tm * tn * 8
    return pl.pallas_call(
        functools.partial(_mm_kernel, n_lhs=n_lhs, kg=kg),
        grid=(m // tm, n // tn),
        in_specs=[pl.BlockSpec((tm, kg), lambda i, j: (i, 0)) for _ in range(n_lhs)]
        + [pl.BlockSpec((None, k, tn), lambda i, j: (layer, 0, j))],
        out_specs=pl.BlockSpec((tm, tn), lambda i, j: (i, j)),
        out_shape=jax.ShapeDtypeStruct((m, n), out_dtype),
        compiler_params=_params(est + (8 << 20), 2),
        name=name,
    )(*lhs_list, w)


def _in_proj_kernel(tab_ref, x_ref, w_ref, o_ref):
    del tab_ref
    o_ref[...] = lax.dot_general(x_ref[...], w_ref[...].astype(BF16), _NT, preferred_element_type=F32)


def _in_proj(x16, w_t, layer, row_offsets, tm, tn, name):
    m, k = x16.shape
    nblk = len(row_offsets)
    est = 2 * (tm * k * 2 + tn * k * 4 + tm * tn * 4) + tm * tn * 8
    grid_spec = pltpu.PrefetchScalarGridSpec(
        num_scalar_prefetch=1,
        grid=(m // tm, nblk),
        in_specs=[
            pl.BlockSpec((tm, k), lambda i, j, tab: (i, 0)),
            pl.BlockSpec((pl.Squeezed(), pl.Element(tn), pl.Element(k)),
                         lambda i, j, tab: (layer, pl.multiple_of(tab[j], 16), 0)),
        ],
        out_specs=pl.BlockSpec((tm, tn), lambda i, j, tab: (i, j)),
    )
    return pl.pallas_call(
        _in_proj_kernel,
        grid_spec=grid_spec,
        out_shape=jax.ShapeDtypeStruct((m, nblk * tn), F32),
        compiler_params=_params(est + (8 << 20), 2),
        name=name,
    )(jnp.asarray(row_offsets, I32), x16, w_t)


def _in_proj_row_tables(tn):
    names = ("cq", "ckv", "kr", "ag", "bq", "bk", "bv", "qi", "ki", "wi", "bg", "u", "cg", "dq", "dg")
    src = dict(zip(names, [0] + [int(c) for c in np.cumsum(IN_SIZES)[:-1]]))
    size = dict(zip(names, IN_SIZES))
    order = [("cq", COL_CQ), ("ag", COL_AG), ("bq", COL_BQ), ("qi", COL_QI), ("bg", COL_BG), ("u", COL_U),
             ("cg", COL_CG), ("dq", COL_DQ), ("dg", COL_DG), ("ckv", COL_CKV), ("bk", COL_BK), ("bv", COL_BV)]
    main = []
    for n, dst in order:
        assert dst == len(main) * tn and size[n] % tn == 0 and src[n] % 16 == 0
        main += [src[n] + t * tn for t in range(size[n] // tn)]
    assert len(main) * tn == N_MAIN and src["wi"] == src["ki"] + size["ki"]
    return np.asarray(main, np.int32), np.asarray([src["kr"], src["ki"]], np.int32)


def _rope(v, c_tab, s_tab, half, period):
    width = v.shape[-1]
    lane = lax.broadcasted_iota(I32, v.shape, 1) & (period - 1)
    swapped = jnp.where(lane < half, pltpu.roll(v, width - half, 1), pltpu.roll(v, half, 1))
    return v * c_tab + swapped * s_tab


def _tile_lanes(t, reps):
    return t if reps == 1 else jnp.concatenate([t] * reps, axis=1)


def _rms(x, g):
    return x * lax.rsqrt(jnp.mean(x * x, axis=-1, keepdims=True) + RMS_EPS) * g


def _silu(g):
    return g * jax.nn.sigmoid(g)


def _prep_mla_kernel(cq_ref, ckv_ref, kr_ref, tab_ref, gcq_ref, gckv_ref, wuq_ref, wk_ref, wvt_ref,
                     q_out, k_out, vt_out):
    scale = (MLA_NOPE + MLA_ROPE) ** -0.5 * LOG2E
    nq = _rms(cq_ref[...], gcq_ref[...]).astype(BF16)
    nkv = _rms(ckv_ref[...], gckv_ref[...]).astype(BF16)
    c_tab = tab_ref[:, TAB_MLA_C * LANES:(TAB_MLA_C + 1) * LANES]
    s_tab = tab_ref[:, TAB_MLA_S * LANES:(TAB_MLA_S + 1) * LANES]
    qf = jnp.dot(nq, wuq_ref[...], preferred_element_type=F32) * scale
    kn = jnp.dot(nkv, wk_ref[...], preferred_element_type=F32)
    lane = lax.broadcasted_iota(I32, kr_ref.shape, 1)
    kr = jnp.where(lane < MLA_ROPE, kr_ref[...], 0.0)
    kr = _rope(kr, c_tab, s_tab, MLA_ROPE // 2, LANES).astype(BF16)
    for h in range(MLA_HEADS):
        base = h * 2 * LANES
        q_out[:, base:base + LANES] = qf[:, base:base + LANES].astype(BF16)
        q_out[:, base + LANES:base + 2 * LANES] = _rope(
            qf[:, base + LANES:base + 2 * LANES], c_tab, s_tab, MLA_ROPE // 2, LANES).astype(BF16)
        k_out[:, base:base + LANES] = kn[:, h * LANES:(h + 1) * LANES].astype(BF16)
        k_out[:, base + LANES:base + 2 * LANES] = kr
    vt = lax.dot_general(wvt_ref[...], nkv, _NT, preferred_element_type=F32)
    half = vt.shape[1] // 2
    vt_out[0] = _vt_with_ones(vt[:, :half], MLA_HEADS, MLA_V)
    vt_out[1] = _vt_with_ones(vt[:, half:], MLA_HEADS, MLA_V)


def _prep_mla(h2d, hs, tabs, g_cq, g_ckv, wuq_p, wk, wvt, batch, seq, tm):
    m = batch * seq
    nt = seq // tm
    hd = MLA_HEADS * 2 * LANES
    return pl.pallas_call(
        _prep_mla_kernel,
        grid=(batch, nt),
        in_specs=[
            pl.BlockSpec((tm, MLA_Q_LORA), lambda b, i: (b * nt + i, COL_CQ // MLA_Q_LORA)),
            pl.BlockSpec((tm, MLA_KV_LORA), lambda b, i: (b * nt + i, COL_CKV // MLA_KV_LORA)),
            pl.BlockSpec((tm, LANES), lambda b, i: (b * nt + i, SMALL_KR // LANES)),
            pl.BlockSpec((tm, N_TABS * LANES), lambda b, i: (b * nt + i, 0)),
            pl.BlockSpec((1, MLA_Q_LORA), lambda b, i: (0, 0)),
            pl.BlockSpec((1, MLA_KV_LORA), lambda b, i: (0, 0)),
            pl.BlockSpec((MLA_Q_LORA, hd), lambda b, i: (0, 0)),
            pl.BlockSpec((MLA_KV_LORA, MLA_HEADS * MLA_NOPE), lambda b, i: (0, 0)),
            pl.BlockSpec((MLA_HEADS * MLA_V, MLA_KV_LORA), lambda b, i: (0, 0)),
        ],
        out_specs=[
            pl.BlockSpec((tm, hd), lambda b, i: (b * nt + i, 0)),
            pl.BlockSpec((tm, hd), lambda b, i: (b * nt + i, 0)),
            pl.BlockSpec((None, 2, MLA_HEADS * VT_ROWS, tm // 2), lambda b, i: (b, i, 0, 0)),
        ],
        out_shape=[
            jax.ShapeDtypeStruct((m, hd), BF16),
            jax.ShapeDtypeStruct((m, hd), BF16),
            jax.ShapeDtypeStruct((batch, 2 * nt, MLA_HEADS * VT_ROWS, tm // 2), BF16),
        ],
        compiler_params=_params(40 << 20, 2),
        name="prep_mla",
    )(h2d, h2d, hs, tabs, g_cq, g_ckv, wuq_p, wk, wvt)


def _softmax_step(s, col_max, vt_chunk, m, acc):
    m_new = jnp.maximum(m, col_max)
    alpha = jnp.exp2(m - m_new)
    p = jnp.exp2(s - m_new)
    acc_new = alpha * acc + jnp.dot(vt_chunk, p.astype(BF16), preferred_element_type=F32)
    return m_new, acc_new


def _normalised_t(acc, dv):
    return (acc[:dv, :] / acc[dv:dv + 1, :]).T


def _vt_with_ones(vt, heads, dv):
    ones = jnp.ones((VT_ROWS - dv, vt.shape[1]), BF16)
    parts = []
    for h in range(heads):
        parts += [vt[h * dv:(h + 1) * dv, :].astype(BF16), ones]
    return jnp.concatenate(parts, axis=0)


def _mla_attn_kernel(q_ref, k_ref, vt_ref, gate_ref, o_ref, s0_ref, s1_ref, *, tile, ch, hp):
    i = pl.program_id(2)
    dk = 2 * LANES
    qs = [q_ref[:, n * dk:(n + 1) * dk] for n in range(hp)]
    slots = (s0_ref, s1_ref)

    def produce(c, slot, diag_offset=None):
        col_max = []
        for n in range(hp):
            kc = k_ref[pl.ds(pl.multiple_of(c * ch, ch), ch), n * dk:(n + 1) * dk]
            s = lax.dot_general(kc, qs[n], _NT, preferred_element_type=F32)
            if diag_offset is not None:
                kidx = lax.broadcasted_iota(I32, (ch, tile), 0) + diag_offset
                qidx = lax.broadcasted_iota(I32, (ch, tile), 1)
                s = jnp.where(kidx <= qidx, s, NEG_BIG)
            slots[slot][n] = s
            col_max.append(jnp.max(s, axis=0, keepdims=True))
        return tuple(col_max)

    def consume(c, slot, col_max, carry):
        return tuple(_softmax_step(slots[slot][n], col_max[n], vt_ref[c, n * VT_ROWS:(n + 1) * VT_ROWS, :],
                                   *carry[n]) for n in range(hp))

    init = tuple((jnp.full((1, tile), NEG_BIG, F32), jnp.zeros((VT_ROWS, tile), F32)) for _ in range(hp))
    diag = 2 * i
    col_a = produce(diag, 0, 0)
    col_b = produce(diag + 1, 1, ch)
    carry = consume(diag, 0, col_a, init)

    def pair(j, state):
        cr, col1, c1 = state
        col0 = produce(2 * j, 0)
        cr = consume(c1, 1, col1, cr)
        col1 = produce(2 * j + 1, 1)
        cr = consume(2 * j, 0, col0, cr)
        return cr, col1, 2 * j + 1

    state = lax.fori_loop(0, i // 2, lambda jj, st: pair(2 * jj + 1, pair(2 * jj, st)),
                          (carry, col_b, diag + 1))
    carry, col1, c1 = lax.fori_loop(0, i & 1, lambda _, st: pair(i - 1, st), state)
    res = consume(c1, 1, col1, carry)
    for n in range(hp):
        o = _normalised_t(res[n][1], MLA_V)
        cols = slice(n * LANES, (n + 1) * LANES)
        o_ref[:, cols] = (o * _silu(gate_ref[:, cols])).astype(o_ref.dtype)


def _mla_attn(q, k, vt, h2d, batch, seq, tile):
    m = batch * seq
    nt = seq // tile
    ch = tile // 2
    hp = 2
    dk = 2 * LANES
    return pl.pallas_call(
        functools.partial(_mla_attn_kernel, tile=tile, ch=ch, hp=hp),
        grid=(batch, MLA_HEADS // hp, nt),
        in_specs=[
            pl.BlockSpec((tile, hp * dk), lambda b, h, i: (b * nt + i, h)),
            pl.BlockSpec((seq, hp * dk), lambda b, h, i: (b, h)),
            pl.BlockSpec((None, 2 * nt, hp * VT_ROWS, ch), lambda b, h, i: (b, 0, h, 0)),
            pl.BlockSpec((tile, hp * LANES), lambda b, h, i: (b * nt + i, COL_AG // (hp * LANES) + h)),
        ],
        out_specs=pl.BlockSpec((tile, hp * LANES), lambda b, h, i: (b * nt + i, h)),
        out_shape=jax.ShapeDtypeStruct((m, GROUP_WIDTH), BF16),
        scratch_shapes=[pltpu.VMEM((hp, ch, tile), F32), pltpu.VMEM((hp, ch, tile), F32)],
        compiler_params=_params(40 << 20, 3),
        name="mla_attn",
    )(q, k, vt, h2d)


def _hi_lo(v):
    hi = v.astype(BF16).astype(F32)
    return hi, v - hi


def _prep_dsa_kernel(q_ref, k_ref, v_ref, qi_ref, ki_ref, tab_ref,
                     q_out, k_out, vt_out, qi4_out, ki4_out, wt_out):
    def tab(t):
        return tab_ref[:, t * LANES:(t + 1) * LANES]

    scale = DSA_HEAD_DIM ** -0.5 * LOG2E
    q = q_ref[...]
    q_out[...] = (_rope(q, _tile_lanes(tab(TAB_DSA_C), DSA_HEADS), _tile_lanes(tab(TAB_DSA_S), DSA_HEADS),
                        DSA_ROT // 2, LANES) * scale).astype(BF16)
    k = k_ref[...]
    k_out[...] = _rope(k, _tile_lanes(tab(TAB_DSA_C), DSA_KV_HEADS), _tile_lanes(tab(TAB_DSA_S), DSA_KV_HEADS),
                       DSA_ROT // 2, LANES).astype(BF16)
    vt_out[...] = _vt_with_ones(v_ref[...].T, DSA_KV_HEADS, DSA_HEAD_DIM)
    reps = IDX_HEADS * IDX_DIM // LANES
    qi = _rope(qi_ref[...], _tile_lanes(tab(TAB_IDX_C), reps), _tile_lanes(tab(TAB_IDX_S), reps),
               IDX_ROT // 2, IDX_DIM)
    qb = DSA_QBLK
    first_half = lax.broadcasted_iota(I32, (qb, LANES), 1) < IDX_DIM
    for blk in range(q.shape[0] // qb):
        for pair in range(IDX_HEADS // 2):
            hi, lo = _hi_lo(qi[blk * qb:(blk + 1) * qb, pair * LANES:(pair + 1) * LANES])
            even = jnp.where(first_half, hi, pltpu.roll(lo, IDX_DIM, 1)).astype(BF16)
            odd = jnp.where(first_half, pltpu.roll(hi, IDX_DIM, 1), lo).astype(BF16)
            for half in range(2):
                qi4_out[blk, pair, 0:qb, half * LANES:(half + 1) * LANES] = even
                qi4_out[blk, pair, qb:2 * qb, half * LANES:(half + 1) * LANES] = odd
    small = ki_ref[...]
    lane = lax.broadcasted_iota(I32, small.shape, 1)
    c_ki = jnp.where(lane < IDX_DIM, tab(TAB_IDX_C), 1.0)
    s_ki = jnp.where(lane < IDX_DIM, tab(TAB_IDX_S), 0.0)
    hi, lo = _hi_lo(_rope(small, c_ki, s_ki, IDX_ROT // 2, IDX_DIM))
    ki4_out[:, 0:LANES] = jnp.where(lane < IDX_DIM, hi, pltpu.roll(hi, IDX_DIM, 1)).astype(BF16)
    ki4_out[:, LANES:2 * LANES] = jnp.where(lane < IDX_DIM, lo, pltpu.roll(lo, IDX_DIM, 1)).astype(BF16)
    wt = small.T[IDX_DIM:IDX_DIM + IDX_HEADS, :]
    wt_out[...] = wt * (IDX_HEADS ** -0.5 * IDX_DIM ** -0.5)


def _prep_dsa(h2d, hs, tabs, batch, seq):
    tm = KEY_CHUNK
    m = batch * seq
    nt = seq // tm
    qpt = tm // DSA_QBLK
    kvw = DSA_KV_HEADS * DSA_HEAD_DIM

    def row(b, i):
        return b * nt + i

    return pl.pallas_call(
        _prep_dsa_kernel,
        grid=(batch, nt),
        in_specs=[
            pl.BlockSpec((tm, GROUP_WIDTH), lambda b, i: (row(b, i), COL_BQ // GROUP_WIDTH)),
            pl.BlockSpec((tm, kvw), lambda b, i: (row(b, i), COL_BK // kvw)),
            pl.BlockSpec((tm, kvw), lambda b, i: (row(b, i), COL_BV // kvw)),
            pl.BlockSpec((tm, GROUP_WIDTH), lambda b, i: (row(b, i), COL_QI // GROUP_WIDTH)),
            pl.BlockSpec((tm, LANES), lambda b, i: (row(b, i), SMALL_KI // LANES)),
            pl.BlockSpec((tm, N_TABS * LANES), lambda b, i: (row(b, i), 0)),
        ],
        out_specs=[
            pl.BlockSpec((tm, GROUP_WIDTH), lambda b, i: (row(b, i), 0)),
            pl.BlockSpec((tm, kvw), lambda b, i: (row(b, i), 0)),
            pl.BlockSpec((None, None, DSA_KV_HEADS * VT_ROWS, tm), lambda b, i: (b, i, 0, 0)),
            pl.BlockSpec((None, qpt, IDX_HEADS // 2, 2 * DSA_QBLK, 4 * IDX_DIM), lambda b, i: (b, i, 0, 0, 0)),
            pl.BlockSpec((tm, 4 * IDX_DIM), lambda b, i: (row(b, i), 0)),
            pl.BlockSpec((None, IDX_HEADS, tm), lambda b, i: (b, 0, i)),
        ],
        out_shape=[
            jax.ShapeDtypeStruct((m, GROUP_WIDTH), BF16),
            jax.ShapeDtypeStruct((m, kvw), BF16),
            jax.ShapeDtypeStruct((batch, nt, DSA_KV_HEADS * VT_ROWS, tm), BF16),
            jax.ShapeDtypeStruct((batch, seq // DSA_QBLK, IDX_HEADS // 2, 2 * DSA_QBLK, 4 * IDX_DIM), BF16),
            jax.ShapeDtypeStruct((m, 4 * IDX_DIM), BF16),
            jax.ShapeDtypeStruct((batch, IDX_HEADS, seq), F32),
        ],
        compiler_params=_params(24 << 20, 2),
        name="prep_dsa",
    )(h2d, h2d, h2d, h2d, hs, tabs)


def _sortable_key(score):
    bits = lax.bitcast_convert_type(score, I32)
    key = jnp.where(bits < 0, bits ^ jnp.int32(0x7FFFFFFF), bits)
    return jnp.where(score == 0.0, 0, key)


def _dsa_kernel(qi4_ref, ki4_ref, wt_ref, q_ref, k_ref, vt_ref, gate_ref, o_ref, key_ref, s0_ref, s1_ref,
                *, topk, seq_bits, grp):
    i = pl.program_id(1)
    n_vis = i // 2
    n_chunks = n_vis + 1
    wt = wt_ref[...]
    ch = KEY_CHUNK
    qb = DSA_QBLK

    def rows(c):
        return pl.ds(pl.multiple_of(c * ch, ch), ch)

    def score_chunk(c):
        kc = ki4_ref[rows(c), :]
        tot = None
        for p in range(IDX_HEADS // 2):
            s = lax.dot_general(kc, qi4_ref[p], _NT, preferred_element_type=F32)
            r = jnp.maximum(s, 0.0)
            t = r[:, :qb] * wt[2 * p:2 * p + 1, :] + r[:, qb:] * wt[2 * p + 1:2 * p + 2, :]
            tot = t if tot is None else tot + t
        return tot

    def phase1(c, carry):
        key_ref[rows(c), :] = _sortable_key(score_chunk(c))
        return carry

    lax.fori_loop(0, n_vis // 2, lambda p, cr: phase1(2 * p + 1, phase1(2 * p, cr)), 0)
    lax.fori_loop(0, n_vis & 1, lambda _, cr: phase1(n_vis - 1, cr), 0)
    kidx = n_vis * ch + lax.broadcasted_iota(I32, (ch, qb), 0)
    qidx = i * qb + lax.broadcasted_iota(I32, (ch, qb), 1)
    key_ref[rows(n_vis), :] = jnp.where(kidx <= qidx, _sortable_key(score_chunk(n_vis)), INT_MIN)

    n_groups = (n_chunks + grp - 1) // grp

    def pad_chunk(c, carry):
        key_ref[rows(c), :] = jnp.full((ch, qb), INT_MIN, I32)
        return carry

    lax.fori_loop(n_chunks, n_groups * grp, pad_chunk, 0)

    def count_ge(t):
        def body(p, acc):
            blk = key_ref[pl.ds(pl.multiple_of(p * grp * ch, grp * ch), grp * ch), :]
            hit = (blk >= t).astype(I32)
            return acc + jnp.sum(hit.reshape(grp * ch // 8, 8, qb), axis=0)
        acc = lax.fori_loop(0, n_groups, body, jnp.zeros((8, qb), I32))
        return jnp.sum(acc, axis=0, keepdims=True)

    thr = jnp.where(count_ge(jnp.zeros((1, qb), I32)) >= topk, 0, INT_MIN).astype(I32)

    def bit_step(b, t):
        cand = t + jnp.left_shift(jnp.int32(1), 30 - b)
        return jnp.where(count_ge(cand) >= topk, cand, t)

    thr = lax.fori_loop(0, 31, bit_step, thr)
    found = thr > INT_MIN
    c_ge = count_ge(jnp.maximum(thr, INT_MIN + 1))
    need = topk - count_ge(thr + 1)
    tied = jnp.logical_and(c_ge > topk, found)

    def count(pred):
        def body(c, acc):
            idx = c * ch + lax.broadcasted_iota(I32, (ch, qb), 0)
            hit = pred(key_ref[rows(c), :], idx).astype(I32)
            return acc + jnp.sum(hit.reshape(ch // 8, 8, qb), axis=0)
        acc = lax.fori_loop(0, n_chunks, body, jnp.zeros((8, qb), I32))
        return jnp.sum(acc, axis=0, keepdims=True)

    def tie_search():
        def step(b, lo):
            cand = lo + jnp.left_shift(jnp.int32(1), seq_bits - 1 - b)
            below = count(lambda keys, idx: jnp.logical_and(keys == thr, idx < cand))
            return jnp.where(below < need, cand, lo)
        return lax.fori_loop(0, seq_bits, step, jnp.zeros((1, qb), I32))

    last_eq = lax.cond(jnp.max(tied.astype(I32)) > 0, tie_search,
                       lambda: jnp.full((1, qb), 2 ** seq_bits, I32))
    thr_sel = jnp.maximum(thr, INT_MIN + 1)

    nq = DSA_GROUP * qb
    one_hot = (lax.broadcasted_iota(I32, (nq, qb), 0) % qb == lax.broadcasted_iota(I32, (nq, qb), 1))
    one_hot = jnp.where(one_hot, 1.0, 0.0).astype(BF16)
    qs = [jnp.concatenate([jnp.concatenate([q_ref[:, (n * DSA_GROUP + g) * LANES:(n * DSA_GROUP + g + 1) * LANES]
                                            for g in range(DSA_GROUP)], axis=0), one_hot], axis=1)
          for n in range(DSA_KV_HEADS)]

    slots = (s0_ref, s1_ref)

    def produce(c, slot):
        keys = key_ref[rows(c), :]
        idx = c * ch + lax.broadcasted_iota(I32, (ch, qb), 0)
        drop = jnp.logical_and(keys == thr, idx > last_eq)
        sel = jnp.logical_and(keys >= thr_sel, jnp.logical_not(drop))
        neg = jnp.where(sel, 0.0, NEG_BIG).astype(BF16)
        col_max = []
        for n in range(DSA_KV_HEADS):
            kc = jnp.concatenate([k_ref[rows(c), n * LANES:(n + 1) * LANES], neg], axis=1)
            s = lax.dot_general(kc, qs[n], _NT, preferred_element_type=F32)
            slots[slot][n] = s
            col_max.append(jnp.max(s, axis=0, keepdims=True))
        return tuple(col_max)

    def consume(c, slot, col_max, carry):
        return tuple(_softmax_step(slots[slot][n], col_max[n],
                                   vt_ref[c, n * VT_ROWS:(n + 1) * VT_ROWS, :], *carry[n])
                     for n in range(DSA_KV_HEADS))

    def phase3(j, state):
        carry, col0 = state
        col1 = produce(2 * j + 1, 1)
        carry = consume(2 * j, 0, col0, carry)
        col0 = produce(2 * j + 2, 0)
        carry = consume(2 * j + 1, 1, col1, carry)
        return carry, col0

    init = tuple((jnp.full((1, nq), NEG_BIG, F32), jnp.zeros((VT_ROWS, nq), F32))
                 for _ in range(DSA_KV_HEADS))
    pairs = (n_chunks - 1) // 2
    state = lax.fori_loop(0, pairs // 2, lambda jj, st: phase3(2 * jj + 1, phase3(2 * jj, st)),
                          (init, produce(0, 0)))
    carry, col0 = lax.fori_loop(0, pairs & 1, lambda _, st: phase3(pairs - 1, st), state)
    last = 2 * pairs

    def tail_two(cr):
        col1 = produce(last + 1, 1)
        cr = consume(last, 0, col0, cr)
        return consume(last + 1, 1, col1, cr)

    res = lax.cond(n_chunks - 1 > last, tail_two, lambda cr: consume(last, 0, col0, cr), carry)
    for n in range(DSA_KV_HEADS):
        o = _normalised_t(res[n][1], DSA_HEAD_DIM)
        for g in range(DSA_GROUP):
            col = (n * DSA_GROUP + g) * LANES
            o_ref[:, col:col + LANES] = (o[g * qb:(g + 1) * qb, :]
                                         * _silu(gate_ref[:, col:col + LANES])).astype(o_ref.dtype)


def _dsa(qi4, ki4, wt, q, k, vt, h2d, batch, seq):
    m = batch * seq
    nq = seq // DSA_QBLK
    nck = seq // KEY_CHUNK
    kvw = DSA_KV_HEADS * DSA_HEAD_DIM
    topk = min(TOPK_MAX, seq // 4)
    seq_bits = int(np.log2(seq))
    assert 2 ** seq_bits == seq and nck % 2 == 0 and topk <= KEY_CHUNK
    return pl.pallas_call(
        functools.partial(_dsa_kernel, topk=topk, seq_bits=seq_bits, grp=2),
        grid=(batch, nq),
        in_specs=[
            pl.BlockSpec((None, None, IDX_HEADS // 2, 2 * DSA_QBLK, 4 * IDX_DIM),
                         lambda b, i: (b, i, 0, 0, 0)),
            pl.BlockSpec((seq, 4 * IDX_DIM), lambda b, i: (b, 0)),
            pl.BlockSpec((None, IDX_HEADS, DSA_QBLK), lambda b, i: (b, 0, i)),
            pl.BlockSpec((DSA_QBLK, GROUP_WIDTH), lambda b, i: (b * nq + i, 0)),
            pl.BlockSpec((seq, kvw), lambda b, i: (b, 0)),
            pl.BlockSpec((None, nck, DSA_KV_HEADS * VT_ROWS, KEY_CHUNK), lambda b, i: (b, 0, 0, 0)),
            pl.BlockSpec((DSA_QBLK, GROUP_WIDTH), lambda b, i: (b * nq + i, COL_BG // GROUP_WIDTH)),
        ],
        out_specs=pl.BlockSpec((DSA_QBLK, GROUP_WIDTH), lambda b, i: (b * nq + i, 0)),
        out_shape=jax.ShapeDtypeStruct((m, GROUP_WIDTH), BF16),
        scratch_shapes=[pltpu.VMEM((seq, DSA_QBLK), I32),
                        pltpu.VMEM((DSA_KV_HEADS, KEY_CHUNK, DSA_GROUP * DSA_QBLK), F32),
                        pltpu.VMEM((DSA_KV_HEADS, KEY_CHUNK, DSA_GROUP * DSA_QBLK), F32)],
        compiler_params=_params(32 << 20, 2),
        name="dsa",
    )(qi4, ki4, wt, q, k, vt, h2d)


def _rglru_kernel(u_ref, gate_ref, cw_ref, cb_ref, wa_ref, ba_ref, wx_ref, bx_ref, lam_ref,
                  o_ref, halo_ref, h_ref, a_scr, x_scr, y_scr, *, batch, tt):
    t = pl.program_id(0)

    @pl.when(t == 0)
    def _():
        halo_ref[...] = jnp.zeros_like(halo_ref)
        h_ref[...] = jnp.zeros_like(h_ref)

    lam = lam_ref[...]
    z = -lam
    softplus = jnp.maximum(z, 0.0) + jnp.log1p(jnp.exp(-jnp.abs(z)))
    row = lax.broadcasted_iota(I32, (tt, LRU_WIDTH), 0)
    for b in range(batch):
        u = u_ref[b]
        prev = halo_ref[b]
        p1, p2, p3 = prev[7:8, :], prev[6:7, :], prev[5:6, :]
        s1 = jnp.where(row == 0, p1, pltpu.roll(u, 1, 0))
        s2 = jnp.where(row == 0, p2, jnp.where(row == 1, p1, pltpu.roll(u, 2, 0)))
        s3 = jnp.where(row == 0, p3, jnp.where(row == 1, p2, jnp.where(row == 2, p1, pltpu.roll(u, 3, 0))))
        conv = (cw_ref[3:4, :] * u + cw_ref[2:3, :] * s1 + cw_ref[1:2, :] * s2 + cw_ref[0:1, :] * s3
                + cb_ref[...])
        halo_ref[b] = u[tt - 8:, :]
        cb16 = conv.astype(BF16)
        r = jax.nn.sigmoid(jnp.dot(cb16, wa_ref[...], preferred_element_type=F32) + ba_ref[...])
        ig = jax.nn.sigmoid(jnp.dot(cb16, wx_ref[...], preferred_element_type=F32) + bx_ref[...])
        log_a = -LRU_C * r * softplus
        a_scr[b] = jnp.exp(log_a)
        th = jnp.tanh(log_a)
        x_scr[b] = jnp.sqrt(-2.0 * th / (1.0 - th)) * (ig * conv)

    def step(s, h):
        h = a_scr[:, pl.ds(s, 1), :] * h + x_scr[:, pl.ds(s, 1), :]
        y_scr[:, pl.ds(s, 1), :] = h
        return h

    h_ref[...] = lax.fori_loop(0, tt, step, h_ref[...], unroll=8)
    o_ref[...] = (y_scr[...] * _silu(gate_ref[...])).astype(o_ref.dtype)


def _rglru(h3d, conv_w, conv_b, wa_bd, b_a, wx_bd, b_x, lam, batch, seq, tt):
    nt = seq // tt
    w = LRU_WIDTH

    def vec(r):
        return pl.BlockSpec((r, w), lambda t: (0, 0))

    return pl.pallas_call(
        functools.partial(_rglru_kernel, batch=batch, tt=tt),
        grid=(nt,),
        in_specs=[
            pl.BlockSpec((batch, tt, w), lambda t: (0, t, COL_U // w)),
            pl.BlockSpec((batch, tt, w), lambda t: (0, t, COL_CG // w)),
            vec(CONV_WIDTH), vec(1),
            pl.BlockSpec((w, w), lambda t: (0, 0)), vec(1),
            pl.BlockSpec((w, w), lambda t: (0, 0)), vec(1),
            vec(1),
        ],
        out_specs=pl.BlockSpec((batch, tt, w), lambda t: (0, t, 0)),
        out_shape=jax.ShapeDtypeStruct((batch, seq, w), BF16),
        scratch_shapes=[pltpu.VMEM((batch, 8, w), F32), pltpu.VMEM((batch, 1, w), F32),
                        pltpu.VMEM((batch, tt, w), F32), pltpu.VMEM((batch, tt, w), F32),
                        pltpu.VMEM((batch, tt, w), F32)],
        compiler_params=_params(32 << 20, 1),
        name="rglru",
    )(h3d, h3d, conv_w, conv_b, wa_bd, b_a, wx_bd, b_x, lam)


def _mem_attn_kernel(q_ref, gate_ref, km_ref, vm_ref, o_ref):
    scale = MEM_HEAD_DIM ** -0.5
    for h in range(MEM_HEADS):
        cols = slice(h * MEM_HEAD_DIM, (h + 1) * MEM_HEAD_DIM)
        qh = (q_ref[:, cols] * scale).astype(BF16)
        s = lax.dot_general(qh, km_ref[:, cols], _NT, preferred_element_type=F32)
        p = jnp.exp(s - jnp.max(s, axis=-1, keepdims=True))
        l = jnp.sum(p, axis=-1, keepdims=True)
        o = jnp.dot(p.astype(BF16), vm_ref[:, cols], preferred_element_type=F32) / l
        o_ref[:, cols] = (o * _silu(gate_ref[:, cols])).astype(o_ref.dtype)


def _mem_attn(h2d, km, vm, batch, seq, tm):
    m = batch * seq
    nt = seq // tm
    w = GROUP_WIDTH
    return pl.pallas_call(
        _mem_attn_kernel,
        grid=(batch, nt),
        in_specs=[
            pl.BlockSpec((tm, w), lambda b, i: (b * nt + i, COL_DQ // w)),
            pl.BlockSpec((tm, w), lambda b, i: (b * nt + i, COL_DG // w)),
            pl.BlockSpec((N_MEM, w), lambda b, i: (b, 0)),
            pl.BlockSpec((N_MEM, w), lambda b, i: (b, 0)),
        ],
        out_specs=pl.BlockSpec((tm, w), lambda b, i: (b * nt + i, 0)),
        out_shape=jax.ShapeDtypeStruct((m, w), BF16),
        compiler_params=_params(24 << 20, 2),
        name="mem_attn",
    )(h2d, h2d, km, vm)


def _out_ln_kernel(*refs, n_lhs, kg, nj, per_emit, d_model):
    lhs = refs[:n_lhs]
    w_ref, x_ref, g_ref, b_ref, o_ref, o16_ref, z_ref, mu_ref, rs_ref = refs[n_lhs:]
    j = pl.program_id(1)

    @pl.when(j < nj)
    def _():
        acc = None
        for g in range(n_lhs):
            part = jnp.dot(lhs[g][...], w_ref[g * kg:(g + 1) * kg, :].astype(BF16), preferred_element_type=F32)
            acc = part if acc is None else acc + part
        z_ref[j] = DEEPNORM_ALPHA * x_ref[...] + acc

    @pl.when(j == nj)
    def _():
        def lane_tiles(v):
            tiles = [v[:, t * LANES:(t + 1) * LANES] for t in range(v.shape[1] // LANES)]
            return functools.reduce(lambda a, b: a + b, tiles)

        tot = functools.reduce(lambda a, b: a + b, [lane_tiles(z_ref[jj]) for jj in range(nj)])
        mu = jnp.sum(tot, axis=-1, keepdims=True) * (1.0 / d_model)
        sq = None
        for jj in range(nj):
            dlt = z_ref[jj] - mu
            part = lane_tiles(dlt * dlt)
            sq = part if sq is None else sq + part
        mu_ref[...] = mu
        rs_ref[...] = lax.rsqrt(jnp.sum(sq, axis=-1, keepdims=True) * (1.0 / d_model) + LN_EPS)

    @pl.when(j >= nj)
    def _():
        e = j - nj
        tn = z_ref.shape[2]
        for t in range(per_emit):
            cols = slice(t * tn, (t + 1) * tn)
            out = (z_ref[e * per_emit + t] - mu_ref[...]) * rs_ref[...] * g_ref[:, cols] + b_ref[:, cols]
            o_ref[:, cols] = out
            o16_ref[:, cols] = out.astype(BF16)


def _out_proj_deepnorm(lhs_list, w, layer, x, ln_g, ln_b, tm, tn, te):
    m, kg = lhs_list[0].shape
    _, k, n = w.shape
    n_lhs = len(lhs_list)
    nj, ne, per_emit = n // tn, n // te, te // tn
    assert k == kg * n_lhs and m % tm == 0 and n % te == 0 and te % tn == 0

    def col(j):
        return jnp.minimum(j, nj - 1)

    def emit(j):
        return jnp.maximum(j - nj, 0)

    est = (nj * tm * tn * 4 + 2 * (tm * k * 2 + k * tn * 4 + tm * tn * 4 + tm * te * 6) + tm * tn * 8)
    return pl.pallas_call(
        functools.partial(_out_ln_kernel, n_lhs=n_lhs, kg=kg, nj=nj, per_emit=per_emit, d_model=n),
        grid=(m // tm, nj + ne),
        in_specs=[pl.BlockSpec((tm, kg), lambda i, j: (i, 0)) for _ in range(n_lhs)]
        + [pl.BlockSpec((None, k, tn), lambda i, j: (layer, 0, col(j))),
           pl.BlockSpec((tm, tn), lambda i, j: (i, col(j))),
           pl.BlockSpec((1, te), lambda i, j: (0, emit(j))),
           pl.BlockSpec((1, te), lambda i, j: (0, emit(j)))],
        out_specs=[pl.BlockSpec((tm, te), lambda i, j: (i, emit(j))),
                   pl.BlockSpec((tm, te), lambda i, j: (i, emit(j)))],
        out_shape=[jax.ShapeDtypeStruct((m, n), F32), jax.ShapeDtypeStruct((m, n), BF16)],
        scratch_shapes=[pltpu.VMEM((nj, tm, tn), F32), pltpu.VMEM((tm, 1), F32), pltpu.VMEM((tm, 1), F32)],
        compiler_params=_params(est + (10 << 20), 2),
        name="out_proj_deepnorm",
    )(*lhs_list, w, x, ln_g, ln_b)


def _rope_tabs(positions):
    pos = positions.astype(F32)
    b, s = pos.shape

    def one(rot, period):
        inv = ROPE_THETA ** (-jnp.arange(0, rot, 2, dtype=F32) / rot)
        ang = pos[:, :, None] * inv
        c, sn = jnp.cos(ang), jnp.sin(ang)
        rest = period - rot
        cp = jnp.concatenate([c, c, jnp.ones((b, s, rest), F32)], axis=-1)
        sp = jnp.concatenate([-sn, sn, jnp.zeros((b, s, rest), F32)], axis=-1)
        reps = LANES // period
        return jnp.tile(cp, (1, 1, reps)), jnp.tile(sp, (1, 1, reps))

    parts = one(MLA_ROPE, LANES) + one(DSA_ROT, LANES) + one(IDX_ROT, IDX_DIM)
    return jnp.concatenate(parts, axis=-1).reshape(b * s, N_TABS * LANES)


def _block_diag(w):
    l = w.shape[0]
    eye = jnp.eye(LRU_BLOCKS, dtype=w.dtype)
    bd = jnp.einsum('lnde,nm->lndme', w, eye)
    return bd.reshape(l, LRU_WIDTH, LRU_WIDTH).astype(BF16)


def _layer(x, x16, mem16, tabs, lw, l, batch, seq):
    m = batch * seq
    main_rows, small_rows = _in_proj_row_tables(256)
    h2d = _in_proj(x16, lw["w_in_t"], l, main_rows, tm=min(2048, m), tn=256, name="in_proj")
    hs = _in_proj(x16, lw["w_in_t"], l, small_rows, tm=min(2048, m), tn=LANES, name="in_proj_small")
    tile_a = min(512, seq)
    qa, ka, vta = _prep_mla(h2d, hs, tabs, lw["g_cq"], lw["g_ckv"], lw["w_uq"], lw["w_uk"], lw["w_uvt"],
                            batch, seq, tile_a)
    ya = _mla_attn(qa, ka, vta, h2d, batch, seq, tile_a)
    qb, kb, vtb, qi4, ki4, wt = _prep_dsa(h2d, hs, tabs, batch, seq)
    yb = _dsa(qi4, ki4, wt, qb, kb, vtb, h2d, batch, seq)
    yc = _rglru(h2d.reshape(batch, seq, N_MAIN), lw["conv_w"], lw["conv_b"], lw["w_rg_a"], lw["b_rg_a"],
                lw["w_rg_x"], lw["b_rg_x"], lw["lam"], batch, seq, min(256, seq))
    yc = yc.reshape(m, GROUP_WIDTH)
    km = _matmul([mem16], lw["w_mem_k"], l, BF16, tm=mem16.shape[0], tn=256, name="mem_proj_k")
    vm = _matmul([mem16], lw["w_mem_v"], l, BF16, tm=mem16.shape[0], tn=256, name="mem_proj_v")
    yd = _mem_attn(h2d, km, vm, batch, seq, min(512, seq))
    return _out_proj_deepnorm([ya, yb, yc, yd], lw["w_o"], l, x, lw["ln_g"], lw["ln_b"],
                              tm=min(1024, m), tn=256, te=512)


def kernel(x, mem, positions, w_in, g_cq, g_ckv, w_uq, w_ukv, conv_w, conv_b, w_rg_a, b_rg_a, w_rg_x,
           b_rg_x, lru_lambda, w_mem_k, w_mem_v, w_o, ln_g, ln_b):
    batch, seq, d = x.shape
    depth = w_in.shape[0]
    tabs = _rope_tabs(positions)
    w_in_t = w_in.transpose(0, 2, 1)
    w_uq_p = jnp.pad(w_uq.reshape(depth, MLA_Q_LORA, MLA_HEADS, MLA_NOPE + MLA_ROPE),
                     [(0, 0), (0, 0), (0, 0), (0, 2 * LANES - MLA_NOPE - MLA_ROPE)])
    w_uq_p = w_uq_p.reshape(depth, MLA_Q_LORA, MLA_HEADS * 2 * LANES).astype(BF16)
    w_ukv4 = w_ukv.reshape(depth, MLA_KV_LORA, MLA_HEADS, MLA_NOPE + MLA_V)
    w_uk = w_ukv4[..., :MLA_NOPE].reshape(depth, MLA_KV_LORA, MLA_HEADS * MLA_NOPE).astype(BF16)
    w_uvt = w_ukv4[..., MLA_NOPE:].reshape(depth, MLA_KV_LORA, MLA_HEADS * MLA_V)
    w_uvt = w_uvt.transpose(0, 2, 1).astype(BF16)
    wa_bd = _block_diag(w_rg_a)
    wx_bd = _block_diag(w_rg_x)
    mem16 = mem.reshape(batch * mem.shape[1], d).astype(BF16)

    xf = x.reshape(batch * seq, d)
    x16 = xf.astype(BF16)
    for l in range(depth):
        lw = dict(w_in_t=w_in_t, g_cq=g_cq[l][None], g_ckv=g_ckv[l][None], w_uq=w_uq_p[l], w_uk=w_uk[l],
                  w_uvt=w_uvt[l], conv_w=conv_w[l], conv_b=conv_b[l][None], w_rg_a=wa_bd[l],
                  b_rg_a=b_rg_a[l][None], w_rg_x=wx_bd[l], b_rg_x=b_rg_x[l][None], lam=lru_lambda[l][None],
                  w_mem_k=w_mem_k, w_mem_v=w_mem_v, w_o=w_o, ln_g=ln_g[l][None], ln_b=ln_b[l][None])
        xf, x16 = _layer(xf, x16, mem16, tabs, lw, l, batch, seq)
    return xf.reshape(batch, seq, d)
```

```python
import functools

import numpy as np
import jax
import jax.numpy as jnp
from jax import lax
from jax.experimental import pallas as pl
from jax.experimental.pallas import tpu as pltpu

F32 = jnp.float32
BF16 = jnp.bfloat16
I32 = jnp.int32

DEPTH = 4
D_MODEL = 4096
N_MEM = 256
GROUP_WIDTH = D_MODEL // 4
ROPE_THETA = 500000.0
MLA_HEADS = 8
MLA_NOPE = 128
MLA_ROPE = 64
MLA_V = 128
MLA_Q_LORA = GROUP_WIDTH
MLA_KV_LORA = GROUP_WIDTH // 2
DSA_HEADS = 8
DSA_KV_HEADS = 2
DSA_GROUP = DSA_HEADS // DSA_KV_HEADS
DSA_HEAD_DIM = 128
DSA_ROT = DSA_HEAD_DIM // 4
IDX_HEADS = 16
IDX_DIM = 64
IDX_ROT = IDX_DIM // 4
TOPK_MAX = 256
LRU_WIDTH = GROUP_WIDTH
LRU_BLOCKS = 16
LRU_BLOCK_DIM = LRU_WIDTH // LRU_BLOCKS
CONV_WIDTH = 4
LRU_C = 8.0
MEM_HEADS = 4
MEM_HEAD_DIM = GROUP_WIDTH // MEM_HEADS
DEEPNORM_ALPHA = (2 * DEPTH) ** 0.25
LN_EPS = 1e-5
RMS_EPS = 1e-6

IN_SIZES = (
    MLA_Q_LORA, MLA_KV_LORA, MLA_ROPE, GROUP_WIDTH,
    GROUP_WIDTH, DSA_KV_HEADS * DSA_HEAD_DIM, DSA_KV_HEADS * DSA_HEAD_DIM,
    IDX_HEADS * IDX_DIM, IDX_DIM, IDX_HEADS, GROUP_WIDTH,
    LRU_WIDTH, GROUP_WIDTH,
    GROUP_WIDTH, GROUP_WIDTH,
)

LANES = 128
V7X_VMEM_BUDGET = 60 * 1024 * 1024

COL_CQ = 0
COL_AG = 1024
COL_BQ = 2048
COL_QI = 3072
COL_BG = 4096
COL_U = 5120
COL_CG = 6144
COL_DQ = 7168
COL_DG = 8192
COL_CKV = 9216
COL_BK = 9728
COL_BV = 9984
N_MAIN = 10240
SMALL_KR = 0
SMALL_KI = 128

TAB_MLA_C, TAB_MLA_S, TAB_DSA_C, TAB_DSA_S, TAB_IDX_C, TAB_IDX_S = range(6)
N_TABS = 6

NEG_BIG = -1e30
INT_MIN = -2 ** 31
LOG2E = 1.4426950408889634
KEY_CHUNK = 256
DSA_QBLK = 128
SEARCH_STOPS = (12, 9, 6, 3, 0)
VT_ROWS = 128 + 16
_NT = (((1,), (1,)), ((), ()))


def _vmem_limit(nbytes):
    return int(min(V7X_VMEM_BUDGET, max(nbytes, 16 * 1024 * 1024)))


def _params(nbytes, ndims):
    return pltpu.CompilerParams(dimension_semantics=("arbitrary",) * ndims,
                                vmem_limit_bytes=_vmem_limit(nbytes))


def _mm_kernel(*refs, n_lhs, kg):
    w_ref = refs[n_lhs]
    o_ref = refs[n_lhs + 1]
    acc = None
    for g in range(n_lhs):
        part = jnp.dot(refs[g][...], w_ref[g * kg:(g + 1) * kg, :].astype(BF16), preferred_element_type=F32)
        acc = part if acc is None else acc + part
    o_ref[...] = acc.astype(o_ref.dtype)


def _matmul(lhs_list, w, layer, out_dtype, tm, tn, name):
    m, kg = lhs_list[0].shape
    _, k, n = w.shape
    n_lhs = len(lhs_list)
    assert k == kg * n_lhs and m % tm == 0 and n % tn == 0
    est = 2 * (tm * k * 2 + k * tn * w.dtype.itemsize + tm * tn * jnp.dtype(out_dtype).itemsize) + tm * tn * 8
    return pl.pallas_call(
        functools.partial(_mm_kernel, n_lhs=n_lhs, kg=kg),
        grid=(m // tm, n // tn),
        in_specs=[pl.BlockSpec((tm, kg), lambda i, j: (i, 0)) for _ in range(n_lhs)]
        + [pl.BlockSpec((None, k, tn), lambda i, j: (layer, 0, j))],
        out_specs=pl.BlockSpec((tm, tn), lambda i, j: (i, j)),
        out_shape=jax.ShapeDtypeStruct((m, n), out_dtype),
        compiler_params=_params(est + (8 << 20), 2),
        name=name,
    )(*lhs_list, w)


def _in_proj_kernel(tab_ref, x_ref, w_ref, o_ref):
    del tab_ref
    o_ref[...] = lax.dot_general(x_ref[...], w_ref[...].astype(BF16), _NT, preferred_element_type=F32)


def _in_proj(x16, w_t, layer, row_offsets, tm, tn, name):
    m, k = x16.shape
    nblk = len(row_offsets)
    est = 2 * (tm * k * 2 + tn * k * 4 + tm * tn * 4) + tm * tn * 8
    grid_spec = pltpu.PrefetchScalarGridSpec(
        num_scalar_prefetch=1,
        grid=(m // tm, nblk),
        in_specs=[
            pl.BlockSpec((tm, k), lambda i, j, tab: (i, 0)),
            pl.BlockSpec((pl.Squeezed(), pl.Element(tn), pl.Element(k)),
                         lambda i, j, tab: (layer, pl.multiple_of(tab[j], 16), 0)),
        ],
        out_specs=pl.BlockSpec((tm, tn), lambda i, j, tab: (i, j)),
    )
    return pl.pallas_call(
        _in_proj_kernel,
        grid_spec=grid_spec,
        out_shape=jax.ShapeDtypeStruct((m, nblk * tn), F32),
        compiler_params=_params(est + (8 << 20), 2),
        name=name,
    )(jnp.asarray(row_offsets, I32), x16, w_t)


def _in_proj_row_tables(tn):
    names = ("cq", "ckv", "kr", "ag", "bq", "bk", "bv", "qi", "ki", "wi", "bg", "u", "cg", "dq", "dg")
    src = dict(zip(names, [0] + [int(c) for c in np.cumsum(IN_SIZES)[:-1]]))
    size = dict(zip(names, IN_SIZES))
    order = [("cq", COL_CQ), ("ag", COL_AG), ("bq", COL_BQ), ("qi", COL_QI), ("bg", COL_BG), ("u", COL_U),
             ("cg", COL_CG), ("dq", COL_DQ), ("dg", COL_DG), ("ckv", COL_CKV), ("bk", COL_BK), ("bv", COL_BV)]
    main = []
    for n, dst in order:
        assert dst == len(main) * tn and size[n] % tn == 0 and src[n] % 16 == 0
        main += [src[n] + t * tn for t in range(size[n] // tn)]
    assert len(main) * tn == N_MAIN and src["wi"] == src["ki"] + size["ki"]
    return np.asarray(main, np.int32), np.asarray([src["kr"], src["ki"]], np.int32)


def _rope(v, c_tab, s_tab, half, period):
    width = v.shape[-1]
    lane = lax.broadcasted_iota(I32, v.shape, 1) & (period - 1)
    swapped = jnp.where(lane < half, pltpu.roll(v, width - half, 1), pltpu.roll(v, half, 1))
    return v * c_tab + swapped * s_tab


def _tile_lanes(t, reps):
    return t if reps == 1 else jnp.concatenate([t] * reps, axis=1)


def _rms(x, g):
    return x * lax.rsqrt(jnp.mean(x * x, axis=-1, keepdims=True) + RMS_EPS) * g


def _silu(g):
    return g * jax.nn.sigmoid(g)


def _prep_mla_kernel(cq_ref, ckv_ref, kr_ref, tab_ref, gcq_ref, gckv_ref, wuq_ref, wk_ref, wvt_ref,
                     q_out, k_out, vt_out):
    scale = (MLA_NOPE + MLA_ROPE) ** -0.5 * LOG2E
    nq = _rms(cq_ref[...], gcq_ref[...]).astype(BF16)
    nkv = _rms(ckv_ref[...], gckv_ref[...]).astype(BF16)
    c_tab = tab_ref[:, TAB_MLA_C * LANES:(TAB_MLA_C + 1) * LANES]
    s_tab = tab_ref[:, TAB_MLA_S * LANES:(TAB_MLA_S + 1) * LANES]
    qf = jnp.dot(nq, wuq_ref[...], preferred_element_type=F32) * scale
    kn = jnp.dot(nkv, wk_ref[...], preferred_element_type=F32)
    lane = lax.broadcasted_iota(I32, kr_ref.shape, 1)
    kr = jnp.where(lane < MLA_ROPE, kr_ref[...], 0.0)
    kr = _rope(kr, c_tab, s_tab, MLA_ROPE // 2, LANES).astype(BF16)
    for h in range(MLA_HEADS):
        base = h * 2 * LANES
        q_out[:, base:base + LANES] = qf[:, base:base + LANES].astype(BF16)
        q_out[:, base + LANES:base + 2 * LANES] = _rope(
            qf[:, base + LANES:base + 2 * LANES], c_tab, s_tab, MLA_ROPE // 2, LANES).astype(BF16)
        k_out[:, base:base + LANES] = kn[:, h * LANES:(h + 1) * LANES].astype(BF16)
        k_out[:, base + LANES:base + 2 * LANES] = kr
    vt = lax.dot_general(wvt_ref[...], nkv, _NT, preferred_element_type=F32)
    half = vt.shape[1] // 2
    vt_out[0] = _vt_with_ones(vt[:, :half], MLA_HEADS, MLA_V)
    vt_out[1] = _vt_with_ones(vt[:, half:], MLA_HEADS, MLA_V)


def _prep_mla(h2d, hs, tabs, g_cq, g_ckv, wuq_p, wk, wvt, batch, seq, tm):
    m = batch * seq
    nt = seq // tm
    hd = MLA_HEADS * 2 * LANES
    return pl.pallas_call(
        _prep_mla_kernel,
        grid=(batch, nt),
        in_specs=[
            pl.BlockSpec((tm, MLA_Q_LORA), lambda b, i: (b * nt + i, COL_CQ // MLA_Q_LORA)),
            pl.BlockSpec((tm, MLA_KV_LORA), lambda b, i: (b * nt + i, COL_CKV // MLA_KV_LORA)),
            pl.BlockSpec((tm, LANES), lambda b, i: (b * nt + i, SMALL_KR // LANES)),
            pl.BlockSpec((tm, N_TABS * LANES), lambda b, i: (b * nt + i, 0)),
            pl.BlockSpec((1, MLA_Q_LORA), lambda b, i: (0, 0)),
            pl.BlockSpec((1, MLA_KV_LORA), lambda b, i: (0, 0)),
            pl.BlockSpec((MLA_Q_LORA, hd), lambda b, i: (0, 0)),
            pl.BlockSpec((MLA_KV_LORA, MLA_HEADS * MLA_NOPE), lambda b, i: (0, 0)),
            pl.BlockSpec((MLA_HEADS * MLA_V, MLA_KV_LORA), lambda b, i: (0, 0)),
        ],
        out_specs=[
            pl.BlockSpec((tm, hd), lambda b, i: (b * nt + i, 0)),
            pl.BlockSpec((tm, hd), lambda b, i: (b * nt + i, 0)),
            pl.BlockSpec((None, 2, MLA_HEADS * VT_ROWS, tm // 2), lambda b, i: (b, i, 0, 0)),
        ],
        out_shape=[
            jax.ShapeDtypeStruct((m, hd), BF16),
            jax.ShapeDtypeStruct((m, hd), BF16),
            jax.ShapeDtypeStruct((batch, 2 * nt, MLA_HEADS * VT_ROWS, tm // 2), BF16),
        ],
        compiler_params=_params(40 << 20, 2),
        name="prep_mla",
    )(h2d, h2d, hs, tabs, g_cq, g_ckv, wuq_p, wk, wvt)


def _softmax_step(s, col_max, vt_chunk, m, acc):
    m_new = jnp.maximum(m, col_max)
    alpha = jnp.exp2(m - m_new)
    p = jnp.exp2(s - m_new)
    acc_new = alpha * acc + jnp.dot(vt_chunk, p.astype(BF16), preferred_element_type=F32)
    return m_new, acc_new


def _normalised_t(acc, dv):
    return (acc[:dv, :] / acc[dv:dv + 1, :]).T


def _vt_with_ones(vt, heads, dv):
    ones = jnp.ones((VT_ROWS - dv, vt.shape[1]), BF16)
    parts = []
    for h in range(heads):
        parts += [vt[h * dv:(h + 1) * dv, :].astype(BF16), ones]
    return jnp.concatenate(parts, axis=0)


def _mla_attn_kernel(q_ref, k_ref, vt_ref, gate_ref, o_ref, s0_ref, s1_ref, *, tile, ch, hp):
    i = pl.program_id(2)
    dk = 2 * LANES
    qs = [q_ref[:, n * dk:(n + 1) * dk] for n in range(hp)]
    slots = (s0_ref, s1_ref)

    def produce(c, slot, diag_offset=None):
        col_max = []
        for n in range(hp):
            kc = k_ref[pl.ds(pl.multiple_of(c * ch, ch), ch), n * dk:(n + 1) * dk]
            s = lax.dot_general(kc, qs[n], _NT, preferred_element_type=F32)
            if diag_offset is not None:
                kidx = lax.broadcasted_iota(I32, (ch, tile), 0) + diag_offset
                qidx = lax.broadcasted_iota(I32, (ch, tile), 1)
                s = jnp.where(kidx <= qidx, s, NEG_BIG)
            slots[slot][n] = s
            col_max.append(jnp.max(s, axis=0, keepdims=True))
        return tuple(col_max)

    def consume(c, slot, col_max, carry):
        return tuple(_softmax_step(slots[slot][n], col_max[n], vt_ref[c, n * VT_ROWS:(n + 1) * VT_ROWS, :],
                                   *carry[n]) for n in range(hp))

    init = tuple((jnp.full((1, tile), NEG_BIG, F32), jnp.zeros((VT_ROWS, tile), F32)) for _ in range(hp))
    diag = 2 * i
    col_a = produce(diag, 0, 0)
    col_b = produce(diag + 1, 1, ch)
    carry = consume(diag, 0, col_a, init)

    def pair(j, state):
        cr, col1, c1 = state
        col0 = produce(2 * j, 0)
        cr = consume(c1, 1, col1, cr)
        col1 = produce(2 * j + 1, 1)
        cr = consume(2 * j, 0, col0, cr)
        return cr, col1, 2 * j + 1

    state = lax.fori_loop(0, i // 2, lambda jj, st: pair(2 * jj + 1, pair(2 * jj, st)),
                          (carry, col_b, diag + 1))
    carry, col1, c1 = lax.fori_loop(0, i & 1, lambda _, st: pair(i - 1, st), state)
    res = consume(c1, 1, col1, carry)
    for n in range(hp):
        o = _normalised_t(res[n][1], MLA_V)
        cols = slice(n * LANES, (n + 1) * LANES)
        o_ref[:, cols] = (o * _silu(gate_ref[:, cols])).astype(o_ref.dtype)


def _mla_attn(q, k, vt, h2d, batch, seq, tile):
    m = batch * seq
    nt = seq // tile
    ch = tile // 2
    hp = 2
    dk = 2 * LANES
    return pl.pallas_call(
        functools.partial(_mla_attn_kernel, tile=tile, ch=ch, hp=hp),
        grid=(batch, MLA_HEADS // hp, nt),
        in_specs=[
            pl.BlockSpec((tile, hp * dk), lambda b, h, i: (b * nt + i, h)),
            pl.BlockSpec((seq, hp * dk), lambda b, h, i: (b, h)),
            pl.BlockSpec((None, 2 * nt, hp * VT_ROWS, ch), lambda b, h, i: (b, 0, h, 0)),
            pl.BlockSpec((tile, hp * LANES), lambda b, h, i: (b * nt + i, COL_AG // (hp * LANES) + h)),
        ],
        out_specs=pl.BlockSpec((tile, hp * LANES), lambda b, h, i: (b * nt + i, h)),
        out_shape=jax.ShapeDtypeStruct((m, GROUP_WIDTH), BF16),
        scratch_shapes=[pltpu.VMEM((hp, ch, tile), F32), pltpu.VMEM((hp, ch, tile), F32)],
        compiler_params=_params(40 << 20, 3),
        name="mla_attn",
    )(q, k, vt, h2d)


def _hi_lo(v):
    hi = v.astype(BF16).astype(F32)
    return hi, v - hi


def _prep_dsa_kernel(q_ref, k_ref, v_ref, qi_ref, ki_ref, tab_ref,
                     q_out, k_out, vt_out, qi4_out, ki4_out, wt_out):
    def tab(t):
        return tab_ref[:, t * LANES:(t + 1) * LANES]

    scale = DSA_HEAD_DIM ** -0.5 * LOG2E
    q = q_ref[...]
    q_out[...] = (_rope(q, _tile_lanes(tab(TAB_DSA_C), DSA_HEADS), _tile_lanes(tab(TAB_DSA_S), DSA_HEADS),
                        DSA_ROT // 2, LANES) * scale).astype(BF16)
    k = k_ref[...]
    k_out[...] = _rope(k, _tile_lanes(tab(TAB_DSA_C), DSA_KV_HEADS), _tile_lanes(tab(TAB_DSA_S), DSA_KV_HEADS),
                       DSA_ROT // 2, LANES).astype(BF16)
    vt_out[...] = _vt_with_ones(v_ref[...].T, DSA_KV_HEADS, DSA_HEAD_DIM)
    reps = IDX_HEADS * IDX_DIM // LANES
    qi = _rope(qi_ref[...], _tile_lanes(tab(TAB_IDX_C), reps), _tile_lanes(tab(TAB_IDX_S), reps),
               IDX_ROT // 2, IDX_DIM)
    qb = DSA_QBLK
    first_half = lax.broadcasted_iota(I32, (qb, LANES), 1) < IDX_DIM
    for blk in range(q.shape[0] // qb):
        for pair in range(IDX_HEADS // 2):
            hi, lo = _hi_lo(qi[blk * qb:(blk + 1) * qb, pair * LANES:(pair + 1) * LANES])
            even = jnp.where(first_half, hi, pltpu.roll(lo, IDX_DIM, 1)).astype(BF16)
            odd = jnp.where(first_half, pltpu.roll(hi, IDX_DIM, 1), lo).astype(BF16)
            for half in range(2):
                qi4_out[blk, pair, 0:qb, half * LANES:(half + 1) * LANES] = even
                qi4_out[blk, pair, qb:2 * qb, half * LANES:(half + 1) * LANES] = odd
    small = ki_ref[...]
    lane = lax.broadcasted_iota(I32, small.shape, 1)
    c_ki = jnp.where(lane < IDX_DIM, tab(TAB_IDX_C), 1.0)
    s_ki = jnp.where(lane < IDX_DIM, tab(TAB_IDX_S), 0.0)
    hi, lo = _hi_lo(_rope(small, c_ki, s_ki, IDX_ROT // 2, IDX_DIM))
    ki4_out[:, 0:LANES] = jnp.where(lane < IDX_DIM, hi, pltpu.roll(hi, IDX_DIM, 1)).astype(BF16)
    ki4_out[:, LANES:2 * LANES] = jnp.where(lane < IDX_DIM, lo, pltpu.roll(lo, IDX_DIM, 1)).astype(BF16)
    wt = small.T[IDX_DIM:IDX_DIM + IDX_HEADS, :]
    wt_out[...] = wt * (IDX_HEADS ** -0.5 * IDX_DIM ** -0.5)


def _prep_dsa(h2d, hs, tabs, batch, seq):
    tm = KEY_CHUNK
    m = batch * seq
    nt = seq // tm
    qpt = tm // DSA_QBLK
    kvw = DSA_KV_HEADS * DSA_HEAD_DIM

    def row(b, i):
        return b * nt + i

    return pl.pallas_call(
        _prep_dsa_kernel,
        grid=(batch, nt),
        in_specs=[
            pl.BlockSpec((tm, GROUP_WIDTH), lambda b, i: (row(b, i), COL_BQ // GROUP_WIDTH)),
            pl.BlockSpec((tm, kvw), lambda b, i: (row(b, i), COL_BK // kvw)),
            pl.BlockSpec((tm, kvw), lambda b, i: (row(b, i), COL_BV // kvw)),
            pl.BlockSpec((tm, GROUP_WIDTH), lambda b, i: (row(b, i), COL_QI // GROUP_WIDTH)),
            pl.BlockSpec((tm, LANES), lambda b, i: (row(b, i), SMALL_KI // LANES)),
            pl.BlockSpec((tm, N_TABS * LANES), lambda b, i: (row(b, i), 0)),
        ],
        out_specs=[
            pl.BlockSpec((tm, GROUP_WIDTH), lambda b, i: (row(b, i), 0)),
            pl.BlockSpec((tm, kvw), lambda b, i: (row(b, i), 0)),
            pl.BlockSpec((None, None, DSA_KV_HEADS * VT_ROWS, tm), lambda b, i: (b, i, 0, 0)),
            pl.BlockSpec((None, qpt, IDX_HEADS // 2, 2 * DSA_QBLK, 4 * IDX_DIM), lambda b, i: (b, i, 0, 0, 0)),
            pl.BlockSpec((tm, 4 * IDX_DIM), lambda b, i: (row(b, i), 0)),
            pl.BlockSpec((None, IDX_HEADS, tm), lambda b, i: (b, 0, i)),
        ],
        out_shape=[
            jax.ShapeDtypeStruct((m, GROUP_WIDTH), BF16),
            jax.ShapeDtypeStruct((m, kvw), BF16),
            jax.ShapeDtypeStruct((batch, nt, DSA_KV_HEADS * VT_ROWS, tm), BF16),
            jax.ShapeDtypeStruct((batch, seq // DSA_QBLK, IDX_HEADS // 2, 2 * DSA_QBLK, 4 * IDX_DIM), BF16),
            jax.ShapeDtypeStruct((m, 4 * IDX_DIM), BF16),
            jax.ShapeDtypeStruct((batch, IDX_HEADS, seq), F32),
        ],
        compiler_params=_params(24 << 20, 2),
        name="prep_dsa",
    )(h2d, h2d, h2d, h2d, hs, tabs)


def _sortable_key(score):
    bits = lax.bitcast_convert_type(score, I32)
    key = jnp.where(bits < 0, bits ^ jnp.int32(0x7FFFFFFF), bits)
    return jnp.where(score == 0.0, 0, key)


def _dsa_kernel(qi4_ref, ki4_ref, wt_ref, q_ref, k_ref, vt_ref, gate_ref, o_ref, key_ref, s0_ref, s1_ref,
                *, topk, seq_bits, grp):
    i = pl.program_id(1)
    n_vis = i // 2
    n_chunks = n_vis + 1
    wt = wt_ref[...]
    ch = KEY_CHUNK
    qb = DSA_QBLK

    def rows(c):
        return pl.ds(pl.multiple_of(c * ch, ch), ch)

    def score_chunk(c):
        kc = ki4_ref[rows(c), :]
        tot = None
        for p in range(IDX_HEADS // 2):
            s = lax.dot_general(kc, qi4_ref[p], _NT, preferred_element_type=F32)
            r = jnp.maximum(s, 0.0)
            t = r[:, :qb] * wt[2 * p:2 * p + 1, :] + r[:, qb:] * wt[2 * p + 1:2 * p + 2, :]
            tot = t if tot is None else tot + t
        return tot

    def phase1(c, carry):
        key_ref[rows(c), :] = _sortable_key(score_chunk(c))
        return carry

    lax.fori_loop(0, n_vis // 2, lambda p, cr: phase1(2 * p + 1, phase1(2 * p, cr)), 0)
    lax.fori_loop(0, n_vis & 1, lambda _, cr: phase1(n_vis - 1, cr), 0)
    kidx = n_vis * ch + lax.broadcasted_iota(I32, (ch, qb), 0)
    qidx = i * qb + lax.broadcasted_iota(I32, (ch, qb), 1)
    key_ref[rows(n_vis), :] = jnp.where(kidx <= qidx, _sortable_key(score_chunk(n_vis)), INT_MIN)

    n_groups = (n_chunks + grp - 1) // grp

    def pad_chunk(c, carry):
        key_ref[rows(c), :] = jnp.full((ch, qb), INT_MIN, I32)
        return carry

    lax.fori_loop(n_chunks, n_groups * grp, pad_chunk, 0)

    def count_ge(t):
        def body(p, acc):
            blk = key_ref[pl.ds(pl.multiple_of(p * grp * ch, grp * ch), grp * ch), :]
            hit = (blk >= t).astype(I32)
            return acc + jnp.sum(hit.reshape(grp * ch // 8, 8, qb), axis=0)
        acc = lax.fori_loop(0, n_groups, body, jnp.zeros((8, qb), I32))
        return jnp.sum(acc, axis=0, keepdims=True)

    c0 = count_ge(jnp.zeros((1, qb), I32))
    state = (jnp.where(c0 >= topk, 0, INT_MIN).astype(I32), c0)

    def bit_step(b, st):
        t, cnt = st
        cand = t + jnp.left_shift(jnp.int32(1), 30 - b)
        c = count_ge(cand)
        ok = c >= topk
        return jnp.where(ok, cand, t), jnp.where(ok, c, cnt)

    def settled(st):
        t, cnt = st
        return jnp.min(jnp.logical_or(cnt == topk, t == INT_MIN).astype(I32)) > 0

    def finish(st):
        t, cnt = st
        need = topk - count_ge(t + 1)
        return t, need, jnp.logical_and(cnt > topk, t > INT_MIN).astype(I32)

    def search(st, stops):
        if not stops:
            return finish(st)
        hi_bit = 30 if len(stops) == len(SEARCH_STOPS) else SEARCH_STOPS[len(SEARCH_STOPS) - len(stops) - 1] - 1
        st = lax.fori_loop(30 - hi_bit, 31 - stops[0], bit_step, st)
        if stops[0] == 0:
            return finish(st)
        return lax.cond(settled(st), lambda s_: (s_[0], jnp.ones((1, qb), I32), jnp.zeros((1, qb), I32)),
                        lambda s_: search(s_, stops[1:]), st)

    thr, need, tied = search(state, SEARCH_STOPS)

    def count(pred):
        def body(c, acc):
            idx = c * ch + lax.broadcasted_iota(I32, (ch, qb), 0)
            hit = pred(key_ref[rows(c), :], idx).astype(I32)
            return acc + jnp.sum(hit.reshape(ch // 8, 8, qb), axis=0)
        acc = lax.fori_loop(0, n_chunks, body, jnp.zeros((8, qb), I32))
        return jnp.sum(acc, axis=0, keepdims=True)

    def tie_search():
        def step(b, lo):
            cand = lo + jnp.left_shift(jnp.int32(1), seq_bits - 1 - b)
            below = count(lambda keys, idx: jnp.logical_and(keys == thr, idx < cand))
            return jnp.where(below < need, cand, lo)
        return lax.fori_loop(0, seq_bits, step, jnp.zeros((1, qb), I32))

    last_eq = lax.cond(jnp.max(tied) > 0, tie_search,
                       lambda: jnp.full((1, qb), 2 ** seq_bits, I32))
    thr_sel = jnp.maximum(thr, INT_MIN + 1)

    nq = DSA_GROUP * qb
    one_hot = (lax.broadcasted_iota(I32, (nq, qb), 0) % qb == lax.broadcasted_iota(I32, (nq, qb), 1))
    one_hot = jnp.where(one_hot, 1.0, 0.0).astype(BF16)
    qs = [jnp.concatenate([jnp.concatenate([q_ref[:, (n * DSA_GROUP + g) * LANES:(n * DSA_GROUP + g + 1) * LANES]
                                            for g in range(DSA_GROUP)], axis=0), one_hot], axis=1)
          for n in range(DSA_KV_HEADS)]

    slots = (s0_ref, s1_ref)

    def produce(c, slot):
        keys = key_ref[rows(c), :]
        idx = c * ch + lax.broadcasted_iota(I32, (ch, qb), 0)
        drop = jnp.logical_and(keys == thr, idx > last_eq)
        sel = jnp.logical_and(keys >= thr_sel, jnp.logical_not(drop))
        neg = jnp.where(sel, 0.0, NEG_BIG).astype(BF16)
        col_max = []
        for n in range(DSA_KV_HEADS):
            kc = jnp.concatenate([k_ref[rows(c), n * LANES:(n + 1) * LANES], neg], axis=1)
            s = lax.dot_general(kc, qs[n], _NT, preferred_element_type=F32)
            slots[slot][n] = s
            col_max.append(jnp.max(s, axis=0, keepdims=True))
        return tuple(col_max)

    def consume(c, slot, col_max, carry):
        return tuple(_softmax_step(slots[slot][n], col_max[n],
                                   vt_ref[c, n * VT_ROWS:(n + 1) * VT_ROWS, :], *carry[n])
                     for n in range(DSA_KV_HEADS))

    def phase3(j, state):
        carry, col0 = state
        col1 = produce(2 * j + 1, 1)
        carry = consume(2 * j, 0, col0, carry)
        col0 = produce(2 * j + 2, 0)
        carry = consume(2 * j + 1, 1, col1, carry)
        return carry, col0

    init = tuple((jnp.full((1, nq), NEG_BIG, F32), jnp.zeros((VT_ROWS, nq), F32))
                 for _ in range(DSA_KV_HEADS))
    pairs = (n_chunks - 1) // 2
    state = lax.fori_loop(0, pairs // 2, lambda jj, st: phase3(2 * jj + 1, phase3(2 * jj, st)),
                          (init, produce(0, 0)))
    carry, col0 = lax.fori_loop(0, pairs & 1, lambda _, st: phase3(pairs - 1, st), state)
    last = 2 * pairs

    def tail_two(cr):
        col1 = produce(last + 1, 1)
        cr = consume(last, 0, col0, cr)
        return consume(last + 1, 1, col1, cr)

    res = lax.cond(n_chunks - 1 > last, tail_two, lambda cr: consume(last, 0, col0, cr), carry)
    for n in range(DSA_KV_HEADS):
        o = _normalised_t(res[n][1], DSA_HEAD_DIM)
        for g in range(DSA_GROUP):
            col = (n * DSA_GROUP + g) * LANES
            o_ref[:, col:col + LANES] = (o[g * qb:(g + 1) * qb, :]
                                         * _silu(gate_ref[:, col:col + LANES])).astype(o_ref.dtype)


def _dsa(qi4, ki4, wt, q, k, vt, h2d, batch, seq):
    m = batch * seq
    nq = seq // DSA_QBLK
    nck = seq // KEY_CHUNK
    kvw = DSA_KV_HEADS * DSA_HEAD_DIM
    topk = min(TOPK_MAX, seq // 4)
    seq_bits = int(np.log2(seq))
    assert 2 ** seq_bits == seq and nck % 2 == 0 and topk <= KEY_CHUNK
    return pl.pallas_call(
        functools.partial(_dsa_kernel, topk=topk, seq_bits=seq_bits, grp=2),
        grid=(batch, nq),
        in_specs=[
            pl.BlockSpec((None, None, IDX_HEADS // 2, 2 * DSA_QBLK, 4 * IDX_DIM),
                         lambda b, i: (b, i, 0, 0, 0)),
            pl.BlockSpec((seq, 4 * IDX_DIM), lambda b, i: (b, 0)),
            pl.BlockSpec((None, IDX_HEADS, DSA_QBLK), lambda b, i: (b, 0, i)),
            pl.BlockSpec((DSA_QBLK, GROUP_WIDTH), lambda b, i: (b * nq + i, 0)),
            pl.BlockSpec((seq, kvw), lambda b, i: (b, 0)),
            pl.BlockSpec((None, nck, DSA_KV_HEADS * VT_ROWS, KEY_CHUNK), lambda b, i: (b, 0, 0, 0)),
            pl.BlockSpec((DSA_QBLK, GROUP_WIDTH), lambda b, i: (b * nq + i, COL_BG // GROUP_WIDTH)),
        ],
        out_specs=pl.BlockSpec((DSA_QBLK, GROUP_WIDTH), lambda b, i: (b * nq + i, 0)),
        out_shape=jax.ShapeDtypeStruct((m, GROUP_WIDTH), BF16),
        scratch_shapes=[pltpu.VMEM((seq, DSA_QBLK), I32),
                        pltpu.VMEM((DSA_KV_HEADS, KEY_CHUNK, DSA_GROUP * DSA_QBLK), F32),
                        pltpu.VMEM((DSA_KV_HEADS, KEY_CHUNK, DSA_GROUP * DSA_QBLK), F32)],
        compiler_params=_params(32 << 20, 2),
        name="dsa",
    )(qi4, ki4, wt, q, k, vt, h2d)


def _rglru_kernel(u_ref, gate_ref, cw_ref, cb_ref, wa_ref, ba_ref, wx_ref, bx_ref, lam_ref,
                  o_ref, halo_ref, h_ref, a_scr, x_scr, y_scr, *, batch, tt):
    t = pl.program_id(0)

    @pl.when(t == 0)
    def _():
        halo_ref[...] = jnp.zeros_like(halo_ref)
        h_ref[...] = jnp.zeros_like(h_ref)

    lam = lam_ref[...]
    z = -lam
    softplus = jnp.maximum(z, 0.0) + jnp.log1p(jnp.exp(-jnp.abs(z)))
    row = lax.broadcasted_iota(I32, (tt, LRU_WIDTH), 0)
    for b in range(batch):
        u = u_ref[b]
        prev = halo_ref[b]
        p1, p2, p3 = prev[7:8, :], prev[6:7, :], prev[5:6, :]
        s1 = jnp.where(row == 0, p1, pltpu.roll(u, 1, 0))
        s2 = jnp.where(row == 0, p2, jnp.where(row == 1, p1, pltpu.roll(u, 2, 0)))
        s3 = jnp.where(row == 0, p3, jnp.where(row == 1, p2, jnp.where(row == 2, p1, pltpu.roll(u, 3, 0))))
        conv = (cw_ref[3:4, :] * u + cw_ref[2:3, :] * s1 + cw_ref[1:2, :] * s2 + cw_ref[0:1, :] * s3
                + cb_ref[...])
        halo_ref[b] = u[tt - 8:, :]
        cb16 = conv.astype(BF16)
        r = jax.nn.sigmoid(jnp.dot(cb16, wa_ref[...], preferred_element_type=F32) + ba_ref[...])
        ig = jax.nn.sigmoid(jnp.dot(cb16, wx_ref[...], preferred_element_type=F32) + bx_ref[...])
        log_a = -LRU_C * r * softplus
        a_scr[b] = jnp.exp(log_a)
        th = jnp.tanh(log_a)
        x_scr[b] = jnp.sqrt(-2.0 * th / (1.0 - th)) * (ig * conv)

    def step(s, h):
        h = a_scr[:, pl.ds(s, 1), :] * h + x_scr[:, pl.ds(s, 1), :]
        y_scr[:, pl.ds(s, 1), :] = h
        return h

    h_ref[...] = lax.fori_loop(0, tt, step, h_ref[...], unroll=8)
    o_ref[...] = (y_scr[...] * _silu(gate_ref[...])).astype(o_ref.dtype)


def _rglru(h3d, conv_w, conv_b, wa_bd, b_a, wx_bd, b_x, lam, batch, seq, tt):
    nt = seq // tt
    w = LRU_WIDTH

    def vec(r):
        return pl.BlockSpec((r, w), lambda t: (0, 0))

    return pl.pallas_call(
        functools.partial(_rglru_kernel, batch=batch, tt=tt),
        grid=(nt,),
        in_specs=[
            pl.BlockSpec((batch, tt, w), lambda t: (0, t, COL_U // w)),
            pl.BlockSpec((batch, tt, w), lambda t: (0, t, COL_CG // w)),
            vec(CONV_WIDTH), vec(1),
            pl.BlockSpec((w, w), lambda t: (0, 0)), vec(1),
            pl.BlockSpec((w, w), lambda t: (0, 0)), vec(1),
            vec(1),
        ],
        out_specs=pl.BlockSpec((batch, tt, w), lambda t: (0, t, 0)),
        out_shape=jax.ShapeDtypeStruct((batch, seq, w), BF16),
        scratch_shapes=[pltpu.VMEM((batch, 8, w), F32), pltpu.VMEM((batch, 1, w), F32),
                        pltpu.VMEM((batch, tt, w), F32), pltpu.VMEM((batch, tt, w), F32),
                        pltpu.VMEM((batch, tt, w), F32)],
        compiler_params=_params(32 << 20, 1),
        name="rglru",
    )(h3d, h3d, conv_w, conv_b, wa_bd, b_a, wx_bd, b_x, lam)


def _mem_attn_kernel(q_ref, gate_ref, km_ref, vm_ref, o_ref):
    scale = MEM_HEAD_DIM ** -0.5
    for h in range(MEM_HEADS):
        cols = slice(h * MEM_HEAD_DIM, (h + 1) * MEM_HEAD_DIM)
        qh = (q_ref[:, cols] * scale).astype(BF16)
        s = lax.dot_general(qh, km_ref[:, cols], _NT, preferred_element_type=F32)
        p = jnp.exp(s - jnp.max(s, axis=-1, keepdims=True))
        l = jnp.sum(p, axis=-1, keepdims=True)
        o = jnp.dot(p.astype(BF16), vm_ref[:, cols], preferred_element_type=F32) / l
        o_ref[:, cols] = (o * _silu(gate_ref[:, cols])).astype(o_ref.dtype)


def _mem_attn(h2d, km, vm, batch, seq, tm):
    m = batch * seq
    nt = seq // tm
    w = GROUP_WIDTH
    return pl.pallas_call(
        _mem_attn_kernel,
        grid=(batch, nt),
        in_specs=[
            pl.BlockSpec((tm, w), lambda b, i: (b * nt + i, COL_DQ // w)),
            pl.BlockSpec((tm, w), lambda b, i: (b * nt + i, COL_DG // w)),
            pl.BlockSpec((N_MEM, w), lambda b, i: (b, 0)),
            pl.BlockSpec((N_MEM, w), lambda b, i: (b, 0)),
        ],
        out_specs=pl.BlockSpec((tm, w), lambda b, i: (b * nt + i, 0)),
        out_shape=jax.ShapeDtypeStruct((m, w), BF16),
        compiler_params=_params(24 << 20, 2),
        name="mem_attn",
    )(h2d, h2d, km, vm)


def _out_ln_kernel(*refs, n_lhs, kg, nj, per_emit, d_model):
    lhs = refs[:n_lhs]
    w_ref, x_ref, g_ref, b_ref, o_ref, o16_ref, z_ref, mu_ref, rs_ref = refs[n_lhs:]
    j = pl.program_id(1)

    @pl.when(j < nj)
    def _():
        acc = None
        for g in range(n_lhs):
            part = jnp.dot(lhs[g][...], w_ref[g * kg:(g + 1) * kg, :].astype(BF16), preferred_element_type=F32)
            acc = part if acc is None else acc + part
        z_ref[j] = DEEPNORM_ALPHA * x_ref[...] + acc

    @pl.when(j == nj)
    def _():
        def lane_tiles(v):
            tiles = [v[:, t * LANES:(t + 1) * LANES] for t in range(v.shape[1] // LANES)]
            return functools.reduce(lambda a, b: a + b, tiles)

        tot = functools.reduce(lambda a, b: a + b, [lane_tiles(z_ref[jj]) for jj in range(nj)])
        mu = jnp.sum(tot, axis=-1, keepdims=True) * (1.0 / d_model)
        sq = None
        for jj in range(nj):
            dlt = z_ref[jj] - mu
            part = lane_tiles(dlt * dlt)
            sq = part if sq is None else sq + part
        mu_ref[...] = mu
        rs_ref[...] = lax.rsqrt(jnp.sum(sq, axis=-1, keepdims=True) * (1.0 / d_model) + LN_EPS)

    @pl.when(j >= nj)
    def _():
        e = j - nj
        tn = z_ref.shape[2]
        for t in range(per_emit):
            cols = slice(t * tn, (t + 1) * tn)
            out = (z_ref[e * per_emit + t] - mu_ref[...]) * rs_ref[...] * g_ref[:, cols] + b_ref[:, cols]
            o_ref[:, cols] = out
            o16_ref[:, cols] = out.astype(BF16)


def _out_proj_deepnorm(lhs_list, w, layer, x, ln_g, ln_b, tm, tn, te):
    m, kg = lhs_list[0].shape
    _, k, n = w.shape
    n_lhs = len(lhs_list)
    nj, ne, per_emit = n // tn, n // te, te // tn
    assert k == kg * n_lhs and m % tm == 0 and n % te == 0 and te % tn == 0

    def col(j):
        return jnp.minimum(j, nj - 1)

    def emit(j):
        return jnp.maximum(j - nj, 0)

    est = (nj * tm * tn * 4 + 2 * (tm * k * 2 + k * tn * 4 + tm * tn * 4 + tm * te * 6) + tm * tn * 8)
    return pl.pallas_call(
        functools.partial(_out_ln_kernel, n_lhs=n_lhs, kg=kg, nj=nj, per_emit=per_emit, d_model=n),
        grid=(m // tm, nj + ne),
        in_specs=[pl.BlockSpec((tm, kg), lambda i, j: (i, 0)) for _ in range(n_lhs)]
        + [pl.BlockSpec((None, k, tn), lambda i, j: (layer, 0, col(j))),
           pl.BlockSpec((tm, tn), lambda i, j: (i, col(j))),
           pl.BlockSpec((1, te), lambda i, j: (0, emit(j))),
           pl.BlockSpec((1, te), lambda i, j: (0, emit(j)))],
        out_specs=[pl.BlockSpec((tm, te), lambda i, j: (i, emit(j))),
                   pl.BlockSpec((tm, te), lambda i, j: (i, emit(j)))],
        out_shape=[jax.ShapeDtypeStruct((m, n), F32), jax.ShapeDtypeStruct((m, n), BF16)],
        scratch_shapes=[pltpu.VMEM((nj, tm, tn), F32), pltpu.VMEM((tm, 1), F32), pltpu.VMEM((tm, 1), F32)],
        compiler_params=_params(est + (10 << 20), 2),
        name="out_proj_deepnorm",
    )(*lhs_list, w, x, ln_g, ln_b)


def _rope_tabs(positions):
    pos = positions.astype(F32)
    b, s = pos.shape

    def one(rot, period):
        inv = ROPE_THETA ** (-jnp.arange(0, rot, 2, dtype=F32) / rot)
        ang = pos[:, :, None] * inv
        c, sn = jnp.cos(ang), jnp.sin(ang)
        rest = period - rot
        cp = jnp.concatenate([c, c, jnp.ones((b, s, rest), F32)], axis=-1)
        sp = jnp.concatenate([-sn, sn, jnp.zeros((b, s, rest), F32)], axis=-1)
        reps = LANES // period
        return jnp.tile(cp, (1, 1, reps)), jnp.tile(sp, (1, 1, reps))

    parts = one(MLA_ROPE, LANES) + one(DSA_ROT, LANES) + one(IDX_ROT, IDX_DIM)
    return jnp.concatenate(parts, axis=-1).reshape(b * s, N_TABS * LANES)


def _block_diag(w):
    l = w.shape[0]
    eye = jnp.eye(LRU_BLOCKS, dtype=w.dtype)
    bd = jnp.einsum('lnde,nm->lndme', w, eye)
    return bd.reshape(l, LRU_WIDTH, LRU_WIDTH).astype(BF16)


def _layer(x, x16, mem16, tabs, lw, l, batch, seq):
    m = batch * seq
    main_rows, small_rows = _in_proj_row_tables(256)
    h2d = _in_proj(x16, lw["w_in_t"], l, main_rows, tm=min(2048, m), tn=256, name="in_proj")
    hs = _in_proj(x16, lw["w_in_t"], l, small_rows, tm=min(2048, m), tn=LANES, name="in_proj_small")
    tile_a = min(512, seq)
    qa, ka, vta = _prep_mla(h2d, hs, tabs, lw["g_cq"], lw["g_ckv"], lw["w_uq"], lw["w_uk"], lw["w_uvt"],
                            batch, seq, tile_a)
    ya = _mla_attn(qa, ka, vta, h2d, batch, seq, tile_a)
    qb, kb, vtb, qi4, ki4, wt = _prep_dsa(h2d, hs, tabs, batch, seq)
    yb = _dsa(qi4, ki4, wt, qb, kb, vtb, h2d, batch, seq)
    yc = _rglru(h2d.reshape(batch, seq, N_MAIN), lw["conv_w"], lw["conv_b"], lw["w_rg_a"], lw["b_rg_a"],
                lw["w_rg_x"], lw["b_rg_x"], lw["lam"], batch, seq, min(256, seq))
    yc = yc.reshape(m, GROUP_WIDTH)
    km = _matmul([mem16], lw["w_mem_k"], l, BF16, tm=mem16.shape[0], tn=256, name="mem_proj_k")
    vm = _matmul([mem16], lw["w_mem_v"], l, BF16, tm=mem16.shape[0], tn=256, name="mem_proj_v")
    yd = _mem_attn(h2d, km, vm, batch, seq, min(512, seq))
    return _out_proj_deepnorm([ya, yb, yc, yd], lw["w_o"], l, x, lw["ln_g"], lw["ln_b"],
                              tm=min(1024, m), tn=256, te=512)


def kernel(x, mem, positions, w_in, g_cq, g_ckv, w_uq, w_ukv, conv_w, conv_b, w_rg_a, b_rg_a, w_rg_x,
           b_rg_x, lru_lambda, w_mem_k, w_mem_v, w_o, ln_g, ln_b):
    batch, seq, d = x.shape
    depth = w_in.shape[0]
    tabs = _rope_tabs(positions)
    w_in_t = w_in.transpose(0, 2, 1)
    w_uq_p = jnp.pad(w_uq.reshape(depth, MLA_Q_LORA, MLA_HEADS, MLA_NOPE + MLA_ROPE),
                     [(0, 0), (0, 0), (0, 0), (0, 2 * LANES - MLA_NOPE - MLA_ROPE)])
    w_uq_p = w_uq_p.reshape(depth, MLA_Q_LORA, MLA_HEADS * 2 * LANES).astype(BF16)
    w_ukv4 = w_ukv.reshape(depth, MLA_KV_LORA, MLA_HEADS, MLA_NOPE + MLA_V)
    w_uk = w_ukv4[..., :MLA_NOPE].reshape(depth, MLA_KV_LORA, MLA_HEADS * MLA_NOPE).astype(BF16)
    w_uvt = w_ukv4[..., MLA_NOPE:].reshape(depth, MLA_KV_LORA, MLA_HEADS * MLA_V)
    w_uvt = w_uvt.transpose(0, 2, 1).astype(BF16)
    wa_bd = _block_diag(w_rg_a)
    wx_bd = _block_diag(w_rg_x)
    mem16 = mem.reshape(batch * mem.shape[1], d).astype(BF16)

    xf = x.reshape(batch * seq, d)
    x16 = xf.astype(BF16)
    for l in range(depth):
        lw = dict(w_in_t=w_in_t, g_cq=g_cq[l][None], g_ckv=g_ckv[l][None], w_uq=w_uq_p[l], w_uk=w_uk[l],
                  w_uvt=w_uvt[l], conv_w=conv_w[l], conv_b=conv_b[l][None], w_rg_a=wa_bd[l],
                  b_rg_a=b_rg_a[l][None], w_rg_x=wx_bd[l], b_rg_x=b_rg_x[l][None], lam=lru_lambda[l][None],
                  w_mem_k=w_mem_k, w_mem_v=w_mem_v, w_o=w_o, ln_g=ln_g[l][None], ln_b=ln_b[l][None])
        xf, x16 = _layer(xf, x16, mem16, tabs, lw, l, batch, seq)
    return xf.reshape(batch, seq, d)
```

```python
import functools

import numpy as np
import jax
import jax.numpy as jnp
from jax import lax
from jax.experimental import pallas as pl
from jax.experimental.pallas import tpu as pltpu

F32 = jnp.float32
BF16 = jnp.bfloat16
I32 = jnp.int32

DEPTH = 4
D_MODEL = 4096
N_MEM = 256
GROUP_WIDTH = D_MODEL // 4
ROPE_THETA = 500000.0
MLA_HEADS = 8
MLA_NOPE = 128
MLA_ROPE = 64
MLA_V = 128
MLA_Q_LORA = GROUP_WIDTH
MLA_KV_LORA = GROUP_WIDTH // 2
DSA_HEADS = 8
DSA_KV_HEADS = 2
DSA_GROUP = DSA_HEADS // DSA_KV_HEADS
DSA_HEAD_DIM = 128
DSA_ROT = DSA_HEAD_DIM // 4
IDX_HEADS = 16
IDX_DIM = 64
IDX_ROT = IDX_DIM // 4
TOPK_MAX = 256
LRU_WIDTH = GROUP_WIDTH
LRU_BLOCKS = 16
LRU_BLOCK_DIM = LRU_WIDTH // LRU_BLOCKS
CONV_WIDTH = 4
LRU_C = 8.0
MEM_HEADS = 4
MEM_HEAD_DIM = GROUP_WIDTH // MEM_HEADS
DEEPNORM_ALPHA = (2 * DEPTH) ** 0.25
LN_EPS = 1e-5
RMS_EPS = 1e-6

IN_SIZES = (
    MLA_Q_LORA, MLA_KV_LORA, MLA_ROPE, GROUP_WIDTH,
    GROUP_WIDTH, DSA_KV_HEADS * DSA_HEAD_DIM, DSA_KV_HEADS * DSA_HEAD_DIM,
    IDX_HEADS * IDX_DIM, IDX_DIM, IDX_HEADS, GROUP_WIDTH,
    LRU_WIDTH, GROUP_WIDTH,
    GROUP_WIDTH, GROUP_WIDTH,
)

LANES = 128
V7X_VMEM_BUDGET = 60 * 1024 * 1024

COL_CQ = 0
COL_AG = 1024
COL_BQ = 2048
COL_QI = 3072
COL_BG = 4096
COL_U = 5120
COL_CG = 6144
COL_DQ = 7168
COL_DG = 8192
COL_CKV = 9216
COL_BK = 9728
COL_BV = 9984
N_MAIN = 10240
SMALL_KR = 0
SMALL_KI = 128

TAB_MLA_C, TAB_MLA_S, TAB_DSA_C, TAB_DSA_S, TAB_IDX_C, TAB_IDX_S = range(6)
N_TABS = 6

NEG_BIG = -1e30
INT_MIN = -2 ** 31
LOG2E = 1.4426950408889634
KEY_CHUNK = 256
DSA_QBLK = 128
SEARCH_STOPS = (12, 9, 6, 3, 0)
VT_ROWS = 128 + 16
_NT = (((1,), (1,)), ((), ()))


def _vmem_limit(nbytes):
    return int(min(V7X_VMEM_BUDGET, max(nbytes, 16 * 1024 * 1024)))


def _params(nbytes, ndims):
    return pltpu.CompilerParams(dimension_semantics=("arbitrary",) * ndims,
                                vmem_limit_bytes=_vmem_limit(nbytes))


def _mm_kernel(*refs, n_lhs, kg):
    w_ref = refs[n_lhs]
    o_ref = refs[n_lhs + 1]
    acc = None
    for g in range(n_lhs):
        part = jnp.dot(refs[g][...], w_ref[g * kg:(g + 1) * kg, :].astype(BF16), preferred_element_type=F32)
        acc = part if acc is None else acc + part
    o_ref[...] = acc.astype(o_ref.dtype)


def _matmul(lhs_list, w, layer, out_dtype, tm, tn, name):
    m, kg = lhs_list[0].shape
    _, k, n = w.shape
    n_lhs = len(lhs_list)
    assert k == kg * n_lhs and m % tm == 0 and n % tn == 0
    est = 2 * (tm * k * 2 + k * tn * w.dtype.itemsize + tm * tn * jnp.dtype(out_dtype).itemsize) + tm * tn * 8
    return pl.pallas_call(
        functools.partial(_mm_kernel, n_lhs=n_lhs, kg=kg),
        grid=(m // tm, n // tn),
        in_specs=[pl.BlockSpec((tm, kg), lambda i, j: (i, 0)) for _ in range(n_lhs)]
        + [pl.BlockSpec((None, k, tn), lambda i, j: (layer, 0, j))],
        out_specs=pl.BlockSpec((tm, tn), lambda i, j: (i, j)),
        out_shape=jax.ShapeDtypeStruct((m, n), out_dtype),
        compiler_params=_params(est + (8 << 20), 2),
        name=name,
    )(*lhs_list, w)


def _in_proj_kernel(tab_ref, x_ref, w_ref, o_ref):
    del tab_ref
    o_ref[...] = lax.dot_general(x_ref[...], w_ref[...].astype(BF16), _NT, preferred_element_type=F32)


def _in_proj(x16, w_t, layer, row_offsets, tm, tn, name):
    m, k = x16.shape
    nblk = len(row_offsets)
    est = 2 * (tm * k * 2 + tn * k * 4 + tm * tn * 4) + tm * tn * 8
    grid_spec = pltpu.PrefetchScalarGridSpec(
        num_scalar_prefetch=1,
        grid=(m // tm, nblk),
        in_specs=[
            pl.BlockSpec((tm, k), lambda i, j, tab: (i, 0)),
            pl.BlockSpec((pl.Squeezed(), pl.Element(tn), pl.Element(k)),
                         lambda i, j, tab: (layer, pl.multiple_of(tab[j], 16), 0)),
        ],
        out_specs=pl.BlockSpec((tm, tn), lambda i, j, tab: (i, j)),
    )
    return pl.pallas_call(
        _in_proj_kernel,
        grid_spec=grid_spec,
        out_shape=jax.ShapeDtypeStruct((m, nblk * tn), F32),
        compiler_params=_params(est + (8 << 20), 2),
        name=name,
    )(jnp.asarray(row_offsets, I32), x16, w_t)


def _in_proj_row_tables(tn):
    names = ("cq", "ckv", "kr", "ag", "bq", "bk", "bv", "qi", "ki", "wi", "bg", "u", "cg", "dq", "dg")
    src = dict(zip(names, [0] + [int(c) for c in np.cumsum(IN_SIZES)[:-1]]))
    size = dict(zip(names, IN_SIZES))
    order = [("cq", COL_CQ), ("ag", COL_AG), ("bq", COL_BQ), ("qi", COL_QI), ("bg", COL_BG), ("u", COL_U),
             ("cg", COL_CG), ("dq", COL_DQ), ("dg", COL_DG), ("ckv", COL_CKV), ("bk", COL_BK), ("bv", COL_BV)]
    main = []
    for n, dst in order:
        assert dst == len(main) * tn and size[n] % tn == 0 and src[n] % 16 == 0
        main += [src[n] + t * tn for t in range(size[n] // tn)]
    assert len(main) * tn == N_MAIN and src["wi"] == src["ki"] + size["ki"]
    return np.asarray(main, np.int32), np.asarray([src["kr"], src["ki"]], np.int32)


def _rope(v, c_tab, s_tab, half, period):
    width = v.shape[-1]
    lane = lax.broadcasted_iota(I32, v.shape, 1) & (period - 1)
    swapped = jnp.where(lane < half, pltpu.roll(v, width - half, 1), pltpu.roll(v, half, 1))
    return v * c_tab + swapped * s_tab


def _tile_lanes(t, reps):
    return t if reps == 1 else jnp.concatenate([t] * reps, axis=1)


def _rms(x, g):
    return x * lax.rsqrt(jnp.mean(x * x, axis=-1, keepdims=True) + RMS_EPS) * g


def _sigmoid(v):
    return 0.5 * jnp.tanh(0.5 * v) + 0.5


def _silu(g):
    return g * _sigmoid(g)


def _prep_mla_kernel(cq_ref, ckv_ref, kr_ref, tab_ref, gcq_ref, gckv_ref, wuq_ref, wk_ref, wvt_ref,
                     q_out, k_out, vt_out):
    scale = (MLA_NOPE + MLA_ROPE) ** -0.5 * LOG2E
    nq = _rms(cq_ref[...], gcq_ref[...]).astype(BF16)
    nkv = _rms(ckv_ref[...], gckv_ref[...]).astype(BF16)
    c_tab = tab_ref[:, TAB_MLA_C * LANES:(TAB_MLA_C + 1) * LANES]
    s_tab = tab_ref[:, TAB_MLA_S * LANES:(TAB_MLA_S + 1) * LANES]
    qf = jnp.dot(nq, wuq_ref[...], preferred_element_type=F32) * scale
    kn = jnp.dot(nkv, wk_ref[...], preferred_element_type=F32)
    lane = lax.broadcasted_iota(I32, kr_ref.shape, 1)
    kr = jnp.where(lane < MLA_ROPE, kr_ref[...], 0.0)
    kr = _rope(kr, c_tab, s_tab, MLA_ROPE // 2, LANES).astype(BF16)
    for h in range(MLA_HEADS):
        base = h * 2 * LANES
        q_out[:, base:base + LANES] = qf[:, base:base + LANES].astype(BF16)
        q_out[:, base + LANES:base + 2 * LANES] = _rope(
            qf[:, base + LANES:base + 2 * LANES], c_tab, s_tab, MLA_ROPE // 2, LANES).astype(BF16)
        k_out[:, base:base + LANES] = kn[:, h * LANES:(h + 1) * LANES].astype(BF16)
        k_out[:, base + LANES:base + 2 * LANES] = kr
    vt = lax.dot_general(wvt_ref[...], nkv, _NT, preferred_element_type=F32)
    half = vt.shape[1] // 2
    vt_out[0] = _vt_with_ones(vt[:, :half], MLA_HEADS, MLA_V)
    vt_out[1] = _vt_with_ones(vt[:, half:], MLA_HEADS, MLA_V)


def _prep_mla(h2d, hs, tabs, g_cq, g_ckv, wuq_p, wk, wvt, batch, seq, tm):
    m = batch * seq
    nt = seq // tm
    hd = MLA_HEADS * 2 * LANES
    return pl.pallas_call(
        _prep_mla_kernel,
        grid=(batch, nt),
        in_specs=[
            pl.BlockSpec((tm, MLA_Q_LORA), lambda b, i: (b * nt + i, COL_CQ // MLA_Q_LORA)),
            pl.BlockSpec((tm, MLA_KV_LORA), lambda b, i: (b * nt + i, COL_CKV // MLA_KV_LORA)),
            pl.BlockSpec((tm, LANES), lambda b, i: (b * nt + i, SMALL_KR // LANES)),
            pl.BlockSpec((tm, N_TABS * LANES), lambda b, i: (b * nt + i, 0)),
            pl.BlockSpec((1, MLA_Q_LORA), lambda b, i: (0, 0)),
            pl.BlockSpec((1, MLA_KV_LORA), lambda b, i: (0, 0)),
            pl.BlockSpec((MLA_Q_LORA, hd), lambda b, i: (0, 0)),
            pl.BlockSpec((MLA_KV_LORA, MLA_HEADS * MLA_NOPE), lambda b, i: (0, 0)),
            pl.BlockSpec((MLA_HEADS * MLA_V, MLA_KV_LORA), lambda b, i: (0, 0)),
        ],
        out_specs=[
            pl.BlockSpec((tm, hd), lambda b, i: (b * nt + i, 0)),
            pl.BlockSpec((tm, hd), lambda b, i: (b * nt + i, 0)),
            pl.BlockSpec((None, 2, MLA_HEADS * VT_ROWS, tm // 2), lambda b, i: (b, i, 0, 0)),
        ],
        out_shape=[
            jax.ShapeDtypeStruct((m, hd), BF16),
            jax.ShapeDtypeStruct((m, hd), BF16),
            jax.ShapeDtypeStruct((batch, 2 * nt, MLA_HEADS * VT_ROWS, tm // 2), BF16),
        ],
        compiler_params=_params(40 << 20, 2),
        name="prep_mla",
    )(h2d, h2d, hs, tabs, g_cq, g_ckv, wuq_p, wk, wvt)


def _softmax_step(s, col_max, vt_chunk, m, acc):
    m_new = jnp.maximum(m, col_max)
    alpha = jnp.exp2(m - m_new)
    p = jnp.exp2(s - m_new)
    acc_new = alpha * acc + jnp.dot(vt_chunk, p.astype(BF16), preferred_element_type=F32)
    return m_new, acc_new


def _normalised_t(acc, dv):
    return (acc[:dv, :] / acc[dv:dv + 1, :]).T


def _vt_with_ones(vt, heads, dv):
    ones = jnp.ones((VT_ROWS - dv, vt.shape[1]), BF16)
    parts = []
    for h in range(heads):
        parts += [vt[h * dv:(h + 1) * dv, :].astype(BF16), ones]
    return jnp.concatenate(parts, axis=0)


def _mla_attn_kernel(q_ref, k_ref, vt_ref, gate_ref, o_ref, s0_ref, s1_ref, *, tile, ch, hp):
    i = pl.program_id(2)
    dk = 2 * LANES
    qs = [q_ref[:, n * dk:(n + 1) * dk] for n in range(hp)]
    slots = (s0_ref, s1_ref)

    def produce(c, slot, diag_offset=None):
        col_max = []
        for n in range(hp):
            kc = k_ref[pl.ds(pl.multiple_of(c * ch, ch), ch), n * dk:(n + 1) * dk]
            s = lax.dot_general(kc, qs[n], _NT, preferred_element_type=F32)
            if diag_offset is not None:
                kidx = lax.broadcasted_iota(I32, (ch, tile), 0) + diag_offset
                qidx = lax.broadcasted_iota(I32, (ch, tile), 1)
                s = jnp.where(kidx <= qidx, s, NEG_BIG)
            slots[slot][n] = s
            col_max.append(jnp.max(s, axis=0, keepdims=True))
        return tuple(col_max)

    def consume(c, slot, col_max, carry):
        return tuple(_softmax_step(slots[slot][n], col_max[n], vt_ref[c, n * VT_ROWS:(n + 1) * VT_ROWS, :],
                                   *carry[n]) for n in range(hp))

    init = tuple((jnp.full((1, tile), NEG_BIG, F32), jnp.zeros((VT_ROWS, tile), F32)) for _ in range(hp))
    diag = 2 * i
    col_a = produce(diag, 0, 0)
    col_b = produce(diag + 1, 1, ch)
    carry = consume(diag, 0, col_a, init)

    def pair(j, state):
        cr, col1, c1 = state
        col0 = produce(2 * j, 0)
        cr = consume(c1, 1, col1, cr)
        col1 = produce(2 * j + 1, 1)
        cr = consume(2 * j, 0, col0, cr)
        return cr, col1, 2 * j + 1

    state = lax.fori_loop(0, i // 2, lambda jj, st: pair(2 * jj + 1, pair(2 * jj, st)),
                          (carry, col_b, diag + 1))
    carry, col1, c1 = lax.fori_loop(0, i & 1, lambda _, st: pair(i - 1, st), state)
    res = consume(c1, 1, col1, carry)
    for n in range(hp):
        o = _normalised_t(res[n][1], MLA_V)
        cols = slice(n * LANES, (n + 1) * LANES)
        o_ref[:, cols] = (o * _silu(gate_ref[:, cols])).astype(o_ref.dtype)


def _mla_attn(q, k, vt, h2d, batch, seq, tile):
    m = batch * seq
    nt = seq // tile
    ch = tile // 2
    hp = 4
    dk = 2 * LANES
    return pl.pallas_call(
        functools.partial(_mla_attn_kernel, tile=tile, ch=ch, hp=hp),
        grid=(batch, MLA_HEADS // hp, nt),
        in_specs=[
            pl.BlockSpec((tile, hp * dk), lambda b, h, i: (b * nt + i, h)),
            pl.BlockSpec((seq, hp * dk), lambda b, h, i: (b, h)),
            pl.BlockSpec((None, 2 * nt, hp * VT_ROWS, ch), lambda b, h, i: (b, 0, h, 0)),
            pl.BlockSpec((tile, hp * LANES), lambda b, h, i: (b * nt + i, COL_AG // (hp * LANES) + h)),
        ],
        out_specs=pl.BlockSpec((tile, hp * LANES), lambda b, h, i: (b * nt + i, h)),
        out_shape=jax.ShapeDtypeStruct((m, GROUP_WIDTH), BF16),
        scratch_shapes=[pltpu.VMEM((hp, ch, tile), F32), pltpu.VMEM((hp, ch, tile), F32)],
        compiler_params=_params(40 << 20, 3),
        name="mla_attn",
    )(q, k, vt, h2d)


def _hi_lo(v):
    hi = v.astype(BF16).astype(F32)
    return hi, v - hi


def _prep_dsa_kernel(q_ref, k_ref, v_ref, qi_ref, ki_ref, tab_ref,
                     q_out, k_out, vt_out, qi2_out, ki4_out, wt_out):
    def tab(t):
        return tab_ref[:, t * LANES:(t + 1) * LANES]

    scale = DSA_HEAD_DIM ** -0.5 * LOG2E
    q = q_ref[...]
    q_out[...] = (_rope(q, _tile_lanes(tab(TAB_DSA_C), DSA_HEADS), _tile_lanes(tab(TAB_DSA_S), DSA_HEADS),
                        DSA_ROT // 2, LANES) * scale).astype(BF16)
    k = k_ref[...]
    k_out[...] = _rope(k, _tile_lanes(tab(TAB_DSA_C), DSA_KV_HEADS), _tile_lanes(tab(TAB_DSA_S), DSA_KV_HEADS),
                       DSA_ROT // 2, LANES).astype(BF16)
    vt_out[...] = _vt_with_ones(v_ref[...].T, DSA_KV_HEADS, DSA_HEAD_DIM)
    reps = IDX_HEADS * IDX_DIM // LANES
    qi = _rope(qi_ref[...], _tile_lanes(tab(TAB_IDX_C), reps), _tile_lanes(tab(TAB_IDX_S), reps),
               IDX_ROT // 2, IDX_DIM)
    qb = DSA_QBLK
    first_half = lax.broadcasted_iota(I32, (qb, LANES), 1) < IDX_DIM
    for blk in range(q.shape[0] // qb):
        for pair in range(IDX_HEADS // 2):
            hi, lo = _hi_lo(qi[blk * qb:(blk + 1) * qb, pair * LANES:(pair + 1) * LANES])
            even = jnp.where(first_half, hi, pltpu.roll(lo, IDX_DIM, 1)).astype(BF16)
            odd = jnp.where(first_half, pltpu.roll(hi, IDX_DIM, 1), lo).astype(BF16)
            qi2_out[blk, pair, 0:qb, :] = even
            qi2_out[blk, pair, qb:2 * qb, :] = odd
    small = ki_ref[...]
    lane = lax.broadcasted_iota(I32, small.shape, 1)
    c_ki = jnp.where(lane < IDX_DIM, tab(TAB_IDX_C), 1.0)
    s_ki = jnp.where(lane < IDX_DIM, tab(TAB_IDX_S), 0.0)
    hi, lo = _hi_lo(_rope(small, c_ki, s_ki, IDX_ROT // 2, IDX_DIM))
    ki4_out[:, 0:LANES] = jnp.where(lane < IDX_DIM, hi, pltpu.roll(hi, IDX_DIM, 1)).astype(BF16)
    ki4_out[:, LANES:2 * LANES] = jnp.where(lane < IDX_DIM, lo, pltpu.roll(lo, IDX_DIM, 1)).astype(BF16)
    wt = small.T[IDX_DIM:IDX_DIM + IDX_HEADS, :]
    wt_out[...] = wt * (IDX_HEADS ** -0.5 * IDX_DIM ** -0.5)


def _prep_dsa(h2d, hs, tabs, batch, seq):
    tm = KEY_CHUNK
    m = batch * seq
    nt = seq // tm
    qpt = tm // DSA_QBLK
    kvw = DSA_KV_HEADS * DSA_HEAD_DIM

    def row(b, i):
        return b * nt + i

    return pl.pallas_call(
        _prep_dsa_kernel,
        grid=(batch, nt),
        in_specs=[
            pl.BlockSpec((tm, GROUP_WIDTH), lambda b, i: (row(b, i), COL_BQ // GROUP_WIDTH)),
            pl.BlockSpec((tm, kvw), lambda b, i: (row(b, i), COL_BK // kvw)),
            pl.BlockSpec((tm, kvw), lambda b, i: (row(b, i), COL_BV // kvw)),
            pl.BlockSpec((tm, GROUP_WIDTH), lambda b, i: (row(b, i), COL_QI // GROUP_WIDTH)),
            pl.BlockSpec((tm, LANES), lambda b, i: (row(b, i), SMALL_KI // LANES)),
            pl.BlockSpec((tm, N_TABS * LANES), lambda b, i: (row(b, i), 0)),
        ],
        out_specs=[
            pl.BlockSpec((tm, GROUP_WIDTH), lambda b, i: (row(b, i), 0)),
            pl.BlockSpec((tm, kvw), lambda b, i: (row(b, i), 0)),
            pl.BlockSpec((None, None, DSA_KV_HEADS * VT_ROWS, tm), lambda b, i: (b, i, 0, 0)),
            pl.BlockSpec((None, qpt, IDX_HEADS // 2, 2 * DSA_QBLK, 2 * IDX_DIM), lambda b, i: (b, i, 0, 0, 0)),
            pl.BlockSpec((tm, 4 * IDX_DIM), lambda b, i: (row(b, i), 0)),
            pl.BlockSpec((None, IDX_HEADS, tm), lambda b, i: (b, 0, i)),
        ],
        out_shape=[
            jax.ShapeDtypeStruct((m, GROUP_WIDTH), BF16),
            jax.ShapeDtypeStruct((m, kvw), BF16),
            jax.ShapeDtypeStruct((batch, nt, DSA_KV_HEADS * VT_ROWS, tm), BF16),
            jax.ShapeDtypeStruct((batch, seq // DSA_QBLK, IDX_HEADS // 2, 2 * DSA_QBLK, 2 * IDX_DIM), BF16),
            jax.ShapeDtypeStruct((m, 4 * IDX_DIM), BF16),
            jax.ShapeDtypeStruct((batch, IDX_HEADS, seq), F32),
        ],
        compiler_params=_params(24 << 20, 2),
        name="prep_dsa",
    )(h2d, h2d, h2d, h2d, hs, tabs)


def _sortable_key(score):
    bits = lax.bitcast_convert_type(score, I32)
    key = jnp.where(bits < 0, bits ^ jnp.int32(0x7FFFFFFF), bits)
    return jnp.where(score == 0.0, 0, key)


def _dsa_kernel(qi2_ref, ki4_ref, wt_ref, q_ref, k_ref, vt_ref, gate_ref, o_ref, key_ref, s0_ref, s1_ref, qi4_ref,
                *, topk, seq_bits, grp):
    i = pl.program_id(1)
    n_vis = i // 2
    n_chunks = n_vis + 1
    wt = wt_ref[...]
    ch = KEY_CHUNK
    qb = DSA_QBLK

    def rows(c):
        return pl.ds(pl.multiple_of(c * ch, ch), ch)

    for p in range(IDX_HEADS // 2):
        qi4_ref[p, :, 0:LANES] = qi2_ref[p]
        qi4_ref[p, :, LANES:2 * LANES] = qi2_ref[p]

    def score_chunk(c):
        kc = ki4_ref[rows(c), :]
        tot = None
        for p in range(IDX_HEADS // 2):
            s = lax.dot_general(kc, qi4_ref[p], _NT, preferred_element_type=F32)
            r = jnp.maximum(s, 0.0)
            t = r[:, :qb] * wt[2 * p:2 * p + 1, :] + r[:, qb:] * wt[2 * p + 1:2 * p + 2, :]
            tot = t if tot is None else tot + t
        return tot

    def phase1(c, carry):
        key_ref[rows(c), :] = _sortable_key(score_chunk(c))
        return carry

    lax.fori_loop(0, n_vis // 2, lambda p, cr: phase1(2 * p + 1, phase1(2 * p, cr)), 0)
    lax.fori_loop(0, n_vis & 1, lambda _, cr: phase1(n_vis - 1, cr), 0)
    kidx = n_vis * ch + lax.broadcasted_iota(I32, (ch, qb), 0)
    qidx = i * qb + lax.broadcasted_iota(I32, (ch, qb), 1)
    key_ref[rows(n_vis), :] = jnp.where(kidx <= qidx, _sortable_key(score_chunk(n_vis)), INT_MIN)

    n_groups = (n_chunks + grp - 1) // grp

    def pad_chunk(c, carry):
        key_ref[rows(c), :] = jnp.full((ch, qb), INT_MIN, I32)
        return carry

    lax.fori_loop(n_chunks, n_groups * grp, pad_chunk, 0)

    def count_ge(t):
        def body(p, acc):
            blk = key_ref[pl.ds(pl.multiple_of(p * grp * ch, grp * ch), grp * ch), :]
            hit = (blk >= t).astype(I32)
            return acc + jnp.sum(hit.reshape(grp * ch // 8, 8, qb), axis=0)
        acc = lax.fori_loop(0, n_groups, body, jnp.zeros((8, qb), I32))
        return jnp.sum(acc, axis=0, keepdims=True)

    c0 = count_ge(jnp.zeros((1, qb), I32))
    state = (jnp.where(c0 >= topk, 0, INT_MIN).astype(I32), c0)

    def bit_step(b, st):
        t, cnt = st
        cand = t + jnp.left_shift(jnp.int32(1), 30 - b)
        c = count_ge(cand)
        ok = c >= topk
        return jnp.where(ok, cand, t), jnp.where(ok, c, cnt)

    def settled(st):
        t, cnt = st
        return jnp.min(jnp.logical_or(cnt == topk, t == INT_MIN).astype(I32)) > 0

    def finish(st):
        t, cnt = st
        need = topk - count_ge(t + 1)
        return t, need, jnp.logical_and(cnt > topk, t > INT_MIN).astype(I32)

    def search(st, stops):
        if not stops:
            return finish(st)
        hi_bit = 30 if len(stops) == len(SEARCH_STOPS) else SEARCH_STOPS[len(SEARCH_STOPS) - len(stops) - 1] - 1
        st = lax.fori_loop(30 - hi_bit, 31 - stops[0], bit_step, st)
        if stops[0] == 0:
            return finish(st)
        return lax.cond(settled(st), lambda s_: (s_[0], jnp.ones((1, qb), I32), jnp.zeros((1, qb), I32)),
                        lambda s_: search(s_, stops[1:]), st)

    thr, need, tied = search(state, SEARCH_STOPS)

    def count(pred):
        def body(c, acc):
            idx = c * ch + lax.broadcasted_iota(I32, (ch, qb), 0)
            hit = pred(key_ref[rows(c), :], idx).astype(I32)
            return acc + jnp.sum(hit.reshape(ch // 8, 8, qb), axis=0)
        acc = lax.fori_loop(0, n_chunks, body, jnp.zeros((8, qb), I32))
        return jnp.sum(acc, axis=0, keepdims=True)

    def tie_search():
        def step(b, lo):
            cand = lo + jnp.left_shift(jnp.int32(1), seq_bits - 1 - b)
            below = count(lambda keys, idx: jnp.logical_and(keys == thr, idx < cand))
            return jnp.where(below < need, cand, lo)
        return lax.fori_loop(0, seq_bits, step, jnp.zeros((1, qb), I32))

    last_eq = lax.cond(jnp.max(tied) > 0, tie_search,
                       lambda: jnp.full((1, qb), 2 ** seq_bits, I32))
    thr_sel = jnp.maximum(thr, INT_MIN + 1)

    nq = DSA_GROUP * qb
    one_hot = (lax.broadcasted_iota(I32, (nq, qb), 0) % qb == lax.broadcasted_iota(I32, (nq, qb), 1))
    one_hot = jnp.where(one_hot, 1.0, 0.0).astype(BF16)
    qs = [jnp.concatenate([jnp.concatenate([q_ref[:, (n * DSA_GROUP + g) * LANES:(n * DSA_GROUP + g + 1) * LANES]
                                            for g in range(DSA_GROUP)], axis=0), one_hot], axis=1)
          for n in range(DSA_KV_HEADS)]

    slots = (s0_ref, s1_ref)

    def produce(c, slot):
        keys = key_ref[rows(c), :]
        idx = c * ch + lax.broadcasted_iota(I32, (ch, qb), 0)
        drop = jnp.logical_and(keys == thr, idx > last_eq)
        sel = jnp.logical_and(keys >= thr_sel, jnp.logical_not(drop))
        neg = jnp.where(sel, 0.0, NEG_BIG).astype(BF16)
        col_max = []
        for n in range(DSA_KV_HEADS):
            kc = jnp.concatenate([k_ref[rows(c), n * LANES:(n + 1) * LANES], neg], axis=1)
            s = lax.dot_general(kc, qs[n], _NT, preferred_element_type=F32)
            slots[slot][n] = s
            col_max.append(jnp.max(s, axis=0, keepdims=True))
        return tuple(col_max)

    def consume(c, slot, col_max, carry):
        return tuple(_softmax_step(slots[slot][n], col_max[n],
                                   vt_ref[c, n * VT_ROWS:(n + 1) * VT_ROWS, :], *carry[n])
                     for n in range(DSA_KV_HEADS))

    def phase3(j, state):
        carry, col0 = state
        col1 = produce(2 * j + 1, 1)
        carry = consume(2 * j, 0, col0, carry)
        col0 = produce(2 * j + 2, 0)
        carry = consume(2 * j + 1, 1, col1, carry)
        return carry, col0

    init = tuple((jnp.full((1, nq), NEG_BIG, F32), jnp.zeros((VT_ROWS, nq), F32))
                 for _ in range(DSA_KV_HEADS))
    pairs = (n_chunks - 1) // 2
    state = lax.fori_loop(0, pairs // 2, lambda jj, st: phase3(2 * jj + 1, phase3(2 * jj, st)),
                          (init, produce(0, 0)))
    carry, col0 = lax.fori_loop(0, pairs & 1, lambda _, st: phase3(pairs - 1, st), state)
    last = 2 * pairs

    def tail_two(cr):
        col1 = produce(last + 1, 1)
        cr = consume(last, 0, col0, cr)
        return consume(last + 1, 1, col1, cr)

    res = lax.cond(n_chunks - 1 > last, tail_two, lambda cr: consume(last, 0, col0, cr), carry)
    for n in range(DSA_KV_HEADS):
        o = _normalised_t(res[n][1], DSA_HEAD_DIM)
        for g in range(DSA_GROUP):
            col = (n * DSA_GROUP + g) * LANES
            o_ref[:, col:col + LANES] = (o[g * qb:(g + 1) * qb, :]
                                         * _silu(gate_ref[:, col:col + LANES])).astype(o_ref.dtype)


def _dsa(qi2, ki4, wt, q, k, vt, h2d, batch, seq):
    m = batch * seq
    nq = seq // DSA_QBLK
    nck = seq // KEY_CHUNK
    kvw = DSA_KV_HEADS * DSA_HEAD_DIM
    topk = min(TOPK_MAX, seq // 4)
    seq_bits = int(np.log2(seq))
    assert 2 ** seq_bits == seq and nck % 2 == 0 and topk <= KEY_CHUNK
    return pl.pallas_call(
        functools.partial(_dsa_kernel, topk=topk, seq_bits=seq_bits, grp=2),
        grid=(batch, nq),
        in_specs=[
            pl.BlockSpec((None, None, IDX_HEADS // 2, 2 * DSA_QBLK, 2 * IDX_DIM),
                         lambda b, i: (b, i, 0, 0, 0)),
            pl.BlockSpec((seq, 4 * IDX_DIM), lambda b, i: (b, 0)),
            pl.BlockSpec((None, IDX_HEADS, DSA_QBLK), lambda b, i: (b, 0, i)),
            pl.BlockSpec((DSA_QBLK, GROUP_WIDTH), lambda b, i: (b * nq + i, 0)),
            pl.BlockSpec((seq, kvw), lambda b, i: (b, 0)),
            pl.BlockSpec((None, nck, DSA_KV_HEADS * VT_ROWS, KEY_CHUNK), lambda b, i: (b, 0, 0, 0)),
            pl.BlockSpec((DSA_QBLK, GROUP_WIDTH), lambda b, i: (b * nq + i, COL_BG // GROUP_WIDTH)),
        ],
        out_specs=pl.BlockSpec((DSA_QBLK, GROUP_WIDTH), lambda b, i: (b * nq + i, 0)),
        out_shape=jax.ShapeDtypeStruct((m, GROUP_WIDTH), BF16),
        scratch_shapes=[pltpu.VMEM((seq, DSA_QBLK), I32),
                        pltpu.VMEM((DSA_KV_HEADS, KEY_CHUNK, DSA_GROUP * DSA_QBLK), F32),
                        pltpu.VMEM((DSA_KV_HEADS, KEY_CHUNK, DSA_GROUP * DSA_QBLK), F32),
                        pltpu.VMEM((IDX_HEADS // 2, 2 * DSA_QBLK, 4 * IDX_DIM), BF16)],
        compiler_params=_params(32 << 20, 2),
        name="dsa",
    )(qi2, ki4, wt, q, k, vt, h2d)


def _rglru_kernel(u_ref, gate_ref, cw_ref, cb_ref, wa_ref, ba_ref, wx_ref, bx_ref, lam_ref,
                  o_ref, halo_ref, h_ref, a_scr, x_scr, y_scr, *, batch, tt):
    t = pl.program_id(0)

    @pl.when(t == 0)
    def _():
        halo_ref[...] = jnp.zeros_like(halo_ref)
        h_ref[...] = jnp.zeros_like(h_ref)

    lam = lam_ref[...]
    z = -lam
    softplus = jnp.maximum(z, 0.0) + jnp.log1p(jnp.exp(-jnp.abs(z)))
    row = lax.broadcasted_iota(I32, (tt, LRU_WIDTH), 0)
    for b in range(batch):
        u = u_ref[b]
        prev = halo_ref[b]
        p1, p2, p3 = prev[7:8, :], prev[6:7, :], prev[5:6, :]
        s1 = jnp.where(row == 0, p1, pltpu.roll(u, 1, 0))
        s2 = jnp.where(row == 0, p2, jnp.where(row == 1, p1, pltpu.roll(u, 2, 0)))
        s3 = jnp.where(row == 0, p3, jnp.where(row == 1, p2, jnp.where(row == 2, p1, pltpu.roll(u, 3, 0))))
        conv = (cw_ref[3:4, :] * u + cw_ref[2:3, :] * s1 + cw_ref[1:2, :] * s2 + cw_ref[0:1, :] * s3
                + cb_ref[...])
        halo_ref[b] = u[tt - 8:, :]
        cb16 = conv.astype(BF16)
        r = _sigmoid(jnp.dot(cb16, wa_ref[...], preferred_element_type=F32) + ba_ref[...])
        ig = _sigmoid(jnp.dot(cb16, wx_ref[...], preferred_element_type=F32) + bx_ref[...])
        log_a = -LRU_C * r * softplus
        a_scr[b] = jnp.exp(log_a)
        th = jnp.tanh(log_a)
        x_scr[b] = jnp.sqrt(-2.0 * th / (1.0 - th)) * (ig * conv)

    def step(s, h):
        h = a_scr[:, pl.ds(s, 1), :] * h + x_scr[:, pl.ds(s, 1), :]
        y_scr[:, pl.ds(s, 1), :] = h
        return h

    h_ref[...] = lax.fori_loop(0, tt, step, h_ref[...], unroll=8)
    o_ref[...] = (y_scr[...] * _silu(gate_ref[...])).astype(o_ref.dtype)


def _rglru(h3d, conv_w, conv_b, wa_bd, b_a, wx_bd, b_x, lam, batch, seq, tt):
    nt = seq // tt
    w = LRU_WIDTH

    def vec(r):
        return pl.BlockSpec((r, w), lambda t: (0, 0))

    return pl.pallas_call(
        functools.partial(_rglru_kernel, batch=batch, tt=tt),
        grid=(nt,),
        in_specs=[
            pl.BlockSpec((batch, tt, w), lambda t: (0, t, COL_U // w)),
            pl.BlockSpec((batch, tt, w), lambda t: (0, t, COL_CG // w)),
            vec(CONV_WIDTH), vec(1),
            pl.BlockSpec((w, w), lambda t: (0, 0)), vec(1),
            pl.BlockSpec((w, w), lambda t: (0, 0)), vec(1),
            vec(1),
        ],
        out_specs=pl.BlockSpec((batch, tt, w), lambda t: (0, t, 0)),
        out_shape=jax.ShapeDtypeStruct((batch, seq, w), BF16),
        scratch_shapes=[pltpu.VMEM((batch, 8, w), F32), pltpu.VMEM((batch, 1, w), F32),
                        pltpu.VMEM((batch, tt, w), F32), pltpu.VMEM((batch, tt, w), F32),
                        pltpu.VMEM((batch, tt, w), F32)],
        compiler_params=_params(32 << 20, 1),
        name="rglru",
    )(h3d, h3d, conv_w, conv_b, wa_bd, b_a, wx_bd, b_x, lam)


def _mem_attn_kernel(q_ref, gate_ref, km_ref, vm_ref, o_ref):
    scale = MEM_HEAD_DIM ** -0.5
    for h in range(MEM_HEADS):
        cols = slice(h * MEM_HEAD_DIM, (h + 1) * MEM_HEAD_DIM)
        qh = (q_ref[:, cols] * scale).astype(BF16)
        s = lax.dot_general(qh, km_ref[:, cols], _NT, preferred_element_type=F32)
        p = jnp.exp(s - jnp.max(s, axis=-1, keepdims=True))
        l = jnp.sum(p, axis=-1, keepdims=True)
        o = jnp.dot(p.astype(BF16), vm_ref[:, cols], preferred_element_type=F32) / l
        o_ref[:, cols] = (o * _silu(gate_ref[:, cols])).astype(o_ref.dtype)


def _mem_attn(h2d, km, vm, batch, seq, tm):
    m = batch * seq
    nt = seq // tm
    w = GROUP_WIDTH
    return pl.pallas_call(
        _mem_attn_kernel,
        grid=(batch, nt),
        in_specs=[
            pl.BlockSpec((tm, w), lambda b, i: (b * nt + i, COL_DQ // w)),
            pl.BlockSpec((tm, w), lambda b, i: (b * nt + i, COL_DG // w)),
            pl.BlockSpec((N_MEM, w), lambda b, i: (b, 0)),
            pl.BlockSpec((N_MEM, w), lambda b, i: (b, 0)),
        ],
        out_specs=pl.BlockSpec((tm, w), lambda b, i: (b * nt + i, 0)),
        out_shape=jax.ShapeDtypeStruct((m, w), BF16),
        compiler_params=_params(24 << 20, 2),
        name="mem_attn",
    )(h2d, h2d, km, vm)


def _out_ln_kernel(*refs, n_lhs, kg, nj, per_emit, d_model):
    lhs = refs[:n_lhs]
    w_ref, x_ref, g_ref, b_ref, o_ref, o16_ref, z_ref, mu_ref, rs_ref = refs[n_lhs:]
    j = pl.program_id(1)

    @pl.when(j < nj)
    def _():
        acc = None
        for g in range(n_lhs):
            part = jnp.dot(lhs[g][...], w_ref[g * kg:(g + 1) * kg, :].astype(BF16), preferred_element_type=F32)
            acc = part if acc is None else acc + part
        z_ref[j] = DEEPNORM_ALPHA * x_ref[...] + acc

    @pl.when(j == nj)
    def _():
        def lane_tiles(v):
            tiles = [v[:, t * LANES:(t + 1) * LANES] for t in range(v.shape[1] // LANES)]
            return functools.reduce(lambda a, b: a + b, tiles)

        tot = functools.reduce(lambda a, b: a + b, [lane_tiles(z_ref[jj]) for jj in range(nj)])
        mu = jnp.sum(tot, axis=-1, keepdims=True) * (1.0 / d_model)
        sq = None
        for jj in range(nj):
            dlt = z_ref[jj] - mu
            part = lane_tiles(dlt * dlt)
            sq = part if sq is None else sq + part
        mu_ref[...] = mu
        rs_ref[...] = lax.rsqrt(jnp.sum(sq, axis=-1, keepdims=True) * (1.0 / d_model) + LN_EPS)

    @pl.when(j >= nj)
    def _():
        e = j - nj
        tn = z_ref.shape[2]
        for t in range(per_emit):
            cols = slice(t * tn, (t + 1) * tn)
            out = (z_ref[e * per_emit + t] - mu_ref[...]) * rs_ref[...] * g_ref[:, cols] + b_ref[:, cols]
            o_ref[:, cols] = out
            o16_ref[:, cols] = out.astype(BF16)


def _out_proj_deepnorm(lhs_list, w, layer, x, ln_g, ln_b, tm, tn, te):
    m, kg = lhs_list[0].shape
    _, k, n = w.shape
    n_lhs = len(lhs_list)
    nj, ne, per_emit = n // tn, n // te, te // tn
    assert k == kg * n_lhs and m % tm == 0 and n % te == 0 and te % tn == 0

    def col(j):
        return jnp.minimum(j, nj - 1)

    def emit(j):
        return jnp.maximum(j - nj, 0)

    est = (nj * tm * tn * 4 + 2 * (tm * k * 2 + k * tn * 4 + tm * tn * 4 + tm * te * 6) + tm * tn * 8)
    return pl.pallas_call(
        functools.partial(_out_ln_kernel, n_lhs=n_lhs, kg=kg, nj=nj, per_emit=per_emit, d_model=n),
        grid=(m // tm, nj + ne),
        in_specs=[pl.BlockSpec((tm, kg), lambda i, j: (i, 0)) for _ in range(n_lhs)]
        + [pl.BlockSpec((None, k, tn), lambda i, j: (layer, 0, col(j))),
           pl.BlockSpec((tm, tn), lambda i, j: (i, col(j))),
           pl.BlockSpec((1, te), lambda i, j: (0, emit(j))),
           pl.BlockSpec((1, te), lambda i, j: (0, emit(j)))],
        out_specs=[pl.BlockSpec((tm, te), lambda i, j: (i, emit(j))),
                   pl.BlockSpec((tm, te), lambda i, j: (i, emit(j)))],
        out_shape=[jax.ShapeDtypeStruct((m, n), F32), jax.ShapeDtypeStruct((m, n), BF16)],
        scratch_shapes=[pltpu.VMEM((nj, tm, tn), F32), pltpu.VMEM((tm, 1), F32), pltpu.VMEM((tm, 1), F32)],
        compiler_params=_params(est + (10 << 20), 2),
        name="out_proj_deepnorm",
    )(*lhs_list, w, x, ln_g, ln_b)


def _rope_tabs(positions):
    pos = positions.astype(F32)
    b, s = pos.shape

    def one(rot, period):
        inv = ROPE_THETA ** (-jnp.arange(0, rot, 2, dtype=F32) / rot)
        ang = pos[:, :, None] * inv
        c, sn = jnp.cos(ang), jnp.sin(ang)
        rest = period - rot
        cp = jnp.concatenate([c, c, jnp.ones((b, s, rest), F32)], axis=-1)
        sp = jnp.concatenate([-sn, sn, jnp.zeros((b, s, rest), F32)], axis=-1)
        reps = LANES // period
        return jnp.tile(cp, (1, 1, reps)), jnp.tile(sp, (1, 1, reps))

    parts = one(MLA_ROPE, LANES) + one(DSA_ROT, LANES) + one(IDX_ROT, IDX_DIM)
    return jnp.concatenate(parts, axis=-1).reshape(b * s, N_TABS * LANES)


def _block_diag(w):
    l = w.shape[0]
    eye = jnp.eye(LRU_BLOCKS, dtype=w.dtype)
    bd = jnp.einsum('lnde,nm->lndme', w, eye)
    return bd.reshape(l, LRU_WIDTH, LRU_WIDTH).astype(BF16)


def _layer(x, x16, mem16, tabs, lw, l, batch, seq):
    m = batch * seq
    main_rows, small_rows = _in_proj_row_tables(256)
    h2d = _in_proj(x16, lw["w_in_t"], l, main_rows, tm=min(2048, m), tn=256, name="in_proj")
    hs = _in_proj(x16, lw["w_in_t"], l, small_rows, tm=min(2048, m), tn=LANES, name="in_proj_small")
    tile_a = min(512, seq)
    qa, ka, vta = _prep_mla(h2d, hs, tabs, lw["g_cq"], lw["g_ckv"], lw["w_uq"], lw["w_uk"], lw["w_uvt"],
                            batch, seq, tile_a)
    ya = _mla_attn(qa, ka, vta, h2d, batch, seq, tile_a)
    qb, kb, vtb, qi2, ki4, wt = _prep_dsa(h2d, hs, tabs, batch, seq)
    yb = _dsa(qi2, ki4, wt, qb, kb, vtb, h2d, batch, seq)
    yc = _rglru(h2d.reshape(batch, seq, N_MAIN), lw["conv_w"], lw["conv_b"], lw["w_rg_a"], lw["b_rg_a"],
                lw["w_rg_x"], lw["b_rg_x"], lw["lam"], batch, seq, min(256, seq))
    yc = yc.reshape(m, GROUP_WIDTH)
    km = _matmul([mem16], lw["w_mem_k"], l, BF16, tm=mem16.shape[0], tn=256, name="mem_proj_k")
    vm = _matmul([mem16], lw["w_mem_v"], l, BF16, tm=mem16.shape[0], tn=256, name="mem_proj_v")
    yd = _mem_attn(h2d, km, vm, batch, seq, min(512, seq))
    return _out_proj_deepnorm([ya, yb, yc, yd], lw["w_o"], l, x, lw["ln_g"], lw["ln_b"],
                              tm=min(1024, m), tn=256, te=512)


def kernel(x, mem, positions, w_in, g_cq, g_ckv, w_uq, w_ukv, conv_w, conv_b, w_rg_a, b_rg_a, w_rg_x,
           b_rg_x, lru_lambda, w_mem_k, w_mem_v, w_o, ln_g, ln_b):
    batch, seq, d = x.shape
    depth = w_in.shape[0]
    tabs = _rope_tabs(positions)
    w_in_t = w_in.transpose(0, 2, 1)
    w_uq_p = jnp.pad(w_uq.reshape(depth, MLA_Q_LORA, MLA_HEADS, MLA_NOPE + MLA_ROPE),
                     [(0, 0), (0, 0), (0, 0), (0, 2 * LANES - MLA_NOPE - MLA_ROPE)])
    w_uq_p = w_uq_p.reshape(depth, MLA_Q_LORA, MLA_HEADS * 2 * LANES).astype(BF16)
    w_ukv4 = w_ukv.reshape(depth, MLA_KV_LORA, MLA_HEADS, MLA_NOPE + MLA_V)
    w_uk = w_ukv4[..., :MLA_NOPE].reshape(depth, MLA_KV_LORA, MLA_HEADS * MLA_NOPE).astype(BF16)
    w_uvt = w_ukv4[..., MLA_NOPE:].reshape(depth, MLA_KV_LORA, MLA_HEADS * MLA_V)
    w_uvt = w_uvt.transpose(0, 2, 1).astype(BF16)
    wa_bd = _block_diag(w_rg_a)
    wx_bd = _block_diag(w_rg_x)
    mem16 = mem.reshape(batch * mem.shape[1], d).astype(BF16)

    xf = x.reshape(batch * seq, d)
    x16 = xf.astype(BF16)
    for l in range(depth):
        lw = dict(w_in_t=w_in_t, g_cq=g_cq[l][None], g_ckv=g_ckv[l][None], w_uq=w_uq_p[l], w_uk=w_uk[l],
                  w_uvt=w_uvt[l], conv_w=conv_w[l], conv_b=conv_b[l][None], w_rg_a=wa_bd[l],
                  b_rg_a=b_rg_a[l][None], w_rg_x=wx_bd[l], b_rg_x=b_rg_x[l][None], lam=lru_lambda[l][None],
                  w_mem_k=w_mem_k, w_mem_v=w_mem_v, w_o=w_o, ln_g=ln_g[l][None], ln_b=ln_b[l][None])
        xf, x16 = _layer(xf, x16, mem16, tabs, lw, l, batch, seq)
    return xf.reshape(batch, seq, d)
```

```python
import functools

import numpy as np
import jax
import jax.numpy as jnp
from jax import lax
from jax.experimental import pallas as pl
from jax.experimental.pallas import tpu as pltpu

F32 = jnp.float32
BF16 = jnp.bfloat16
I32 = jnp.int32

DEPTH = 4
D_MODEL = 4096
N_MEM = 256
GROUP_WIDTH = D_MODEL // 4
ROPE_THETA = 500000.0
MLA_HEADS = 8
MLA_NOPE = 128
MLA_ROPE = 64
MLA_V = 128
MLA_Q_LORA = GROUP_WIDTH
MLA_KV_LORA = GROUP_WIDTH // 2
DSA_HEADS = 8
DSA_KV_HEADS = 2
DSA_GROUP = DSA_HEADS // DSA_KV_HEADS
DSA_HEAD_DIM = 128
DSA_ROT = DSA_HEAD_DIM // 4
IDX_HEADS = 16
IDX_DIM = 64
IDX_ROT = IDX_DIM // 4
TOPK_MAX = 256
LRU_WIDTH = GROUP_WIDTH
LRU_BLOCKS = 16
LRU_BLOCK_DIM = LRU_WIDTH // LRU_BLOCKS
CONV_WIDTH = 4
LRU_C = 8.0
MEM_HEADS = 4
MEM_HEAD_DIM = GROUP_WIDTH // MEM_HEADS
DEEPNORM_ALPHA = (2 * DEPTH) ** 0.25
LN_EPS = 1e-5
RMS_EPS = 1e-6

IN_SIZES = (
    MLA_Q_LORA, MLA_KV_LORA, MLA_ROPE, GROUP_WIDTH,
    GROUP_WIDTH, DSA_KV_HEADS * DSA_HEAD_DIM, DSA_KV_HEADS * DSA_HEAD_DIM,
    IDX_HEADS * IDX_DIM, IDX_DIM, IDX_HEADS, GROUP_WIDTH,
    LRU_WIDTH, GROUP_WIDTH,
    GROUP_WIDTH, GROUP_WIDTH,
)

LANES = 128
V7X_VMEM_BUDGET = 60 * 1024 * 1024

COL_CQ = 0
COL_AG = 1024
COL_BQ = 2048
COL_QI = 3072
COL_BG = 4096
COL_U = 5120
COL_CG = 6144
COL_DQ = 7168
COL_DG = 8192
COL_CKV = 9216
COL_BK = 9728
COL_BV = 9984
N_MAIN = 10240
SMALL_KR = 0
SMALL_KI = 128

TAB_MLA_C, TAB_MLA_S, TAB_DSA_C, TAB_DSA_S, TAB_IDX_C, TAB_IDX_S = range(6)
N_TABS = 6

NEG_BIG = -1e30
INT_MIN = -2 ** 31
LOG2E = 1.4426950408889634
KEY_CHUNK = 256
DSA_QBLK = 128
SEARCH_STOPS = (11, 8, 6, 3, 0)
VT_ROWS = 128 + 16
_NT = (((1,), (1,)), ((), ()))


def _vmem_limit(nbytes):
    return int(min(V7X_VMEM_BUDGET, max(nbytes, 16 * 1024 * 1024)))


def _params(nbytes, ndims):
    return pltpu.CompilerParams(dimension_semantics=("arbitrary",) * ndims,
                                vmem_limit_bytes=_vmem_limit(nbytes))


def _mm_kernel(*refs, n_lhs, kg):
    w_ref = refs[n_lhs]
    o_ref = refs[n_lhs + 1]
    acc = None
    for g in range(n_lhs):
        part = jnp.dot(refs[g][...], w_ref[g * kg:(g + 1) * kg, :].astype(BF16), preferred_element_type=F32)
        acc = part if acc is None else acc + part
    o_ref[...] = acc.astype(o_ref.dtype)


def _matmul(lhs_list, w, layer, out_dtype, tm, tn, name):
    m, kg = lhs_list[0].shape
    _, k, n = w.shape
    n_lhs = len(lhs_list)
    assert k == kg * n_lhs and m % tm == 0 and n % tn == 0
    est = 2 * (tm * k * 2 + k * tn * w.dtype.itemsize + tm * tn * jnp.dtype(out_dtype).itemsize) + tm * tn * 8
    return pl.pallas_call(
        functools.partial(_mm_kernel, n_lhs=n_lhs, kg=kg),
        grid=(m // tm, n // tn),
        in_specs=[pl.BlockSpec((tm, kg), lambda i, j: (i, 0)) for _ in range(n_lhs)]
        + [pl.BlockSpec((None, k, tn), lambda i, j: (layer, 0, j))],
        out_specs=pl.BlockSpec((tm, tn), lambda i, j: (i, j)),
        out_shape=jax.ShapeDtypeStruct((m, n), out_dtype),
        compiler_params=_params(est + (8 << 20), 2),
        name=name,
    )(*lhs_list, w)


def _in_proj_kernel(tab_ref, x_ref, w_ref, o_ref):
    del tab_ref
    o_ref[...] = lax.dot_general(x_ref[...], w_ref[...].astype(BF16), _NT, preferred_element_type=F32)


def _in_proj(x16, w_t, layer, row_offsets, tm, tn, name):
    m, k = x16.shape
    nblk = len(row_offsets)
    est = 2 * (tm * k * 2 + tn * k * 4 + tm * tn * 4) + tm * tn * 8
    grid_spec = pltpu.PrefetchScalarGridSpec(
        num_scalar_prefetch=1,
        grid=(m // tm, nblk),
        in_specs=[
            pl.BlockSpec((tm, k), lambda i, j, tab: (i, 0)),
            pl.BlockSpec((pl.Squeezed(), pl.Element(tn), pl.Element(k)),
                         lambda i, j, tab: (layer, pl.multiple_of(tab[j], 16), 0)),
        ],
        out_specs=pl.BlockSpec((tm, tn), lambda i, j, tab: (i, j)),
    )
    return pl.pallas_call(
        _in_proj_kernel,
        grid_spec=grid_spec,
        out_shape=jax.ShapeDtypeStruct((m, nblk * tn), F32),
        compiler_params=_params(est + (8 << 20), 2),
        name=name,
    )(jnp.asarray(row_offsets, I32), x16, w_t)


def _in_proj_small_kernel(x_ref, wa_ref, wb_ref, o_ref):
    w = jnp.concatenate([wa_ref[...], wb_ref[...]], axis=0).astype(BF16)
    o_ref[...] = lax.dot_general(x_ref[...], w, _NT, preferred_element_type=F32)


def _in_proj_small(x16, w_t, layer, rows_a, rows_b, tm):
    m, k = x16.shape

    def window(r0):
        return pl.BlockSpec((pl.Squeezed(), pl.Element(LANES), pl.Element(k)), lambda i: (layer, r0, 0))

    est = 2 * (tm * k * 2 + 2 * LANES * k * 4 + tm * 2 * LANES * 4) + 2 * LANES * k * 6
    return pl.pallas_call(
        _in_proj_small_kernel,
        grid=(m // tm,),
        in_specs=[pl.BlockSpec((tm, k), lambda i: (i, 0)), window(rows_a), window(rows_b)],
        out_specs=pl.BlockSpec((tm, 2 * LANES), lambda i: (i, 0)),
        out_shape=jax.ShapeDtypeStruct((m, 2 * LANES), F32),
        compiler_params=_params(est + (8 << 20), 1),
        name="in_proj_small",
    )(x16, w_t, w_t)


def _in_proj_row_tables(tn):
    names = ("cq", "ckv", "kr", "ag", "bq", "bk", "bv", "qi", "ki", "wi", "bg", "u", "cg", "dq", "dg")
    src = dict(zip(names, [0] + [int(c) for c in np.cumsum(IN_SIZES)[:-1]]))
    size = dict(zip(names, IN_SIZES))
    order = [("cq", COL_CQ), ("ag", COL_AG), ("bq", COL_BQ), ("qi", COL_QI), ("bg", COL_BG), ("u", COL_U),
             ("cg", COL_CG), ("dq", COL_DQ), ("dg", COL_DG), ("ckv", COL_CKV), ("bk", COL_BK), ("bv", COL_BV)]
    main = []
    for n, dst in order:
        assert dst == len(main) * tn and size[n] % tn == 0 and src[n] % 16 == 0
        main += [src[n] + t * tn for t in range(size[n] // tn)]
    assert len(main) * tn == N_MAIN and src["wi"] == src["ki"] + size["ki"]
    return np.asarray(main, np.int32), np.asarray([src["kr"], src["ki"]], np.int32)


def _rope(v, c_tab, s_tab, half, period):
    width = v.shape[-1]
    lane = lax.broadcasted_iota(I32, v.shape, 1) & (period - 1)
    swapped = jnp.where(lane < half, pltpu.roll(v, width - half, 1), pltpu.roll(v, half, 1))
    return v * c_tab + swapped * s_tab


def _tile_lanes(t, reps):
    return t if reps == 1 else jnp.concatenate([t] * reps, axis=1)


def _rms(x, g):
    return x * lax.rsqrt(jnp.mean(x * x, axis=-1, keepdims=True) + RMS_EPS) * g


def _sigmoid(v):
    return 0.5 * jnp.tanh(0.5 * v) + 0.5


def _silu(g):
    return g * _sigmoid(g)


def _prep_mla_kernel(cq_ref, ckv_ref, kr_ref, tab_ref, gcq_ref, gckv_ref, wuq_ref, wk_ref, wvt_ref,
                     q_out, k_out, vt_out):
    scale = (MLA_NOPE + MLA_ROPE) ** -0.5 * LOG2E
    nq = _rms(cq_ref[...], gcq_ref[...]).astype(BF16)
    nkv = _rms(ckv_ref[...], gckv_ref[...]).astype(BF16)
    c_tab = tab_ref[:, TAB_MLA_C * LANES:(TAB_MLA_C + 1) * LANES]
    s_tab = tab_ref[:, TAB_MLA_S * LANES:(TAB_MLA_S + 1) * LANES]
    qf = jnp.dot(nq, wuq_ref[...], preferred_element_type=F32) * scale
    kn = jnp.dot(nkv, wk_ref[...], preferred_element_type=F32)
    lane = lax.broadcasted_iota(I32, kr_ref.shape, 1)
    kr = jnp.where(lane < MLA_ROPE, kr_ref[...], 0.0)
    kr = _rope(kr, c_tab, s_tab, MLA_ROPE // 2, LANES).astype(BF16)
    for h in range(MLA_HEADS):
        base = h * 2 * LANES
        q_out[:, base:base + LANES] = qf[:, base:base + LANES].astype(BF16)
        q_out[:, base + LANES:base + 2 * LANES] = _rope(
            qf[:, base + LANES:base + 2 * LANES], c_tab, s_tab, MLA_ROPE // 2, LANES).astype(BF16)
        k_out[:, base:base + LANES] = kn[:, h * LANES:(h + 1) * LANES].astype(BF16)
        k_out[:, base + LANES:base + 2 * LANES] = kr
    vt = lax.dot_general(wvt_ref[...], nkv, _NT, preferred_element_type=F32)
    half = vt.shape[1] // 2
    vt_out[0] = _vt_with_ones(vt[:, :half], MLA_HEADS, MLA_V)
    vt_out[1] = _vt_with_ones(vt[:, half:], MLA_HEADS, MLA_V)


def _prep_mla(h2d, hs, tabs, g_cq, g_ckv, wuq_p, wk, wvt, batch, seq, tm):
    m = batch * seq
    nt = seq // tm
    hd = MLA_HEADS * 2 * LANES
    return pl.pallas_call(
        _prep_mla_kernel,
        grid=(batch, nt),
        in_specs=[
            pl.BlockSpec((tm, MLA_Q_LORA), lambda b, i: (b * nt + i, COL_CQ // MLA_Q_LORA)),
            pl.BlockSpec((tm, MLA_KV_LORA), lambda b, i: (b * nt + i, COL_CKV // MLA_KV_LORA)),
            pl.BlockSpec((tm, LANES), lambda b, i: (b * nt + i, SMALL_KR // LANES)),
            pl.BlockSpec((tm, N_TABS * LANES), lambda b, i: (b * nt + i, 0)),
            pl.BlockSpec((1, MLA_Q_LORA), lambda b, i: (0, 0)),
            pl.BlockSpec((1, MLA_KV_LORA), lambda b, i: (0, 0)),
            pl.BlockSpec((MLA_Q_LORA, hd), lambda b, i: (0, 0)),
            pl.BlockSpec((MLA_KV_LORA, MLA_HEADS * MLA_NOPE), lambda b, i: (0, 0)),
            pl.BlockSpec((MLA_HEADS * MLA_V, MLA_KV_LORA), lambda b, i: (0, 0)),
        ],
        out_specs=[
            pl.BlockSpec((tm, hd), lambda b, i: (b * nt + i, 0)),
            pl.BlockSpec((tm, hd), lambda b, i: (b * nt + i, 0)),
            pl.BlockSpec((None, 2, MLA_HEADS * VT_ROWS, tm // 2), lambda b, i: (b, i, 0, 0)),
        ],
        out_shape=[
            jax.ShapeDtypeStruct((m, hd), BF16),
            jax.ShapeDtypeStruct((m, hd), BF16),
            jax.ShapeDtypeStruct((batch, 2 * nt, MLA_HEADS * VT_ROWS, tm // 2), BF16),
        ],
        compiler_params=_params(40 << 20, 2),
        name="prep_mla",
    )(h2d, h2d, hs, tabs, g_cq, g_ckv, wuq_p, wk, wvt)


def _softmax_step(s, col_max, vt_chunk, m, acc):
    m_new = jnp.maximum(m, col_max)
    alpha = jnp.exp2(m - m_new)
    p = jnp.exp2(s - m_new)
    acc_new = alpha * acc + jnp.dot(vt_chunk, p.astype(BF16), preferred_element_type=F32)
    return m_new, acc_new


def _normalised_t(acc, dv):
    return (acc[:dv, :] / acc[dv:dv + 1, :]).T


def _vt_with_ones(vt, heads, dv):
    ones = jnp.ones((VT_ROWS - dv, vt.shape[1]), BF16)
    parts = []
    for h in range(heads):
        parts += [vt[h * dv:(h + 1) * dv, :].astype(BF16), ones]
    return jnp.concatenate(parts, axis=0)


def _mla_attn_kernel(q_ref, k_ref, vt_ref, gate_ref, o_ref, s0_ref, s1_ref, *, tile, ch, hp):
    i = pl.program_id(2)
    dk = 2 * LANES
    qs = [q_ref[:, n * dk:(n + 1) * dk] for n in range(hp)]
    slots = (s0_ref, s1_ref)

    def produce(c, slot, diag_offset=None):
        col_max = []
        for n in range(hp):
            kc = k_ref[pl.ds(pl.multiple_of(c * ch, ch), ch), n * dk:(n + 1) * dk]
            s = lax.dot_general(kc, qs[n], _NT, preferred_element_type=F32)
            if diag_offset is not None:
                kidx = lax.broadcasted_iota(I32, (ch, tile), 0) + diag_offset
                qidx = lax.broadcasted_iota(I32, (ch, tile), 1)
                s = jnp.where(kidx <= qidx, s, NEG_BIG)
            slots[slot][n] = s
            col_max.append(jnp.max(s, axis=0, keepdims=True))
        return tuple(col_max)

    def consume(c, slot, col_max, carry):
        return tuple(_softmax_step(slots[slot][n], col_max[n], vt_ref[c, n * VT_ROWS:(n + 1) * VT_ROWS, :],
                                   *carry[n]) for n in range(hp))

    init = tuple((jnp.full((1, tile), NEG_BIG, F32), jnp.zeros((VT_ROWS, tile), F32)) for _ in range(hp))
    diag = 2 * i
    col_a = produce(diag, 0, 0)
    col_b = produce(diag + 1, 1, ch)
    carry = consume(diag, 0, col_a, init)

    def pair(j, state):
        cr, col1, c1 = state
        col0 = produce(2 * j, 0)
        cr = consume(c1, 1, col1, cr)
        col1 = produce(2 * j + 1, 1)
        cr = consume(2 * j, 0, col0, cr)
        return cr, col1, 2 * j + 1

    state = lax.fori_loop(0, i // 2, lambda jj, st: pair(2 * jj + 1, pair(2 * jj, st)),
                          (carry, col_b, diag + 1))
    carry, col1, c1 = lax.fori_loop(0, i & 1, lambda _, st: pair(i - 1, st), state)
    res = consume(c1, 1, col1, carry)
    for n in range(hp):
        o = _normalised_t(res[n][1], MLA_V)
        cols = slice(n * LANES, (n + 1) * LANES)
        o_ref[:, cols] = (o * _silu(gate_ref[:, cols])).astype(o_ref.dtype)


def _mla_attn(q, k, vt, h2d, batch, seq, tile):
    m = batch * seq
    nt = seq // tile
    ch = tile // 2
    hp = 4
    dk = 2 * LANES
    return pl.pallas_call(
        functools.partial(_mla_attn_kernel, tile=tile, ch=ch, hp=hp),
        grid=(batch, MLA_HEADS // hp, nt),
        in_specs=[
            pl.BlockSpec((tile, hp * dk), lambda b, h, i: (b * nt + i, h)),
            pl.BlockSpec((seq, hp * dk), lambda b, h, i: (b, h)),
            pl.BlockSpec((None, 2 * nt, hp * VT_ROWS, ch), lambda b, h, i: (b, 0, h, 0)),
            pl.BlockSpec((tile, hp * LANES), lambda b, h, i: (b * nt + i, COL_AG // (hp * LANES) + h)),
        ],
        out_specs=pl.BlockSpec((tile, hp * LANES), lambda b, h, i: (b * nt + i, h)),
        out_shape=jax.ShapeDtypeStruct((m, GROUP_WIDTH), BF16),
        scratch_shapes=[pltpu.VMEM((hp, ch, tile), F32), pltpu.VMEM((hp, ch, tile), F32)],
        compiler_params=_params(40 << 20, 3),
        name="mla_attn",
    )(q, k, vt, h2d)


def _hi_lo(v):
    hi = v.astype(BF16).astype(F32)
    return hi, v - hi


def _prep_dsa_kernel(q_ref, k_ref, v_ref, qi_ref, ki_ref, tab_ref,
                     q_out, k_out, vt_out, qi2_out, ki4_out, wt_out):
    def tab(t):
        return tab_ref[:, t * LANES:(t + 1) * LANES]

    scale = DSA_HEAD_DIM ** -0.5 * LOG2E
    q = q_ref[...]
    q_out[...] = (_rope(q, _tile_lanes(tab(TAB_DSA_C), DSA_HEADS), _tile_lanes(tab(TAB_DSA_S), DSA_HEADS),
                        DSA_ROT // 2, LANES) * scale).astype(BF16)
    k = k_ref[...]
    k_out[...] = _rope(k, _tile_lanes(tab(TAB_DSA_C), DSA_KV_HEADS), _tile_lanes(tab(TAB_DSA_S), DSA_KV_HEADS),
                       DSA_ROT // 2, LANES).astype(BF16)
    vt_out[...] = _vt_with_ones(v_ref[...].T, DSA_KV_HEADS, DSA_HEAD_DIM)
    reps = IDX_HEADS * IDX_DIM // LANES
    qi = _rope(qi_ref[...], _tile_lanes(tab(TAB_IDX_C), reps), _tile_lanes(tab(TAB_IDX_S), reps),
               IDX_ROT // 2, IDX_DIM)
    qb = DSA_QBLK
    first_half = lax.broadcasted_iota(I32, (qb, LANES), 1) < IDX_DIM
    for blk in range(q.shape[0] // qb):
        for pair in range(IDX_HEADS // 2):
            hi, lo = _hi_lo(qi[blk * qb:(blk + 1) * qb, pair * LANES:(pair + 1) * LANES])
            even = jnp.where(first_half, hi, pltpu.roll(lo, IDX_DIM, 1)).astype(BF16)
            odd = jnp.where(first_half, pltpu.roll(hi, IDX_DIM, 1), lo).astype(BF16)
            qi2_out[blk, pair, 0:qb, :] = even
            qi2_out[blk, pair, qb:2 * qb, :] = odd
    small = ki_ref[...]
    lane = lax.broadcasted_iota(I32, small.shape, 1)
    c_ki = jnp.where(lane < IDX_DIM, tab(TAB_IDX_C), 1.0)
    s_ki = jnp.where(lane < IDX_DIM, tab(TAB_IDX_S), 0.0)
    hi, lo = _hi_lo(_rope(small, c_ki, s_ki, IDX_ROT // 2, IDX_DIM))
    ki4_out[:, 0:LANES] = jnp.where(lane < IDX_DIM, hi, pltpu.roll(hi, IDX_DIM, 1)).astype(BF16)
    ki4_out[:, LANES:2 * LANES] = jnp.where(lane < IDX_DIM, lo, pltpu.roll(lo, IDX_DIM, 1)).astype(BF16)
    wt = small.T[IDX_DIM:IDX_DIM + IDX_HEADS, :]
    wt_out[...] = wt * (IDX_HEADS ** -0.5 * IDX_DIM ** -0.5)


def _prep_dsa(h2d, hs, tabs, batch, seq):
    tm = KEY_CHUNK
    m = batch * seq
    nt = seq // tm
    qpt = tm // DSA_QBLK
    kvw = DSA_KV_HEADS * DSA_HEAD_DIM

    def row(b, i):
        return b * nt + i

    return pl.pallas_call(
        _prep_dsa_kernel,
        grid=(batch, nt),
        in_specs=[
            pl.BlockSpec((tm, GROUP_WIDTH), lambda b, i: (row(b, i), COL_BQ // GROUP_WIDTH)),
            pl.BlockSpec((tm, kvw), lambda b, i: (row(b, i), COL_BK // kvw)),
            pl.BlockSpec((tm, kvw), lambda b, i: (row(b, i), COL_BV // kvw)),
            pl.BlockSpec((tm, GROUP_WIDTH), lambda b, i: (row(b, i), COL_QI // GROUP_WIDTH)),
            pl.BlockSpec((tm, LANES), lambda b, i: (row(b, i), SMALL_KI // LANES)),
            pl.BlockSpec((tm, N_TABS * LANES), lambda b, i: (row(b, i), 0)),
        ],
        out_specs=[
            pl.BlockSpec((tm, GROUP_WIDTH), lambda b, i: (row(b, i), 0)),
            pl.BlockSpec((tm, kvw), lambda b, i: (row(b, i), 0)),
            pl.BlockSpec((None, None, DSA_KV_HEADS * VT_ROWS, tm), lambda b, i: (b, i, 0, 0)),
            pl.BlockSpec((None, qpt, IDX_HEADS // 2, 2 * DSA_QBLK, 2 * IDX_DIM), lambda b, i: (b, i, 0, 0, 0)),
            pl.BlockSpec((tm, 4 * IDX_DIM), lambda b, i: (row(b, i), 0)),
            pl.BlockSpec((None, IDX_HEADS, tm), lambda b, i: (b, 0, i)),
        ],
        out_shape=[
            jax.ShapeDtypeStruct((m, GROUP_WIDTH), BF16),
            jax.ShapeDtypeStruct((m, kvw), BF16),
            jax.ShapeDtypeStruct((batch, nt, DSA_KV_HEADS * VT_ROWS, tm), BF16),
            jax.ShapeDtypeStruct((batch, seq // DSA_QBLK, IDX_HEADS // 2, 2 * DSA_QBLK, 2 * IDX_DIM), BF16),
            jax.ShapeDtypeStruct((m, 4 * IDX_DIM), BF16),
            jax.ShapeDtypeStruct((batch, IDX_HEADS, seq), F32),
        ],
        compiler_params=_params(24 << 20, 2),
        name="prep_dsa",
    )(h2d, h2d, h2d, h2d, hs, tabs)


def _sortable_key(score):
    bits = lax.bitcast_convert_type(score, I32)
    key = jnp.where(bits < 0, bits ^ jnp.int32(0x7FFFFFFF), bits)
    return jnp.where(score == 0.0, 0, key)


def _dsa_kernel(qi2_ref, ki4_ref, wt_ref, q_ref, k_ref, vt_ref, gate_ref, o_ref, key_ref, s0_ref, s1_ref, qi4_ref,
                *, topk, seq_bits, grp):
    i = pl.program_id(1)
    n_vis = i // 2
    n_chunks = n_vis + 1
    wt = wt_ref[...]
    ch = KEY_CHUNK
    qb = DSA_QBLK

    def rows(c):
        return pl.ds(pl.multiple_of(c * ch, ch), ch)

    for p in range(IDX_HEADS // 2):
        qi4_ref[p, :, 0:LANES] = qi2_ref[p]
        qi4_ref[p, :, LANES:2 * LANES] = qi2_ref[p]

    def score_chunk(c):
        kc = ki4_ref[rows(c), :]
        tot = None
        for p in range(IDX_HEADS // 2):
            s = lax.dot_general(kc, qi4_ref[p], _NT, preferred_element_type=F32)
            r = jnp.maximum(s, 0.0)
            t = r[:, :qb] * wt[2 * p:2 * p + 1, :] + r[:, qb:] * wt[2 * p + 1:2 * p + 2, :]
            tot = t if tot is None else tot + t
        return tot

    def phase1(c, carry):
        key_ref[rows(c), :] = _sortable_key(score_chunk(c))
        return carry

    lax.fori_loop(0, n_vis // 2, lambda p, cr: phase1(2 * p + 1, phase1(2 * p, cr)), 0)
    lax.fori_loop(0, n_vis & 1, lambda _, cr: phase1(n_vis - 1, cr), 0)
    kidx = n_vis * ch + lax.broadcasted_iota(I32, (ch, qb), 0)
    qidx = i * qb + lax.broadcasted_iota(I32, (ch, qb), 1)
    key_ref[rows(n_vis), :] = jnp.where(kidx <= qidx, _sortable_key(score_chunk(n_vis)), INT_MIN)

    n_groups = (n_chunks + grp - 1) // grp

    def pad_chunk(c, carry):
        key_ref[rows(c), :] = jnp.full((ch, qb), INT_MIN, I32)
        return carry

    lax.fori_loop(n_chunks, n_groups * grp, pad_chunk, 0)

    def count_ge(t):
        def body(p, acc):
            blk = key_ref[pl.ds(pl.multiple_of(p * grp * ch, grp * ch), grp * ch), :]
            hit = (blk >= t).astype(I32)
            return acc + jnp.sum(hit.reshape(grp * ch // 8, 8, qb), axis=0)
        acc = lax.fori_loop(0, n_groups, body, jnp.zeros((8, qb), I32))
        return jnp.sum(acc, axis=0, keepdims=True)

    c0 = count_ge(jnp.zeros((1, qb), I32))
    state = (jnp.where(c0 >= topk, 0, INT_MIN).astype(I32), c0)

    def bit_step(b, st):
        t, cnt = st
        cand = t + jnp.left_shift(jnp.int32(1), 30 - b)
        c = count_ge(cand)
        ok = c >= topk
        return jnp.where(ok, cand, t), jnp.where(ok, c, cnt)

    def settled(st):
        t, cnt = st
        return jnp.min(jnp.logical_or(cnt == topk, t == INT_MIN).astype(I32)) > 0

    def finish(st):
        t, cnt = st
        need = topk - count_ge(t + 1)
        return t, need, jnp.logical_and(cnt > topk, t > INT_MIN).astype(I32)

    def search(st, stops):
        if not stops:
            return finish(st)
        hi_bit = 30 if len(stops) == len(SEARCH_STOPS) else SEARCH_STOPS[len(SEARCH_STOPS) - len(stops) - 1] - 1
        st = lax.fori_loop(30 - hi_bit, 31 - stops[0], bit_step, st)
        if stops[0] == 0:
            return finish(st)
        return lax.cond(settled(st), lambda s_: (s_[0], jnp.ones((1, qb), I32), jnp.zeros((1, qb), I32)),
                        lambda s_: search(s_, stops[1:]), st)

    thr, need, tied = search(state, SEARCH_STOPS)

    def count(pred):
        def body(c, acc):
            idx = c * ch + lax.broadcasted_iota(I32, (ch, qb), 0)
            hit = pred(key_ref[rows(c), :], idx).astype(I32)
            return acc + jnp.sum(hit.reshape(ch // 8, 8, qb), axis=0)
        acc = lax.fori_loop(0, n_chunks, body, jnp.zeros((8, qb), I32))
        return jnp.sum(acc, axis=0, keepdims=True)

    def tie_search():
        def step(b, lo):
            cand = lo + jnp.left_shift(jnp.int32(1), seq_bits - 1 - b)
            below = count(lambda keys, idx: jnp.logical_and(keys == thr, idx < cand))
            return jnp.where(below < need, cand, lo)
        return lax.fori_loop(0, seq_bits, step, jnp.zeros((1, qb), I32))

    last_eq = lax.cond(jnp.max(tied) > 0, tie_search,
                       lambda: jnp.full((1, qb), 2 ** seq_bits, I32))
    thr_sel = jnp.maximum(thr, INT_MIN + 1)

    nq = DSA_GROUP * qb
    one_hot = (lax.broadcasted_iota(I32, (nq, qb), 0) % qb == lax.broadcasted_iota(I32, (nq, qb), 1))
    one_hot = jnp.where(one_hot, 1.0, 0.0).astype(BF16)
    qs = [jnp.concatenate([jnp.concatenate([q_ref[:, (n * DSA_GROUP + g) * LANES:(n * DSA_GROUP + g + 1) * LANES]
                                            for g in range(DSA_GROUP)], axis=0), one_hot], axis=1)
          for n in range(DSA_KV_HEADS)]

    slots = (s0_ref, s1_ref)

    def produce(c, slot):
        keys = key_ref[rows(c), :]
        idx = c * ch + lax.broadcasted_iota(I32, (ch, qb), 0)
        drop = jnp.logical_and(keys == thr, idx > last_eq)
        sel = jnp.logical_and(keys >= thr_sel, jnp.logical_not(drop))
        neg = jnp.where(sel, 0.0, NEG_BIG).astype(BF16)
        col_max = []
        for n in range(DSA_KV_HEADS):
            kc = jnp.concatenate([k_ref[rows(c), n * LANES:(n + 1) * LANES], neg], axis=1)
            s = lax.dot_general(kc, qs[n], _NT, preferred_element_type=F32)
            slots[slot][n] = s
            col_max.append(jnp.max(s, axis=0, keepdims=True))
        return tuple(col_max)

    def consume(c, slot, col_max, carry):
        return tuple(_softmax_step(slots[slot][n], col_max[n],
                                   vt_ref[c, n * VT_ROWS:(n + 1) * VT_ROWS, :], *carry[n])
                     for n in range(DSA_KV_HEADS))

    def phase3(j, state):
        carry, col0 = state
        col1 = produce(2 * j + 1, 1)
        carry = consume(2 * j, 0, col0, carry)
        col0 = produce(2 * j + 2, 0)
        carry = consume(2 * j + 1, 1, col1, carry)
        return carry, col0

    init = tuple((jnp.full((1, nq), NEG_BIG, F32), jnp.zeros((VT_ROWS, nq), F32))
                 for _ in range(DSA_KV_HEADS))
    pairs = (n_chunks - 1) // 2
    state = lax.fori_loop(0, pairs // 2, lambda jj, st: phase3(2 * jj + 1, phase3(2 * jj, st)),
                          (init, produce(0, 0)))
    carry, col0 = lax.fori_loop(0, pairs & 1, lambda _, st: phase3(pairs - 1, st), state)
    last = 2 * pairs

    def tail_two(cr):
        col1 = produce(last + 1, 1)
        cr = consume(last, 0, col0, cr)
        return consume(last + 1, 1, col1, cr)

    res = lax.cond(n_chunks - 1 > last, tail_two, lambda cr: consume(last, 0, col0, cr), carry)
    for n in range(DSA_KV_HEADS):
        o = _normalised_t(res[n][1], DSA_HEAD_DIM)
        for g in range(DSA_GROUP):
            col = (n * DSA_GROUP + g) * LANES
            o_ref[:, col:col + LANES] = (o[g * qb:(g + 1) * qb, :]
                                         * _silu(gate_ref[:, col:col + LANES])).astype(o_ref.dtype)


def _dsa(qi2, ki4, wt, q, k, vt, h2d, batch, seq):
    m = batch * seq
    nq = seq // DSA_QBLK
    nck = seq // KEY_CHUNK
    kvw = DSA_KV_HEADS * DSA_HEAD_DIM
    topk = min(TOPK_MAX, seq // 4)
    seq_bits = int(np.log2(seq))
    assert 2 ** seq_bits == seq and nck % 2 == 0 and topk <= KEY_CHUNK
    return pl.pallas_call(
        functools.partial(_dsa_kernel, topk=topk, seq_bits=seq_bits, grp=2),
        grid=(batch, nq),
        in_specs=[
            pl.BlockSpec((None, None, IDX_HEADS // 2, 2 * DSA_QBLK, 2 * IDX_DIM),
                         lambda b, i: (b, i, 0, 0, 0)),
            pl.BlockSpec((seq, 4 * IDX_DIM), lambda b, i: (b, 0)),
            pl.BlockSpec((None, IDX_HEADS, DSA_QBLK), lambda b, i: (b, 0, i)),
            pl.BlockSpec((DSA_QBLK, GROUP_WIDTH), lambda b, i: (b * nq + i, 0)),
            pl.BlockSpec((seq, kvw), lambda b, i: (b, 0)),
            pl.BlockSpec((None, nck, DSA_KV_HEADS * VT_ROWS, KEY_CHUNK), lambda b, i: (b, 0, 0, 0)),
            pl.BlockSpec((DSA_QBLK, GROUP_WIDTH), lambda b, i: (b * nq + i, COL_BG // GROUP_WIDTH)),
        ],
        out_specs=pl.BlockSpec((DSA_QBLK, GROUP_WIDTH), lambda b, i: (b * nq + i, 0)),
        out_shape=jax.ShapeDtypeStruct((m, GROUP_WIDTH), BF16),
        scratch_shapes=[pltpu.VMEM((seq, DSA_QBLK), I32),
                        pltpu.VMEM((DSA_KV_HEADS, KEY_CHUNK, DSA_GROUP * DSA_QBLK), F32),
                        pltpu.VMEM((DSA_KV_HEADS, KEY_CHUNK, DSA_GROUP * DSA_QBLK), F32),
                        pltpu.VMEM((IDX_HEADS // 2, 2 * DSA_QBLK, 4 * IDX_DIM), BF16)],
        compiler_params=_params(32 << 20, 2),
        name="dsa",
    )(qi2, ki4, wt, q, k, vt, h2d)


def _rglru_kernel(u_ref, gate_ref, cw_ref, cb_ref, wa_ref, ba_ref, wx_ref, bx_ref, lam_ref,
                  o_ref, halo_ref, h_ref, a_scr, x_scr, y_scr, *, batch, tt):
    t = pl.program_id(0)

    @pl.when(t == 0)
    def _():
        halo_ref[...] = jnp.zeros_like(halo_ref)
        h_ref[...] = jnp.zeros_like(h_ref)

    lam = lam_ref[...]
    z = -lam
    softplus = jnp.maximum(z, 0.0) + jnp.log1p(jnp.exp(-jnp.abs(z)))
    row = lax.broadcasted_iota(I32, (tt, LRU_WIDTH), 0)
    for b in range(batch):
        u = u_ref[b]
        prev = halo_ref[b]
        p1, p2, p3 = prev[7:8, :], prev[6:7, :], prev[5:6, :]
        s1 = jnp.where(row == 0, p1, pltpu.roll(u, 1, 0))
        s2 = jnp.where(row == 0, p2, jnp.where(row == 1, p1, pltpu.roll(u, 2, 0)))
        s3 = jnp.where(row == 0, p3, jnp.where(row == 1, p2, jnp.where(row == 2, p1, pltpu.roll(u, 3, 0))))
        conv = (cw_ref[3:4, :] * u + cw_ref[2:3, :] * s1 + cw_ref[1:2, :] * s2 + cw_ref[0:1, :] * s3
                + cb_ref[...])
        halo_ref[b] = u[tt - 8:, :]
        cb16 = conv.astype(BF16)
        r = _sigmoid(jnp.dot(cb16, wa_ref[...], preferred_element_type=F32) + ba_ref[...])
        ig = _sigmoid(jnp.dot(cb16, wx_ref[...], preferred_element_type=F32) + bx_ref[...])
        log_a = -LRU_C * r * softplus
        a_scr[b] = jnp.exp(log_a)
        th = jnp.tanh(log_a)
        x_scr[b] = jnp.sqrt(-2.0 * th / (1.0 - th)) * (ig * conv)

    def step(s, h):
        h = a_scr[:, pl.ds(s, 1), :] * h + x_scr[:, pl.ds(s, 1), :]
        y_scr[:, pl.ds(s, 1), :] = h
        return h

    h_ref[...] = lax.fori_loop(0, tt, step, h_ref[...], unroll=8)
    o_ref[...] = (y_scr[...] * _silu(gate_ref[...])).astype(o_ref.dtype)


def _rglru(h3d, conv_w, conv_b, wa_bd, b_a, wx_bd, b_x, lam, batch, seq, tt):
    nt = seq // tt
    w = LRU_WIDTH

    def vec(r):
        return pl.BlockSpec((r, w), lambda t: (0, 0))

    return pl.pallas_call(
        functools.partial(_rglru_kernel, batch=batch, tt=tt),
        grid=(nt,),
        in_specs=[
            pl.BlockSpec((batch, tt, w), lambda t: (0, t, COL_U // w)),
            pl.BlockSpec((batch, tt, w), lambda t: (0, t, COL_CG // w)),
            vec(CONV_WIDTH), vec(1),
            pl.BlockSpec((w, w), lambda t: (0, 0)), vec(1),
            pl.BlockSpec((w, w), lambda t: (0, 0)), vec(1),
            vec(1),
        ],
        out_specs=pl.BlockSpec((batch, tt, w), lambda t: (0, t, 0)),
        out_shape=jax.ShapeDtypeStruct((batch, seq, w), BF16),
        scratch_shapes=[pltpu.VMEM((batch, 8, w), F32), pltpu.VMEM((batch, 1, w), F32),
                        pltpu.VMEM((batch, tt, w), F32), pltpu.VMEM((batch, tt, w), F32),
                        pltpu.VMEM((batch, tt, w), F32)],
        compiler_params=_params(32 << 20, 1),
        name="rglru",
    )(h3d, h3d, conv_w, conv_b, wa_bd, b_a, wx_bd, b_x, lam)


def _mem_attn_kernel(q_ref, gate_ref, km_ref, vm_ref, o_ref):
    scale = MEM_HEAD_DIM ** -0.5
    for h in range(MEM_HEADS):
        cols = slice(h * MEM_HEAD_DIM, (h + 1) * MEM_HEAD_DIM)
        qh = (q_ref[:, cols] * scale).astype(BF16)
        s = lax.dot_general(qh, km_ref[:, cols], _NT, preferred_element_type=F32)
        p = jnp.exp(s - jnp.max(s, axis=-1, keepdims=True))
        l = jnp.sum(p, axis=-1, keepdims=True)
        o = jnp.dot(p.astype(BF16), vm_ref[:, cols], preferred_element_type=F32) / l
        o_ref[:, cols] = (o * _silu(gate_ref[:, cols])).astype(o_ref.dtype)


def _mem_attn(h2d, km, vm, batch, seq, tm):
    m = batch * seq
    nt = seq // tm
    w = GROUP_WIDTH
    return pl.pallas_call(
        _mem_attn_kernel,
        grid=(batch, nt),
        in_specs=[
            pl.BlockSpec((tm, w), lambda b, i: (b * nt + i, COL_DQ // w)),
            pl.BlockSpec((tm, w), lambda b, i: (b * nt + i, COL_DG // w)),
            pl.BlockSpec((N_MEM, w), lambda b, i: (b, 0)),
            pl.BlockSpec((N_MEM, w), lambda b, i: (b, 0)),
        ],
        out_specs=pl.BlockSpec((tm, w), lambda b, i: (b * nt + i, 0)),
        out_shape=jax.ShapeDtypeStruct((m, w), BF16),
        compiler_params=_params(24 << 20, 2),
        name="mem_attn",
    )(h2d, h2d, km, vm)


def _out_ln_kernel(*refs, n_lhs, kg, nj, per_emit, d_model):
    lhs = refs[:n_lhs]
    w_ref, x_ref, g_ref, b_ref, o_ref, o16_ref, z_ref, mu_ref, rs_ref = refs[n_lhs:]
    j = pl.program_id(1)

    @pl.when(j < nj)
    def _():
        acc = None
        for g in range(n_lhs):
            part = jnp.dot(lhs[g][...], w_ref[g * kg:(g + 1) * kg, :].astype(BF16), preferred_element_type=F32)
            acc = part if acc is None else acc + part
        z_ref[j] = DEEPNORM_ALPHA * x_ref[...] + acc

    @pl.when(j == nj)
    def _():
        def lane_tiles(v):
            tiles = [v[:, t * LANES:(t + 1) * LANES] for t in range(v.shape[1] // LANES)]
            return functools.reduce(lambda a, b: a + b, tiles)

        tot = functools.reduce(lambda a, b: a + b, [lane_tiles(z_ref[jj]) for jj in range(nj)])
        mu = jnp.sum(tot, axis=-1, keepdims=True) * (1.0 / d_model)
        sq = None
        for jj in range(nj):
            dlt = z_ref[jj] - mu
            part = lane_tiles(dlt * dlt)
            sq = part if sq is None else sq + part
        mu_ref[...] = mu
        rs_ref[...] = lax.rsqrt(jnp.sum(sq, axis=-1, keepdims=True) * (1.0 / d_model) + LN_EPS)

    @pl.when(j >= nj)
    def _():
        e = j - nj
        tn = z_ref.shape[2]
        for t in range(per_emit):
            cols = slice(t * tn, (t + 1) * tn)
            out = (z_ref[e * per_emit + t] - mu_ref[...]) * rs_ref[...] * g_ref[:, cols] + b_ref[:, cols]
            o_ref[:, cols] = out
            o16_ref[:, cols] = out.astype(BF16)


def _out_proj_deepnorm(lhs_list, w, layer, x, ln_g, ln_b, tm, tn, te):
    m, kg = lhs_list[0].shape
    _, k, n = w.shape
    n_lhs = len(lhs_list)
    nj, ne, per_emit = n // tn, n // te, te // tn
    assert k == kg * n_lhs and m % tm == 0 and n % te == 0 and te % tn == 0

    def col(j):
        return jnp.minimum(j, nj - 1)

    def emit(j):
        return jnp.maximum(j - nj, 0)

    est = (nj * tm * tn * 4 + 2 * (tm * k * 2 + k * tn * 4 + tm * tn * 4 + tm * te * 6) + tm * tn * 8)
    return pl.pallas_call(
        functools.partial(_out_ln_kernel, n_lhs=n_lhs, kg=kg, nj=nj, per_emit=per_emit, d_model=n),
        grid=(m // tm, nj + ne),
        in_specs=[pl.BlockSpec((tm, kg), lambda i, j: (i, 0)) for _ in range(n_lhs)]
        + [pl.BlockSpec((None, k, tn), lambda i, j: (layer, 0, col(j))),
           pl.BlockSpec((tm, tn), lambda i, j: (i, col(j))),
           pl.BlockSpec((1, te), lambda i, j: (0, emit(j))),
           pl.BlockSpec((1, te), lambda i, j: (0, emit(j)))],
        out_specs=[pl.BlockSpec((tm, te), lambda i, j: (i, emit(j))),
                   pl.BlockSpec((tm, te), lambda i, j: (i, emit(j)))],
        out_shape=[jax.ShapeDtypeStruct((m, n), F32), jax.ShapeDtypeStruct((m, n), BF16)],
        scratch_shapes=[pltpu.VMEM((nj, tm, tn), F32), pltpu.VMEM((tm, 1), F32), pltpu.VMEM((tm, 1), F32)],
        compiler_params=_params(est + (10 << 20), 2),
        name="out_proj_deepnorm",
    )(*lhs_list, w, x, ln_g, ln_b)


def _rope_tabs(positions):
    pos = positions.astype(F32)
    b, s = pos.shape

    def one(rot, period):
        inv = ROPE_THETA ** (-jnp.arange(0, rot, 2, dtype=F32) / rot)
        ang = pos[:, :, None] * inv
        c, sn = jnp.cos(ang), jnp.sin(ang)
        rest = period - rot
        cp = jnp.concatenate([c, c, jnp.ones((b, s, rest), F32)], axis=-1)
        sp = jnp.concatenate([-sn, sn, jnp.zeros((b, s, rest), F32)], axis=-1)
        reps = LANES // period
        return jnp.tile(cp, (1, 1, reps)), jnp.tile(sp, (1, 1, reps))

    parts = one(MLA_ROPE, LANES) + one(DSA_ROT, LANES) + one(IDX_ROT, IDX_DIM)
    return jnp.concatenate(parts, axis=-1).reshape(b * s, N_TABS * LANES)


def _block_diag(w):
    l = w.shape[0]
    eye = jnp.eye(LRU_BLOCKS, dtype=w.dtype)
    bd = jnp.einsum('lnde,nm->lndme', w, eye)
    return bd.reshape(l, LRU_WIDTH, LRU_WIDTH).astype(BF16)


def _layer(x, x16, mem16, tabs, lw, l, batch, seq):
    m = batch * seq
    main_rows, small_rows = _in_proj_row_tables(256)
    h2d = _in_proj(x16, lw["w_in_t"], l, main_rows, tm=min(2048, m), tn=256, name="in_proj")
    hs = _in_proj_small(x16, lw["w_in_t"], l, int(small_rows[0]), int(small_rows[1]), tm=min(2048, m))
    tile_a = min(512, seq)
    qa, ka, vta = _prep_mla(h2d, hs, tabs, lw["g_cq"], lw["g_ckv"], lw["w_uq"], lw["w_uk"], lw["w_uvt"],
                            batch, seq, tile_a)
    ya = _mla_attn(qa, ka, vta, h2d, batch, seq, tile_a)
    qb, kb, vtb, qi2, ki4, wt = _prep_dsa(h2d, hs, tabs, batch, seq)
    yb = _dsa(qi2, ki4, wt, qb, kb, vtb, h2d, batch, seq)
    yc = _rglru(h2d.reshape(batch, seq, N_MAIN), lw["conv_w"], lw["conv_b"], lw["w_rg_a"], lw["b_rg_a"],
                lw["w_rg_x"], lw["b_rg_x"], lw["lam"], batch, seq, min(256, seq))
    yc = yc.reshape(m, GROUP_WIDTH)
    km = _matmul([mem16], lw["w_mem_k"], l, BF16, tm=mem16.shape[0], tn=256, name="mem_proj_k")
    vm = _matmul([mem16], lw["w_mem_v"], l, BF16, tm=mem16.shape[0], tn=256, name="mem_proj_v")
    yd = _mem_attn(h2d, km, vm, batch, seq, min(512, seq))
    return _out_proj_deepnorm([ya, yb, yc, yd], lw["w_o"], l, x, lw["ln_g"], lw["ln_b"],
                              tm=min(1024, m), tn=256, te=512)


def kernel(x, mem, positions, w_in, g_cq, g_ckv, w_uq, w_ukv, conv_w, conv_b, w_rg_a, b_rg_a, w_rg_x,
           b_rg_x, lru_lambda, w_mem_k, w_mem_v, w_o, ln_g, ln_b):
    batch, seq, d = x.shape
    depth = w_in.shape[0]
    tabs = _rope_tabs(positions)
    w_in_t = w_in.transpose(0, 2, 1)
    w_uq_p = jnp.pad(w_uq.reshape(depth, MLA_Q_LORA, MLA_HEADS, MLA_NOPE + MLA_ROPE),
                     [(0, 0), (0, 0), (0, 0), (0, 2 * LANES - MLA_NOPE - MLA_ROPE)])
    w_uq_p = w_uq_p.reshape(depth, MLA_Q_LORA, MLA_HEADS * 2 * LANES).astype(BF16)
    w_ukv4 = w_ukv.reshape(depth, MLA_KV_LORA, MLA_HEADS, MLA_NOPE + MLA_V)
    w_uk = w_ukv4[..., :MLA_NOPE].reshape(depth, MLA_KV_LORA, MLA_HEADS * MLA_NOPE).astype(BF16)
    w_uvt = w_ukv4[..., MLA_NOPE:].reshape(depth, MLA_KV_LORA, MLA_HEADS * MLA_V)
    w_uvt = w_uvt.transpose(0, 2, 1).astype(BF16)
    wa_bd = _block_diag(w_rg_a)
    wx_bd = _block_diag(w_rg_x)
    mem16 = mem.reshape(batch * mem.shape[1], d).astype(BF16)

    xf = x.reshape(batch * seq, d)
    x16 = xf.astype(BF16)
    for l in range(depth):
        lw = dict(w_in_t=w_in_t, g_cq=g_cq[l][None], g_ckv=g_ckv[l][None], w_uq=w_uq_p[l], w_uk=w_uk[l],
                  w_uvt=w_uvt[l], conv_w=conv_w[l], conv_b=conv_b[l][None], w_rg_a=wa_bd[l],
                  b_rg_a=b_rg_a[l][None], w_rg_x=wx_bd[l], b_rg_x=b_rg_x[l][None], lam=lru_lambda[l][None],
                  w_mem_k=w_mem_k, w_mem_v=w_mem_v, w_o=w_o, ln_g=ln_g[l][None], ln_b=ln_b[l][None])
        xf, x16 = _layer(xf, x16, mem16, tabs, lw, l, batch, seq)
    return xf.reshape(batch, seq, d)
```

```python
import functools

import numpy as np
import jax
import jax.numpy as jnp
from jax import lax
from jax.experimental import pallas as pl
from jax.experimental.pallas import tpu as pltpu

F32 = jnp.float32
BF16 = jnp.bfloat16
I32 = jnp.int32

DEPTH = 4
D_MODEL = 4096
N_MEM = 256
GROUP_WIDTH = D_MODEL // 4
ROPE_THETA = 500000.0
MLA_HEADS = 8
MLA_NOPE = 128
MLA_ROPE = 64
MLA_V = 128
MLA_Q_LORA = GROUP_WIDTH
MLA_KV_LORA = GROUP_WIDTH // 2
DSA_HEADS = 8
DSA_KV_HEADS = 2
DSA_GROUP = DSA_HEADS // DSA_KV_HEADS
DSA_HEAD_DIM = 128
DSA_ROT = DSA_HEAD_DIM // 4
IDX_HEADS = 16
IDX_DIM = 64
IDX_ROT = IDX_DIM // 4
TOPK_MAX = 256
LRU_WIDTH = GROUP_WIDTH
LRU_BLOCKS = 16
LRU_BLOCK_DIM = LRU_WIDTH // LRU_BLOCKS
CONV_WIDTH = 4
LRU_C = 8.0
MEM_HEADS = 4
MEM_HEAD_DIM = GROUP_WIDTH // MEM_HEADS
DEEPNORM_ALPHA = (2 * DEPTH) ** 0.25
LN_EPS = 1e-5
RMS_EPS = 1e-6

IN_SIZES = (
    MLA_Q_LORA, MLA_KV_LORA, MLA_ROPE, GROUP_WIDTH,
    GROUP_WIDTH, DSA_KV_HEADS * DSA_HEAD_DIM, DSA_KV_HEADS * DSA_HEAD_DIM,
    IDX_HEADS * IDX_DIM, IDX_DIM, IDX_HEADS, GROUP_WIDTH,
    LRU_WIDTH, GROUP_WIDTH,
    GROUP_WIDTH, GROUP_WIDTH,
)

LANES = 128
V7X_VMEM_BUDGET = 60 * 1024 * 1024

COL_CQ = 0
COL_AG = 1024
COL_BQ = 2048
COL_QI = 3072
COL_BG = 4096
COL_U = 5120
COL_CG = 6144
COL_DQ = 7168
COL_DG = 8192
COL_CKV = 9216
COL_BK = 9728
COL_BV = 9984
N_MAIN = 10240
SMALL_KR = 0
SMALL_KI = 128

TAB_MLA_C, TAB_MLA_S, TAB_DSA_C, TAB_DSA_S, TAB_IDX_C, TAB_IDX_S = range(6)
N_TABS = 6

NEG_BIG = -1e30
INT_MIN = -2 ** 31
LOG2E = 1.4426950408889634
KEY_CHUNK = 256
DSA_QBLK = 128
SEARCH_STOPS = (11, 8, 6, 3, 0)
VT_ROWS = 128 + 16
_NT = (((1,), (1,)), ((), ()))


def _vmem_limit(nbytes):
    return int(min(V7X_VMEM_BUDGET, max(nbytes, 16 * 1024 * 1024)))


def _params(nbytes, ndims):
    return pltpu.CompilerParams(dimension_semantics=("arbitrary",) * ndims,
                                vmem_limit_bytes=_vmem_limit(nbytes))


def _mm_kernel(*refs, n_lhs, kg):
    w_ref = refs[n_lhs]
    o_ref = refs[n_lhs + 1]
    acc = None
    for g in range(n_lhs):
        part = jnp.dot(refs[g][...], w_ref[g * kg:(g + 1) * kg, :].astype(BF16), preferred_element_type=F32)
        acc = part if acc is None else acc + part
    o_ref[...] = acc.astype(o_ref.dtype)


def _matmul(lhs_list, w, layer, out_dtype, tm, tn, name):
    m, kg = lhs_list[0].shape
    _, k, n = w.shape
    n_lhs = len(lhs_list)
    assert k == kg * n_lhs and m % tm == 0 and n % tn == 0
    est = 2 * (tm * k * 2 + k * tn * w.dtype.itemsize + tm * tn * jnp.dtype(out_dtype).itemsize) + tm * tn * 8
    return pl.pallas_call(
        functools.partial(_mm_kernel, n_lhs=n_lhs, kg=kg),
        grid=(m // tm, n // tn),
        in_specs=[pl.BlockSpec((tm, kg), lambda i, j: (i, 0)) for _ in range(n_lhs)]
        + [pl.BlockSpec((None, k, tn), lambda i, j: (layer, 0, j))],
        out_specs=pl.BlockSpec((tm, tn), lambda i, j: (i, j)),
        out_shape=jax.ShapeDtypeStruct((m, n), out_dtype),
        compiler_params=_params(est + (8 << 20), 2),
        name=name,
    )(*lhs_list, w)


def _in_proj_kernel(tab_ref, x_ref, w_ref, o_ref):
    del tab_ref
    o_ref[...] = lax.dot_general(x_ref[...], w_ref[...].astype(BF16), _NT, preferred_element_type=F32)


def _in_proj(x16, w_t, layer, row_offsets, tm, tn, name):
    m, k = x16.shape
    nblk = len(row_offsets)
    est = 2 * (tm * k * 2 + tn * k * 4 + tm * tn * 4) + tm * tn * 8
    grid_spec = pltpu.PrefetchScalarGridSpec(
        num_scalar_prefetch=1,
        grid=(m // tm, nblk),
        in_specs=[
            pl.BlockSpec((tm, k), lambda i, j, tab: (i, 0)),
            pl.BlockSpec((pl.Squeezed(), pl.Element(tn), pl.Element(k)),
                         lambda i, j, tab: (layer, pl.multiple_of(tab[j], 16), 0)),
        ],
        out_specs=pl.BlockSpec((tm, tn), lambda i, j, tab: (i, j)),
    )
    return pl.pallas_call(
        _in_proj_kernel,
        grid_spec=grid_spec,
        out_shape=jax.ShapeDtypeStruct((m, nblk * tn), F32),
        compiler_params=_params(est + (8 << 20), 2),
        name=name,
    )(jnp.asarray(row_offsets, I32), x16, w_t)


def _in_proj_small_kernel(x_ref, wa_ref, wb_ref, o_ref):
    w = jnp.concatenate([wa_ref[...], wb_ref[...]], axis=0).astype(BF16)
    o_ref[...] = lax.dot_general(x_ref[...], w, _NT, preferred_element_type=F32)


def _in_proj_small(x16, w_t, layer, rows_a, rows_b, tm):
    m, k = x16.shape

    def window(r0):
        return pl.BlockSpec((pl.Squeezed(), pl.Element(LANES), pl.Element(k)), lambda i: (layer, r0, 0))

    est = 2 * (tm * k * 2 + 2 * LANES * k * 4 + tm * 2 * LANES * 4) + 2 * LANES * k * 6
    return pl.pallas_call(
        _in_proj_small_kernel,
        grid=(m // tm,),
        in_specs=[pl.BlockSpec((tm, k), lambda i: (i, 0)), window(rows_a), window(rows_b)],
        out_specs=pl.BlockSpec((tm, 2 * LANES), lambda i: (i, 0)),
        out_shape=jax.ShapeDtypeStruct((m, 2 * LANES), F32),
        compiler_params=_params(est + (8 << 20), 1),
        name="in_proj_small",
    )(x16, w_t, w_t)


def _in_proj_row_tables(tn):
    names = ("cq", "ckv", "kr", "ag", "bq", "bk", "bv", "qi", "ki", "wi", "bg", "u", "cg", "dq", "dg")
    src = dict(zip(names, [0] + [int(c) for c in np.cumsum(IN_SIZES)[:-1]]))
    size = dict(zip(names, IN_SIZES))
    order = [("cq", COL_CQ), ("ag", COL_AG), ("bq", COL_BQ), ("qi", COL_QI), ("bg", COL_BG), ("u", COL_U),
             ("cg", COL_CG), ("dq", COL_DQ), ("dg", COL_DG), ("ckv", COL_CKV), ("bk", COL_BK), ("bv", COL_BV)]
    merged = []
    for n, dst in order:
        if merged and merged[-1][0] + merged[-1][2] == src[n] and merged[-1][1] + merged[-1][2] == dst:
            merged[-1] = (merged[-1][0], merged[-1][1], merged[-1][2] + size[n])
        else:
            merged.append((src[n], dst, size[n]))
    main = []
    for s0, dst, width in merged:
        assert dst == len(main) * tn and width % tn == 0 and s0 % 16 == 0
        main += [s0 + t * tn for t in range(width // tn)]
    assert len(main) * tn == N_MAIN and src["wi"] == src["ki"] + size["ki"]
    return np.asarray(main, np.int32), np.asarray([src["kr"], src["ki"]], np.int32)


def _rope(v, c_tab, s_tab, half, period):
    width = v.shape[-1]
    lane = lax.broadcasted_iota(I32, v.shape, 1) & (period - 1)
    swapped = jnp.where(lane < half, pltpu.roll(v, width - half, 1), pltpu.roll(v, half, 1))
    return v * c_tab + swapped * s_tab


def _tile_lanes(t, reps):
    return t if reps == 1 else jnp.concatenate([t] * reps, axis=1)


def _rms(x, g):
    return x * lax.rsqrt(jnp.mean(x * x, axis=-1, keepdims=True) + RMS_EPS) * g


def _sigmoid(v):
    return 0.5 * jnp.tanh(0.5 * v) + 0.5


def _silu(g):
    return g * _sigmoid(g)


def _prep_mla_kernel(cq_ref, ckv_ref, kr_ref, tab_ref, gcq_ref, gckv_ref, wuq_ref, wk_ref, wvt_ref,
                     q_out, k_out, vt_out):
    scale = (MLA_NOPE + MLA_ROPE) ** -0.5 * LOG2E
    nq = _rms(cq_ref[...], gcq_ref[...]).astype(BF16)
    nkv = _rms(ckv_ref[...], gckv_ref[...]).astype(BF16)
    c_tab = tab_ref[:, TAB_MLA_C * LANES:(TAB_MLA_C + 1) * LANES]
    s_tab = tab_ref[:, TAB_MLA_S * LANES:(TAB_MLA_S + 1) * LANES]
    qf = jnp.dot(nq, wuq_ref[...], preferred_element_type=F32) * scale
    kn = jnp.dot(nkv, wk_ref[...], preferred_element_type=F32)
    lane = lax.broadcasted_iota(I32, kr_ref.shape, 1)
    kr = jnp.where(lane < MLA_ROPE, kr_ref[...], 0.0)
    kr = _rope(kr, c_tab, s_tab, MLA_ROPE // 2, LANES).astype(BF16)
    for h in range(MLA_HEADS):
        base = h * 2 * LANES
        q_out[:, base:base + LANES] = qf[:, base:base + LANES].astype(BF16)
        q_out[:, base + LANES:base + 2 * LANES] = _rope(
            qf[:, base + LANES:base + 2 * LANES], c_tab, s_tab, MLA_ROPE // 2, LANES).astype(BF16)
        k_out[:, base:base + LANES] = kn[:, h * LANES:(h + 1) * LANES].astype(BF16)
        k_out[:, base + LANES:base + 2 * LANES] = kr
    vt = lax.dot_general(wvt_ref[...], nkv, _NT, preferred_element_type=F32)
    half = vt.shape[1] // 2
    vt_out[0] = _vt_with_ones(vt[:, :half], MLA_HEADS, MLA_V)
    vt_out[1] = _vt_with_ones(vt[:, half:], MLA_HEADS, MLA_V)


def _prep_mla(h2d, hs, tabs, g_cq, g_ckv, wuq_p, wk, wvt, batch, seq, tm):
    m = batch * seq
    nt = seq // tm
    hd = MLA_HEADS * 2 * LANES
    return pl.pallas_call(
        _prep_mla_kernel,
        grid=(batch, nt),
        in_specs=[
            pl.BlockSpec((tm, MLA_Q_LORA), lambda b, i: (b * nt + i, COL_CQ // MLA_Q_LORA)),
            pl.BlockSpec((tm, MLA_KV_LORA), lambda b, i: (b * nt + i, COL_CKV // MLA_KV_LORA)),
            pl.BlockSpec((tm, LANES), lambda b, i: (b * nt + i, SMALL_KR // LANES)),
            pl.BlockSpec((tm, N_TABS * LANES), lambda b, i: (b * nt + i, 0)),
            pl.BlockSpec((1, MLA_Q_LORA), lambda b, i: (0, 0)),
            pl.BlockSpec((1, MLA_KV_LORA), lambda b, i: (0, 0)),
            pl.BlockSpec((MLA_Q_LORA, hd), lambda b, i: (0, 0)),
            pl.BlockSpec((MLA_KV_LORA, MLA_HEADS * MLA_NOPE), lambda b, i: (0, 0)),
            pl.BlockSpec((MLA_HEADS * MLA_V, MLA_KV_LORA), lambda b, i: (0, 0)),
        ],
        out_specs=[
            pl.BlockSpec((tm, hd), lambda b, i: (b * nt + i, 0)),
            pl.BlockSpec((tm, hd), lambda b, i: (b * nt + i, 0)),
            pl.BlockSpec((None, 2, MLA_HEADS * VT_ROWS, tm // 2), lambda b, i: (b, i, 0, 0)),
        ],
        out_shape=[
            jax.ShapeDtypeStruct((m, hd), BF16),
            jax.ShapeDtypeStruct((m, hd), BF16),
            jax.ShapeDtypeStruct((batch, 2 * nt, MLA_HEADS * VT_ROWS, tm // 2), BF16),
        ],
        compiler_params=_params(40 << 20, 2),
        name="prep_mla",
    )(h2d, h2d, hs, tabs, g_cq, g_ckv, wuq_p, wk, wvt)


def _softmax_step(s, col_max, vt_chunk, m, acc):
    m_new = jnp.maximum(m, col_max)
    alpha = jnp.exp2(m - m_new)
    p = jnp.exp2(s - m_new)
    acc_new = alpha * acc + jnp.dot(vt_chunk, p.astype(BF16), preferred_element_type=F32)
    return m_new, acc_new


def _normalised_t(acc, dv):
    return (acc[:dv, :] / acc[dv:dv + 1, :]).T


def _vt_with_ones(vt, heads, dv):
    ones = jnp.ones((VT_ROWS - dv, vt.shape[1]), BF16)
    parts = []
    for h in range(heads):
        parts += [vt[h * dv:(h + 1) * dv, :].astype(BF16), ones]
    return jnp.concatenate(parts, axis=0)


def _mla_attn_kernel(q_ref, k_ref, vt_ref, gate_ref, o_ref, s0_ref, s1_ref, *, tile, ch, hp):
    i = pl.program_id(2)
    dk = 2 * LANES
    qs = [q_ref[:, n * dk:(n + 1) * dk] for n in range(hp)]
    slots = (s0_ref, s1_ref)

    def produce(c, slot, diag_offset=None):
        col_max = []
        for n in range(hp):
            kc = k_ref[pl.ds(pl.multiple_of(c * ch, ch), ch), n * dk:(n + 1) * dk]
            s = lax.dot_general(kc, qs[n], _NT, preferred_element_type=F32)
            if diag_offset is not None:
                kidx = lax.broadcasted_iota(I32, (ch, tile), 0) + diag_offset
                qidx = lax.broadcasted_iota(I32, (ch, tile), 1)
                s = jnp.where(kidx <= qidx, s, NEG_BIG)
            slots[slot][n] = s
            col_max.append(jnp.max(s, axis=0, keepdims=True))
        return tuple(col_max)

    def consume(c, slot, col_max, carry):
        return tuple(_softmax_step(slots[slot][n], col_max[n], vt_ref[c, n * VT_ROWS:(n + 1) * VT_ROWS, :],
                                   *carry[n]) for n in range(hp))

    init = tuple((jnp.full((1, tile), NEG_BIG, F32), jnp.zeros((VT_ROWS, tile), F32)) for _ in range(hp))
    diag = 2 * i
    col_a = produce(diag, 0, 0)
    col_b = produce(diag + 1, 1, ch)
    carry = consume(diag, 0, col_a, init)

    def pair(j, state):
        cr, col1, c1 = state
        col0 = produce(2 * j, 0)
        cr = consume(c1, 1, col1, cr)
        col1 = produce(2 * j + 1, 1)
        cr = consume(2 * j, 0, col0, cr)
        return cr, col1, 2 * j + 1

    state = lax.fori_loop(0, i // 2, lambda jj, st: pair(2 * jj + 1, pair(2 * jj, st)),
                          (carry, col_b, diag + 1))
    carry, col1, c1 = lax.fori_loop(0, i & 1, lambda _, st: pair(i - 1, st), state)
    res = consume(c1, 1, col1, carry)
    for n in range(hp):
        o = _normalised_t(res[n][1], MLA_V)
        cols = slice(n * LANES, (n + 1) * LANES)
        o_ref[:, cols] = (o * _silu(gate_ref[:, cols])).astype(o_ref.dtype)


def _mla_attn(q, k, vt, h2d, batch, seq, tile):
    m = batch * seq
    nt = seq // tile
    ch = tile // 2
    hp = 4
    dk = 2 * LANES
    return pl.pallas_call(
        functools.partial(_mla_attn_kernel, tile=tile, ch=ch, hp=hp),
        grid=(batch, MLA_HEADS // hp, nt),
        in_specs=[
            pl.BlockSpec((tile, hp * dk), lambda b, h, i: (b * nt + i, h)),
            pl.BlockSpec((seq, hp * dk), lambda b, h, i: (b, h)),
            pl.BlockSpec((None, 2 * nt, hp * VT_ROWS, ch), lambda b, h, i: (b, 0, h, 0)),
            pl.BlockSpec((tile, hp * LANES), lambda b, h, i: (b * nt + i, COL_AG // (hp * LANES) + h)),
        ],
        out_specs=pl.BlockSpec((tile, hp * LANES), lambda b, h, i: (b * nt + i, h)),
        out_shape=jax.ShapeDtypeStruct((m, GROUP_WIDTH), BF16),
        scratch_shapes=[pltpu.VMEM((hp, ch, tile), F32), pltpu.VMEM((hp, ch, tile), F32)],
        compiler_params=_params(40 << 20, 3),
        name="mla_attn",
    )(q, k, vt, h2d)


def _hi_lo(v):
    hi = v.astype(BF16).astype(F32)
    return hi, v - hi


def _prep_dsa_kernel(q_ref, k_ref, v_ref, qi_ref, ki_ref, tab_ref,
                     q_out, k_out, vt_out, qi2_out, ki4_out, wt_out):
    def tab(t):
        return tab_ref[:, t * LANES:(t + 1) * LANES]

    scale = DSA_HEAD_DIM ** -0.5 * LOG2E
    q = q_ref[...]
    q_out[...] = (_rope(q, _tile_lanes(tab(TAB_DSA_C), DSA_HEADS), _tile_lanes(tab(TAB_DSA_S), DSA_HEADS),
                        DSA_ROT // 2, LANES) * scale).astype(BF16)
    k = k_ref[...]
    k_out[...] = _rope(k, _tile_lanes(tab(TAB_DSA_C), DSA_KV_HEADS), _tile_lanes(tab(TAB_DSA_S), DSA_KV_HEADS),
                       DSA_ROT // 2, LANES).astype(BF16)
    vt_out[...] = _vt_with_ones(v_ref[...].T, DSA_KV_HEADS, DSA_HEAD_DIM)
    reps = IDX_HEADS * IDX_DIM // LANES
    qi = _rope(qi_ref[...], _tile_lanes(tab(TAB_IDX_C), reps), _tile_lanes(tab(TAB_IDX_S), reps),
               IDX_ROT // 2, IDX_DIM)
    qb = DSA_QBLK
    first_half = lax.broadcasted_iota(I32, (qb, LANES), 1) < IDX_DIM
    for blk in range(q.shape[0] // qb):
        for pair in range(IDX_HEADS // 2):
            hi, lo = _hi_lo(qi[blk * qb:(blk + 1) * qb, pair * LANES:(pair + 1) * LANES])
            even = jnp.where(first_half, hi, pltpu.roll(lo, IDX_DIM, 1)).astype(BF16)
            odd = jnp.where(first_half, pltpu.roll(hi, IDX_DIM, 1), lo).astype(BF16)
            qi2_out[blk, pair, 0:qb, :] = even
            qi2_out[blk, pair, qb:2 * qb, :] = odd
    small = ki_ref[...]
    lane = lax.broadcasted_iota(I32, small.shape, 1)
    c_ki = jnp.where(lane < IDX_DIM, tab(TAB_IDX_C), 1.0)
    s_ki = jnp.where(lane < IDX_DIM, tab(TAB_IDX_S), 0.0)
    hi, lo = _hi_lo(_rope(small, c_ki, s_ki, IDX_ROT // 2, IDX_DIM))
    ki4_out[:, 0:LANES] = jnp.where(lane < IDX_DIM, hi, pltpu.roll(hi, IDX_DIM, 1)).astype(BF16)
    ki4_out[:, LANES:2 * LANES] = jnp.where(lane < IDX_DIM, lo, pltpu.roll(lo, IDX_DIM, 1)).astype(BF16)
    wt = small.T[IDX_DIM:IDX_DIM + IDX_HEADS, :]
    wt_out[...] = wt * (IDX_HEADS ** -0.5 * IDX_DIM ** -0.5)


def _prep_dsa(h2d, hs, tabs, batch, seq):
    tm = KEY_CHUNK
    m = batch * seq
    nt = seq // tm
    qpt = tm // DSA_QBLK
    kvw = DSA_KV_HEADS * DSA_HEAD_DIM

    def row(b, i):
        return b * nt + i

    return pl.pallas_call(
        _prep_dsa_kernel,
        grid=(batch, nt),
        in_specs=[
            pl.BlockSpec((tm, GROUP_WIDTH), lambda b, i: (row(b, i), COL_BQ // GROUP_WIDTH)),
            pl.BlockSpec((tm, kvw), lambda b, i: (row(b, i), COL_BK // kvw)),
            pl.BlockSpec((tm, kvw), lambda b, i: (row(b, i), COL_BV // kvw)),
            pl.BlockSpec((tm, GROUP_WIDTH), lambda b, i: (row(b, i), COL_QI // GROUP_WIDTH)),
            pl.BlockSpec((tm, LANES), lambda b, i: (row(b, i), SMALL_KI // LANES)),
            pl.BlockSpec((tm, N_TABS * LANES), lambda b, i: (row(b, i), 0)),
        ],
        out_specs=[
            pl.BlockSpec((tm, GROUP_WIDTH), lambda b, i: (row(b, i), 0)),
            pl.BlockSpec((tm, kvw), lambda b, i: (row(b, i), 0)),
            pl.BlockSpec((None, None, DSA_KV_HEADS * VT_ROWS, tm), lambda b, i: (b, i, 0, 0)),
            pl.BlockSpec((None, qpt, IDX_HEADS // 2, 2 * DSA_QBLK, 2 * IDX_DIM), lambda b, i: (b, i, 0, 0, 0)),
            pl.BlockSpec((tm, 4 * IDX_DIM), lambda b, i: (row(b, i), 0)),
            pl.BlockSpec((None, IDX_HEADS, tm), lambda b, i: (b, 0, i)),
        ],
        out_shape=[
            jax.ShapeDtypeStruct((m, GROUP_WIDTH), BF16),
            jax.ShapeDtypeStruct((m, kvw), BF16),
            jax.ShapeDtypeStruct((batch, nt, DSA_KV_HEADS * VT_ROWS, tm), BF16),
            jax.ShapeDtypeStruct((batch, seq // DSA_QBLK, IDX_HEADS // 2, 2 * DSA_QBLK, 2 * IDX_DIM), BF16),
            jax.ShapeDtypeStruct((m, 4 * IDX_DIM), BF16),
            jax.ShapeDtypeStruct((batch, IDX_HEADS, seq), F32),
        ],
        compiler_params=_params(24 << 20, 2),
        name="prep_dsa",
    )(h2d, h2d, h2d, h2d, hs, tabs)


def _sortable_key(score):
    bits = lax.bitcast_convert_type(score, I32)
    key = jnp.where(bits < 0, bits ^ jnp.int32(0x7FFFFFFF), bits)
    return jnp.where(score == 0.0, 0, key)


def _dsa_kernel(qi2_ref, ki4_ref, wt_ref, q_ref, k_ref, vt_ref, gate_ref, o_ref, key_ref, s0_ref, s1_ref, qi4_ref,
                *, topk, seq_bits, grp):
    i = pl.program_id(1)
    n_vis = i // 2
    n_chunks = n_vis + 1
    wt = wt_ref[...]
    ch = KEY_CHUNK
    qb = DSA_QBLK

    def rows(c):
        return pl.ds(pl.multiple_of(c * ch, ch), ch)

    for p in range(IDX_HEADS // 2):
        qi4_ref[p, :, 0:LANES] = qi2_ref[p]
        qi4_ref[p, :, LANES:2 * LANES] = qi2_ref[p]

    def score_chunk(c):
        kc = ki4_ref[rows(c), :]
        tot = None
        for p in range(IDX_HEADS // 2):
            s = lax.dot_general(kc, qi4_ref[p], _NT, preferred_element_type=F32)
            r = jnp.maximum(s, 0.0)
            t = r[:, :qb] * wt[2 * p:2 * p + 1, :] + r[:, qb:] * wt[2 * p + 1:2 * p + 2, :]
            tot = t if tot is None else tot + t
        return tot

    def phase1(c, carry):
        key_ref[rows(c), :] = _sortable_key(score_chunk(c))
        return carry

    lax.fori_loop(0, n_vis // 2, lambda p, cr: phase1(2 * p + 1, phase1(2 * p, cr)), 0)
    lax.fori_loop(0, n_vis & 1, lambda _, cr: phase1(n_vis - 1, cr), 0)
    kidx = n_vis * ch + lax.broadcasted_iota(I32, (ch, qb), 0)
    qidx = i * qb + lax.broadcasted_iota(I32, (ch, qb), 1)
    key_ref[rows(n_vis), :] = jnp.where(kidx <= qidx, _sortable_key(score_chunk(n_vis)), INT_MIN)

    n_groups = (n_chunks + grp - 1) // grp

    def pad_chunk(c, carry):
        key_ref[rows(c), :] = jnp.full((ch, qb), INT_MIN, I32)
        return carry

    lax.fori_loop(n_chunks, n_groups * grp, pad_chunk, 0)

    def count_ge(t):
        def body(p, acc):
            blk = key_ref[pl.ds(pl.multiple_of(p * grp * ch, grp * ch), grp * ch), :]
            hit = (blk >= t).astype(I32)
            return acc + jnp.sum(hit.reshape(grp * ch // 8, 8, qb), axis=0)
        acc = lax.fori_loop(0, n_groups, body, jnp.zeros((8, qb), I32))
        return jnp.sum(acc, axis=0, keepdims=True)

    c0 = count_ge(jnp.zeros((1, qb), I32))
    state = (jnp.where(c0 >= topk, 0, INT_MIN).astype(I32), c0)

    def bit_step(b, st):
        t, cnt = st
        cand = t + jnp.left_shift(jnp.int32(1), 30 - b)
        c = count_ge(cand)
        ok = c >= topk
        return jnp.where(ok, cand, t), jnp.where(ok, c, cnt)

    def settled(st):
        t, cnt = st
        return jnp.min(jnp.logical_or(cnt == topk, t == INT_MIN).astype(I32)) > 0

    def finish(st):
        t, cnt = st
        need = topk - count_ge(t + 1)
        return t, need, jnp.logical_and(cnt > topk, t > INT_MIN).astype(I32)

    def search(st, stops):
        if not stops:
            return finish(st)
        hi_bit = 30 if len(stops) == len(SEARCH_STOPS) else SEARCH_STOPS[len(SEARCH_STOPS) - len(stops) - 1] - 1
        st = lax.fori_loop(30 - hi_bit, 31 - stops[0], bit_step, st)
        if stops[0] == 0:
            return finish(st)
        return lax.cond(settled(st), lambda s_: (s_[0], jnp.ones((1, qb), I32), jnp.zeros((1, qb), I32)),
                        lambda s_: search(s_, stops[1:]), st)

    thr, need, tied = search(state, SEARCH_STOPS)

    def count(pred):
        def body(c, acc):
            idx = c * ch + lax.broadcasted_iota(I32, (ch, qb), 0)
            hit = pred(key_ref[rows(c), :], idx).astype(I32)
            return acc + jnp.sum(hit.reshape(ch // 8, 8, qb), axis=0)
        acc = lax.fori_loop(0, n_chunks, body, jnp.zeros((8, qb), I32))
        return jnp.sum(acc, axis=0, keepdims=True)

    def tie_search():
        def step(b, lo):
            cand = lo + jnp.left_shift(jnp.int32(1), seq_bits - 1 - b)
            below = count(lambda keys, idx: jnp.logical_and(keys == thr, idx < cand))
            return jnp.where(below < need, cand, lo)
        return lax.fori_loop(0, seq_bits, step, jnp.zeros((1, qb), I32))

    last_eq = lax.cond(jnp.max(tied) > 0, tie_search,
                       lambda: jnp.full((1, qb), 2 ** seq_bits, I32))
    thr_sel = jnp.maximum(thr, INT_MIN + 1)

    nq = DSA_GROUP * qb
    one_hot = (lax.broadcasted_iota(I32, (nq, qb), 0) % qb == lax.broadcasted_iota(I32, (nq, qb), 1))
    one_hot = jnp.where(one_hot, 1.0, 0.0).astype(BF16)
    qs = [jnp.concatenate([jnp.concatenate([q_ref[:, (n * DSA_GROUP + g) * LANES:(n * DSA_GROUP + g + 1) * LANES]
                                            for g in range(DSA_GROUP)], axis=0), one_hot], axis=1)
          for n in range(DSA_KV_HEADS)]

    slots = (s0_ref, s1_ref)

    def produce(c, slot):
        keys = key_ref[rows(c), :]
        idx = c * ch + lax.broadcasted_iota(I32, (ch, qb), 0)
        drop = jnp.logical_and(keys == thr, idx > last_eq)
        sel = jnp.logical_and(keys >= thr_sel, jnp.logical_not(drop))
        neg = jnp.where(sel, 0.0, NEG_BIG).astype(BF16)
        col_max = []
        for n in range(DSA_KV_HEADS):
            kc = jnp.concatenate([k_ref[rows(c), n * LANES:(n + 1) * LANES], neg], axis=1)
            s = lax.dot_general(kc, qs[n], _NT, preferred_element_type=F32)
            slots[slot][n] = s
            col_max.append(jnp.max(s, axis=0, keepdims=True))
        return tuple(col_max)

    def consume(c, slot, col_max, carry):
        return tuple(_softmax_step(slots[slot][n], col_max[n],
                                   vt_ref[c, n * VT_ROWS:(n + 1) * VT_ROWS, :], *carry[n])
                     for n in range(DSA_KV_HEADS))

    def phase3(j, state):
        carry, col0 = state
        col1 = produce(2 * j + 1, 1)
        carry = consume(2 * j, 0, col0, carry)
        col0 = produce(2 * j + 2, 0)
        carry = consume(2 * j + 1, 1, col1, carry)
        return carry, col0

    init = tuple((jnp.full((1, nq), NEG_BIG, F32), jnp.zeros((VT_ROWS, nq), F32))
                 for _ in range(DSA_KV_HEADS))
    pairs = (n_chunks - 1) // 2
    state = lax.fori_loop(0, pairs // 2, lambda jj, st: phase3(2 * jj + 1, phase3(2 * jj, st)),
                          (init, produce(0, 0)))
    carry, col0 = lax.fori_loop(0, pairs & 1, lambda _, st: phase3(pairs - 1, st), state)
    last = 2 * pairs

    def tail_two(cr):
        col1 = produce(last + 1, 1)
        cr = consume(last, 0, col0, cr)
        return consume(last + 1, 1, col1, cr)

    res = lax.cond(n_chunks - 1 > last, tail_two, lambda cr: consume(last, 0, col0, cr), carry)
    for n in range(DSA_KV_HEADS):
        o = _normalised_t(res[n][1], DSA_HEAD_DIM)
        for g in range(DSA_GROUP):
            col = (n * DSA_GROUP + g) * LANES
            o_ref[:, col:col + LANES] = (o[g * qb:(g + 1) * qb, :]
                                         * _silu(gate_ref[:, col:col + LANES])).astype(o_ref.dtype)


def _dsa(qi2, ki4, wt, q, k, vt, h2d, batch, seq):
    m = batch * seq
    nq = seq // DSA_QBLK
    nck = seq // KEY_CHUNK
    kvw = DSA_KV_HEADS * DSA_HEAD_DIM
    topk = min(TOPK_MAX, seq // 4)
    seq_bits = int(np.log2(seq))
    assert 2 ** seq_bits == seq and nck % 2 == 0 and topk <= KEY_CHUNK
    return pl.pallas_call(
        functools.partial(_dsa_kernel, topk=topk, seq_bits=seq_bits, grp=2),
        grid=(batch, nq),
        in_specs=[
            pl.BlockSpec((None, None, IDX_HEADS // 2, 2 * DSA_QBLK, 2 * IDX_DIM),
                         lambda b, i: (b, i, 0, 0, 0)),
            pl.BlockSpec((seq, 4 * IDX_DIM), lambda b, i: (b, 0)),
            pl.BlockSpec((None, IDX_HEADS, DSA_QBLK), lambda b, i: (b, 0, i)),
            pl.BlockSpec((DSA_QBLK, GROUP_WIDTH), lambda b, i: (b * nq + i, 0)),
            pl.BlockSpec((seq, kvw), lambda b, i: (b, 0)),
            pl.BlockSpec((None, nck, DSA_KV_HEADS * VT_ROWS, KEY_CHUNK), lambda b, i: (b, 0, 0, 0)),
            pl.BlockSpec((DSA_QBLK, GROUP_WIDTH), lambda b, i: (b * nq + i, COL_BG // GROUP_WIDTH)),
        ],
        out_specs=pl.BlockSpec((DSA_QBLK, GROUP_WIDTH), lambda b, i: (b * nq + i, 0)),
        out_shape=jax.ShapeDtypeStruct((m, GROUP_WIDTH), BF16),
        scratch_shapes=[pltpu.VMEM((seq, DSA_QBLK), I32),
                        pltpu.VMEM((DSA_KV_HEADS, KEY_CHUNK, DSA_GROUP * DSA_QBLK), F32),
                        pltpu.VMEM((DSA_KV_HEADS, KEY_CHUNK, DSA_GROUP * DSA_QBLK), F32),
                        pltpu.VMEM((IDX_HEADS // 2, 2 * DSA_QBLK, 4 * IDX_DIM), BF16)],
        compiler_params=_params(32 << 20, 2),
        name="dsa",
    )(qi2, ki4, wt, q, k, vt, h2d)


def _rglru_kernel(u_ref, gate_ref, cw_ref, cb_ref, wa_ref, ba_ref, wx_ref, bx_ref, lam_ref,
                  o_ref, halo_ref, h_ref, a_scr, x_scr, y_scr, *, batch, tt):
    t = pl.program_id(0)

    @pl.when(t == 0)
    def _():
        halo_ref[...] = jnp.zeros_like(halo_ref)
        h_ref[...] = jnp.zeros_like(h_ref)

    lam = lam_ref[...]
    z = -lam
    softplus = jnp.maximum(z, 0.0) + jnp.log1p(jnp.exp(-jnp.abs(z)))
    row = lax.broadcasted_iota(I32, (tt, LRU_WIDTH), 0)
    for b in range(batch):
        u = u_ref[b]
        prev = halo_ref[b]
        p1, p2, p3 = prev[7:8, :], prev[6:7, :], prev[5:6, :]
        s1 = jnp.where(row == 0, p1, pltpu.roll(u, 1, 0))
        s2 = jnp.where(row == 0, p2, jnp.where(row == 1, p1, pltpu.roll(u, 2, 0)))
        s3 = jnp.where(row == 0, p3, jnp.where(row == 1, p2, jnp.where(row == 2, p1, pltpu.roll(u, 3, 0))))
        conv = (cw_ref[3:4, :] * u + cw_ref[2:3, :] * s1 + cw_ref[1:2, :] * s2 + cw_ref[0:1, :] * s3
                + cb_ref[...])
        halo_ref[b] = u[tt - 8:, :]
        cb16 = conv.astype(BF16)
        r = _sigmoid(jnp.dot(cb16, wa_ref[...], preferred_element_type=F32) + ba_ref[...])
        ig = _sigmoid(jnp.dot(cb16, wx_ref[...], preferred_element_type=F32) + bx_ref[...])
        log_a = -LRU_C * r * softplus
        a_scr[b] = jnp.exp(log_a)
        th = jnp.tanh(log_a)
        x_scr[b] = jnp.sqrt(-2.0 * th / (1.0 - th)) * (ig * conv)

    def step(s, h):
        h = a_scr[:, pl.ds(s, 1), :] * h + x_scr[:, pl.ds(s, 1), :]
        y_scr[:, pl.ds(s, 1), :] = h
        return h

    h_ref[...] = lax.fori_loop(0, tt, step, h_ref[...], unroll=8)
    o_ref[...] = (y_scr[...] * _silu(gate_ref[...])).astype(o_ref.dtype)


def _rglru(h3d, conv_w, conv_b, wa_bd, b_a, wx_bd, b_x, lam, batch, seq, tt):
    nt = seq // tt
    w = LRU_WIDTH

    def vec(r):
        return pl.BlockSpec((r, w), lambda t: (0, 0))

    return pl.pallas_call(
        functools.partial(_rglru_kernel, batch=batch, tt=tt),
        grid=(nt,),
        in_specs=[
            pl.BlockSpec((batch, tt, w), lambda t: (0, t, COL_U // w)),
            pl.BlockSpec((batch, tt, w), lambda t: (0, t, COL_CG // w)),
            vec(CONV_WIDTH), vec(1),
            pl.BlockSpec((w, w), lambda t: (0, 0)), vec(1),
            pl.BlockSpec((w, w), lambda t: (0, 0)), vec(1),
            vec(1),
        ],
        out_specs=pl.BlockSpec((batch, tt, w), lambda t: (0, t, 0)),
        out_shape=jax.ShapeDtypeStruct((batch, seq, w), BF16),
        scratch_shapes=[pltpu.VMEM((batch, 8, w), F32), pltpu.VMEM((batch, 1, w), F32),
                        pltpu.VMEM((batch, tt, w), F32), pltpu.VMEM((batch, tt, w), F32),
                        pltpu.VMEM((batch, tt, w), F32)],
        compiler_params=_params(batch * tt * w * 36 + (20 << 20), 1),
        name="rglru",
    )(h3d, h3d, conv_w, conv_b, wa_bd, b_a, wx_bd, b_x, lam)


def _mem_attn_kernel(q_ref, gate_ref, km_ref, vm_ref, o_ref):
    scale = MEM_HEAD_DIM ** -0.5
    for h in range(MEM_HEADS):
        cols = slice(h * MEM_HEAD_DIM, (h + 1) * MEM_HEAD_DIM)
        qh = (q_ref[:, cols] * scale).astype(BF16)
        s = lax.dot_general(qh, km_ref[:, cols], _NT, preferred_element_type=F32)
        p = jnp.exp(s - jnp.max(s, axis=-1, keepdims=True))
        l = jnp.sum(p, axis=-1, keepdims=True)
        o = jnp.dot(p.astype(BF16), vm_ref[:, cols], preferred_element_type=F32) / l
        o_ref[:, cols] = (o * _silu(gate_ref[:, cols])).astype(o_ref.dtype)


def _mem_attn(h2d, km, vm, batch, seq, tm):
    m = batch * seq
    nt = seq // tm
    w = GROUP_WIDTH
    return pl.pallas_call(
        _mem_attn_kernel,
        grid=(batch, nt),
        in_specs=[
            pl.BlockSpec((tm, w), lambda b, i: (b * nt + i, COL_DQ // w)),
            pl.BlockSpec((tm, w), lambda b, i: (b * nt + i, COL_DG // w)),
            pl.BlockSpec((N_MEM, w), lambda b, i: (b, 0)),
            pl.BlockSpec((N_MEM, w), lambda b, i: (b, 0)),
        ],
        out_specs=pl.BlockSpec((tm, w), lambda b, i: (b * nt + i, 0)),
        out_shape=jax.ShapeDtypeStruct((m, w), BF16),
        compiler_params=_params(24 << 20, 2),
        name="mem_attn",
    )(h2d, h2d, km, vm)


def _out_ln_kernel(*refs, n_lhs, kg, nj, per_emit, d_model):
    lhs = refs[:n_lhs]
    w_ref, x_ref, g_ref, b_ref, o_ref, o16_ref, z_ref, mu_ref, rs_ref = refs[n_lhs:]
    j = pl.program_id(1)

    @pl.when(j < nj)
    def _():
        acc = None
        for g in range(n_lhs):
            part = jnp.dot(lhs[g][...], w_ref[g * kg:(g + 1) * kg, :].astype(BF16), preferred_element_type=F32)
            acc = part if acc is None else acc + part
        z_ref[j] = DEEPNORM_ALPHA * x_ref[...] + acc

    @pl.when(j == nj)
    def _():
        def lane_tiles(v):
            tiles = [v[:, t * LANES:(t + 1) * LANES] for t in range(v.shape[1] // LANES)]
            return functools.reduce(lambda a, b: a + b, tiles)

        tot = functools.reduce(lambda a, b: a + b, [lane_tiles(z_ref[jj]) for jj in range(nj)])
        mu = jnp.sum(tot, axis=-1, keepdims=True) * (1.0 / d_model)
        sq = None
        for jj in range(nj):
            dlt = z_ref[jj] - mu
            part = lane_tiles(dlt * dlt)
            sq = part if sq is None else sq + part
        mu_ref[...] = mu
        rs_ref[...] = lax.rsqrt(jnp.sum(sq, axis=-1, keepdims=True) * (1.0 / d_model) + LN_EPS)

    @pl.when(j >= nj)
    def _():
        e = j - nj
        tn = z_ref.shape[2]
        for t in range(per_emit):
            cols = slice(t * tn, (t + 1) * tn)
            out = (z_ref[e * per_emit + t] - mu_ref[...]) * rs_ref[...] * g_ref[:, cols] + b_ref[:, cols]
            o_ref[:, cols] = out
            o16_ref[:, cols] = out.astype(BF16)


def _out_proj_deepnorm(lhs_list, w, layer, x, ln_g, ln_b, tm, tn, te):
    m, kg = lhs_list[0].shape
    _, k, n = w.shape
    n_lhs = len(lhs_list)
    nj, ne, per_emit = n // tn, n // te, te // tn
    assert k == kg * n_lhs and m % tm == 0 and n % te == 0 and te % tn == 0

    def col(j):
        return jnp.minimum(j, nj - 1)

    def emit(j):
        return jnp.maximum(j - nj, 0)

    est = (nj * tm * tn * 4 + 2 * (tm * k * 2 + k * tn * 4 + tm * tn * 4 + tm * te * 6) + tm * tn * 8)
    return pl.pallas_call(
        functools.partial(_out_ln_kernel, n_lhs=n_lhs, kg=kg, nj=nj, per_emit=per_emit, d_model=n),
        grid=(m // tm, nj + ne),
        in_specs=[pl.BlockSpec((tm, kg), lambda i, j: (i, 0)) for _ in range(n_lhs)]
        + [pl.BlockSpec((None, k, tn), lambda i, j: (layer, 0, col(j))),
           pl.BlockSpec((tm, tn), lambda i, j: (i, col(j))),
           pl.BlockSpec((1, te), lambda i, j: (0, emit(j))),
           pl.BlockSpec((1, te), lambda i, j: (0, emit(j)))],
        out_specs=[pl.BlockSpec((tm, te), lambda i, j: (i, emit(j))),
                   pl.BlockSpec((tm, te), lambda i, j: (i, emit(j)))],
        out_shape=[jax.ShapeDtypeStruct((m, n), F32), jax.ShapeDtypeStruct((m, n), BF16)],
        scratch_shapes=[pltpu.VMEM((nj, tm, tn), F32), pltpu.VMEM((tm, 1), F32), pltpu.VMEM((tm, 1), F32)],
        compiler_params=_params(est + (10 << 20), 2),
        name="out_proj_deepnorm",
    )(*lhs_list, w, x, ln_g, ln_b)


def _rope_tabs(positions):
    pos = positions.astype(F32)
    b, s = pos.shape

    def one(rot, period):
        inv = ROPE_THETA ** (-jnp.arange(0, rot, 2, dtype=F32) / rot)
        ang = pos[:, :, None] * inv
        c, sn = jnp.cos(ang), jnp.sin(ang)
        rest = period - rot
        cp = jnp.concatenate([c, c, jnp.ones((b, s, rest), F32)], axis=-1)
        sp = jnp.concatenate([-sn, sn, jnp.zeros((b, s, rest), F32)], axis=-1)
        reps = LANES // period
        return jnp.tile(cp, (1, 1, reps)), jnp.tile(sp, (1, 1, reps))

    parts = one(MLA_ROPE, LANES) + one(DSA_ROT, LANES) + one(IDX_ROT, IDX_DIM)
    return jnp.concatenate(parts, axis=-1).reshape(b * s, N_TABS * LANES)


def _block_diag(w):
    l = w.shape[0]
    eye = jnp.eye(LRU_BLOCKS, dtype=w.dtype)
    bd = jnp.einsum('lnde,nm->lndme', w, eye)
    return bd.reshape(l, LRU_WIDTH, LRU_WIDTH).astype(BF16)


def _layer(x, x16, mem16, tabs, lw, l, batch, seq):
    m = batch * seq
    main_rows, small_rows = _in_proj_row_tables(512)
    h2d = _in_proj(x16, lw["w_in_t"], l, main_rows, tm=min(1024, m), tn=512, name="in_proj")
    hs = _in_proj_small(x16, lw["w_in_t"], l, int(small_rows[0]), int(small_rows[1]), tm=min(2048, m))
    tile_a = min(512, seq)
    qa, ka, vta = _prep_mla(h2d, hs, tabs, lw["g_cq"], lw["g_ckv"], lw["w_uq"], lw["w_uk"], lw["w_uvt"],
                            batch, seq, tile_a)
    ya = _mla_attn(qa, ka, vta, h2d, batch, seq, tile_a)
    qb, kb, vtb, qi2, ki4, wt = _prep_dsa(h2d, hs, tabs, batch, seq)
    yb = _dsa(qi2, ki4, wt, qb, kb, vtb, h2d, batch, seq)
    yc = _rglru(h2d.reshape(batch, seq, N_MAIN), lw["conv_w"], lw["conv_b"], lw["w_rg_a"], lw["b_rg_a"],
                lw["w_rg_x"], lw["b_rg_x"], lw["lam"], batch, seq, min(512, seq))
    yc = yc.reshape(m, GROUP_WIDTH)
    km = _matmul([mem16], lw["w_mem_k"], l, BF16, tm=mem16.shape[0], tn=256, name="mem_proj_k")
    vm = _matmul([mem16], lw["w_mem_v"], l, BF16, tm=mem16.shape[0], tn=256, name="mem_proj_v")
    yd = _mem_attn(h2d, km, vm, batch, seq, min(1024, seq))
    return _out_proj_deepnorm([ya, yb, yc, yd], lw["w_o"], l, x, lw["ln_g"], lw["ln_b"],
                              tm=min(1024, m), tn=256, te=512)


def kernel(x, mem, positions, w_in, g_cq, g_ckv, w_uq, w_ukv, conv_w, conv_b, w_rg_a, b_rg_a, w_rg_x,
           b_rg_x, lru_lambda, w_mem_k, w_mem_v, w_o, ln_g, ln_b):
    batch, seq, d = x.shape
    depth = w_in.shape[0]
    tabs = _rope_tabs(positions)
    w_in_t = w_in.transpose(0, 2, 1)
    w_uq_p = jnp.pad(w_uq.reshape(depth, MLA_Q_LORA, MLA_HEADS, MLA_NOPE + MLA_ROPE),
                     [(0, 0), (0, 0), (0, 0), (0, 2 * LANES - MLA_NOPE - MLA_ROPE)])
    w_uq_p = w_uq_p.reshape(depth, MLA_Q_LORA, MLA_HEADS * 2 * LANES).astype(BF16)
    w_ukv4 = w_ukv.reshape(depth, MLA_KV_LORA, MLA_HEADS, MLA_NOPE + MLA_V)
    w_uk = w_ukv4[..., :MLA_NOPE].reshape(depth, MLA_KV_LORA, MLA_HEADS * MLA_NOPE).astype(BF16)
    w_uvt = w_ukv4[..., MLA_NOPE:].reshape(depth, MLA_KV_LORA, MLA_HEADS * MLA_V)
    w_uvt = w_uvt.transpose(0, 2, 1).astype(BF16)
    wa_bd = _block_diag(w_rg_a)
    wx_bd = _block_diag(w_rg_x)
    mem16 = mem.reshape(batch * mem.shape[1], d).astype(BF16)

    xf = x.reshape(batch * seq, d)
    x16 = xf.astype(BF16)
    for l in range(depth):
        lw = dict(w_in_t=w_in_t, g_cq=g_cq[l][None], g_ckv=g_ckv[l][None], w_uq=w_uq_p[l], w_uk=w_uk[l],
                  w_uvt=w_uvt[l], conv_w=conv_w[l], conv_b=conv_b[l][None], w_rg_a=wa_bd[l],
                  b_rg_a=b_rg_a[l][None], w_rg_x=wx_bd[l], b_rg_x=b_rg_x[l][None], lam=lru_lambda[l][None],
                  w_mem_k=w_mem_k, w_mem_v=w_mem_v, w_o=w_o, ln_g=ln_g[l][None], ln_b=ln_b[l][None])
        xf, x16 = _layer(xf, x16, mem16, tabs, lw, l, batch, seq)
    return xf.reshape(batch, seq, d)
```

```python
import functools

import numpy as np
import jax
import jax.numpy as jnp
from jax import lax
from jax.experimental import pallas as pl
from jax.experimental.pallas import tpu as pltpu

F32 = jnp.float32
BF16 = jnp.bfloat16
I32 = jnp.int32

DEPTH = 4
D_MODEL = 4096
N_MEM = 256
GROUP_WIDTH = D_MODEL // 4
ROPE_THETA = 500000.0
MLA_HEADS = 8
MLA_NOPE = 128
MLA_ROPE = 64
MLA_V = 128
MLA_Q_LORA = GROUP_WIDTH
MLA_KV_LORA = GROUP_WIDTH // 2
DSA_HEADS = 8
DSA_KV_HEADS = 2
DSA_GROUP = DSA_HEADS // DSA_KV_HEADS
DSA_HEAD_DIM = 128
DSA_ROT = DSA_HEAD_DIM // 4
IDX_HEADS = 16
IDX_DIM = 64
IDX_ROT = IDX_DIM // 4
TOPK_MAX = 256
LRU_WIDTH = GROUP_WIDTH
LRU_BLOCKS = 16
LRU_BLOCK_DIM = LRU_WIDTH // LRU_BLOCKS
CONV_WIDTH = 4
LRU_C = 8.0
MEM_HEADS = 4
MEM_HEAD_DIM = GROUP_WIDTH // MEM_HEADS
DEEPNORM_ALPHA = (2 * DEPTH) ** 0.25
LN_EPS = 1e-5
RMS_EPS = 1e-6

IN_SIZES = (
    MLA_Q_LORA, MLA_KV_LORA, MLA_ROPE, GROUP_WIDTH,
    GROUP_WIDTH, DSA_KV_HEADS * DSA_HEAD_DIM, DSA_KV_HEADS * DSA_HEAD_DIM,
    IDX_HEADS * IDX_DIM, IDX_DIM, IDX_HEADS, GROUP_WIDTH,
    LRU_WIDTH, GROUP_WIDTH,
    GROUP_WIDTH, GROUP_WIDTH,
)

LANES = 128
V7X_VMEM_BUDGET = 60 * 1024 * 1024

COL_CQ = 0
COL_AG = 1024
COL_BQ = 2048
COL_QI = 3072
COL_BG = 4096
COL_U = 5120
COL_CG = 6144
COL_DQ = 7168
COL_DG = 8192
COL_CKV = 9216
COL_BK = 9728
COL_BV = 9984
N_MAIN = 10240
SMALL_KR = 0
SMALL_KI = 128

TAB_MLA_C, TAB_MLA_S, TAB_DSA_C, TAB_DSA_S, TAB_IDX_C, TAB_IDX_S = range(6)
N_TABS = 6

NEG_BIG = -1e30
INT_MIN = -2 ** 31
LOG2E = 1.4426950408889634
KEY_CHUNK = 256
DSA_QBLK = 128
SEARCH_STOPS = (11, 8, 6, 3, 0)
VT_ROWS = 128 + 16
_NT = (((1,), (1,)), ((), ()))


def _vmem_limit(nbytes):
    return int(min(V7X_VMEM_BUDGET, max(nbytes, 16 * 1024 * 1024)))


def _params(nbytes, ndims):
    return pltpu.CompilerParams(dimension_semantics=("arbitrary",) * ndims,
                                vmem_limit_bytes=_vmem_limit(nbytes))


def _mm_kernel(*refs, n_lhs, kg):
    w_ref = refs[n_lhs]
    o_ref = refs[n_lhs + 1]
    acc = None
    for g in range(n_lhs):
        part = jnp.dot(refs[g][...], w_ref[g * kg:(g + 1) * kg, :].astype(BF16), preferred_element_type=F32)
        acc = part if acc is None else acc + part
    o_ref[...] = acc.astype(o_ref.dtype)


def _matmul(lhs_list, w, layer, out_dtype, tm, tn, name):
    m, kg = lhs_list[0].shape
    _, k, n = w.shape
    n_lhs = len(lhs_list)
    assert k == kg * n_lhs and m % tm == 0 and n % tn == 0
    est = 2 * (tm * k * 2 + k * tn * w.dtype.itemsize + tm * tn * jnp.dtype(out_dtype).itemsize) + tm * tn * 8
    return pl.pallas_call(
        functools.partial(_mm_kernel, n_lhs=n_lhs, kg=kg),
        grid=(m // tm, n // tn),
        in_specs=[pl.BlockSpec((tm, kg), lambda i, j: (i, 0)) for _ in range(n_lhs)]
        + [pl.BlockSpec((None, k, tn), lambda i, j: (layer, 0, j))],
        out_specs=pl.BlockSpec((tm, tn), lambda i, j: (i, j)),
        out_shape=jax.ShapeDtypeStruct((m, n), out_dtype),
        compiler_params=_params(est + (8 << 20), 2),
        name=name,
    )(*lhs_list, w)


def _in_proj_kernel(tab_ref, x_ref, w_ref, o_ref):
    del tab_ref
    o_ref[...] = lax.dot_general(x_ref[...], w_ref[...].astype(BF16), _NT, preferred_element_type=F32)


def _in_proj(x16, w_t, layer, row_offsets, tm, tn, name):
    m, k = x16.shape
    nblk = len(row_offsets)
    est = 2 * (tm * k * 2 + tn * k * 4 + tm * tn * 4) + tm * tn * 8
    grid_spec = pltpu.PrefetchScalarGridSpec(
        num_scalar_prefetch=1,
        grid=(m // tm, nblk),
        in_specs=[
            pl.BlockSpec((tm, k), lambda i, j, tab: (i, 0)),
            pl.BlockSpec((pl.Squeezed(), pl.Element(tn), pl.Element(k)),
                         lambda i, j, tab: (layer, pl.multiple_of(tab[j], 16), 0)),
        ],
        out_specs=pl.BlockSpec((tm, tn), lambda i, j, tab: (i, j)),
    )
    return pl.pallas_call(
        _in_proj_kernel,
        grid_spec=grid_spec,
        out_shape=jax.ShapeDtypeStruct((m, nblk * tn), F32),
        compiler_params=_params(est + (8 << 20), 2),
        name=name,
    )(jnp.asarray(row_offsets, I32), x16, w_t)


def _in_proj_small_kernel(x_ref, wa_ref, wb_ref, o_ref):
    w = jnp.concatenate([wa_ref[...], wb_ref[...]], axis=0).astype(BF16)
    o_ref[...] = lax.dot_general(x_ref[...], w, _NT, preferred_element_type=F32)


def _in_proj_small(x16, w_t, layer, rows_a, rows_b, tm):
    m, k = x16.shape

    def window(r0):
        return pl.BlockSpec((pl.Squeezed(), pl.Element(LANES), pl.Element(k)), lambda i: (layer, r0, 0))

    est = 2 * (tm * k * 2 + 2 * LANES * k * 4 + tm * 2 * LANES * 4) + 2 * LANES * k * 6
    return pl.pallas_call(
        _in_proj_small_kernel,
        grid=(m // tm,),
        in_specs=[pl.BlockSpec((tm, k), lambda i: (i, 0)), window(rows_a), window(rows_b)],
        out_specs=pl.BlockSpec((tm, 2 * LANES), lambda i: (i, 0)),
        out_shape=jax.ShapeDtypeStruct((m, 2 * LANES), F32),
        compiler_params=_params(est + (8 << 20), 1),
        name="in_proj_small",
    )(x16, w_t, w_t)


def _in_proj_row_tables(tn):
    names = ("cq", "ckv", "kr", "ag", "bq", "bk", "bv", "qi", "ki", "wi", "bg", "u", "cg", "dq", "dg")
    src = dict(zip(names, [0] + [int(c) for c in np.cumsum(IN_SIZES)[:-1]]))
    size = dict(zip(names, IN_SIZES))
    order = [("cq", COL_CQ), ("ag", COL_AG), ("bq", COL_BQ), ("qi", COL_QI), ("bg", COL_BG), ("u", COL_U),
             ("cg", COL_CG), ("dq", COL_DQ), ("dg", COL_DG), ("ckv", COL_CKV), ("bk", COL_BK), ("bv", COL_BV)]
    merged = []
    for n, dst in order:
        if merged and merged[-1][0] + merged[-1][2] == src[n] and merged[-1][1] + merged[-1][2] == dst:
            merged[-1] = (merged[-1][0], merged[-1][1], merged[-1][2] + size[n])
        else:
            merged.append((src[n], dst, size[n]))
    main = []
    for s0, dst, width in merged:
        assert dst == len(main) * tn and width % tn == 0 and s0 % 16 == 0
        main += [s0 + t * tn for t in range(width // tn)]
    assert len(main) * tn == N_MAIN and src["wi"] == src["ki"] + size["ki"]
    return np.asarray(main, np.int32), np.asarray([src["kr"], src["ki"]], np.int32)


def _rope(v, c_tab, s_tab, half, period):
    width = v.shape[-1]
    lane = lax.broadcasted_iota(I32, v.shape, 1) & (period - 1)
    swapped = jnp.where(lane < half, pltpu.roll(v, width - half, 1), pltpu.roll(v, half, 1))
    return v * c_tab + swapped * s_tab


def _tile_lanes(t, reps):
    return t if reps == 1 else jnp.concatenate([t] * reps, axis=1)


def _rms(x, g):
    return x * lax.rsqrt(jnp.mean(x * x, axis=-1, keepdims=True) + RMS_EPS) * g


def _sigmoid(v):
    return 0.5 * jnp.tanh(0.5 * v) + 0.5


def _silu(g):
    return g * _sigmoid(g)


def _prep_mla_kernel(cq_ref, ckv_ref, kr_ref, tab_ref, gcq_ref, gckv_ref, wuq_ref, wk_ref, wvt_ref,
                     q_out, k_out, vt_out):
    scale = (MLA_NOPE + MLA_ROPE) ** -0.5 * LOG2E
    nq = _rms(cq_ref[...], gcq_ref[...]).astype(BF16)
    nkv = _rms(ckv_ref[...], gckv_ref[...]).astype(BF16)
    c_tab = tab_ref[:, TAB_MLA_C * LANES:(TAB_MLA_C + 1) * LANES]
    s_tab = tab_ref[:, TAB_MLA_S * LANES:(TAB_MLA_S + 1) * LANES]
    qf = jnp.dot(nq, wuq_ref[...], preferred_element_type=F32) * scale
    kn = jnp.dot(nkv, wk_ref[...], preferred_element_type=F32)
    lane = lax.broadcasted_iota(I32, kr_ref.shape, 1)
    kr = jnp.where(lane < MLA_ROPE, kr_ref[...], 0.0)
    kr = _rope(kr, c_tab, s_tab, MLA_ROPE // 2, LANES).astype(BF16)
    for h in range(MLA_HEADS):
        base = h * 2 * LANES
        q_out[:, base:base + LANES] = qf[:, base:base + LANES].astype(BF16)
        q_out[:, base + LANES:base + 2 * LANES] = _rope(
            qf[:, base + LANES:base + 2 * LANES], c_tab, s_tab, MLA_ROPE // 2, LANES).astype(BF16)
        k_out[:, base:base + LANES] = kn[:, h * LANES:(h + 1) * LANES].astype(BF16)
        k_out[:, base + LANES:base + 2 * LANES] = kr
    vt = lax.dot_general(wvt_ref[...], nkv, _NT, preferred_element_type=F32)
    half = vt.shape[1] // 2
    vt_out[0] = _vt_with_ones(vt[:, :half], MLA_HEADS, MLA_V)
    vt_out[1] = _vt_with_ones(vt[:, half:], MLA_HEADS, MLA_V)


def _prep_mla(h2d, hs, tabs, g_cq, g_ckv, wuq_p, wk, wvt, batch, seq, tm):
    m = batch * seq
    nt = seq // tm
    hd = MLA_HEADS * 2 * LANES
    return pl.pallas_call(
        _prep_mla_kernel,
        grid=(batch, nt),
        in_specs=[
            pl.BlockSpec((tm, MLA_Q_LORA), lambda b, i: (b * nt + i, COL_CQ // MLA_Q_LORA)),
            pl.BlockSpec((tm, MLA_KV_LORA), lambda b, i: (b * nt + i, COL_CKV // MLA_KV_LORA)),
            pl.BlockSpec((tm, LANES), lambda b, i: (b * nt + i, SMALL_KR // LANES)),
            pl.BlockSpec((tm, N_TABS * LANES), lambda b, i: (b * nt + i, 0)),
            pl.BlockSpec((1, MLA_Q_LORA), lambda b, i: (0, 0)),
            pl.BlockSpec((1, MLA_KV_LORA), lambda b, i: (0, 0)),
            pl.BlockSpec((MLA_Q_LORA, hd), lambda b, i: (0, 0)),
            pl.BlockSpec((MLA_KV_LORA, MLA_HEADS * MLA_NOPE), lambda b, i: (0, 0)),
            pl.BlockSpec((MLA_HEADS * MLA_V, MLA_KV_LORA), lambda b, i: (0, 0)),
        ],
        out_specs=[
            pl.BlockSpec((tm, hd), lambda b, i: (b * nt + i, 0)),
            pl.BlockSpec((tm, hd), lambda b, i: (b * nt + i, 0)),
            pl.BlockSpec((None, 2, MLA_HEADS * VT_ROWS, tm // 2), lambda b, i: (b, i, 0, 0)),
        ],
        out_shape=[
            jax.ShapeDtypeStruct((m, hd), BF16),
            jax.ShapeDtypeStruct((m, hd), BF16),
            jax.ShapeDtypeStruct((batch, 2 * nt, MLA_HEADS * VT_ROWS, tm // 2), BF16),
        ],
        compiler_params=_params(40 << 20, 2),
        name="prep_mla",
    )(h2d, h2d, hs, tabs, g_cq, g_ckv, wuq_p, wk, wvt)


def _softmax_step(s, col_max, vt_chunk, m, acc):
    m_new = jnp.maximum(m, col_max)
    alpha = jnp.exp2(m - m_new)
    p = jnp.exp2(s - m_new)
    acc_new = alpha * acc + jnp.dot(vt_chunk, p.astype(BF16), preferred_element_type=F32)
    return m_new, acc_new


def _normalised_t(acc, dv):
    return (acc[:dv, :] / acc[dv:dv + 1, :]).T


def _vt_with_ones(vt, heads, dv):
    ones = jnp.ones((VT_ROWS - dv, vt.shape[1]), BF16)
    parts = []
    for h in range(heads):
        parts += [vt[h * dv:(h + 1) * dv, :].astype(BF16), ones]
    return jnp.concatenate(parts, axis=0)


def _mla_attn_kernel(q_ref, k_ref, vt_ref, gate_ref, o_ref, s0_ref, s1_ref, *, tile, ch, hp):
    i = pl.program_id(2)
    dk = 2 * LANES
    qs = [q_ref[:, n * dk:(n + 1) * dk] for n in range(hp)]
    slots = (s0_ref, s1_ref)

    def produce(c, slot, diag_offset=None):
        col_max = []
        for n in range(hp):
            kc = k_ref[pl.ds(pl.multiple_of(c * ch, ch), ch), n * dk:(n + 1) * dk]
            s = lax.dot_general(kc, qs[n], _NT, preferred_element_type=F32)
            if diag_offset is not None:
                kidx = lax.broadcasted_iota(I32, (ch, tile), 0) + diag_offset
                qidx = lax.broadcasted_iota(I32, (ch, tile), 1)
                s = jnp.where(kidx <= qidx, s, NEG_BIG)
            slots[slot][n] = s
            col_max.append(jnp.max(s, axis=0, keepdims=True))
        return tuple(col_max)

    def consume(c, slot, col_max, carry):
        return tuple(_softmax_step(slots[slot][n], col_max[n], vt_ref[c, n * VT_ROWS:(n + 1) * VT_ROWS, :],
                                   *carry[n]) for n in range(hp))

    init = tuple((jnp.full((1, tile), NEG_BIG, F32), jnp.zeros((VT_ROWS, tile), F32)) for _ in range(hp))
    diag = 2 * i
    col_a = produce(diag, 0, 0)
    col_b = produce(diag + 1, 1, ch)
    carry = consume(diag, 0, col_a, init)

    def pair(j, state):
        cr, col1, c1 = state
        col0 = produce(2 * j, 0)
        cr = consume(c1, 1, col1, cr)
        col1 = produce(2 * j + 1, 1)
        cr = consume(2 * j, 0, col0, cr)
        return cr, col1, 2 * j + 1

    state = lax.fori_loop(0, i // 2, lambda jj, st: pair(2 * jj + 1, pair(2 * jj, st)),
                          (carry, col_b, diag + 1))
    carry, col1, c1 = lax.fori_loop(0, i & 1, lambda _, st: pair(i - 1, st), state)
    res = consume(c1, 1, col1, carry)
    for n in range(hp):
        o = _normalised_t(res[n][1], MLA_V)
        cols = slice(n * LANES, (n + 1) * LANES)
        o_ref[:, cols] = (o * _silu(gate_ref[:, cols])).astype(o_ref.dtype)


def _mla_attn(q, k, vt, h2d, batch, seq, tile):
    m = batch * seq
    nt = seq // tile
    ch = tile // 2
    hp = 4
    dk = 2 * LANES
    return pl.pallas_call(
        functools.partial(_mla_attn_kernel, tile=tile, ch=ch, hp=hp),
        grid=(batch, MLA_HEADS // hp, nt),
        in_specs=[
            pl.BlockSpec((tile, hp * dk), lambda b, h, i: (b * nt + i, h)),
            pl.BlockSpec((seq, hp * dk), lambda b, h, i: (b, h)),
            pl.BlockSpec((None, 2 * nt, hp * VT_ROWS, ch), lambda b, h, i: (b, 0, h, 0)),
            pl.BlockSpec((tile, hp * LANES), lambda b, h, i: (b * nt + i, COL_AG // (hp * LANES) + h)),
        ],
        out_specs=pl.BlockSpec((tile, hp * LANES), lambda b, h, i: (b * nt + i, h)),
        out_shape=jax.ShapeDtypeStruct((m, GROUP_WIDTH), BF16),
        scratch_shapes=[pltpu.VMEM((hp, ch, tile), F32), pltpu.VMEM((hp, ch, tile), F32)],
        compiler_params=_params(40 << 20, 3),
        name="mla_attn",
    )(q, k, vt, h2d)


def _hi_lo(v):
    hi = v.astype(BF16).astype(F32)
    return hi, v - hi


def _prep_dsa_kernel(q_ref, k_ref, v_ref, qi_ref, ki_ref, tab_ref,
                     q_out, k_out, vt_out, qi2_out, ki4_out, wt_out):
    def tab(t):
        return tab_ref[:, t * LANES:(t + 1) * LANES]

    scale = DSA_HEAD_DIM ** -0.5 * LOG2E
    q = q_ref[...]
    q_out[...] = (_rope(q, _tile_lanes(tab(TAB_DSA_C), DSA_HEADS), _tile_lanes(tab(TAB_DSA_S), DSA_HEADS),
                        DSA_ROT // 2, LANES) * scale).astype(BF16)
    k = k_ref[...]
    k_out[...] = _rope(k, _tile_lanes(tab(TAB_DSA_C), DSA_KV_HEADS), _tile_lanes(tab(TAB_DSA_S), DSA_KV_HEADS),
                       DSA_ROT // 2, LANES).astype(BF16)
    vt_out[...] = _vt_with_ones(v_ref[...].T, DSA_KV_HEADS, DSA_HEAD_DIM)
    reps = IDX_HEADS * IDX_DIM // LANES
    qi = _rope(qi_ref[...], _tile_lanes(tab(TAB_IDX_C), reps), _tile_lanes(tab(TAB_IDX_S), reps),
               IDX_ROT // 2, IDX_DIM)
    qb = DSA_QBLK
    first_half = lax.broadcasted_iota(I32, (qb, LANES), 1) < IDX_DIM
    for blk in range(q.shape[0] // qb):
        for pair in range(IDX_HEADS // 2):
            hi, lo = _hi_lo(qi[blk * qb:(blk + 1) * qb, pair * LANES:(pair + 1) * LANES])
            even = jnp.where(first_half, hi, pltpu.roll(lo, IDX_DIM, 1)).astype(BF16)
            odd = jnp.where(first_half, pltpu.roll(hi, IDX_DIM, 1), lo).astype(BF16)
            qi2_out[blk, pair, 0:qb, :] = even
            qi2_out[blk, pair, qb:2 * qb, :] = odd
    small = ki_ref[...]
    lane = lax.broadcasted_iota(I32, small.shape, 1)
    c_ki = jnp.where(lane < IDX_DIM, tab(TAB_IDX_C), 1.0)
    s_ki = jnp.where(lane < IDX_DIM, tab(TAB_IDX_S), 0.0)
    hi, lo = _hi_lo(_rope(small, c_ki, s_ki, IDX_ROT // 2, IDX_DIM))
    ki4_out[:, 0:LANES] = jnp.where(lane < IDX_DIM, hi, pltpu.roll(hi, IDX_DIM, 1)).astype(BF16)
    ki4_out[:, LANES:2 * LANES] = jnp.where(lane < IDX_DIM, lo, pltpu.roll(lo, IDX_DIM, 1)).astype(BF16)
    wt = small.T[IDX_DIM:IDX_DIM + IDX_HEADS, :]
    wt_out[...] = wt * (IDX_HEADS ** -0.5 * IDX_DIM ** -0.5)


def _prep_dsa(h2d, hs, tabs, batch, seq):
    tm = KEY_CHUNK
    m = batch * seq
    nt = seq // tm
    qpt = tm // DSA_QBLK
    kvw = DSA_KV_HEADS * DSA_HEAD_DIM

    def row(b, i):
        return b * nt + i

    return pl.pallas_call(
        _prep_dsa_kernel,
        grid=(batch, nt),
        in_specs=[
            pl.BlockSpec((tm, GROUP_WIDTH), lambda b, i: (row(b, i), COL_BQ // GROUP_WIDTH)),
            pl.BlockSpec((tm, kvw), lambda b, i: (row(b, i), COL_BK // kvw)),
            pl.BlockSpec((tm, kvw), lambda b, i: (row(b, i), COL_BV // kvw)),
            pl.BlockSpec((tm, GROUP_WIDTH), lambda b, i: (row(b, i), COL_QI // GROUP_WIDTH)),
            pl.BlockSpec((tm, LANES), lambda b, i: (row(b, i), SMALL_KI // LANES)),
            pl.BlockSpec((tm, N_TABS * LANES), lambda b, i: (row(b, i), 0)),
        ],
        out_specs=[
            pl.BlockSpec((tm, GROUP_WIDTH), lambda b, i: (row(b, i), 0)),
            pl.BlockSpec((tm, kvw), lambda b, i: (row(b, i), 0)),
            pl.BlockSpec((None, None, DSA_KV_HEADS * VT_ROWS, tm), lambda b, i: (b, i, 0, 0)),
            pl.BlockSpec((None, qpt, IDX_HEADS // 2, 2 * DSA_QBLK, 2 * IDX_DIM), lambda b, i: (b, i, 0, 0, 0)),
            pl.BlockSpec((tm, 4 * IDX_DIM), lambda b, i: (row(b, i), 0)),
            pl.BlockSpec((None, IDX_HEADS, tm), lambda b, i: (b, 0, i)),
        ],
        out_shape=[
            jax.ShapeDtypeStruct((m, GROUP_WIDTH), BF16),
            jax.ShapeDtypeStruct((m, kvw), BF16),
            jax.ShapeDtypeStruct((batch, nt, DSA_KV_HEADS * VT_ROWS, tm), BF16),
            jax.ShapeDtypeStruct((batch, seq // DSA_QBLK, IDX_HEADS // 2, 2 * DSA_QBLK, 2 * IDX_DIM), BF16),
            jax.ShapeDtypeStruct((m, 4 * IDX_DIM), BF16),
            jax.ShapeDtypeStruct((batch, IDX_HEADS, seq), F32),
        ],
        compiler_params=_params(24 << 20, 2),
        name="prep_dsa",
    )(h2d, h2d, h2d, h2d, hs, tabs)


def _sortable_key(score):
    bits = lax.bitcast_convert_type(score, I32)
    key = jnp.where(bits < 0, bits ^ jnp.int32(0x7FFFFFFF), bits)
    return jnp.where(score == 0.0, 0, key)


def _dsa_kernel(qi2_ref, ki4_ref, wt_ref, q_ref, k_ref, vt_ref, gate_ref, o_ref, key_ref, s0_ref, s1_ref, qi4_ref,
                *, topk, seq_bits, grp):
    i = pl.program_id(1)
    n_vis = i // 2
    n_chunks = n_vis + 1
    wt = wt_ref[...]
    ch = KEY_CHUNK
    qb = DSA_QBLK

    def rows(c):
        return pl.ds(pl.multiple_of(c * ch, ch), ch)

    for p in range(IDX_HEADS // 2):
        qi4_ref[p, :, 0:LANES] = qi2_ref[p]
        qi4_ref[p, :, LANES:2 * LANES] = qi2_ref[p]

    def score_chunk(c):
        kc = ki4_ref[rows(c), :]
        tot = None
        for p in range(IDX_HEADS // 2):
            s = lax.dot_general(kc, qi4_ref[p], _NT, preferred_element_type=F32)
            r = jnp.maximum(s, 0.0)
            t = r[:, :qb] * wt[2 * p:2 * p + 1, :] + r[:, qb:] * wt[2 * p + 1:2 * p + 2, :]
            tot = t if tot is None else tot + t
        return tot

    def phase1(c, carry):
        key_ref[rows(c), :] = _sortable_key(score_chunk(c))
        return carry

    lax.fori_loop(0, n_vis // 2, lambda p, cr: phase1(2 * p + 1, phase1(2 * p, cr)), 0)
    lax.fori_loop(0, n_vis & 1, lambda _, cr: phase1(n_vis - 1, cr), 0)
    kidx = n_vis * ch + lax.broadcasted_iota(I32, (ch, qb), 0)
    qidx = i * qb + lax.broadcasted_iota(I32, (ch, qb), 1)
    key_ref[rows(n_vis), :] = jnp.where(kidx <= qidx, _sortable_key(score_chunk(n_vis)), INT_MIN)

    n_groups = (n_chunks + grp - 1) // grp

    def pad_chunk(c, carry):
        key_ref[rows(c), :] = jnp.full((ch, qb), INT_MIN, I32)
        return carry

    lax.fori_loop(n_chunks, n_groups * grp, pad_chunk, 0)

    def count_ge(t):
        def body(p, acc):
            blk = key_ref[pl.ds(pl.multiple_of(p * grp * ch, grp * ch), grp * ch), :]
            hit = (blk >= t).astype(I32)
            return acc + jnp.sum(hit.reshape(grp * ch // 8, 8, qb), axis=0)
        acc = lax.fori_loop(0, n_groups, body, jnp.zeros((8, qb), I32))
        return jnp.sum(acc, axis=0, keepdims=True)

    c0 = count_ge(jnp.zeros((1, qb), I32))
    state = (jnp.where(c0 >= topk, 0, INT_MIN).astype(I32), c0)

    def bit_step(b, st):
        t, cnt = st
        cand = t + jnp.left_shift(jnp.int32(1), 30 - b)
        c = count_ge(cand)
        ok = c >= topk
        return jnp.where(ok, cand, t), jnp.where(ok, c, cnt)

    def settled(st):
        t, cnt = st
        return jnp.min(jnp.logical_or(cnt == topk, t == INT_MIN).astype(I32)) > 0

    def finish(st):
        t, cnt = st
        need = topk - count_ge(t + 1)
        return t, need, jnp.logical_and(cnt > topk, t > INT_MIN).astype(I32)

    def search(st, stops):
        if not stops:
            return finish(st)
        hi_bit = 30 if len(stops) == len(SEARCH_STOPS) else SEARCH_STOPS[len(SEARCH_STOPS) - len(stops) - 1] - 1
        st = lax.fori_loop(30 - hi_bit, 31 - stops[0], bit_step, st)
        if stops[0] == 0:
            return finish(st)
        return lax.cond(settled(st), lambda s_: (s_[0], jnp.ones((1, qb), I32), jnp.zeros((1, qb), I32)),
                        lambda s_: search(s_, stops[1:]), st)

    thr, need, tied = search(state, SEARCH_STOPS)

    def count(pred):
        def body(c, acc):
            idx = c * ch + lax.broadcasted_iota(I32, (ch, qb), 0)
            hit = pred(key_ref[rows(c), :], idx).astype(I32)
            return acc + jnp.sum(hit.reshape(ch // 8, 8, qb), axis=0)
        acc = lax.fori_loop(0, n_chunks, body, jnp.zeros((8, qb), I32))
        return jnp.sum(acc, axis=0, keepdims=True)

    def tie_search():
        def step(b, lo):
            cand = lo + jnp.left_shift(jnp.int32(1), seq_bits - 1 - b)
            below = count(lambda keys, idx: jnp.logical_and(keys == thr, idx < cand))
            return jnp.where(below < need, cand, lo)
        return lax.fori_loop(0, seq_bits, step, jnp.zeros((1, qb), I32))

    last_eq = lax.cond(jnp.max(tied) > 0, tie_search,
                       lambda: jnp.full((1, qb), 2 ** seq_bits, I32))
    thr_sel = jnp.maximum(thr, INT_MIN + 1)

    nq = DSA_GROUP * qb
    one_hot = (lax.broadcasted_iota(I32, (nq, qb), 0) % qb == lax.broadcasted_iota(I32, (nq, qb), 1))
    one_hot = jnp.where(one_hot, 1.0, 0.0).astype(BF16)
    qs = [jnp.concatenate([jnp.concatenate([q_ref[:, (n * DSA_GROUP + g) * LANES:(n * DSA_GROUP + g + 1) * LANES]
                                            for g in range(DSA_GROUP)], axis=0), one_hot], axis=1)
          for n in range(DSA_KV_HEADS)]

    slots = (s0_ref, s1_ref)

    def produce(c, slot):
        keys = key_ref[rows(c), :]
        idx = c * ch + lax.broadcasted_iota(I32, (ch, qb), 0)
        drop = jnp.logical_and(keys == thr, idx > last_eq)
        sel = jnp.logical_and(keys >= thr_sel, jnp.logical_not(drop))
        neg = jnp.where(sel, 0.0, NEG_BIG).astype(BF16)
        col_max = []
        for n in range(DSA_KV_HEADS):
            kc = jnp.concatenate([k_ref[rows(c), n * LANES:(n + 1) * LANES], neg], axis=1)
            s = lax.dot_general(kc, qs[n], _NT, preferred_element_type=F32)
            slots[slot][n] = s
            col_max.append(jnp.max(s, axis=0, keepdims=True))
        return tuple(col_max)

    def consume(c, slot, col_max, carry):
        return tuple(_softmax_step(slots[slot][n], col_max[n],
                                   vt_ref[c, n * VT_ROWS:(n + 1) * VT_ROWS, :], *carry[n])
                     for n in range(DSA_KV_HEADS))

    def phase3(j, state):
        carry, col0 = state
        col1 = produce(2 * j + 1, 1)
        carry = consume(2 * j, 0, col0, carry)
        col0 = produce(2 * j + 2, 0)
        carry = consume(2 * j + 1, 1, col1, carry)
        return carry, col0

    init = tuple((jnp.full((1, nq), NEG_BIG, F32), jnp.zeros((VT_ROWS, nq), F32))
                 for _ in range(DSA_KV_HEADS))
    pairs = (n_chunks - 1) // 2
    state = lax.fori_loop(0, pairs // 2, lambda jj, st: phase3(2 * jj + 1, phase3(2 * jj, st)),
                          (init, produce(0, 0)))
    carry, col0 = lax.fori_loop(0, pairs & 1, lambda _, st: phase3(pairs - 1, st), state)
    last = 2 * pairs

    def tail_two(cr):
        col1 = produce(last + 1, 1)
        cr = consume(last, 0, col0, cr)
        return consume(last + 1, 1, col1, cr)

    res = lax.cond(n_chunks - 1 > last, tail_two, lambda cr: consume(last, 0, col0, cr), carry)
    for n in range(DSA_KV_HEADS):
        o = _normalised_t(res[n][1], DSA_HEAD_DIM)
        for g in range(DSA_GROUP):
            col = (n * DSA_GROUP + g) * LANES
            o_ref[:, col:col + LANES] = (o[g * qb:(g + 1) * qb, :]
                                         * _silu(gate_ref[:, col:col + LANES])).astype(o_ref.dtype)


def _dsa(qi2, ki4, wt, q, k, vt, h2d, batch, seq):
    m = batch * seq
    nq = seq // DSA_QBLK
    nck = seq // KEY_CHUNK
    kvw = DSA_KV_HEADS * DSA_HEAD_DIM
    topk = min(TOPK_MAX, seq // 4)
    seq_bits = int(np.log2(seq))
    assert 2 ** seq_bits == seq and nck % 2 == 0 and topk <= KEY_CHUNK
    return pl.pallas_call(
        functools.partial(_dsa_kernel, topk=topk, seq_bits=seq_bits, grp=2),
        grid=(batch, nq),
        in_specs=[
            pl.BlockSpec((None, None, IDX_HEADS // 2, 2 * DSA_QBLK, 2 * IDX_DIM),
                         lambda b, i: (b, i, 0, 0, 0)),
            pl.BlockSpec((seq, 4 * IDX_DIM), lambda b, i: (b, 0)),
            pl.BlockSpec((None, IDX_HEADS, DSA_QBLK), lambda b, i: (b, 0, i)),
            pl.BlockSpec((DSA_QBLK, GROUP_WIDTH), lambda b, i: (b * nq + i, 0)),
            pl.BlockSpec((seq, kvw), lambda b, i: (b, 0)),
            pl.BlockSpec((None, nck, DSA_KV_HEADS * VT_ROWS, KEY_CHUNK), lambda b, i: (b, 0, 0, 0)),
            pl.BlockSpec((DSA_QBLK, GROUP_WIDTH), lambda b, i: (b * nq + i, COL_BG // GROUP_WIDTH)),
        ],
        out_specs=pl.BlockSpec((DSA_QBLK, GROUP_WIDTH), lambda b, i: (b * nq + i, 0)),
        out_shape=jax.ShapeDtypeStruct((m, GROUP_WIDTH), BF16),
        scratch_shapes=[pltpu.VMEM((seq, DSA_QBLK), I32),
                        pltpu.VMEM((DSA_KV_HEADS, KEY_CHUNK, DSA_GROUP * DSA_QBLK), F32),
                        pltpu.VMEM((DSA_KV_HEADS, KEY_CHUNK, DSA_GROUP * DSA_QBLK), F32),
                        pltpu.VMEM((IDX_HEADS // 2, 2 * DSA_QBLK, 4 * IDX_DIM), BF16)],
        compiler_params=_params(32 << 20, 2),
        name="dsa",
    )(qi2, ki4, wt, q, k, vt, h2d)


def _rglru_kernel(u_ref, gate_ref, cw_ref, cb_ref, wa_ref, ba_ref, wx_ref, bx_ref, lam_ref,
                  o_ref, halo_ref, h_ref, a_scr, x_scr, y_scr, *, batch, tt):
    t = pl.program_id(0)

    @pl.when(t == 0)
    def _():
        halo_ref[...] = jnp.zeros_like(halo_ref)
        h_ref[...] = jnp.zeros_like(h_ref)

    lam = lam_ref[...]
    z = -lam
    softplus = jnp.maximum(z, 0.0) + jnp.log1p(jnp.exp(-jnp.abs(z)))
    row = lax.broadcasted_iota(I32, (tt, LRU_WIDTH), 0)
    for b in range(batch):
        u = u_ref[b]
        prev = halo_ref[b]
        p1, p2, p3 = prev[7:8, :], prev[6:7, :], prev[5:6, :]
        s1 = jnp.where(row == 0, p1, pltpu.roll(u, 1, 0))
        s2 = jnp.where(row == 0, p2, jnp.where(row == 1, p1, pltpu.roll(u, 2, 0)))
        s3 = jnp.where(row == 0, p3, jnp.where(row == 1, p2, jnp.where(row == 2, p1, pltpu.roll(u, 3, 0))))
        conv = (cw_ref[3:4, :] * u + cw_ref[2:3, :] * s1 + cw_ref[1:2, :] * s2 + cw_ref[0:1, :] * s3
                + cb_ref[...])
        halo_ref[b] = u[tt - 8:, :]
        cb16 = conv.astype(BF16)
        r = _sigmoid(jnp.dot(cb16, wa_ref[...], preferred_element_type=F32) + ba_ref[...])
        ig = _sigmoid(jnp.dot(cb16, wx_ref[...], preferred_element_type=F32) + bx_ref[...])
        log_a = -LRU_C * r * softplus
        a_scr[b] = jnp.exp(log_a)
        th = jnp.tanh(log_a)
        x_scr[b] = jnp.sqrt(-2.0 * th / (1.0 - th)) * (ig * conv)

    def step(s, h):
        h = a_scr[:, pl.ds(s, 1), :] * h + x_scr[:, pl.ds(s, 1), :]
        y_scr[:, pl.ds(s, 1), :] = h
        return h

    h_ref[...] = lax.fori_loop(0, tt, step, h_ref[...], unroll=8)
    o_ref[...] = (y_scr[...] * _silu(gate_ref[...])).astype(o_ref.dtype)


def _rglru(h3d, conv_w, conv_b, wa_bd, b_a, wx_bd, b_x, lam, batch, seq, tt):
    nt = seq // tt
    w = LRU_WIDTH

    def vec(r):
        return pl.BlockSpec((r, w), lambda t: (0, 0))

    return pl.pallas_call(
        functools.partial(_rglru_kernel, batch=batch, tt=tt),
        grid=(nt,),
        in_specs=[
            pl.BlockSpec((batch, tt, w), lambda t: (0, t, COL_U // w)),
            pl.BlockSpec((batch, tt, w), lambda t: (0, t, COL_CG // w)),
            vec(CONV_WIDTH), vec(1),
            pl.BlockSpec((w, w), lambda t: (0, 0)), vec(1),
            pl.BlockSpec((w, w), lambda t: (0, 0)), vec(1),
            vec(1),
        ],
        out_specs=pl.BlockSpec((batch, tt, w), lambda t: (0, t, 0)),
        out_shape=jax.ShapeDtypeStruct((batch, seq, w), BF16),
        scratch_shapes=[pltpu.VMEM((batch, 8, w), F32), pltpu.VMEM((batch, 1, w), F32),
                        pltpu.VMEM((batch, tt, w), F32), pltpu.VMEM((batch, tt, w), F32),
                        pltpu.VMEM((batch, tt, w), F32)],
        compiler_params=_params(batch * tt * w * 36 + (20 << 20), 1),
        name="rglru",
    )(h3d, h3d, conv_w, conv_b, wa_bd, b_a, wx_bd, b_x, lam)


def _mem_attn_kernel(q_ref, gate_ref, km_ref, vm_ref, o_ref):
    scale = MEM_HEAD_DIM ** -0.5
    for h in range(MEM_HEADS):
        cols = slice(h * MEM_HEAD_DIM, (h + 1) * MEM_HEAD_DIM)
        qh = (q_ref[:, cols] * scale).astype(BF16)
        s = lax.dot_general(qh, km_ref[:, cols], _NT, preferred_element_type=F32)
        p = jnp.exp(s - jnp.max(s, axis=-1, keepdims=True))
        l = jnp.sum(p, axis=-1, keepdims=True)
        o = jnp.dot(p.astype(BF16), vm_ref[:, cols], preferred_element_type=F32) / l
        o_ref[:, cols] = (o * _silu(gate_ref[:, cols])).astype(o_ref.dtype)


def _mem_attn(h2d, km, vm, batch, seq, tm):
    m = batch * seq
    nt = seq // tm
    w = GROUP_WIDTH
    return pl.pallas_call(
        _mem_attn_kernel,
        grid=(batch, nt),
        in_specs=[
            pl.BlockSpec((tm, w), lambda b, i: (b * nt + i, COL_DQ // w)),
            pl.BlockSpec((tm, w), lambda b, i: (b * nt + i, COL_DG // w)),
            pl.BlockSpec((N_MEM, w), lambda b, i: (b, 0)),
            pl.BlockSpec((N_MEM, w), lambda b, i: (b, 0)),
        ],
        out_specs=pl.BlockSpec((tm, w), lambda b, i: (b * nt + i, 0)),
        out_shape=jax.ShapeDtypeStruct((m, w), BF16),
        compiler_params=_params(2 * tm * w * 10 + (12 << 20), 2),
        name="mem_attn",
    )(h2d, h2d, km, vm)


def _out_ln_kernel(*refs, n_lhs, kg, nj, per_emit, d_model):
    lhs = refs[:n_lhs]
    w_ref, x_ref, g_ref, b_ref, o_ref, o16_ref, z_ref, mu_ref, rs_ref = refs[n_lhs:]
    j = pl.program_id(1)

    @pl.when(j < nj)
    def _():
        acc = None
        for g in range(n_lhs):
            part = jnp.dot(lhs[g][...], w_ref[g * kg:(g + 1) * kg, :].astype(BF16), preferred_element_type=F32)
            acc = part if acc is None else acc + part
        z_ref[j] = DEEPNORM_ALPHA * x_ref[...] + acc

    @pl.when(j == nj)
    def _():
        def lane_tiles(v):
            tiles = [v[:, t * LANES:(t + 1) * LANES] for t in range(v.shape[1] // LANES)]
            return functools.reduce(lambda a, b: a + b, tiles)

        tot = functools.reduce(lambda a, b: a + b, [lane_tiles(z_ref[jj]) for jj in range(nj)])
        mu = jnp.sum(tot, axis=-1, keepdims=True) * (1.0 / d_model)
        sq = None
        for jj in range(nj):
            dlt = z_ref[jj] - mu
            part = lane_tiles(dlt * dlt)
            sq = part if sq is None else sq + part
        mu_ref[...] = mu
        rs_ref[...] = lax.rsqrt(jnp.sum(sq, axis=-1, keepdims=True) * (1.0 / d_model) + LN_EPS)

    @pl.when(j >= nj)
    def _():
        e = j - nj
        tn = z_ref.shape[2]
        for t in range(per_emit):
            cols = slice(t * tn, (t + 1) * tn)
            out = (z_ref[e * per_emit + t] - mu_ref[...]) * rs_ref[...] * g_ref[:, cols] + b_ref[:, cols]
            o_ref[:, cols] = out
            o16_ref[:, cols] = out.astype(BF16)


def _out_proj_deepnorm(lhs_list, w, layer, x, ln_g, ln_b, tm, tn, te):
    m, kg = lhs_list[0].shape
    _, k, n = w.shape
    n_lhs = len(lhs_list)
    nj, ne, per_emit = n // tn, n // te, te // tn
    assert k == kg * n_lhs and m % tm == 0 and n % te == 0 and te % tn == 0

    def col(j):
        return jnp.minimum(j, nj - 1)

    def emit(j):
        return jnp.maximum(j - nj, 0)

    est = (nj * tm * tn * 4 + 2 * (tm * k * 2 + k * tn * 4 + tm * tn * 4 + tm * te * 6) + tm * tn * 8)
    return pl.pallas_call(
        functools.partial(_out_ln_kernel, n_lhs=n_lhs, kg=kg, nj=nj, per_emit=per_emit, d_model=n),
        grid=(m // tm, nj + ne),
        in_specs=[pl.BlockSpec((tm, kg), lambda i, j: (i, 0)) for _ in range(n_lhs)]
        + [pl.BlockSpec((None, k, tn), lambda i, j: (layer, 0, col(j))),
           pl.BlockSpec((tm, tn), lambda i, j: (i, col(j))),
           pl.BlockSpec((1, te), lambda i, j: (0, emit(j))),
           pl.BlockSpec((1, te), lambda i, j: (0, emit(j)))],
        out_specs=[pl.BlockSpec((tm, te), lambda i, j: (i, emit(j))),
                   pl.BlockSpec((tm, te), lambda i, j: (i, emit(j)))],
        out_shape=[jax.ShapeDtypeStruct((m, n), F32), jax.ShapeDtypeStruct((m, n), BF16)],
        scratch_shapes=[pltpu.VMEM((nj, tm, tn), F32), pltpu.VMEM((tm, 1), F32), pltpu.VMEM((tm, 1), F32)],
        compiler_params=_params(est + (10 << 20), 2),
        name="out_proj_deepnorm",
    )(*lhs_list, w, x, ln_g, ln_b)


def _rope_tabs(positions):
    pos = positions.astype(F32)
    b, s = pos.shape

    def one(rot, period):
        inv = ROPE_THETA ** (-jnp.arange(0, rot, 2, dtype=F32) / rot)
        ang = pos[:, :, None] * inv
        c, sn = jnp.cos(ang), jnp.sin(ang)
        rest = period - rot
        cp = jnp.concatenate([c, c, jnp.ones((b, s, rest), F32)], axis=-1)
        sp = jnp.concatenate([-sn, sn, jnp.zeros((b, s, rest), F32)], axis=-1)
        reps = LANES // period
        return jnp.tile(cp, (1, 1, reps)), jnp.tile(sp, (1, 1, reps))

    parts = one(MLA_ROPE, LANES) + one(DSA_ROT, LANES) + one(IDX_ROT, IDX_DIM)
    return jnp.concatenate(parts, axis=-1).reshape(b * s, N_TABS * LANES)


def _block_diag(w):
    l = w.shape[0]
    eye = jnp.eye(LRU_BLOCKS, dtype=w.dtype)
    bd = jnp.einsum('lnde,nm->lndme', w, eye)
    return bd.reshape(l, LRU_WIDTH, LRU_WIDTH).astype(BF16)


def _layer(x, x16, mem16, tabs, lw, l, batch, seq):
    m = batch * seq
    main_rows, small_rows = _in_proj_row_tables(512)
    h2d = _in_proj(x16, lw["w_in_t"], l, main_rows, tm=min(1024, m), tn=512, name="in_proj")
    hs = _in_proj_small(x16, lw["w_in_t"], l, int(small_rows[0]), int(small_rows[1]), tm=min(2048, m))
    tile_a = min(512, seq)
    qa, ka, vta = _prep_mla(h2d, hs, tabs, lw["g_cq"], lw["g_ckv"], lw["w_uq"], lw["w_uk"], lw["w_uvt"],
                            batch, seq, tile_a)
    ya = _mla_attn(qa, ka, vta, h2d, batch, seq, tile_a)
    qb, kb, vtb, qi2, ki4, wt = _prep_dsa(h2d, hs, tabs, batch, seq)
    yb = _dsa(qi2, ki4, wt, qb, kb, vtb, h2d, batch, seq)
    yc = _rglru(h2d.reshape(batch, seq, N_MAIN), lw["conv_w"], lw["conv_b"], lw["w_rg_a"], lw["b_rg_a"],
                lw["w_rg_x"], lw["b_rg_x"], lw["lam"], batch, seq, min(512, seq))
    yc = yc.reshape(m, GROUP_WIDTH)
    km = _matmul([mem16], lw["w_mem_k"], l, BF16, tm=mem16.shape[0], tn=512, name="mem_proj_k")
    vm = _matmul([mem16], lw["w_mem_v"], l, BF16, tm=mem16.shape[0], tn=512, name="mem_proj_v")
    yd = _mem_attn(h2d, km, vm, batch, seq, min(2048, seq))
    return _out_proj_deepnorm([ya, yb, yc, yd], lw["w_o"], l, x, lw["ln_g"], lw["ln_b"],
                              tm=min(1024, m), tn=256, te=512)


def kernel(x, mem, positions, w_in, g_cq, g_ckv, w_uq, w_ukv, conv_w, conv_b, w_rg_a, b_rg_a, w_rg_x,
           b_rg_x, lru_lambda, w_mem_k, w_mem_v, w_o, ln_g, ln_b):
    batch, seq, d = x.shape
    depth = w_in.shape[0]
    tabs = _rope_tabs(positions)
    w_in_t = w_in.transpose(0, 2, 1)
    w_uq_p = jnp.pad(w_uq.reshape(depth, MLA_Q_LORA, MLA_HEADS, MLA_NOPE + MLA_ROPE),
                     [(0, 0), (0, 0), (0, 0), (0, 2 * LANES - MLA_NOPE - MLA_ROPE)])
    w_uq_p = w_uq_p.reshape(depth, MLA_Q_LORA, MLA_HEADS * 2 * LANES).astype(BF16)
    w_ukv4 = w_ukv.reshape(depth, MLA_KV_LORA, MLA_HEADS, MLA_NOPE + MLA_V)
    w_uk = w_ukv4[..., :MLA_NOPE].reshape(depth, MLA_KV_LORA, MLA_HEADS * MLA_NOPE).astype(BF16)
    w_uvt = w_ukv4[..., MLA_NOPE:].reshape(depth, MLA_KV_LORA, MLA_HEADS * MLA_V)
    w_uvt = w_uvt.transpose(0, 2, 1).astype(BF16)
    wa_bd = _block_diag(w_rg_a)
    wx_bd = _block_diag(w_rg_x)
    mem16 = mem.reshape(batch * mem.shape[1], d).astype(BF16)

    xf = x.reshape(batch * seq, d)
    x16 = xf.astype(BF16)
    for l in range(depth):
        lw = dict(w_in_t=w_in_t, g_cq=g_cq[l][None], g_ckv=g_ckv[l][None], w_uq=w_uq_p[l], w_uk=w_uk[l],
                  w_uvt=w_uvt[l], conv_w=conv_w[l], conv_b=conv_b[l][None], w_rg_a=wa_bd[l],
                  b_rg_a=b_rg_a[l][None], w_rg_x=wx_bd[l], b_rg_x=b_rg_x[l][None], lam=lru_lambda[l][None],
                  w_mem_k=w_mem_k, w_mem_v=w_mem_v, w_o=w_o, ln_g=ln_g[l][None], ln_b=ln_b[l][None])
        xf, x16 = _layer(xf, x16, mem16, tabs, lw, l, batch, seq)
    return xf.reshape(batch, seq, d)
```
